```python
import jax
import jax.numpy as jnp
from jax import lax
import numpy as np

D_MODEL = 2048
BATCH = 4
SEQ = 4096
DEPTH = 2

GRID_W = 64
CTX_LEN = 256
HEAD_DIM = 64
ROPE_THETA = 10000.0
EPS = 1e-6
NEG_INF = -1e30

A_HEADS = 8
A_KV_HEADS = 2
A_REP = A_HEADS // A_KV_HEADS
A_WINDOW = 128
A_BLOCK = 128

B_HEADS = 8
B_Q_RANK = 384
B_KV_RANK = 128
B_NOPE = 64
B_ROPE = 32
B_QK = B_NOPE + B_ROPE
B_V = 64
B_QBLOCK = 128

C_HEADS = 8
C_CHUNK = 128

D_HEADS = 8
NA_ROWS = 8
NA_COLS = 16

MIX_WIDTH = (A_HEADS + B_HEADS + C_HEADS + D_HEADS) * HEAD_DIM
IN_SIZES = (A_HEADS * HEAD_DIM, A_KV_HEADS * HEAD_DIM, A_KV_HEADS * HEAD_DIM,
            B_Q_RANK, B_KV_RANK, B_ROPE,
            C_HEADS * HEAD_DIM, C_HEADS * HEAD_DIM, C_HEADS * HEAD_DIM, C_HEADS * HEAD_DIM,
            D_HEADS * HEAD_DIM, D_HEADS * HEAD_DIM, D_HEADS * HEAD_DIM)
IN_COLS = sum(IN_SIZES)

D_FF = 7168
N_EXPERTS = 8
TOP_K = 2
EXPERT_FF = 7168
N_DENSE = (DEPTH + 1) // 2
N_MOE = DEPTH // 2

kernel_name = 'hybrid_parallel_heads_diffusion_block'


def rms_norm(x, gain=None):
    xf = x.astype(jnp.float32)
    y = xf * lax.rsqrt(jnp.mean(xf * xf, axis=-1, keepdims=True) + EPS)
    if gain is not None:
        y = y * gain.astype(jnp.float32)
    return y.astype(x.dtype)


def modulate(x, shift, scale):
    return x * (1.0 + scale) + shift


def axial_rope_tables(n_tokens, rot_dim):
    n_freq = rot_dim // 4
    t = jnp.arange(n_tokens, dtype=jnp.int32)
    row = (t // GRID_W).astype(jnp.float32)
    col = (t % GRID_W).astype(jnp.float32)
    freqs = ROPE_THETA ** (-jnp.arange(n_freq, dtype=jnp.float32) / n_freq)
    ang = jnp.concatenate([row[:, None] * freqs[None, :], col[:, None] * freqs[None, :]], axis=-1)
    return jnp.cos(ang), jnp.sin(ang)


def apply_rope(x, cos, sin):
    half = x.shape[-1] // 2
    x1, x2 = x[..., :half], x[..., half:]
    c = cos[None, :, None, :].astype(x.dtype)
    s = sin[None, :, None, :].astype(x.dtype)
    return jnp.concatenate([x1 * c - x2 * s, x1 * s + x2 * c], axis=-1)


def context_attention(q, k, v, scale, sink=None):
    s = jnp.einsum('bqhd,bkhd->bhqk', q, k).astype(jnp.float32) * scale
    n_keys = k.shape[1]
    if sink is not None:
        sink_logit = jnp.broadcast_to(sink.astype(jnp.float32)[None, :, None, None], s.shape[:3] + (1,))
        s = jnp.concatenate([s, sink_logit], axis=-1)
    p = jax.nn.softmax(s, axis=-1)[..., :n_keys]
    o = jnp.einsum('bhqk,bkhd->bqhd', p.astype(v.dtype), v)
    return o.reshape(o.shape[0], o.shape[1], -1)


def window_gqa_mixer(q, k, v, qc, kc, vc, q_gain, k_gain, sink, cos, sin, with_ctx_out):
    B, S, _ = q.shape
    L = kc.shape[1]
    scale = HEAD_DIM ** -0.5
    q = apply_rope(rms_norm(q.reshape(B, S, A_HEADS, HEAD_DIM), q_gain), cos, sin)
    k = apply_rope(rms_norm(k.reshape(B, S, A_KV_HEADS, HEAD_DIM), k_gain), cos, sin)
    v = v.reshape(B, S, A_KV_HEADS, HEAD_DIM)
    kc = rms_norm(kc.reshape(B, L, A_KV_HEADS, HEAD_DIM), k_gain)
    vc = vc.reshape(B, L, A_KV_HEADS, HEAD_DIM)
    sink32 = sink.astype(jnp.float32)
    nb = S // A_BLOCK
    qb = q.reshape(B, nb, A_BLOCK, A_KV_HEADS, A_REP, HEAD_DIM)

    def band(t):
        tp = jnp.pad(t, ((0, 0), (A_BLOCK, A_BLOCK), (0, 0), (0, 0)))
        tp = tp.reshape(B, nb + 2, A_BLOCK, A_KV_HEADS, HEAD_DIM)
        return jnp.concatenate([tp[:, :-2], tp[:, 1:-1], tp[:, 2:]], axis=2)

    kb, vb = band(k), band(v)
    n_band = 3 * A_BLOCK
    qi = jnp.arange(A_BLOCK)[:, None]
    kj = jnp.arange(n_band)[None, :] - A_BLOCK
    kpos = jnp.arange(nb)[:, None, None] * A_BLOCK + kj
    valid = (jnp.abs(qi - kj) <= A_WINDOW) & (kpos >= 0) & (kpos < S)
    s_win = jnp.einsum('bnqgrd,bnkgd->bgrnqk', qb, kb).astype(jnp.float32) * scale
    s_win = jnp.where(valid, s_win, NEG_INF)
    s_ctx = jnp.einsum('bnqgrd,bkgd->bgrnqk', qb, kc).astype(jnp.float32) * scale
    s_sink = jnp.broadcast_to(sink32.reshape(A_KV_HEADS, A_REP)[None, :, :, None, None, None],
                              s_win.shape[:-1] + (1,))
    p = jax.nn.softmax(jnp.concatenate([s_win, s_ctx, s_sink], axis=-1), axis=-1)
    o = (jnp.einsum('bgrnqk,bnkgd->bnqgrd', p[..., :n_band].astype(v.dtype), vb)
         + jnp.einsum('bgrnqk,bkgd->bnqgrd', p[..., n_band:n_band + L].astype(v.dtype), vc))
    o = o.reshape(B, S, A_HEADS * HEAD_DIM)
    o_ctx = None
    if with_ctx_out:
        qcr = rms_norm(qc.reshape(B, L, A_HEADS, HEAD_DIM), q_gain)
        o_ctx = context_attention(qcr, jnp.repeat(kc, A_REP, axis=2), jnp.repeat(vc, A_REP, axis=2), scale, sink32)
    return o, o_ctx


def mla_mixer(cq, ckv, kr, cq_c, ckv_c, kr_c, qa_gain, kva_gain, w_uq, w_ukv, q_gain, k_gain, cos, sin,
              with_ctx_out):
    scale = B_QK ** -0.5

    def queries(c_q, rotate):
        Bn, T, _ = c_q.shape
        q = rms_norm((rms_norm(c_q, qa_gain) @ w_uq).reshape(Bn, T, B_HEADS, B_QK), q_gain)
        if rotate:
            q = jnp.concatenate([q[..., :B_NOPE], apply_rope(q[..., B_NOPE:], cos, sin)], axis=-1)
        return q

    def keys_values(c_kv, k_rope, rotate):
        Bn, T, _ = c_kv.shape
        kv = (rms_norm(c_kv, kva_gain) @ w_ukv).reshape(Bn, T, B_HEADS, B_NOPE + B_V)
        k_shared = jnp.broadcast_to(k_rope[:, :, None, :], (Bn, T, B_HEADS, B_ROPE))
        k = rms_norm(jnp.concatenate([kv[..., :B_NOPE], k_shared], axis=-1), k_gain)
        if rotate:
            k = jnp.concatenate([k[..., :B_NOPE], apply_rope(k[..., B_NOPE:], cos, sin)], axis=-1)
        return k, kv[..., B_NOPE:]

    q = queries(cq, True)
    k, v = keys_values(ckv, kr, True)
    k_c, v_c = keys_values(ckv_c, kr_c, False)
    B, S = q.shape[0], q.shape[1]
    nq = S // B_QBLOCK
    q_blocks = jnp.moveaxis(q.reshape(B, nq, B_QBLOCK, B_HEADS, B_QK), 1, 0)

    def attend(q_blk):
        s = jnp.concatenate([jnp.einsum('bqhd,bkhd->bhqk', q_blk, k),
                             jnp.einsum('bqhd,bkhd->bhqk', q_blk, k_c)], axis=-1)
        p = jax.nn.softmax(s.astype(jnp.float32) * scale, axis=-1)
        return (jnp.einsum('bhqk,bkhd->bqhd', p[..., :S].astype(v.dtype), v)
                + jnp.einsum('bhqk,bkhd->bqhd', p[..., S:].astype(v.dtype), v_c))

    o = jnp.moveaxis(lax.map(attend, q_blocks), 0, 1).reshape(B, S, B_HEADS * B_V)
    o_ctx = context_attention(queries(cq_c, False), k_c, v_c, scale) if with_ctx_out else None
    return o, o_ctx


def retention_scan(q, k, v, log_gamma, s0, with_output):
    B, T, H, dk = q.shape
    dv = v.shape[-1]
    n = T // C_CHUNK
    kf = k.astype(jnp.float32).reshape(B, n, C_CHUNK, H, dk)
    vf = v.astype(jnp.float32).reshape(B, n, C_CHUNK, H, dv)
    pos = jnp.arange(C_CHUNK, dtype=jnp.float32)
    lg = log_gamma[:, None]
    k_w = jnp.exp(lg * (C_CHUNK - 1.0 - pos))
    kv_chunk = jnp.einsum('bnjhd,hj,bnjhe->bnhde', kf, k_w, vf)
    chunk_decay = jnp.exp(log_gamma * C_CHUNK)[None, :, None, None]

    def step(s, kv):
        return s * chunk_decay + kv, s

    s_final, s_prev = lax.scan(step, s0, jnp.moveaxis(kv_chunk, 1, 0))
    if not with_output:
        return None, s_final
    qf = q.astype(jnp.float32).reshape(B, n, C_CHUNK, H, dk)
    diff = pos[:, None] - pos[None, :]
    decay = jnp.where(diff >= 0, jnp.exp(lg[:, :, None] * jnp.maximum(diff, 0.0)), 0.0)
    scores = jnp.einsum('bnihd,bnjhd->bnhij', qf, kf) * decay
    o_intra = jnp.einsum('bnhij,bnjhe->bnihe', scores, vf)
    q_w = jnp.exp(lg * (pos + 1.0))
    o_cross = jnp.einsum('bnihd,hi,nbhde->bnihe', qf, q_w, s_prev)
    return (o_intra + o_cross).reshape(B, T, H, dv), s_final


def retention_mixer(q, k, v, g, qc, kc, vc, gc, log_decay, cos, sin, with_ctx_out):
    B, S, _ = q.shape
    L = kc.shape[1]
    k_scale = HEAD_DIM ** -0.5

    def heads(t):
        return t.reshape(t.shape[0], t.shape[1], C_HEADS, HEAD_DIM)

    q_l = apply_rope(heads(q), cos, sin)
    k_l = apply_rope(heads(k), cos, sin) * k_scale
    v_l = heads(v)
    q_c, k_c, v_c = heads(qc), heads(kc) * k_scale, heads(vc)
    log_gamma = -jnp.exp(log_decay.astype(jnp.float32))
    s0 = jnp.zeros((B, C_HEADS, HEAD_DIM, HEAD_DIM), jnp.float32)
    o = jnp.zeros((B, S, C_HEADS, HEAD_DIM), jnp.float32)
    o_c = jnp.zeros((B, L, C_HEADS, HEAD_DIM), jnp.float32) if with_ctx_out else None
    for direction in range(2):
        rev = (lambda t: t[:, ::-1]) if direction == 1 else (lambda t: t)
        oc_d, s_ctx = retention_scan(rev(q_c), rev(k_c), rev(v_c), log_gamma[direction], s0, with_ctx_out)
        o_d, _ = retention_scan(rev(q_l), rev(k_l), rev(v_l), log_gamma[direction], s_ctx, True)
        o = o + rev(o_d)
        if with_ctx_out:
            o_c = o_c + rev(oc_d)

    def finish(o_, g_):
        o_n = rms_norm(o_).reshape(o_.shape[0], o_.shape[1], -1)
        return (jax.nn.silu(g_.astype(jnp.float32)) * o_n).astype(g_.dtype)

    return finish(o, g), (finish(o_c, gc) if with_ctx_out else None)


def neighborhood_mixer(q, k, v, qc, kc, vc, q_gain, k_gain, rpb, with_ctx_out):
    B, S, _ = q.shape
    L = kc.shape[1]
    rows = S // GRID_W
    kh = min(NA_ROWS, rows)
    kw = NA_COLS
    scale = HEAD_DIM ** -0.5
    qg = rms_norm(q.reshape(B, rows, GRID_W, D_HEADS, HEAD_DIM), q_gain)
    kg = rms_norm(k.reshape(B, rows, GRID_W, D_HEADS, HEAD_DIM), k_gain)
    vg = v.reshape(B, rows, GRID_W, D_HEADS, HEAD_DIM)
    kc_ = rms_norm(kc.reshape(B, L, D_HEADS, HEAD_DIM), k_gain)
    vc_ = vc.reshape(B, L, D_HEADS, HEAD_DIM)
    r = jnp.arange(rows)
    row_idx = jnp.clip(r - kh // 2, 0, rows - kh)[:, None] + jnp.arange(kh)[None, :]
    k_band = kg[:, row_idx]
    v_band = vg[:, row_idx]
    cidx = jnp.arange(GRID_W)
    col_start = jnp.clip(cidx - kw // 2, 0, GRID_W - kw)
    col_ok = (cidx[None, :] >= col_start[:, None]) & (cidx[None, :] < col_start[:, None] + kw)
    d_row = row_idx - r[:, None] + (NA_ROWS - 1)
    d_col = jnp.clip(cidx[None, :] - cidx[:, None] + (NA_COLS - 1), 0, 2 * NA_COLS - 2)
    bias = rpb[:, d_row[:, None, :, None], d_col[None, :, None, :]]
    s_win = jnp.einsum('brwhd,brkchd->bhrwkc', qg, k_band).astype(jnp.float32) * scale
    s_win = jnp.where(col_ok[:, None, :], s_win + bias.astype(jnp.float32)[None], NEG_INF)
    n_win = kh * GRID_W
    s_win = s_win.reshape(B, D_HEADS, rows, GRID_W, n_win)
    s_ctx = jnp.einsum('brwhd,bkhd->bhrwk', qg, kc_).astype(jnp.float32) * scale
    p = jax.nn.softmax(jnp.concatenate([s_win, s_ctx], axis=-1), axis=-1)
    p_win = p[..., :n_win].reshape(B, D_HEADS, rows, GRID_W, kh, GRID_W).astype(v.dtype)
    o = (jnp.einsum('bhrwkc,brkchd->brwhd', p_win, v_band)
         + jnp.einsum('bhrwk,bkhd->brwhd', p[..., n_win:].astype(v.dtype), vc_))
    o = o.reshape(B, S, D_HEADS * HEAD_DIM)
    o_ctx = None
    if with_ctx_out:
        qcr = rms_norm(qc.reshape(B, L, D_HEADS, HEAD_DIM), q_gain)
        o_ctx = context_attention(qcr, kc_, vc_, scale)
    return o, o_ctx


def parallel_mixers(h, hc, w_in, w_out, a_q_norm, a_k_norm, a_sink, b_q_a_norm, b_kv_a_norm, b_w_uq, b_w_ukv,
                    b_q_norm, b_k_norm, c_log_decay, d_q_norm, d_k_norm, d_rpb, with_ctx_out):
    S = h.shape[1]
    split_at = [int(i) for i in np.cumsum(IN_SIZES)[:-1]]
    aq, ak, av, bq, bkv, bkr, cq, ck, cv, cg, dq, dk, dv = jnp.split(h @ w_in, split_at, axis=-1)
    aq_c, ak_c, av_c, bq_c, bkv_c, bkr_c, cq_c, ck_c, cv_c, cg_c, dq_c, dk_c, dv_c = jnp.split(
        hc @ w_in, split_at, axis=-1)
    cos_h, sin_h = axial_rope_tables(S, HEAD_DIM)
    cos_r, sin_r = axial_rope_tables(S, B_ROPE)
    oa, oa_c = window_gqa_mixer(aq, ak, av, aq_c, ak_c, av_c, a_q_norm, a_k_norm, a_sink, cos_h, sin_h,
                                with_ctx_out)
    ob, ob_c = mla_mixer(bq, bkv, bkr, bq_c, bkv_c, bkr_c, b_q_a_norm, b_kv_a_norm, b_w_uq, b_w_ukv,
                         b_q_norm, b_k_norm, cos_r, sin_r, with_ctx_out)
    oc, oc_c = retention_mixer(cq, ck, cv, cg, cq_c, ck_c, cv_c, cg_c, c_log_decay, cos_h, sin_h, with_ctx_out)
    od, od_c = neighborhood_mixer(dq, dk, dv, dq_c, dk_c, dv_c, d_q_norm, d_k_norm, d_rpb, with_ctx_out)
    y = jnp.concatenate([oa, ob, oc, od], axis=-1) @ w_out
    y_c = (jnp.concatenate([oa_c, ob_c, oc_c, od_c], axis=-1) @ w_out) if with_ctx_out else None
    return y, y_c


def swiglu(h, w_gate_up, w_down):
    gate, up = jnp.split(h @ w_gate_up, 2, axis=-1)
    return (jax.nn.silu(gate) * up) @ w_down


def moe_swiglu(h, router, w_gate_up, w_down):
    logits = (h @ router).astype(jnp.float32)
    top_val, top_idx = lax.top_k(logits, TOP_K)
    gates = jax.nn.softmax(top_val, axis=-1)
    combine = jnp.einsum('btk,btke->bte', gates, jax.nn.one_hot(top_idx, N_EXPERTS, dtype=jnp.float32))
    out = jnp.zeros_like(h)
    for e in range(N_EXPERTS):
        out = out + combine[..., e:e + 1].astype(h.dtype) * swiglu(h, w_gate_up[e], w_down[e])
    return out


def channel_mixer(t, layer, ffn_w_gate_up, ffn_w_down, moe_router, moe_w_gate_up, moe_w_down):
    i = layer // 2
    if layer % 2 == 0:
        return swiglu(t, ffn_w_gate_up[i], ffn_w_down[i])
    return moe_swiglu(t, moe_router[i], moe_w_gate_up[i], moe_w_down[i])


def setup_inputs(seed: int = 0) -> dict:
    key = jax.random.key(seed)
    keys = iter(jax.random.split(key, 40))

    def normal(shape, std):
        return jax.random.normal(next(keys), shape, jnp.float32) * std

    def gain(shape):
        return 1.0 + 0.1 * jax.random.normal(next(keys), shape, jnp.float32)

    heads_idx = jnp.arange(C_HEADS, dtype=jnp.float32)
    base_log_decay = jnp.log(-jnp.log(1.0 - 2.0 ** (-5.0 - heads_idx)))
    return {
        'x': normal((BATCH, SEQ, D_MODEL), 1.0),
        'c': normal((BATCH, D_MODEL), 1.0),
        'ctx': normal((BATCH, CTX_LEN, D_MODEL), 1.0),
        'c_ctx': normal((D_MODEL,), 1.0),
        'w_ada': normal((DEPTH, D_MODEL, 6 * D_MODEL), 0.5 * D_MODEL ** -0.5),
        'b_ada': normal((DEPTH, 6 * D_MODEL), 0.02),
        'norm_mix': gain((DEPTH, D_MODEL)),
        'norm_ffn': gain((DEPTH, D_MODEL)),
        'w_in': normal((DEPTH, D_MODEL, IN_COLS), D_MODEL ** -0.5),
        'w_out': normal((DEPTH, MIX_WIDTH, D_MODEL), MIX_WIDTH ** -0.5),
        'a_q_norm': gain((DEPTH, HEAD_DIM)),
        'a_k_norm': gain((DEPTH, HEAD_DIM)),
        'a_sink': normal((DEPTH, A_HEADS), 0.5),
        'b_q_a_norm': gain((DEPTH, B_Q_RANK)),
        'b_kv_a_norm': gain((DEPTH, B_KV_RANK)),
        'b_w_uq': normal((DEPTH, B_Q_RANK, B_HEADS * B_QK), B_Q_RANK ** -0.5),
        'b_w_ukv': normal((DEPTH, B_KV_RANK, B_HEADS * (B_NOPE + B_V)), B_KV_RANK ** -0.5),
        'b_q_norm': gain((DEPTH, B_QK)),
        'b_k_norm': gain((DEPTH, B_QK)),
        'c_log_decay': base_log_decay[None, None, :] + normal((DEPTH, 2, C_HEADS), 0.1),
        'd_q_norm': gain((DEPTH, HEAD_DIM)),
        'd_k_norm': gain((DEPTH, HEAD_DIM)),
        'd_rpb': normal((DEPTH, D_HEADS, 2 * NA_ROWS - 1, 2 * NA_COLS - 1), 0.1),
        'ffn_w_gate_up': normal((N_DENSE, D_MODEL, 2 * D_FF), D_MODEL ** -0.5),
        'ffn_w_down': normal((N_DENSE, D_FF, D_MODEL), D_FF ** -0.5),
        'moe_router': normal((N_MOE, D_MODEL, N_EXPERTS), D_MODEL ** -0.5),
        'moe_w_gate_up': normal((N_MOE, N_EXPERTS, D_MODEL, 2 * EXPERT_FF), D_MODEL ** -0.5),
        'moe_w_down': normal((N_MOE, N_EXPERTS, EXPERT_FF, D_MODEL), EXPERT_FF ** -0.5),
    }


def reference(x, c, ctx, c_ctx, w_ada, b_ada, norm_mix, norm_ffn, w_in, w_out, a_q_norm, a_k_norm, a_sink,
              b_q_a_norm, b_kv_a_norm, b_w_uq, b_w_ukv, b_q_norm, b_k_norm, c_log_decay, d_q_norm, d_k_norm,
              d_rpb, ffn_w_gate_up, ffn_w_down, moe_router, moe_w_gate_up, moe_w_down):
    xc = ctx
    silu_c = jax.nn.silu(c)
    silu_cc = jax.nn.silu(c_ctx)
    for l in range(DEPTH):
        last = l == DEPTH - 1
        mod = silu_c @ w_ada[l] + b_ada[l]
        mod_c = silu_cc @ w_ada[l] + b_ada[l]
        sh1, sc1, g1, sh2, sc2, g2 = [m[:, None, :] for m in jnp.split(mod, 6, axis=-1)]
        sh1c, sc1c, g1c, sh2c, sc2c, g2c = jnp.split(mod_c, 6, axis=-1)
        h = modulate(rms_norm(x, norm_mix[l]), sh1, sc1)
        hc = modulate(rms_norm(xc, norm_mix[l]), sh1c, sc1c)
        y, y_c = parallel_mixers(h, hc, w_in[l], w_out[l], a_q_norm[l], a_k_norm[l], a_sink[l],
                                 b_q_a_norm[l], b_kv_a_norm[l], b_w_uq[l], b_w_ukv[l], b_q_norm[l], b_k_norm[l],
                                 c_log_decay[l], d_q_norm[l], d_k_norm[l], d_rpb[l], not last)
        x = x + g1 * y
        h2 = modulate(rms_norm(x, norm_ffn[l]), sh2, sc2)
        x = x + g2 * channel_mixer(h2, l, ffn_w_gate_up, ffn_w_down, moe_router, moe_w_gate_up, moe_w_down)
        if not last:
            xc = xc + g1c * y_c
            h2c = modulate(rms_norm(xc, norm_ffn[l]), sh2c, sc2c)
            xc = xc + g2c * channel_mixer(h2c, l, ffn_w_gate_up, ffn_w_down, moe_router, moe_w_gate_up,
                                          moe_w_down)
    return x
```

```python
import functools

import numpy as np
import jax
import jax.numpy as jnp
from jax import lax
from jax.experimental import pallas as pl
from jax.experimental.pallas import tpu as pltpu

F32 = jnp.float32
BF16 = jnp.bfloat16

GRID_W = 64
HEAD_DIM = 64
ROPE_THETA = 10000.0
EPS = 1e-6
NEG_INF = -1e30

A_HEADS = 8
A_KV_HEADS = 2
A_BLOCK = 128
B_HEADS = 8
B_Q_RANK = 384
B_KV_RANK = 128
B_NOPE = 64
B_ROPE = 32
B_QK = B_NOPE + B_ROPE
B_V = 64
C_HEADS = 8
C_CHUNK = 128
D_HEADS = 8
NA_ROWS = 8
NA_COLS = 16
TOP_K = 2

LANES = 128
NA_QROWS = 4
NA_KROWS = NA_ROWS + NA_QROWS
NA_CASES = 4

P_AQ, P_CQ, P_CK, P_DQ, P_DK, P_AK, P_AV, P_CV, P_CG, P_DV, P_BQ, P_BKV, P_BKR, P_END = (
    0, 512, 1024, 1536, 2048, 2560, 2816, 3072, 3584, 4096, 4608, 4992, 5120, 5376)
PREP_W = P_AV

VMEM_LIMIT = 56 * 1024 * 1024


def _cp(sem, vmem=None):
    return pltpu.CompilerParams(dimension_semantics=sem, vmem_limit_bytes=vmem)


def _nt(a, b):
    return lax.dot_general(a, b, (((1,), (1,)), ((), ())), preferred_element_type=F32)


def _ada_kernel(c_ref, w_ref, b_ref, o_ref):
    c = c_ref[...]
    s = (c * jax.nn.sigmoid(c)).astype(BF16)
    o_ref[0] = jnp.dot(s, w_ref[0].astype(BF16), preferred_element_type=F32) + b_ref[0]


def _ada(cc, w_ada, b_ada):
    depth, d, n = w_ada.shape
    tn = 1024
    return pl.pallas_call(
        _ada_kernel,
        grid=(depth, n // tn),
        in_specs=[pl.BlockSpec((8, d), lambda l, j: (0, 0)),
                  pl.BlockSpec((1, d, tn), lambda l, j: (l, 0, j)),
                  pl.BlockSpec((1, 1, tn), lambda l, j: (l, 0, j))],
        out_specs=pl.BlockSpec((1, 8, tn), lambda l, j: (l, 0, j)),
        out_shape=jax.ShapeDtypeStruct((depth, 8, n), F32),
        compiler_params=_cp(("arbitrary", "arbitrary"), VMEM_LIMIT),
        name="ada",
    )(cc, w_ada, b_ada.reshape(depth, 1, n))


def _resnorm_kernel(*refs, has_res, with_router, n_exp):
    refs = list(refs)
    x_ref = refs.pop(0)
    if has_res:
        y_ref = refs.pop(0)
        gate_ref = refs.pop(0)
    gain_ref, sh_ref, sc_ref = refs[:3]
    refs = refs[3:]
    if with_router:
        router_ref = refs.pop(0)
    if has_res:
        xo_ref = refs.pop(0)
    h_ref = refs.pop(0)
    x = x_ref[...]
    if has_res:
        x = x + gate_ref[0, 0] * y_ref[...].astype(F32)
        xo_ref[...] = x
    ms = jnp.mean(x * x, axis=-1, keepdims=True)
    h = (x * lax.rsqrt(ms + EPS) * gain_ref[...]) * (1.0 + sc_ref[0, 0]) + sh_ref[0, 0]
    h_ref[...] = h.astype(h_ref.dtype)
    if with_router:
        hf_ref, route_ref = refs
        hf_ref[...] = h
        logits = jnp.dot(h, router_ref[...], preferred_element_type=F32, precision=lax.Precision.HIGHEST)
        lane = lax.broadcasted_iota(jnp.int32, logits.shape, 1)
        logits = jnp.where(lane < n_exp, logits, -jnp.inf)
        m1 = jnp.max(logits, axis=-1, keepdims=True)
        i1 = jnp.min(jnp.where(logits == m1, lane, LANES), axis=-1, keepdims=True)
        rest = jnp.where(lane == i1, -jnp.inf, logits)
        m2 = jnp.max(rest, axis=-1, keepdims=True)
        i2 = jnp.min(jnp.where(rest == m2, lane, LANES), axis=-1, keepdims=True)
        e2 = jnp.exp(m2 - m1)
        g1 = 1.0 / (1.0 + e2)
        g2 = e2 / (1.0 + e2)
        route = jnp.where(lane == 0, i1.astype(F32),
                          jnp.where(lane == 1, i2.astype(F32),
                                    jnp.where(lane == 2, g1, jnp.where(lane == 3, g2, 0.0))))
        route_ref[...] = route


def _mod_view(mod):
    return mod.reshape(mod.shape[0], 6, 1, mod.shape[1] // 6)


def _resnorm(x, y, gate_mod, gate_k, mod, gain, sh_k, sc_k, *, rows, seq, n_batch, router=None):
    d = x.shape[1]
    tr = 256
    has_res = y is not None
    with_router = router is not None

    def grp(i):
        return jnp.minimum((i * tr) // seq, n_batch)

    def modspec(k):
        return pl.BlockSpec((1, 1, 1, d), lambda i: (grp(i), k, 0, 0))

    row = pl.BlockSpec((tr, d), lambda i: (i, 0))
    in_specs = [row]
    args = [x]
    if has_res:
        in_specs += [row, modspec(gate_k)]
        args += [y, _mod_view(gate_mod)]
    in_specs += [pl.BlockSpec((1, d), lambda i: (0, 0)), modspec(sh_k), modspec(sc_k)]
    args += [gain.reshape(1, d).astype(F32), _mod_view(mod), _mod_view(mod)]
    out_shape, out_specs = [], []
    n_exp = 0
    if with_router:
        n_exp = router.shape[1]
        router = jnp.pad(router.astype(F32), ((0, 0), (0, LANES - n_exp)))
        in_specs.append(pl.BlockSpec(router.shape, lambda i: (0, 0)))
        args.append(router)
    if has_res:
        out_shape.append(jax.ShapeDtypeStruct((rows, d), F32))
        out_specs.append(row)
    out_shape.append(jax.ShapeDtypeStruct((rows, d), BF16))
    out_specs.append(row)
    if with_router:
        out_shape += [jax.ShapeDtypeStruct((rows, d), F32), jax.ShapeDtypeStruct((rows, LANES), F32)]
        out_specs += [row, pl.BlockSpec((tr, LANES), lambda i: (i, 0))]
    kern = functools.partial(_resnorm_kernel, has_res=has_res, with_router=with_router, n_exp=n_exp)
    return pl.pallas_call(
        kern, grid=(rows // tr,), in_specs=in_specs, out_specs=out_specs, out_shape=out_shape,
        compiler_params=_cp(("arbitrary",), VMEM_LIMIT), name="resnorm",
    )(*args)


def _mm_kernel(*refs, n_x):
    w_ref, o_ref = refs[n_x], refs[n_x + 1]
    if n_x == 1:
        x = refs[0][...]
    else:
        x = jnp.concatenate([r[...] for r in refs[:n_x]], axis=1)
    o_ref[...] = jnp.dot(x, w_ref[...], preferred_element_type=F32).astype(o_ref.dtype)


def _mm(xs, w, *, rows, tm, tn, out_dtype, name):
    n = w.shape[1]
    in_specs = [pl.BlockSpec((tm, a.shape[1]), lambda i, j: (i, 0)) for a in xs]
    in_specs.append(pl.BlockSpec((w.shape[0], tn), lambda i, j: (0, j)))
    return pl.pallas_call(
        functools.partial(_mm_kernel, n_x=len(xs)),
        grid=(rows // tm, n // tn),
        in_specs=in_specs,
        out_specs=pl.BlockSpec((tm, tn), lambda i, j: (i, j)),
        out_shape=jax.ShapeDtypeStruct((rows, n), out_dtype),
        compiler_params=_cp(("arbitrary", "arbitrary"), VMEM_LIMIT), name=name,
    )(*xs, w)


def _group_meansq(x, g_ref, width):
    x2 = x * x
    hi = x2.astype(BF16)
    lo = (x2 - hi.astype(F32)).astype(BF16)
    g = g_ref[:width, :width]
    ss = jnp.dot(hi, g, preferred_element_type=F32) + jnp.dot(lo, g, preferred_element_type=F32)
    return ss * (1.0 / HEAD_DIM)


def _prep_kernel(p_ref, ch_ref, sh_ref, g_ref, gains_ref, o_ref):
    tr = p_ref.shape[0]

    def seg(off, width):
        return p_ref[:, off:off + width].astype(F32)

    def norm(x, row, width):
        return x * lax.rsqrt(_group_meansq(x, g_ref, width) + EPS) * gains_ref[row:row + 1, :width]

    def rope(x, width):
        lane = lax.broadcasted_iota(jnp.int32, (tr, width), 1)
        first = (lane % HEAD_DIM) < (HEAD_DIM // 2)
        cos = jnp.concatenate([ch_ref[...]] * (width // LANES), axis=1)
        sin = jnp.concatenate([sh_ref[...]] * (width // LANES), axis=1)
        swapped = jnp.where(first, pltpu.roll(x, width - HEAD_DIM // 2, 1), pltpu.roll(x, HEAD_DIM // 2, 1))
        return x * cos + swapped * sin

    scale = HEAD_DIM ** -0.5
    o_ref[:, P_AQ:P_AQ + 512] = (rope(norm(seg(P_AQ, 512), 0, 512), 512) * scale).astype(BF16)
    o_ref[:, P_CQ:P_CQ + 512] = (rope(seg(P_CQ, 512), 512) * scale).astype(BF16)
    o_ref[:, P_CK:P_CK + 512] = rope(seg(P_CK, 512), 512).astype(BF16)
    o_ref[:, P_DQ:P_DQ + 512] = (norm(seg(P_DQ, 512), 2, 512) * scale).astype(BF16)
    o_ref[:, P_DK:P_DK + 512] = norm(seg(P_DK, 512), 3, 512).astype(BF16)
    o_ref[:, P_AK:P_AK + 256] = rope(norm(seg(P_AK, 256), 1, 256), 256).astype(BF16)


def _prep(proj, cos_h, sin_h, gmat, gains, *, rows, seq, n_lat):
    tr = 256
    nlat = seq // tr

    def tab(i):
        return (jnp.where(i * tr < n_lat, i % nlat, nlat), 0)

    return pl.pallas_call(
        _prep_kernel,
        grid=(rows // tr,),
        in_specs=[pl.BlockSpec((tr, PREP_W), lambda i: (i, 0)),
                  pl.BlockSpec((tr, LANES), tab),
                  pl.BlockSpec((tr, LANES), tab),
                  pl.BlockSpec((512, 512), lambda i: (0, 0)),
                  pl.BlockSpec((4, 512), lambda i: (0, 0))],
        out_specs=pl.BlockSpec((tr, PREP_W), lambda i: (i, 0)),
        out_shape=jax.ShapeDtypeStruct((rows, PREP_W), BF16),
        compiler_params=_cp(("arbitrary",), VMEM_LIMIT), name="prep",
    )(proj, cos_h, sin_h, gmat, gains)


def _prepb_kernel(bq_ref, bkv_ref, bkr_ref, cb_ref, sb_ref, wuq_ref, wukv_ref, gains_ref, q_ref, k_ref, v_ref):
    tr = bq_ref.shape[0]
    cq = bq_ref[...].astype(F32)
    cqn = cq * lax.rsqrt(jnp.mean(cq * cq, axis=-1, keepdims=True) + EPS) * gains_ref[0:1, :]
    qup = jnp.dot(cqn.astype(BF16), wuq_ref[...], preferred_element_type=F32)
    ckv = bkv_ref[...].astype(F32)
    ckvn = ckv * lax.rsqrt(jnp.mean(ckv * ckv, axis=-1, keepdims=True) + EPS) * gains_ref[1:2, :LANES]
    kvup = jnp.dot(ckvn.astype(BF16), wukv_ref[...], preferred_element_type=F32)
    kr = bkr_ref[...].astype(F32)
    cb, sb = cb_ref[...], sb_ref[...]
    lane = lax.broadcasted_iota(jnp.int32, (tr, LANES), 1)
    half = B_ROPE // 2
    m1 = (lane >= B_NOPE) & (lane < B_NOPE + half)
    m2 = (lane >= B_NOPE + half) & (lane < B_QK)

    def rope(x):
        swapped = jnp.where(m1, pltpu.roll(x, LANES - half, 1), jnp.where(m2, pltpu.roll(x, half, 1), 0.0))
        return x * cb + swapped * sb

    def headnorm(x, row):
        ms = jnp.sum(x * x, axis=-1, keepdims=True) * (1.0 / B_QK)
        return x * lax.rsqrt(ms + EPS) * gains_ref[row:row + 1, :LANES]

    scale = B_QK ** -0.5
    for h in range(B_HEADS):
        sl = slice(h * LANES, (h + 1) * LANES)
        q_ref[:, sl] = (rope(headnorm(qup[:, sl], 2)) * scale).astype(BF16)
        k_ref[:, sl] = rope(headnorm(kvup[:, sl] + kr, 3)).astype(BF16)
    v_ref[...] = kvup[:, B_HEADS * LANES:].astype(BF16)


def _prepb(proj, cos_b, sin_b, wuq, wukv, gains, *, rows, seq, n_lat):
    tr = 256
    nlat = seq // tr

    def tab(i):
        return (jnp.where(i * tr < n_lat, i % nlat, nlat), 0)

    return pl.pallas_call(
        _prepb_kernel,
        grid=(rows // tr,),
        in_specs=[pl.BlockSpec((tr, B_Q_RANK), lambda i: (i, P_BQ // B_Q_RANK)),
                  pl.BlockSpec((tr, LANES), lambda i: (i, P_BKV // LANES)),
                  pl.BlockSpec((tr, LANES), lambda i: (i, P_BKR // LANES)),
                  pl.BlockSpec((tr, LANES), tab),
                  pl.BlockSpec((tr, LANES), tab),
                  pl.BlockSpec(wuq.shape, lambda i: (0, 0)),
                  pl.BlockSpec(wukv.shape, lambda i: (0, 0)),
                  pl.BlockSpec((4, B_Q_RANK), lambda i: (0, 0))],
        out_specs=[pl.BlockSpec((tr, B_HEADS * LANES), lambda i: (i, 0)),
                   pl.BlockSpec((tr, B_HEADS * LANES), lambda i: (i, 0)),
                   pl.BlockSpec((tr, B_HEADS * B_V), lambda i: (i, 0))],
        out_shape=[jax.ShapeDtypeStruct((rows, B_HEADS * LANES), BF16),
                   jax.ShapeDtypeStruct((rows, B_HEADS * LANES), BF16),
                   jax.ShapeDtypeStruct((rows, B_HEADS * B_V), BF16)],
        compiler_params=_cp(("arbitrary",), VMEM_LIMIT), name="prep_mla",
    )(proj, proj, proj, cos_b, sin_b, wuq, wukv, gains)


def _wina_kernel(sink_ref, q_ref, kp_ref, ko_ref, kn_ref, kc_ref, vp_ref, vo_ref, vn_ref, vc_ref, o_ref,
                 *, nb, seq):
    n = pl.program_id(1)
    blk = A_BLOCK
    n_ctx = kc_ref.shape[0]
    kcat = jnp.concatenate([kp_ref[...], ko_ref[...], kn_ref[...], kc_ref[...]], axis=0)
    vcat = jnp.concatenate([vp_ref[...], vo_ref[...], vn_ref[...], vc_ref[...]], axis=0)
    n_keys = 3 * blk + n_ctx
    qi = lax.broadcasted_iota(jnp.int32, (blk, n_keys), 0)
    kj = lax.broadcasted_iota(jnp.int32, (blk, n_keys), 1)
    band = kj - blk
    kpos = n * blk + band
    valid = ((jnp.abs(qi - band) <= blk) & (kpos >= 0) & (kpos < seq) & (n < nb)) | (kj >= 3 * blk)
    lane = lax.broadcasted_iota(jnp.int32, (blk, LANES), 1)
    low = lane < HEAD_DIM
    for j in range(A_HEADS // 2):
        g = j // (A_HEADS // A_KV_HEADS // 2)
        qc = q_ref[:, j * LANES:(j + 1) * LANES]
        kg = kcat[:, g * LANES:(g + 1) * LANES]
        vg = vcat[:, g * LANES:(g + 1) * LANES]
        outs = []
        for e in range(2):
            qm = jnp.where(low if e == 0 else jnp.logical_not(low), qc, jnp.zeros_like(qc))
            s = jnp.where(valid, _nt(qm, kg), NEG_INF)
            sk = sink_ref[2 * j + e]
            m = jnp.maximum(jnp.max(s, axis=-1, keepdims=True), sk)
            p = jnp.exp(s - m)
            den = jnp.sum(p, axis=-1, keepdims=True) + jnp.exp(sk - m)
            outs.append(jnp.dot(p.astype(BF16), vg, preferred_element_type=F32) / den)
        o_ref[:, j * LANES:(j + 1) * LANES] = jnp.where(low, outs[0], outs[1]).astype(o_ref.dtype)


def _wina(prep, proj, sink, *, n_batch, seq, n_ctx, with_ctx):
    blk = A_BLOCK
    nb = seq // blk
    ncb = n_ctx // blk
    steps = nb + (ncb if with_ctx else 0)
    lat_blocks = n_batch * nb
    rows_out = n_batch * seq + (n_batch * n_ctx if with_ctx else 0)

    def qmap(b, n, s):
        return (jnp.where(n < nb, b * nb + n, lat_blocks + b * ncb + (n - nb)), 0)

    def kmap(delta, col):
        def f(b, n, s):
            return (b * nb + jnp.clip(n + delta, 0, nb - 1), col)
        return f

    def cmap(col):
        def f(b, n, s):
            return (n_batch * seq // n_ctx + b, col)
        return f

    kcol, vcol = P_AK // 256, P_AV // 256
    in_specs = [pl.BlockSpec((blk, 512), qmap)]
    in_specs += [pl.BlockSpec((blk, 256), kmap(dl, kcol)) for dl in (-1, 0, 1)]
    in_specs += [pl.BlockSpec((n_ctx, 256), cmap(kcol))]
    in_specs += [pl.BlockSpec((blk, 256), kmap(dl, vcol)) for dl in (-1, 0, 1)]
    in_specs += [pl.BlockSpec((n_ctx, 256), cmap(vcol))]
    return pl.pallas_call(
        functools.partial(_wina_kernel, nb=nb, seq=seq),
        grid_spec=pltpu.PrefetchScalarGridSpec(
            num_scalar_prefetch=1, grid=(n_batch, steps), in_specs=in_specs,
            out_specs=pl.BlockSpec((blk, 512), qmap)),
        out_shape=jax.ShapeDtypeStruct((rows_out, 512), BF16),
        compiler_params=_cp(("arbitrary", "arbitrary"), VMEM_LIMIT), name="mixer_window",
    )(sink, prep, prep, prep, prep, prep, proj, proj, proj, proj)


def _mla_kernel(q_ref, kl_ref, vl_ref, kc_ref, vc_ref, o_ref, *, nq, tk):
    tq = q_ref.shape[0]
    n_chunks = jnp.where(pl.program_id(2) < nq, kl_ref.shape[0] // tk, 0)
    outs = []
    for e in range(2):
        q = q_ref[:, e * LANES:(e + 1) * LANES]

        def step(k, v, m, l, acc):
            s = _nt(q, k)
            m_new = jnp.maximum(m, jnp.max(s, axis=-1, keepdims=True))
            a = jnp.exp(m - m_new)
            p = jnp.exp(s - m_new)
            l = a * l + jnp.sum(p, axis=-1, keepdims=True)
            acc = a * acc + jnp.dot(p.astype(BF16), v, preferred_element_type=F32)
            return m_new, l, acc

        carry = (jnp.full((tq, 1), -jnp.inf, F32), jnp.zeros((tq, 1), F32), jnp.zeros((tq, LANES), F32))
        def body(c, carry):
            off = pl.multiple_of(c * tk, tk)
            return step(kl_ref[pl.ds(off, tk), e * LANES:(e + 1) * LANES], vl_ref[pl.ds(off, tk), :], *carry)

        carry = lax.fori_loop(0, n_chunks, body, carry)
        m, l, acc = step(kc_ref[:, e * LANES:(e + 1) * LANES], vc_ref[...], *carry)
        outs.append(acc / l)
    lane = lax.broadcasted_iota(jnp.int32, (tq, LANES), 1)
    o_ref[...] = jnp.where(lane < B_V, outs[0], outs[1]).astype(o_ref.dtype)


def _mla(qb, kb, vb, *, n_batch, seq, n_ctx, with_ctx):
    pairs = B_HEADS // 2
    tq = 256
    assert n_ctx == tq
    nq = seq // tq
    ctx_blk = n_batch * seq // n_ctx
    rows_out = n_batch * seq + (n_batch * n_ctx if with_ctx else 0)

    def qmap(b, j, i):
        return (jnp.where(i < nq, b * nq + i, ctx_blk + b), j)

    return pl.pallas_call(
        functools.partial(_mla_kernel, nq=nq, tk=512),
        grid=(n_batch, pairs, nq + (1 if with_ctx else 0)),
        in_specs=[pl.BlockSpec((tq, 256), qmap),
                  pl.BlockSpec((seq, 256), lambda b, j, i: (b, j)),
                  pl.BlockSpec((seq, LANES), lambda b, j, i: (b, j)),
                  pl.BlockSpec((n_ctx, 256), lambda b, j, i: (ctx_blk + b, j)),
                  pl.BlockSpec((n_ctx, LANES), lambda b, j, i: (ctx_blk + b, j))],
        out_specs=pl.BlockSpec((tq, LANES), qmap),
        out_shape=jax.ShapeDtypeStruct((rows_out, 512), BF16),
        compiler_params=_cp(("arbitrary", "arbitrary", "arbitrary"), VMEM_LIMIT), name="mixer_mla",
    )(qb, kb, vb, kb, vb)


def _ret_kernel(q_ref, k_ref, v_ref, ld_ref, o_ref, s_ref):
    d = pl.program_id(1)
    step = pl.program_id(2)
    cc = C_CHUNK

    @pl.when(step == 0)
    def _():
        s_ref[...] = jnp.zeros_like(s_ref)

    fwd = d == 0
    ri = lax.broadcasted_iota(jnp.int32, (cc, cc), 0)
    ci = lax.broadcasted_iota(jnp.int32, (cc, cc), 1)
    diff = jnp.where(fwd, ri - ci, ci - ri)
    dpos = jnp.maximum(diff, 0).astype(F32)
    pos = ri.astype(F32)
    qpow = jnp.where(fwd, pos + 1.0, cc - pos)
    kpow = jnp.where(fwd, cc - 1.0 - pos, pos)
    low = ci < HEAD_DIM
    blockdiag = (ri < HEAD_DIM) == low
    for j in range(C_HEADS // 2):
        sl = slice(j * LANES, (j + 1) * LANES)
        q, k, v = q_ref[:, sl], k_ref[:, sl], v_ref[:, sl]
        lg_e = -jnp.exp(ld_ref[0, 2 * j:2 * j + 1, :])
        lg_o = -jnp.exp(ld_ref[0, 2 * j + 1:2 * j + 2, :])
        lgl = jnp.where(low[0:1, :], lg_e, lg_o)
        outs = []
        for e, lg in ((0, lg_e), (1, lg_o)):
            qm = jnp.where(low if e == 0 else jnp.logical_not(low), q, jnp.zeros_like(q))
            dec = jnp.where(diff >= 0, jnp.exp(lg * dpos), 0.0)
            sc = _nt(qm, k) * dec
            outs.append(jnp.dot(sc.astype(BF16), v, preferred_element_type=F32))
        o_intra = jnp.where(low, outs[0], outs[1])
        state = s_ref[j]
        qw = (q.astype(F32) * jnp.exp(lgl * qpow)).astype(BF16)
        o_cross = jnp.dot(qw, state.astype(BF16), preferred_element_type=F32)
        kw = (k.astype(F32) * jnp.exp(lgl * kpow)).astype(BF16)
        kv = lax.dot_general(kw, v, (((0,), (0,)), ((), ())), preferred_element_type=F32)
        cdec = jnp.where(ri < HEAD_DIM, jnp.exp(lg_e * cc), jnp.exp(lg_o * cc))
        s_ref[j] = state * cdec + jnp.where(blockdiag, kv, 0.0)
        o_ref[0, :, sl] = o_intra + o_cross


def _retention(prep, proj, ld_head, *, n_batch, seq, n_ctx):
    cc = C_CHUNK
    nl, nc = seq // cc, n_ctx // cc
    steps = nc + nl
    lat_blocks = n_batch * nl
    rows = n_batch * (seq + n_ctx)

    def rowblk(b, d, s):
        c_ctx = jnp.where(d == 0, s, nc - 1 - s)
        c_lat = jnp.where(d == 0, s - nc, nl - 1 - (s - nc))
        return jnp.where(s < nc, lat_blocks + b * nc + c_ctx, b * nl + c_lat)

    def spec(col):
        return pl.BlockSpec((cc, 512), lambda b, d, s: (rowblk(b, d, s), col))

    return pl.pallas_call(
        _ret_kernel,
        grid=(n_batch, 2, steps),
        in_specs=[spec(P_CQ // 512), spec(P_CK // 512), spec(P_CV // 512),
                  pl.BlockSpec((1, C_HEADS, LANES), lambda b, d, s: (d, 0, 0))],
        out_specs=pl.BlockSpec((1, cc, 512), lambda b, d, s: (d, rowblk(b, d, s), 0)),
        out_shape=jax.ShapeDtypeStruct((2, rows, 512), F32),
        scratch_shapes=[pltpu.VMEM((C_HEADS // 2, LANES, LANES), F32)],
        compiler_params=_cp(("arbitrary", "arbitrary", "arbitrary"), VMEM_LIMIT), name="mixer_retention",
    )(prep, prep, proj, ld_head)


def _retfin_kernel(o_ref, g_ref, gm_ref, out_ref):
    o = o_ref[0] + o_ref[1]
    on = o * lax.rsqrt(_group_meansq(o, gm_ref, 512) + EPS)
    g = g_ref[...].astype(F32)
    out_ref[...] = (g * jax.nn.sigmoid(g) * on).astype(out_ref.dtype)


def _retention_finish(o2, proj, gmat, *, rows):
    tr = 256
    return pl.pallas_call(
        _retfin_kernel,
        grid=(rows // tr,),
        in_specs=[pl.BlockSpec((2, tr, 512), lambda i: (0, i, 0)),
                  pl.BlockSpec((tr, 512), lambda i: (i, P_CG // 512)),
                  pl.BlockSpec((512, 512), lambda i: (0, 0))],
        out_specs=pl.BlockSpec((tr, 512), lambda i: (i, 0)),
        out_shape=jax.ShapeDtypeStruct((rows, 512), BF16),
        compiler_params=_cp(("arbitrary",), VMEM_LIMIT), name="retention_finish",
    )(o2, proj, gmat)


def _nbr_kernel(q_ref, k_ref, v_ref, kc_ref, vc_ref, abm_ref, o_ref, *, rows):
    st = pl.program_id(1)
    start = jnp.clip(NA_QROWS * st - NA_ROWS // 2, 0, rows - NA_KROWS)
    off = pl.multiple_of(start * GRID_W, GRID_W)
    nk = NA_KROWS * GRID_W
    tq = q_ref.shape[0]
    lane = lax.broadcasted_iota(jnp.int32, (tq, LANES), 1)
    low = lane < HEAD_DIM
    for j in range(D_HEADS // 2):
        sl = slice(j * LANES, (j + 1) * LANES)
        qc = q_ref[:, sl]
        kw = k_ref[pl.ds(off, nk), sl]
        vw = v_ref[pl.ds(off, nk), sl]
        kc, vc = kc_ref[:, sl], vc_ref[:, sl]
        outs = []
        for e in range(2):
            qm = jnp.where(low if e == 0 else jnp.logical_not(low), qc, jnp.zeros_like(qc))
            s1 = _nt(qm, kw) + abm_ref[0, 2 * j + e]
            s2 = _nt(qm, kc)
            m = jnp.maximum(jnp.max(s1, axis=-1, keepdims=True), jnp.max(s2, axis=-1, keepdims=True))
            p1 = jnp.exp(s1 - m)
            p2 = jnp.exp(s2 - m)
            den = jnp.sum(p1, axis=-1, keepdims=True) + jnp.sum(p2, axis=-1, keepdims=True)
            o = (jnp.dot(p1.astype(BF16), vw, preferred_element_type=F32)
                 + jnp.dot(p2.astype(BF16), vc, preferred_element_type=F32))
            outs.append(o / den)
        o_ref[:, sl] = jnp.where(low, outs[0], outs[1]).astype(o_ref.dtype)


def _nbr(prep, proj, abm, *, n_batch, seq, n_ctx, with_ctx):
    rows = seq // GRID_W
    tq = NA_QROWS * GRID_W
    assert n_ctx == tq and rows % NA_QROWS == 0 and rows >= NA_KROWS + NA_QROWS
    nst = rows // NA_QROWS
    steps = nst + (1 if with_ctx else 0)
    rows_out = n_batch * seq + (n_batch * n_ctx if with_ctx else 0)

    def qmap(col):
        def f(b, s):
            return (jnp.where(s < nst, b * nst + s, n_batch * nst + b), col)
        return f

    def case(b, s):
        c = jnp.where(s == 0, 0, jnp.where(s == nst - 1, 2, jnp.where(s == nst, 3, 1)))
        return (c, 0, 0, 0)

    ctx_blk = n_batch * seq // n_ctx
    return pl.pallas_call(
        functools.partial(_nbr_kernel, rows=rows),
        grid=(n_batch, steps),
        in_specs=[pl.BlockSpec((tq, 512), qmap(P_DQ // 512)),
                  pl.BlockSpec((seq, 512), lambda b, s: (b, P_DK // 512)),
                  pl.BlockSpec((seq, 512), lambda b, s: (b, P_DV // 512)),
                  pl.BlockSpec((n_ctx, 512), lambda b, s: (ctx_blk + b, P_DK // 512)),
                  pl.BlockSpec((n_ctx, 512), lambda b, s: (ctx_blk + b, P_DV // 512)),
                  pl.BlockSpec((1, D_HEADS, tq, NA_KROWS * GRID_W), case)],
        out_specs=pl.BlockSpec((tq, 512), qmap(0)),
        out_shape=jax.ShapeDtypeStruct((rows_out, 512), BF16),
        compiler_params=_cp(("arbitrary", "arbitrary"), VMEM_LIMIT), name="mixer_neighbourhood",
    )(prep, prep, proj, prep, proj, abm)


def _nbr_bias_tables(rpb, rows):
    w = GRID_W
    cidx = np.arange(w)
    col_start = np.clip(cidx - NA_COLS // 2, 0, w - NA_COLS)
    col_ok = (cidx[None, :] >= col_start[:, None]) & (cidx[None, :] < col_start[:, None] + NA_COLS)
    d_col = np.clip(cidx[None, :] - cidx[:, None] + (NA_COLS - 1), 0, 2 * NA_COLS - 2)
    n_dr = 2 * NA_ROWS - 1
    t = jnp.take(rpb.astype(F32), jnp.asarray(d_col.reshape(-1)), axis=2)
    t = t.reshape(rpb.shape[0], n_dr, w, w).transpose(0, 2, 1, 3)
    t = jnp.where(jnp.asarray(col_ok)[None, :, None, :], t, NEG_INF)
    t = jnp.concatenate([t, jnp.full((rpb.shape[0], w, 1, w), NEG_INF, F32)], axis=2)
    idx = np.full((NA_CASES, NA_QROWS, NA_KROWS), n_dr, np.int32)
    for c, r0 in enumerate((0, NA_QROWS, rows - NA_QROWS)):
        start = int(np.clip(r0 - NA_ROWS // 2, 0, rows - NA_KROWS))
        for i in range(NA_QROWS):
            r = r0 + i
            row_start = int(np.clip(r - NA_ROWS // 2, 0, rows - NA_ROWS))
            for kk in range(NA_KROWS):
                kr = start + kk
                if row_start <= kr < row_start + NA_ROWS:
                    idx[c, i, kk] = kr - r + (NA_ROWS - 1)
    out = jnp.take(t, jnp.asarray(idx.reshape(-1)), axis=2)
    out = out.reshape(rpb.shape[0], w, NA_CASES, NA_QROWS, NA_KROWS, w).transpose(2, 0, 3, 1, 4, 5)
    return out.reshape(NA_CASES, rpb.shape[0], NA_QROWS * w, NA_KROWS * w)


def _ffn_kernel(te_ref, tv_ref, x_ref, wg_ref, wu_ref, wd_ref, o_ref, acc_ref):
    i, j = pl.program_id(0), pl.program_id(1)
    last = pl.num_programs(1) - 1
    valid = tv_ref[i] > 0

    @pl.when(j == 0)
    def _():
        acc_ref[...] = jnp.zeros_like(acc_ref)

    @pl.when(valid)
    def _():
        x = x_ref[...]
        g = jnp.dot(x, wg_ref[0], preferred_element_type=F32)
        u = jnp.dot(x, wu_ref[0], preferred_element_type=F32)
        a = (g * jax.nn.sigmoid(g) * u).astype(BF16)
        acc_ref[...] += jnp.dot(a, wd_ref[0], preferred_element_type=F32)

    @pl.when(j == last)
    def _():
        o_ref[...] = acc_ref[...].astype(o_ref.dtype)


def _ffn(x, w_gate_up, w_down, tile_expert, tile_valid, *, rows, tm, fc, out_dtype):
    n_exp, d, ff2 = w_gate_up.shape
    ff = ff2 // 2
    nj = ff // fc

    def jeff(i, j, tv):
        return jnp.where(tv[i] > 0, j, nj - 1)

    return pl.pallas_call(
        _ffn_kernel,
        grid_spec=pltpu.PrefetchScalarGridSpec(
            num_scalar_prefetch=2, grid=(rows // tm, nj),
            in_specs=[pl.BlockSpec((tm, d), lambda i, j, te, tv: (i, 0)),
                      pl.BlockSpec((1, d, fc), lambda i, j, te, tv: (te[i], 0, jeff(i, j, tv))),
                      pl.BlockSpec((1, d, fc), lambda i, j, te, tv: (te[i], 0, nj + jeff(i, j, tv))),
                      pl.BlockSpec((1, fc, d), lambda i, j, te, tv: (te[i], jeff(i, j, tv), 0))],
            out_specs=pl.BlockSpec((tm, d), lambda i, j, te, tv: (i, 0)),
            scratch_shapes=[pltpu.VMEM((tm, d), F32)]),
        out_shape=jax.ShapeDtypeStruct((rows, d), out_dtype),
        compiler_params=_cp(("arbitrary", "arbitrary"), VMEM_LIMIT), name="swiglu",
    )(tile_expert, tile_valid, x, w_gate_up, w_gate_up, w_down)


def _gather_kernel(idx_ref, src_ref, o_ref, buf_ref, sem):
    base = pl.program_id(0) * buf_ref.shape[0]
    n = buf_ref.shape[0]

    def copy(r):
        return pltpu.make_async_copy(src_ref.at[pl.ds(idx_ref[base + r], 1), :], buf_ref.at[pl.ds(r, 1), :], sem)

    def issue(r, c):
        copy(r).start()
        return c

    def wait(r, c):
        copy(r).wait()
        return c

    lax.fori_loop(0, n, issue, 0)
    lax.fori_loop(0, n, wait, 0)
    o_ref[...] = buf_ref[...].astype(o_ref.dtype)


def _gather_rows(src, idx, *, out_dtype):
    n_out = idx.shape[0]
    d = src.shape[1]
    tr = 256
    return pl.pallas_call(
        _gather_kernel,
        grid_spec=pltpu.PrefetchScalarGridSpec(
            num_scalar_prefetch=1, grid=(n_out // tr,),
            in_specs=[pl.BlockSpec(memory_space=pl.ANY)],
            out_specs=pl.BlockSpec((tr, d), lambda i, idx: (i, 0)),
            scratch_shapes=[pltpu.VMEM((tr, d), src.dtype), pltpu.SemaphoreType.DMA(())]),
        out_shape=jax.ShapeDtypeStruct((n_out, d), out_dtype),
        compiler_params=_cp(("arbitrary",), VMEM_LIMIT), name="moe_dispatch",
    )(idx, src)


def _combine_kernel(p0_ref, p1_ref, y_ref, x_ref, route_ref, gate_ref, o_ref, b0_ref, b1_ref, sem):
    n = b0_ref.shape[0]
    base = pl.program_id(0) * n

    def copy0(r):
        return pltpu.make_async_copy(y_ref.at[pl.ds(p0_ref[base + r], 1), :], b0_ref.at[pl.ds(r, 1), :], sem.at[0])

    def copy1(r):
        return pltpu.make_async_copy(y_ref.at[pl.ds(p1_ref[base + r], 1), :], b1_ref.at[pl.ds(r, 1), :], sem.at[1])

    def issue(r, c):
        copy0(r).start()
        copy1(r).start()
        return c

    def wait(r, c):
        copy0(r).wait()
        copy1(r).wait()
        return c

    lax.fori_loop(0, n, issue, 0)
    lax.fori_loop(0, n, wait, 0)
    route = route_ref[...]
    moe = route[:, 2:3] * b0_ref[...] + route[:, 3:4] * b1_ref[...]
    o_ref[...] = x_ref[...] + gate_ref[0, 0] * moe


def _combine(y, x, route, mod, gate_k, pos0, pos1, *, seq, n_batch):
    t, d = x.shape
    tr = 256
    return pl.pallas_call(
        _combine_kernel,
        grid_spec=pltpu.PrefetchScalarGridSpec(
            num_scalar_prefetch=2, grid=(t // tr,),
            in_specs=[pl.BlockSpec(memory_space=pl.ANY),
                      pl.BlockSpec((tr, d), lambda i, a, b: (i, 0)),
                      pl.BlockSpec((tr, LANES), lambda i, a, b: (i, 0)),
                      pl.BlockSpec((1, 1, 1, d),
                                   lambda i, a, b: (jnp.minimum((i * tr) // seq, n_batch), gate_k, 0, 0))],
            out_specs=pl.BlockSpec((tr, d), lambda i, a, b: (i, 0)),
            scratch_shapes=[pltpu.VMEM((tr, d), F32), pltpu.VMEM((tr, d), F32), pltpu.SemaphoreType.DMA((2,))]),
        out_shape=jax.ShapeDtypeStruct((t, d), F32),
        compiler_params=_cp(("arbitrary",), VMEM_LIMIT), name="moe_combine",
    )(pos0, pos1, y, x, route, _mod_view(mod))


def _route_meta(idx, n_exp, tm):
    t = idx.shape[0]
    flat = idx.reshape(-1)
    onehot = (flat[:, None] == jnp.arange(n_exp, dtype=jnp.int32)[None, :]).astype(jnp.int32)
    counts = jnp.sum(onehot, axis=0)
    rank = jnp.take_along_axis(jnp.cumsum(onehot, axis=0) - onehot, flat[:, None], axis=1)[:, 0]
    padded = ((counts + tm - 1) // tm) * tm
    ends = jnp.cumsum(padded)
    starts = ends - padded
    pos = starts[flat] + rank
    n_rows = TOP_K * t + n_exp * tm
    row_token = jnp.zeros((n_rows,), jnp.int32).at[pos].set(jnp.arange(TOP_K * t, dtype=jnp.int32) // TOP_K)
    tile_start = jnp.arange(n_rows // tm, dtype=jnp.int32) * tm
    tile_valid = (tile_start < ends[-1]).astype(jnp.int32)
    tile_expert = jnp.minimum(jnp.searchsorted(ends, tile_start, side="right"), n_exp - 1).astype(jnp.int32)
    last_valid = jnp.maximum(ends[-1] // tm - 1, 0)
    tile_expert = jnp.where(tile_valid > 0, tile_expert, tile_expert[last_valid])
    pos = pos.reshape(t, TOP_K).astype(jnp.int32)
    return row_token, tile_expert, tile_valid, pos[:, 0], pos[:, 1]


def _rope_tables(seq, pad_rows):
    t = jnp.arange(seq, dtype=jnp.int32)
    row = (t // GRID_W).astype(F32)
    col = (t % GRID_W).astype(F32)

    def angles(rot_dim):
        n_freq = rot_dim // 4
        freqs = ROPE_THETA ** (-jnp.arange(n_freq, dtype=F32) / n_freq)
        ang = jnp.concatenate([row[:, None] * freqs[None, :], col[:, None] * freqs[None, :]], axis=-1)
        return jnp.cos(ang), jnp.sin(ang)

    ch, sh = angles(HEAD_DIM)
    cos_h = jnp.concatenate([ch] * 4, axis=1)
    sin_h = jnp.concatenate([-sh, sh, -sh, sh], axis=1)
    cb, sb = angles(B_ROPE)
    one = jnp.ones((seq, B_NOPE), F32)
    zero = jnp.zeros((seq, B_NOPE), F32)
    tail1 = jnp.ones((seq, LANES - B_QK), F32)
    tail0 = jnp.zeros((seq, LANES - B_QK), F32)
    cos_b = jnp.concatenate([one, cb, cb, tail1], axis=1)
    sin_b = jnp.concatenate([zero, -sb, sb, tail0], axis=1)

    def pad(tab, fill):
        return jnp.concatenate([tab, jnp.full((pad_rows, LANES), fill, F32)], axis=0)

    return pad(cos_h, 1.0), pad(sin_h, 0.0), pad(cos_b, 1.0), pad(sin_b, 0.0)


def _permute_w_in(w):
    d = w.shape[0]
    o = np.cumsum([0, 512, 128, 128, B_Q_RANK, B_KV_RANK, B_ROPE, 512, 512, 512, 512, 512, 512, 512])
    aq, ak, av, bq, bkv, bkr, cq, ck, cv, cg, dq, dk, dv = [w[:, o[i]:o[i + 1]] for i in range(13)]

    def dup(m):
        return jnp.concatenate([m[:, :64], m[:, :64], m[:, 64:], m[:, 64:]], axis=1)

    z = lambda n: jnp.zeros((d, n), w.dtype)
    parts = [aq, cq, ck, dq, dk, dup(ak), dup(av), cv, cg, dv, bq, bkv,
             z(B_NOPE), bkr, z(LANES - B_QK), z(P_END - P_BKR - LANES)]
    return jnp.concatenate(parts, axis=1).astype(BF16)


def _block_diag_ones():
    i = np.arange(512) // HEAD_DIM
    return jnp.asarray((i[:, None] == i[None, :]).astype(np.float32), dtype=BF16)


def _pad_lanes(v, width):
    return jnp.concatenate([v.astype(F32), jnp.zeros((width - v.shape[0],), F32)])


def kernel(x, c, ctx, c_ctx, w_ada, b_ada, norm_mix, norm_ffn, w_in, w_out, a_q_norm, a_k_norm, a_sink,
           b_q_a_norm, b_kv_a_norm, b_w_uq, b_w_ukv, b_q_norm, b_k_norm, c_log_decay, d_q_norm, d_k_norm,
           d_rpb, ffn_w_gate_up, ffn_w_down, moe_router, moe_w_gate_up, moe_w_down):
    n_batch, seq, d = x.shape
    n_ctx = ctx.shape[1]
    depth = w_ada.shape[0]
    n_lat = n_batch * seq
    n_all = n_lat + n_batch * n_ctx
    grid_rows = seq // GRID_W
    tm = next(t for t in (1024, 512, 256) if n_lat % t == 0 and n_all % t == 0)

    cc = jnp.concatenate([c, c_ctx[None, :], jnp.zeros((8 - n_batch - 1, d), F32)], axis=0)
    mod = _ada(cc, w_ada, b_ada)
    xs = jnp.concatenate([x.reshape(n_lat, d), ctx.reshape(n_batch * n_ctx, d)], axis=0)
    cos_h, sin_h, cos_b, sin_b = _rope_tables(seq, 256)
    gmat = _block_diag_ones()

    y = None
    y_gate, y_mod = 0, None
    for l in range(depth):
        last = l == depth - 1
        mod_l = mod[l]
        if y is None:
            (h,) = _resnorm(xs, None, None, 0, mod_l, norm_mix[l], 0, 1, rows=n_all, seq=seq, n_batch=n_batch)
        else:
            xs, h = _resnorm(xs, y, y_mod, y_gate, mod_l, norm_mix[l], 0, 1, rows=n_all, seq=seq,
                             n_batch=n_batch)
        proj = _mm([h], _permute_w_in(w_in[l]), rows=n_all, tm=tm, tn=768, out_dtype=BF16, name="proj_in")
        gains = jnp.stack([jnp.tile(a_q_norm[l], 8), _pad_lanes(jnp.tile(a_k_norm[l], 4), 512),
                           jnp.tile(d_q_norm[l], 8), jnp.tile(d_k_norm[l], 8)]).astype(F32)
        prep = _prep(proj, cos_h, sin_h, gmat, gains, rows=n_all, seq=seq, n_lat=n_lat)
        wuq = jnp.pad(b_w_uq[l].reshape(B_Q_RANK, B_HEADS, B_QK),
                      ((0, 0), (0, 0), (0, LANES - B_QK))).reshape(B_Q_RANK, B_HEADS * LANES).astype(BF16)
        wukv3 = b_w_ukv[l].reshape(B_KV_RANK, B_HEADS, B_NOPE + B_V)
        wukv = jnp.concatenate(
            [jnp.pad(wukv3[:, :, :B_NOPE], ((0, 0), (0, 0), (0, LANES - B_NOPE))).reshape(B_KV_RANK, -1),
             wukv3[:, :, B_NOPE:].reshape(B_KV_RANK, -1)], axis=1).astype(BF16)
        gains_b = jnp.stack([b_q_a_norm[l].astype(F32), _pad_lanes(b_kv_a_norm[l], B_Q_RANK),
                             _pad_lanes(b_q_norm[l], B_Q_RANK), _pad_lanes(b_k_norm[l], B_Q_RANK)])
        qb, kb, vb = _prepb(proj, cos_b, sin_b, wuq, wukv, gains_b, rows=n_all, seq=seq, n_lat=n_lat)

        oa = _wina(prep, proj, a_sink[l].astype(F32), n_batch=n_batch, seq=seq, n_ctx=n_ctx, with_ctx=not last)
        ob = _mla(qb, kb, vb, n_batch=n_batch, seq=seq, n_ctx=n_ctx, with_ctx=not last)
        ld_head = jnp.broadcast_to(c_log_decay[l].astype(F32)[:, :, None], (2, C_HEADS, LANES))
        oc2 = _retention(prep, proj, ld_head, n_batch=n_batch, seq=seq, n_ctx=n_ctx)
        abm = _nbr_bias_tables(d_rpb[l], grid_rows)
        od = _nbr(prep, proj, abm, n_batch=n_batch, seq=seq, n_ctx=n_ctx, with_ctx=not last)
        rows_l = n_lat if last else n_all
        oc = _retention_finish(oc2, proj, gmat, rows=rows_l)
        ymix = _mm([oa, ob, oc, od], w_out[l].astype(BF16), rows=rows_l, tm=tm, tn=1024, out_dtype=BF16,
                   name="proj_out")
        i = l // 2
        if l % 2 == 0:
            xs, h2 = _resnorm(xs, ymix, mod_l, 2, mod_l, norm_ffn[l], 3, 4, rows=rows_l, seq=seq,
                              n_batch=n_batch)
            n_tiles = rows_l // tm
            y = _ffn(h2, ffn_w_gate_up[i:i + 1].astype(BF16), ffn_w_down[i:i + 1].astype(BF16),
                     jnp.zeros((n_tiles,), jnp.int32), jnp.ones((n_tiles,), jnp.int32),
                     rows=rows_l, tm=tm, fc=512, out_dtype=BF16)
            y_gate, y_mod = 5, mod_l
        else:
            xs, h2, h2f, route = _resnorm(xs, ymix, mod_l, 2, mod_l, norm_ffn[l], 3, 4, rows=rows_l, seq=seq,
                                          n_batch=n_batch, router=moe_router[i])
            top_idx = route[:, :TOP_K].astype(jnp.int32)
            row_token, tile_expert, tile_valid, pos0, pos1 = _route_meta(top_idx, moe_router.shape[2], tm)
            xg = _gather_rows(h2f, row_token, out_dtype=BF16)
            yg = _ffn(xg, moe_w_gate_up[i].astype(BF16), moe_w_down[i].astype(BF16), tile_expert, tile_valid,
                      rows=xg.shape[0], tm=tm, fc=512, out_dtype=F32)
            xs = _combine(yg, xs, route, mod_l, 5, pos0, pos1, seq=seq, n_batch=n_batch)
            y = None
    if y is not None:
        xs, _ = _resnorm(xs, y, y_mod, y_gate, y_mod, norm_ffn[depth - 1], 3, 4, rows=xs.shape[0], seq=seq,
                         n_batch=n_batch)
    return xs[:n_lat].reshape(n_batch, seq, d)
```

```python
import functools

import numpy as np
import jax
import jax.numpy as jnp
from jax import lax
from jax.experimental import pallas as pl
from jax.experimental.pallas import tpu as pltpu

F32 = jnp.float32
BF16 = jnp.bfloat16

GRID_W = 64
HEAD_DIM = 64
ROPE_THETA = 10000.0
EPS = 1e-6
NEG_INF = -1e30

A_HEADS = 8
A_KV_HEADS = 2
A_BLOCK = 128
B_HEADS = 8
B_Q_RANK = 384
B_KV_RANK = 128
B_NOPE = 64
B_ROPE = 32
B_QK = B_NOPE + B_ROPE
B_V = 64
C_HEADS = 8
C_CHUNK = 128
D_HEADS = 8
NA_ROWS = 8
NA_COLS = 16
TOP_K = 2

LANES = 128
NA_QROWS = 4
NA_KROWS = NA_ROWS + NA_QROWS
NA_CASES = 4

P_AQ, P_CQ, P_CK, P_DQ, P_DK, P_AK, P_AV, P_CV, P_CG, P_DV, P_BQ, P_BKV, P_BKR, P_END = (
    0, 512, 1024, 1536, 2048, 2560, 2816, 3072, 3584, 4096, 4608, 4992, 5120, 5376)
PREP_W = P_AV

VMEM_LIMIT = 56 * 1024 * 1024

MLA_LOGIT_SCALE = B_QK ** -0.5 * float(np.log2(np.e))
MLA_SAFE_BOUND = 40.0


def _cp(sem, vmem=None):
    return pltpu.CompilerParams(dimension_semantics=sem, vmem_limit_bytes=vmem)


def _nt(a, b):
    return lax.dot_general(a, b, (((1,), (1,)), ((), ())), preferred_element_type=F32)


def _ada_kernel(c_ref, w_ref, b_ref, o_ref):
    c = c_ref[...]
    s = (c * jax.nn.sigmoid(c)).astype(BF16)
    o_ref[0] = jnp.dot(s, w_ref[0].astype(BF16), preferred_element_type=F32) + b_ref[0]


def _ada(cc, w_ada, b_ada):
    depth, d, n = w_ada.shape
    tn = 1024
    return pl.pallas_call(
        _ada_kernel,
        grid=(depth, n // tn),
        in_specs=[pl.BlockSpec((8, d), lambda l, j: (0, 0)),
                  pl.BlockSpec((1, d, tn), lambda l, j: (l, 0, j)),
                  pl.BlockSpec((1, 1, tn), lambda l, j: (l, 0, j))],
        out_specs=pl.BlockSpec((1, 8, tn), lambda l, j: (l, 0, j)),
        out_shape=jax.ShapeDtypeStruct((depth, 8, n), F32),
        compiler_params=_cp(("arbitrary", "arbitrary"), VMEM_LIMIT),
        name="ada",
    )(cc, w_ada, b_ada.reshape(depth, 1, n))


def _resnorm_kernel(*refs, has_res, with_router, n_exp):
    refs = list(refs)
    x_ref = refs.pop(0)
    if has_res:
        y_ref = refs.pop(0)
        gate_ref = refs.pop(0)
    gain_ref, sh_ref, sc_ref = refs[:3]
    refs = refs[3:]
    if with_router:
        router_ref = refs.pop(0)
    if has_res:
        xo_ref = refs.pop(0)
    h_ref = refs.pop(0)
    x = x_ref[...]
    if has_res:
        x = x + gate_ref[0, 0] * y_ref[...].astype(F32)
        xo_ref[...] = x
    ms = jnp.mean(x * x, axis=-1, keepdims=True)
    h = (x * lax.rsqrt(ms + EPS) * gain_ref[...]) * (1.0 + sc_ref[0, 0]) + sh_ref[0, 0]
    h_ref[...] = h.astype(h_ref.dtype)
    if with_router:
        hf_ref, route_ref = refs
        hf_ref[...] = h
        logits = jnp.dot(h, router_ref[...], preferred_element_type=F32, precision=lax.Precision.HIGHEST)
        lane = lax.broadcasted_iota(jnp.int32, logits.shape, 1)
        logits = jnp.where(lane < n_exp, logits, -jnp.inf)
        m1 = jnp.max(logits, axis=-1, keepdims=True)
        i1 = jnp.min(jnp.where(logits == m1, lane, LANES), axis=-1, keepdims=True)
        rest = jnp.where(lane == i1, -jnp.inf, logits)
        m2 = jnp.max(rest, axis=-1, keepdims=True)
        i2 = jnp.min(jnp.where(rest == m2, lane, LANES), axis=-1, keepdims=True)
        e2 = jnp.exp(m2 - m1)
        g1 = 1.0 / (1.0 + e2)
        g2 = e2 / (1.0 + e2)
        route = jnp.where(lane == 0, i1.astype(F32),
                          jnp.where(lane == 1, i2.astype(F32),
                                    jnp.where(lane == 2, g1, jnp.where(lane == 3, g2, 0.0))))
        route_ref[...] = route


def _mod_view(mod):
    return mod.reshape(mod.shape[0], 6, 1, mod.shape[1] // 6)


def _resnorm(x, y, gate_mod, gate_k, mod, gain, sh_k, sc_k, *, rows, seq, n_batch, router=None):
    d = x.shape[1]
    tr = 256
    has_res = y is not None
    with_router = router is not None

    def grp(i):
        return jnp.minimum((i * tr) // seq, n_batch)

    def modspec(k):
        return pl.BlockSpec((1, 1, 1, d), lambda i: (grp(i), k, 0, 0))

    row = pl.BlockSpec((tr, d), lambda i: (i, 0))
    in_specs = [row]
    args = [x]
    if has_res:
        in_specs += [row, modspec(gate_k)]
        args += [y, _mod_view(gate_mod)]
    in_specs += [pl.BlockSpec((1, d), lambda i: (0, 0)), modspec(sh_k), modspec(sc_k)]
    args += [gain.reshape(1, d).astype(F32), _mod_view(mod), _mod_view(mod)]
    out_shape, out_specs = [], []
    n_exp = 0
    if with_router:
        n_exp = router.shape[1]
        router = jnp.pad(router.astype(F32), ((0, 0), (0, LANES - n_exp)))
        in_specs.append(pl.BlockSpec(router.shape, lambda i: (0, 0)))
        args.append(router)
    if has_res:
        out_shape.append(jax.ShapeDtypeStruct((rows, d), F32))
        out_specs.append(row)
    out_shape.append(jax.ShapeDtypeStruct((rows, d), BF16))
    out_specs.append(row)
    if with_router:
        out_shape += [jax.ShapeDtypeStruct((rows, d), F32), jax.ShapeDtypeStruct((rows, LANES), F32)]
        out_specs += [row, pl.BlockSpec((tr, LANES), lambda i: (i, 0))]
    kern = functools.partial(_resnorm_kernel, has_res=has_res, with_router=with_router, n_exp=n_exp)
    return pl.pallas_call(
        kern, grid=(rows // tr,), in_specs=in_specs, out_specs=out_specs, out_shape=out_shape,
        compiler_params=_cp(("arbitrary",), VMEM_LIMIT), name="resnorm",
    )(*args)


def _mm_kernel(*refs, n_x):
    w_ref, o_ref = refs[n_x], refs[n_x + 1]
    if n_x == 1:
        x = refs[0][...]
    else:
        x = jnp.concatenate([r[...] for r in refs[:n_x]], axis=1)
    o_ref[...] = jnp.dot(x, w_ref[...], preferred_element_type=F32).astype(o_ref.dtype)


def _mm(xs, w, *, rows, tm, tn, out_dtype, name):
    n = w.shape[1]
    in_specs = [pl.BlockSpec((tm, a.shape[1]), lambda i, j: (i, 0)) for a in xs]
    in_specs.append(pl.BlockSpec((w.shape[0], tn), lambda i, j: (0, j)))
    return pl.pallas_call(
        functools.partial(_mm_kernel, n_x=len(xs)),
        grid=(rows // tm, n // tn),
        in_specs=in_specs,
        out_specs=pl.BlockSpec((tm, tn), lambda i, j: (i, j)),
        out_shape=jax.ShapeDtypeStruct((rows, n), out_dtype),
        compiler_params=_cp(("arbitrary", "arbitrary"), VMEM_LIMIT), name=name,
    )(*xs, w)


def _group_meansq(x, g_ref, width):
    x2 = x * x
    hi = x2.astype(BF16)
    lo = (x2 - hi.astype(F32)).astype(BF16)
    g = g_ref[:width, :width]
    ss = jnp.dot(hi, g, preferred_element_type=F32) + jnp.dot(lo, g, preferred_element_type=F32)
    return ss * (1.0 / HEAD_DIM)


def _prep_kernel(p_ref, ch_ref, sh_ref, g_ref, gains_ref, o_ref):
    tr = p_ref.shape[0]

    def seg(off, width):
        return p_ref[:, off:off + width].astype(F32)

    def norm(x, row, width):
        return x * lax.rsqrt(_group_meansq(x, g_ref, width) + EPS) * gains_ref[row:row + 1, :width]

    def rope(x, width):
        lane = lax.broadcasted_iota(jnp.int32, (tr, width), 1)
        first = (lane % HEAD_DIM) < (HEAD_DIM // 2)
        cos = jnp.concatenate([ch_ref[...]] * (width // LANES), axis=1)
        sin = jnp.concatenate([sh_ref[...]] * (width // LANES), axis=1)
        swapped = jnp.where(first, pltpu.roll(x, width - HEAD_DIM // 2, 1), pltpu.roll(x, HEAD_DIM // 2, 1))
        return x * cos + swapped * sin

    scale = HEAD_DIM ** -0.5
    o_ref[:, P_AQ:P_AQ + 512] = (rope(norm(seg(P_AQ, 512), 0, 512), 512) * scale).astype(BF16)
    o_ref[:, P_CQ:P_CQ + 512] = (rope(seg(P_CQ, 512), 512) * scale).astype(BF16)
    o_ref[:, P_CK:P_CK + 512] = rope(seg(P_CK, 512), 512).astype(BF16)
    o_ref[:, P_DQ:P_DQ + 512] = (norm(seg(P_DQ, 512), 2, 512) * scale).astype(BF16)
    o_ref[:, P_DK:P_DK + 512] = norm(seg(P_DK, 512), 3, 512).astype(BF16)
    o_ref[:, P_AK:P_AK + 256] = rope(norm(seg(P_AK, 256), 1, 256), 256).astype(BF16)


def _prep(proj, cos_h, sin_h, gmat, gains, *, rows, seq, n_lat):
    tr = 256
    nlat = seq // tr

    def tab(i):
        return (jnp.where(i * tr < n_lat, i % nlat, nlat), 0)

    return pl.pallas_call(
        _prep_kernel,
        grid=(rows // tr,),
        in_specs=[pl.BlockSpec((tr, PREP_W), lambda i: (i, 0)),
                  pl.BlockSpec((tr, LANES), tab),
                  pl.BlockSpec((tr, LANES), tab),
                  pl.BlockSpec((512, 512), lambda i: (0, 0)),
                  pl.BlockSpec((4, 512), lambda i: (0, 0))],
        out_specs=pl.BlockSpec((tr, PREP_W), lambda i: (i, 0)),
        out_shape=jax.ShapeDtypeStruct((rows, PREP_W), BF16),
        compiler_params=_cp(("arbitrary",), VMEM_LIMIT), name="prep",
    )(proj, cos_h, sin_h, gmat, gains)


def _prepb_kernel(bq_ref, bkv_ref, bkr_ref, cb_ref, sb_ref, wuq_ref, wukv_ref, gains_ref, q_ref, k_ref, v_ref):
    tr = bq_ref.shape[0]
    cq = bq_ref[...].astype(F32)
    cqn = cq * lax.rsqrt(jnp.mean(cq * cq, axis=-1, keepdims=True) + EPS) * gains_ref[0:1, :]
    qup = jnp.dot(cqn.astype(BF16), wuq_ref[...], preferred_element_type=F32)
    ckv = bkv_ref[...].astype(F32)
    ckvn = ckv * lax.rsqrt(jnp.mean(ckv * ckv, axis=-1, keepdims=True) + EPS) * gains_ref[1:2, :LANES]
    kvup = jnp.dot(ckvn.astype(BF16), wukv_ref[...], preferred_element_type=F32)
    kr = bkr_ref[...].astype(F32)
    cb, sb = cb_ref[...], sb_ref[...]
    lane = lax.broadcasted_iota(jnp.int32, (tr, LANES), 1)
    half = B_ROPE // 2
    m1 = (lane >= B_NOPE) & (lane < B_NOPE + half)
    m2 = (lane >= B_NOPE + half) & (lane < B_QK)

    def rope(x):
        swapped = jnp.where(m1, pltpu.roll(x, LANES - half, 1), jnp.where(m2, pltpu.roll(x, half, 1), 0.0))
        return x * cb + swapped * sb

    def headnorm(x, row):
        ms = jnp.sum(x * x, axis=-1, keepdims=True) * (1.0 / B_QK)
        return x * lax.rsqrt(ms + EPS) * gains_ref[row:row + 1, :LANES]

    shift_lane = lane == B_QK
    neg_bound = gains_ref[4:5, :LANES]
    for h in range(B_HEADS):
        sl = slice(h * LANES, (h + 1) * LANES)
        q = rope(headnorm(qup[:, sl], 2)) * MLA_LOGIT_SCALE
        q_ref[:, sl] = jnp.where(shift_lane, 1.0, q).astype(BF16)
        k = rope(headnorm(kvup[:, sl] + kr, 3))
        k_ref[:, sl] = jnp.where(shift_lane, neg_bound, k).astype(BF16)
    ones_blk = jnp.where(lane < B_V, 1.0, 0.0).astype(BF16)
    for g in range(B_HEADS // 2):
        v_ref[:, 2 * g * LANES:(2 * g + 1) * LANES] = kvup[:, (B_HEADS + g) * LANES:(B_HEADS + g + 1) * LANES].astype(BF16)
        v_ref[:, (2 * g + 1) * LANES:(2 * g + 2) * LANES] = ones_blk


def _prepb(proj, cos_b, sin_b, wuq, wukv, gains, *, rows, seq, n_lat):
    tr = 256
    nlat = seq // tr

    def tab(i):
        return (jnp.where(i * tr < n_lat, i % nlat, nlat), 0)

    return pl.pallas_call(
        _prepb_kernel,
        grid=(rows // tr,),
        in_specs=[pl.BlockSpec((tr, B_Q_RANK), lambda i: (i, P_BQ // B_Q_RANK)),
                  pl.BlockSpec((tr, LANES), lambda i: (i, P_BKV // LANES)),
                  pl.BlockSpec((tr, LANES), lambda i: (i, P_BKR // LANES)),
                  pl.BlockSpec((tr, LANES), tab),
                  pl.BlockSpec((tr, LANES), tab),
                  pl.BlockSpec(wuq.shape, lambda i: (0, 0)),
                  pl.BlockSpec(wukv.shape, lambda i: (0, 0)),
                  pl.BlockSpec((8, B_Q_RANK), lambda i: (0, 0))],
        out_specs=[pl.BlockSpec((tr, B_HEADS * LANES), lambda i: (i, 0)),
                   pl.BlockSpec((tr, B_HEADS * LANES), lambda i: (i, 0)),
                   pl.BlockSpec((tr, B_HEADS * LANES), lambda i: (i, 0))],
        out_shape=[jax.ShapeDtypeStruct((rows, B_HEADS * LANES), BF16),
                   jax.ShapeDtypeStruct((rows, B_HEADS * LANES), BF16),
                   jax.ShapeDtypeStruct((rows, B_HEADS * LANES), BF16)],
        compiler_params=_cp(("arbitrary",), VMEM_LIMIT), name="prep_mla",
    )(proj, proj, proj, cos_b, sin_b, wuq, wukv, gains)


def _wina_kernel(sink_ref, q_ref, kp_ref, ko_ref, kn_ref, kc_ref, vp_ref, vo_ref, vn_ref, vc_ref, o_ref,
                 *, nb, seq):
    n = pl.program_id(1)
    blk = A_BLOCK
    n_ctx = kc_ref.shape[0]
    kcat = jnp.concatenate([kp_ref[...], ko_ref[...], kn_ref[...], kc_ref[...]], axis=0)
    vcat = jnp.concatenate([vp_ref[...], vo_ref[...], vn_ref[...], vc_ref[...]], axis=0)
    n_keys = 3 * blk + n_ctx
    qi = lax.broadcasted_iota(jnp.int32, (blk, n_keys), 0)
    kj = lax.broadcasted_iota(jnp.int32, (blk, n_keys), 1)
    band = kj - blk
    kpos = n * blk + band
    valid = ((jnp.abs(qi - band) <= blk) & (kpos >= 0) & (kpos < seq) & (n < nb)) | (kj >= 3 * blk)
    lane = lax.broadcasted_iota(jnp.int32, (blk, LANES), 1)
    low = lane < HEAD_DIM
    for j in range(A_HEADS // 2):
        g = j // (A_HEADS // A_KV_HEADS // 2)
        qc = q_ref[:, j * LANES:(j + 1) * LANES]
        kg = kcat[:, g * LANES:(g + 1) * LANES]
        vg = vcat[:, g * LANES:(g + 1) * LANES]
        outs = []
        for e in range(2):
            qm = jnp.where(low if e == 0 else jnp.logical_not(low), qc, jnp.zeros_like(qc))
            s = jnp.where(valid, _nt(qm, kg), NEG_INF)
            sk = sink_ref[2 * j + e]
            m = jnp.maximum(jnp.max(s, axis=-1, keepdims=True), sk)
            p = jnp.exp(s - m)
            den = jnp.sum(p, axis=-1, keepdims=True) + jnp.exp(sk - m)
            outs.append(jnp.dot(p.astype(BF16), vg, preferred_element_type=F32) / den)
        o_ref[:, j * LANES:(j + 1) * LANES] = jnp.where(low, outs[0], outs[1]).astype(o_ref.dtype)


def _wina(prep, proj, sink, *, n_batch, seq, n_ctx, with_ctx):
    blk = A_BLOCK
    nb = seq // blk
    ncb = n_ctx // blk
    steps = nb + (ncb if with_ctx else 0)
    lat_blocks = n_batch * nb
    rows_out = n_batch * seq + (n_batch * n_ctx if with_ctx else 0)

    def qmap(b, n, s):
        return (jnp.where(n < nb, b * nb + n, lat_blocks + b * ncb + (n - nb)), 0)

    def kmap(delta, col):
        def f(b, n, s):
            return (b * nb + jnp.clip(n + delta, 0, nb - 1), col)
        return f

    def cmap(col):
        def f(b, n, s):
            return (n_batch * seq // n_ctx + b, col)
        return f

    kcol, vcol = P_AK // 256, P_AV // 256
    in_specs = [pl.BlockSpec((blk, 512), qmap)]
    in_specs += [pl.BlockSpec((blk, 256), kmap(dl, kcol)) for dl in (-1, 0, 1)]
    in_specs += [pl.BlockSpec((n_ctx, 256), cmap(kcol))]
    in_specs += [pl.BlockSpec((blk, 256), kmap(dl, vcol)) for dl in (-1, 0, 1)]
    in_specs += [pl.BlockSpec((n_ctx, 256), cmap(vcol))]
    return pl.pallas_call(
        functools.partial(_wina_kernel, nb=nb, seq=seq),
        grid_spec=pltpu.PrefetchScalarGridSpec(
            num_scalar_prefetch=1, grid=(n_batch, steps), in_specs=in_specs,
            out_specs=pl.BlockSpec((blk, 512), qmap)),
        out_shape=jax.ShapeDtypeStruct((rows_out, 512), BF16),
        compiler_params=_cp(("arbitrary", "arbitrary"), VMEM_LIMIT), name="mixer_window",
    )(sink, prep, prep, prep, prep, prep, proj, proj, proj, proj)


def _mla_kernel(safe_ref, q_ref, kl_ref, vl_ref, kc_ref, vc_ref, o_ref, *, nq, tk_fast, tk_exact, hps):
    tq = q_ref.shape[0]
    seq = kl_ref.shape[0]
    is_lat = pl.program_id(2) < nq
    safe = safe_ref[0] > 0
    pair_w = 2 * LANES
    lane = lax.broadcasted_iota(jnp.int32, (tq, LANES), 1)

    def q(h):
        return q_ref[:, h * LANES:(h + 1) * LANES]

    def write(nums, dens):
        for g in range(hps // 2):
            o_ref[:, g * LANES:(g + 1) * LANES] = jnp.where(
                lane < B_V, nums[2 * g] / dens[2 * g], nums[2 * g + 1] / dens[2 * g + 1]).astype(o_ref.dtype)

    def fast(chunks):
        accs = [None] * hps
        for k_ref, v_ref, off, n in chunks:
            for h in range(hps):
                g = h // 2
                p = jnp.exp2(_nt(q(h), k_ref[off:off + n, h * LANES:(h + 1) * LANES])).astype(BF16)
                d = jnp.dot(p, v_ref[off:off + n, g * pair_w:(g + 1) * pair_w], preferred_element_type=F32)
                accs[h] = d if accs[h] is None else accs[h] + d
        write([a[:, :LANES] for a in accs], [a[:, LANES:LANES + 1] for a in accs])

    ctx_chunk = (kc_ref, vc_ref, 0, kc_ref.shape[0])

    @pl.when(safe & is_lat)
    def _():
        fast([(kl_ref, vl_ref, c * tk_fast, tk_fast) for c in range(seq // tk_fast)] + [ctx_chunk])

    @pl.when(safe & jnp.logical_not(is_lat))
    def _():
        fast([ctx_chunk])

    @pl.when(jnp.logical_not(safe))
    def _():
        def step(kf, vf, carry):
            out = []
            for h in range(hps):
                m, l, acc = carry[h]
                s = _nt(q(h), kf(h))
                m_new = jnp.maximum(m, jnp.max(s, axis=-1, keepdims=True))
                a = jnp.exp2(m - m_new)
                p = jnp.exp2(s - m_new)
                l = a * l + jnp.sum(p, axis=-1, keepdims=True)
                acc = a * acc + jnp.dot(p.astype(BF16), vf(h // 2), preferred_element_type=F32)
                out.append((m_new, l, acc))
            return tuple(out)

        def body(c, carry):
            off = pl.multiple_of(c * tk_exact, tk_exact)
            return step(lambda h: kl_ref[pl.ds(off, tk_exact), h * LANES:(h + 1) * LANES],
                        lambda g: vl_ref[pl.ds(off, tk_exact), g * pair_w:g * pair_w + LANES], carry)

        init = tuple((jnp.full((tq, 1), -jnp.inf, F32), jnp.zeros((tq, 1), F32), jnp.zeros((tq, LANES), F32))
                     for _ in range(hps))
        carry = lax.fori_loop(0, jnp.where(is_lat, seq // tk_exact, 0), body, init)
        carry = step(lambda h: kc_ref[:, h * LANES:(h + 1) * LANES],
                     lambda g: vc_ref[:, g * pair_w:g * pair_w + LANES], carry)
        write([c[2] for c in carry], [c[1] for c in carry])


def _mla_logit_bound(q_gain, k_gain):
    b = B_QK * MLA_LOGIT_SCALE * jnp.max(jnp.abs(q_gain)) * jnp.max(jnp.abs(k_gain))
    return (1.02 * b + 0.5).astype(F32)


def _mla(qb, kb, vb, safe, *, n_batch, seq, n_ctx, with_ctx):
    hps = 4
    tq = 256
    assert n_ctx == tq
    nq = seq // tq
    ctx_blk = n_batch * seq // n_ctx
    rows_out = n_batch * seq + (n_batch * n_ctx if with_ctx else 0)
    kw, vw = hps * LANES, hps * B_V

    def qmap(b, j, i, s):
        return (jnp.where(i < nq, b * nq + i, ctx_blk + b), j)

    return pl.pallas_call(
        functools.partial(_mla_kernel, nq=nq, tk_fast=min(2048, seq), tk_exact=512, hps=hps),
        grid_spec=pltpu.PrefetchScalarGridSpec(
            num_scalar_prefetch=1,
            grid=(n_batch, B_HEADS // hps, nq + (1 if with_ctx else 0)),
            in_specs=[pl.BlockSpec((tq, kw), qmap),
                      pl.BlockSpec((seq, kw), lambda b, j, i, s: (b, j)),
                      pl.BlockSpec((seq, kw), lambda b, j, i, s: (b, j)),
                      pl.BlockSpec((n_ctx, kw), lambda b, j, i, s: (ctx_blk + b, j)),
                      pl.BlockSpec((n_ctx, kw), lambda b, j, i, s: (ctx_blk + b, j))],
            out_specs=pl.BlockSpec((tq, vw), qmap)),
        out_shape=jax.ShapeDtypeStruct((rows_out, 512), BF16),
        compiler_params=_cp(("arbitrary", "arbitrary", "arbitrary"), VMEM_LIMIT), name="mixer_mla",
    )(safe, qb, kb, vb, kb, vb)


def _ret_kernel(q_ref, k_ref, v_ref, ld_ref, o_ref, s_ref):
    d = pl.program_id(1)
    step = pl.program_id(2)
    cc = C_CHUNK

    @pl.when(step == 0)
    def _():
        s_ref[...] = jnp.zeros_like(s_ref)

    fwd = d == 0
    ri = lax.broadcasted_iota(jnp.int32, (cc, cc), 0)
    ci = lax.broadcasted_iota(jnp.int32, (cc, cc), 1)
    diff = jnp.where(fwd, ri - ci, ci - ri)
    dpos = jnp.maximum(diff, 0).astype(F32)
    pos = ri.astype(F32)
    qpow = jnp.where(fwd, pos + 1.0, cc - pos)
    kpow = jnp.where(fwd, cc - 1.0 - pos, pos)
    low = ci < HEAD_DIM
    blockdiag = (ri < HEAD_DIM) == low
    for j in range(C_HEADS // 2):
        sl = slice(j * LANES, (j + 1) * LANES)
        q, k, v = q_ref[:, sl], k_ref[:, sl], v_ref[:, sl]
        lg_e = -jnp.exp(ld_ref[0, 2 * j:2 * j + 1, :])
        lg_o = -jnp.exp(ld_ref[0, 2 * j + 1:2 * j + 2, :])
        lgl = jnp.where(low[0:1, :], lg_e, lg_o)
        outs = []
        for e, lg in ((0, lg_e), (1, lg_o)):
            qm = jnp.where(low if e == 0 else jnp.logical_not(low), q, jnp.zeros_like(q))
            dec = jnp.where(diff >= 0, jnp.exp(lg * dpos), 0.0)
            sc = _nt(qm, k) * dec
            outs.append(jnp.dot(sc.astype(BF16), v, preferred_element_type=F32))
        o_intra = jnp.where(low, outs[0], outs[1])
        state = s_ref[j]
        qw = (q.astype(F32) * jnp.exp(lgl * qpow)).astype(BF16)
        o_cross = jnp.dot(qw, state.astype(BF16), preferred_element_type=F32)
        kw = (k.astype(F32) * jnp.exp(lgl * kpow)).astype(BF16)
        kv = lax.dot_general(kw, v, (((0,), (0,)), ((), ())), preferred_element_type=F32)
        cdec = jnp.where(ri < HEAD_DIM, jnp.exp(lg_e * cc), jnp.exp(lg_o * cc))
        s_ref[j] = state * cdec + jnp.where(blockdiag, kv, 0.0)
        o_ref[0, :, sl] = o_intra + o_cross


def _retention(prep, proj, ld_head, *, n_batch, seq, n_ctx):
    cc = C_CHUNK
    nl, nc = seq // cc, n_ctx // cc
    steps = nc + nl
    lat_blocks = n_batch * nl
    rows = n_batch * (seq + n_ctx)

    def rowblk(b, d, s):
        c_ctx = jnp.where(d == 0, s, nc - 1 - s)
        c_lat = jnp.where(d == 0, s - nc, nl - 1 - (s - nc))
        return jnp.where(s < nc, lat_blocks + b * nc + c_ctx, b * nl + c_lat)

    def spec(col):
        return pl.BlockSpec((cc, 512), lambda b, d, s: (rowblk(b, d, s), col))

    return pl.pallas_call(
        _ret_kernel,
        grid=(n_batch, 2, steps),
        in_specs=[spec(P_CQ // 512), spec(P_CK // 512), spec(P_CV // 512),
                  pl.BlockSpec((1, C_HEADS, LANES), lambda b, d, s: (d, 0, 0))],
        out_specs=pl.BlockSpec((1, cc, 512), lambda b, d, s: (d, rowblk(b, d, s), 0)),
        out_shape=jax.ShapeDtypeStruct((2, rows, 512), F32),
        scratch_shapes=[pltpu.VMEM((C_HEADS // 2, LANES, LANES), F32)],
        compiler_params=_cp(("arbitrary", "arbitrary", "arbitrary"), VMEM_LIMIT), name="mixer_retention",
    )(prep, prep, proj, ld_head)


def _retfin_kernel(o_ref, g_ref, gm_ref, out_ref):
    o = o_ref[0] + o_ref[1]
    on = o * lax.rsqrt(_group_meansq(o, gm_ref, 512) + EPS)
    g = g_ref[...].astype(F32)
    out_ref[...] = (g * jax.nn.sigmoid(g) * on).astype(out_ref.dtype)


def _retention_finish(o2, proj, gmat, *, rows):
    tr = 256
    return pl.pallas_call(
        _retfin_kernel,
        grid=(rows // tr,),
        in_specs=[pl.BlockSpec((2, tr, 512), lambda i: (0, i, 0)),
                  pl.BlockSpec((tr, 512), lambda i: (i, P_CG // 512)),
                  pl.BlockSpec((512, 512), lambda i: (0, 0))],
        out_specs=pl.BlockSpec((tr, 512), lambda i: (i, 0)),
        out_shape=jax.ShapeDtypeStruct((rows, 512), BF16),
        compiler_params=_cp(("arbitrary",), VMEM_LIMIT), name="retention_finish",
    )(o2, proj, gmat)


def _nbr_kernel(q_ref, k_ref, v_ref, kc_ref, vc_ref, abm_ref, o_ref, *, rows):
    st = pl.program_id(1)
    start = jnp.clip(NA_QROWS * st - NA_ROWS // 2, 0, rows - NA_KROWS)
    off = pl.multiple_of(start * GRID_W, GRID_W)
    nk = NA_KROWS * GRID_W
    tq = q_ref.shape[0]
    lane = lax.broadcasted_iota(jnp.int32, (tq, LANES), 1)
    low = lane < HEAD_DIM
    for j in range(D_HEADS // 2):
        sl = slice(j * LANES, (j + 1) * LANES)
        qc = q_ref[:, sl]
        kw = k_ref[pl.ds(off, nk), sl]
        vw = v_ref[pl.ds(off, nk), sl]
        kc, vc = kc_ref[:, sl], vc_ref[:, sl]
        outs = []
        for e in range(2):
            qm = jnp.where(low if e == 0 else jnp.logical_not(low), qc, jnp.zeros_like(qc))
            s1 = _nt(qm, kw) + abm_ref[0, 2 * j + e]
            s2 = _nt(qm, kc)
            m = jnp.maximum(jnp.max(s1, axis=-1, keepdims=True), jnp.max(s2, axis=-1, keepdims=True))
            p1 = jnp.exp(s1 - m)
            p2 = jnp.exp(s2 - m)
            den = jnp.sum(p1, axis=-1, keepdims=True) + jnp.sum(p2, axis=-1, keepdims=True)
            o = (jnp.dot(p1.astype(BF16), vw, preferred_element_type=F32)
                 + jnp.dot(p2.astype(BF16), vc, preferred_element_type=F32))
            outs.append(o / den)
        o_ref[:, sl] = jnp.where(low, outs[0], outs[1]).astype(o_ref.dtype)


def _nbr(prep, proj, abm, *, n_batch, seq, n_ctx, with_ctx):
    rows = seq // GRID_W
    tq = NA_QROWS * GRID_W
    assert n_ctx == tq and rows % NA_QROWS == 0 and rows >= NA_KROWS + NA_QROWS
    nst = rows // NA_QROWS
    steps = nst + (1 if with_ctx else 0)
    rows_out = n_batch * seq + (n_batch * n_ctx if with_ctx else 0)

    def qmap(col):
        def f(b, s):
            return (jnp.where(s < nst, b * nst + s, n_batch * nst + b), col)
        return f

    def case(b, s):
        c = jnp.where(s == 0, 0, jnp.where(s == nst - 1, 2, jnp.where(s == nst, 3, 1)))
        return (c, 0, 0, 0)

    ctx_blk = n_batch * seq // n_ctx
    return pl.pallas_call(
        functools.partial(_nbr_kernel, rows=rows),
        grid=(n_batch, steps),
        in_specs=[pl.BlockSpec((tq, 512), qmap(P_DQ // 512)),
                  pl.BlockSpec((seq, 512), lambda b, s: (b, P_DK // 512)),
                  pl.BlockSpec((seq, 512), lambda b, s: (b, P_DV // 512)),
                  pl.BlockSpec((n_ctx, 512), lambda b, s: (ctx_blk + b, P_DK // 512)),
                  pl.BlockSpec((n_ctx, 512), lambda b, s: (ctx_blk + b, P_DV // 512)),
                  pl.BlockSpec((1, D_HEADS, tq, NA_KROWS * GRID_W), case)],
        out_specs=pl.BlockSpec((tq, 512), qmap(0)),
        out_shape=jax.ShapeDtypeStruct((rows_out, 512), BF16),
        compiler_params=_cp(("arbitrary", "arbitrary"), VMEM_LIMIT), name="mixer_neighbourhood",
    )(prep, prep, proj, prep, proj, abm)


def _nbr_bias_tables(rpb, rows):
    w = GRID_W
    cidx = np.arange(w)
    col_start = np.clip(cidx - NA_COLS // 2, 0, w - NA_COLS)
    col_ok = (cidx[None, :] >= col_start[:, None]) & (cidx[None, :] < col_start[:, None] + NA_COLS)
    d_col = np.clip(cidx[None, :] - cidx[:, None] + (NA_COLS - 1), 0, 2 * NA_COLS - 2)
    n_dr = 2 * NA_ROWS - 1
    t = jnp.take(rpb.astype(F32), jnp.asarray(d_col.reshape(-1)), axis=2)
    t = t.reshape(rpb.shape[0], n_dr, w, w).transpose(0, 2, 1, 3)
    t = jnp.where(jnp.asarray(col_ok)[None, :, None, :], t, NEG_INF)
    t = jnp.concatenate([t, jnp.full((rpb.shape[0], w, 1, w), NEG_INF, F32)], axis=2)
    idx = np.full((NA_CASES, NA_QROWS, NA_KROWS), n_dr, np.int32)
    for c, r0 in enumerate((0, NA_QROWS, rows - NA_QROWS)):
        start = int(np.clip(r0 - NA_ROWS // 2, 0, rows - NA_KROWS))
        for i in range(NA_QROWS):
            r = r0 + i
            row_start = int(np.clip(r - NA_ROWS // 2, 0, rows - NA_ROWS))
            for kk in range(NA_KROWS):
                kr = start + kk
                if row_start <= kr < row_start + NA_ROWS:
                    idx[c, i, kk] = kr - r + (NA_ROWS - 1)
    out = jnp.take(t, jnp.asarray(idx.reshape(-1)), axis=2)
    out = out.reshape(rpb.shape[0], w, NA_CASES, NA_QROWS, NA_KROWS, w).transpose(2, 0, 3, 1, 4, 5)
    return out.reshape(NA_CASES, rpb.shape[0], NA_QROWS * w, NA_KROWS * w)


def _ffn_kernel(te_ref, tv_ref, x_ref, wg_ref, wu_ref, wd_ref, o_ref, acc_ref):
    i, j = pl.program_id(0), pl.program_id(1)
    last = pl.num_programs(1) - 1
    valid = tv_ref[i] > 0

    @pl.when(j == 0)
    def _():
        acc_ref[...] = jnp.zeros_like(acc_ref)

    @pl.when(valid)
    def _():
        x = x_ref[...]
        g = jnp.dot(x, wg_ref[0], preferred_element_type=F32)
        u = jnp.dot(x, wu_ref[0], preferred_element_type=F32)
        a = (g * jax.nn.sigmoid(g) * u).astype(BF16)
        acc_ref[...] += jnp.dot(a, wd_ref[0], preferred_element_type=F32)

    @pl.when(j == last)
    def _():
        o_ref[...] = acc_ref[...].astype(o_ref.dtype)


def _ffn(x, w_gate_up, w_down, tile_expert, tile_valid, *, rows, tm, fc, out_dtype):
    n_exp, d, ff2 = w_gate_up.shape
    ff = ff2 // 2
    nj = ff // fc

    def jeff(i, j, tv):
        return jnp.where(tv[i] > 0, j, nj - 1)

    return pl.pallas_call(
        _ffn_kernel,
        grid_spec=pltpu.PrefetchScalarGridSpec(
            num_scalar_prefetch=2, grid=(rows // tm, nj),
            in_specs=[pl.BlockSpec((tm, d), lambda i, j, te, tv: (i, 0)),
                      pl.BlockSpec((1, d, fc), lambda i, j, te, tv: (te[i], 0, jeff(i, j, tv))),
                      pl.BlockSpec((1, d, fc), lambda i, j, te, tv: (te[i], 0, nj + jeff(i, j, tv))),
                      pl.BlockSpec((1, fc, d), lambda i, j, te, tv: (te[i], jeff(i, j, tv), 0))],
            out_specs=pl.BlockSpec((tm, d), lambda i, j, te, tv: (i, 0)),
            scratch_shapes=[pltpu.VMEM((tm, d), F32)]),
        out_shape=jax.ShapeDtypeStruct((rows, d), out_dtype),
        compiler_params=_cp(("arbitrary", "arbitrary"), VMEM_LIMIT), name="swiglu",
    )(tile_expert, tile_valid, x, w_gate_up, w_gate_up, w_down)


def _gather_kernel(idx_ref, src_ref, o_ref, buf_ref, sem):
    base = pl.program_id(0) * buf_ref.shape[0]
    n = buf_ref.shape[0]

    def copy(r):
        return pltpu.make_async_copy(src_ref.at[pl.ds(idx_ref[base + r], 1), :], buf_ref.at[pl.ds(r, 1), :], sem)

    def issue(r, c):
        copy(r).start()
        return c

    def wait(r, c):
        copy(r).wait()
        return c

    lax.fori_loop(0, n, issue, 0)
    lax.fori_loop(0, n, wait, 0)
    o_ref[...] = buf_ref[...].astype(o_ref.dtype)


def _gather_rows(src, idx, *, out_dtype):
    n_out = idx.shape[0]
    d = src.shape[1]
    tr = 256
    return pl.pallas_call(
        _gather_kernel,
        grid_spec=pltpu.PrefetchScalarGridSpec(
            num_scalar_prefetch=1, grid=(n_out // tr,),
            in_specs=[pl.BlockSpec(memory_space=pl.ANY)],
            out_specs=pl.BlockSpec((tr, d), lambda i, idx: (i, 0)),
            scratch_shapes=[pltpu.VMEM((tr, d), src.dtype), pltpu.SemaphoreType.DMA(())]),
        out_shape=jax.ShapeDtypeStruct((n_out, d), out_dtype),
        compiler_params=_cp(("arbitrary",), VMEM_LIMIT), name="moe_dispatch",
    )(idx, src)


def _combine_kernel(p0_ref, p1_ref, y_ref, x_ref, route_ref, gate_ref, o_ref, b0_ref, b1_ref, sem):
    n = b0_ref.shape[0]
    base = pl.program_id(0) * n

    def copy0(r):
        return pltpu.make_async_copy(y_ref.at[pl.ds(p0_ref[base + r], 1), :], b0_ref.at[pl.ds(r, 1), :], sem.at[0])

    def copy1(r):
        return pltpu.make_async_copy(y_ref.at[pl.ds(p1_ref[base + r], 1), :], b1_ref.at[pl.ds(r, 1), :], sem.at[1])

    def issue(r, c):
        copy0(r).start()
        copy1(r).start()
        return c

    def wait(r, c):
        copy0(r).wait()
        copy1(r).wait()
        return c

    lax.fori_loop(0, n, issue, 0)
    lax.fori_loop(0, n, wait, 0)
    route = route_ref[...]
    moe = route[:, 2:3] * b0_ref[...] + route[:, 3:4] * b1_ref[...]
    o_ref[...] = x_ref[...] + gate_ref[0, 0] * moe


def _combine(y, x, route, mod, gate_k, pos0, pos1, *, seq, n_batch):
    t, d = x.shape
    tr = 256
    return pl.pallas_call(
        _combine_kernel,
        grid_spec=pltpu.PrefetchScalarGridSpec(
            num_scalar_prefetch=2, grid=(t // tr,),
            in_specs=[pl.BlockSpec(memory_space=pl.ANY),
                      pl.BlockSpec((tr, d), lambda i, a, b: (i, 0)),
                      pl.BlockSpec((tr, LANES), lambda i, a, b: (i, 0)),
                      pl.BlockSpec((1, 1, 1, d),
                                   lambda i, a, b: (jnp.minimum((i * tr) // seq, n_batch), gate_k, 0, 0))],
            out_specs=pl.BlockSpec((tr, d), lambda i, a, b: (i, 0)),
            scratch_shapes=[pltpu.VMEM((tr, d), F32), pltpu.VMEM((tr, d), F32), pltpu.SemaphoreType.DMA((2,))]),
        out_shape=jax.ShapeDtypeStruct((t, d), F32),
        compiler_params=_cp(("arbitrary",), VMEM_LIMIT), name="moe_combine",
    )(pos0, pos1, y, x, route, _mod_view(mod))


def _route_meta(idx, n_exp, tm):
    t = idx.shape[0]
    flat = idx.reshape(-1)
    onehot = (flat[:, None] == jnp.arange(n_exp, dtype=jnp.int32)[None, :]).astype(jnp.int32)
    counts = jnp.sum(onehot, axis=0)
    rank = jnp.take_along_axis(jnp.cumsum(onehot, axis=0) - onehot, flat[:, None], axis=1)[:, 0]
    padded = ((counts + tm - 1) // tm) * tm
    ends = jnp.cumsum(padded)
    starts = ends - padded
    pos = starts[flat] + rank
    n_rows = TOP_K * t + n_exp * tm
    row_token = jnp.zeros((n_rows,), jnp.int32).at[pos].set(jnp.arange(TOP_K * t, dtype=jnp.int32) // TOP_K)
    tile_start = jnp.arange(n_rows // tm, dtype=jnp.int32) * tm
    tile_valid = (tile_start < ends[-1]).astype(jnp.int32)
    tile_expert = jnp.sum((ends[None, :] <= tile_start[:, None]).astype(jnp.int32), axis=1)
    tile_expert = jnp.minimum(tile_expert, n_exp - 1)
    last_valid = jnp.maximum(ends[-1] // tm - 1, 0)
    tile_expert = jnp.where(tile_valid > 0, tile_expert, tile_expert[last_valid])
    pos = pos.reshape(t, TOP_K).astype(jnp.int32)
    return row_token, tile_expert, tile_valid, pos[:, 0], pos[:, 1]


def _rope_tables(seq, pad_rows):
    t = jnp.arange(seq, dtype=jnp.int32)
    row = (t // GRID_W).astype(F32)
    col = (t % GRID_W).astype(F32)

    def angles(rot_dim):
        n_freq = rot_dim // 4
        freqs = ROPE_THETA ** (-jnp.arange(n_freq, dtype=F32) / n_freq)
        ang = jnp.concatenate([row[:, None] * freqs[None, :], col[:, None] * freqs[None, :]], axis=-1)
        return jnp.cos(ang), jnp.sin(ang)

    ch, sh = angles(HEAD_DIM)
    cos_h = jnp.concatenate([ch] * 4, axis=1)
    sin_h = jnp.concatenate([-sh, sh, -sh, sh], axis=1)
    cb, sb = angles(B_ROPE)
    one = jnp.ones((seq, B_NOPE), F32)
    zero = jnp.zeros((seq, B_NOPE), F32)
    tail1 = jnp.ones((seq, LANES - B_QK), F32)
    tail0 = jnp.zeros((seq, LANES - B_QK), F32)
    cos_b = jnp.concatenate([one, cb, cb, tail1], axis=1)
    sin_b = jnp.concatenate([zero, -sb, sb, tail0], axis=1)

    def pad(tab, fill):
        return jnp.concatenate([tab, jnp.full((pad_rows, LANES), fill, F32)], axis=0)

    return pad(cos_h, 1.0), pad(sin_h, 0.0), pad(cos_b, 1.0), pad(sin_b, 0.0)


def _permute_w_in(w):
    d = w.shape[0]
    o = np.cumsum([0, 512, 128, 128, B_Q_RANK, B_KV_RANK, B_ROPE, 512, 512, 512, 512, 512, 512, 512])
    aq, ak, av, bq, bkv, bkr, cq, ck, cv, cg, dq, dk, dv = [w[:, o[i]:o[i + 1]] for i in range(13)]

    def dup(m):
        return jnp.concatenate([m[:, :64], m[:, :64], m[:, 64:], m[:, 64:]], axis=1)

    z = lambda n: jnp.zeros((d, n), w.dtype)
    parts = [aq, cq, ck, dq, dk, dup(ak), dup(av), cv, cg, dv, bq, bkv,
             z(B_NOPE), bkr, z(LANES - B_QK), z(P_END - P_BKR - LANES)]
    return jnp.concatenate(parts, axis=1).astype(BF16)


def _block_diag_ones():
    i = np.arange(512) // HEAD_DIM
    return jnp.asarray((i[:, None] == i[None, :]).astype(np.float32), dtype=BF16)


def _pad_lanes(v, width):
    return jnp.concatenate([v.astype(F32), jnp.zeros((width - v.shape[0],), F32)])


def kernel(x, c, ctx, c_ctx, w_ada, b_ada, norm_mix, norm_ffn, w_in, w_out, a_q_norm, a_k_norm, a_sink,
           b_q_a_norm, b_kv_a_norm, b_w_uq, b_w_ukv, b_q_norm, b_k_norm, c_log_decay, d_q_norm, d_k_norm,
           d_rpb, ffn_w_gate_up, ffn_w_down, moe_router, moe_w_gate_up, moe_w_down):
    n_batch, seq, d = x.shape
    n_ctx = ctx.shape[1]
    depth = w_ada.shape[0]
    n_lat = n_batch * seq
    n_all = n_lat + n_batch * n_ctx
    grid_rows = seq // GRID_W
    tm = next(t for t in (1024, 512, 256) if n_lat % t == 0 and n_all % t == 0)

    cc = jnp.concatenate([c, c_ctx[None, :], jnp.zeros((8 - n_batch - 1, d), F32)], axis=0)
    mod = _ada(cc, w_ada, b_ada)
    xs = jnp.concatenate([x.reshape(n_lat, d), ctx.reshape(n_batch * n_ctx, d)], axis=0)
    cos_h, sin_h, cos_b, sin_b = _rope_tables(seq, 256)
    gmat = _block_diag_ones()

    y = None
    y_gate, y_mod = 0, None
    for l in range(depth):
        last = l == depth - 1
        mod_l = mod[l]
        if y is None:
            (h,) = _resnorm(xs, None, None, 0, mod_l, norm_mix[l], 0, 1, rows=n_all, seq=seq, n_batch=n_batch)
        else:
            xs, h = _resnorm(xs, y, y_mod, y_gate, mod_l, norm_mix[l], 0, 1, rows=n_all, seq=seq,
                             n_batch=n_batch)
        proj = _mm([h], _permute_w_in(w_in[l]), rows=n_all, tm=tm, tn=768, out_dtype=BF16, name="proj_in")
        gains = jnp.stack([jnp.tile(a_q_norm[l], 8), _pad_lanes(jnp.tile(a_k_norm[l], 4), 512),
                           jnp.tile(d_q_norm[l], 8), jnp.tile(d_k_norm[l], 8)]).astype(F32)
        prep = _prep(proj, cos_h, sin_h, gmat, gains, rows=n_all, seq=seq, n_lat=n_lat)
        wuq = jnp.pad(b_w_uq[l].reshape(B_Q_RANK, B_HEADS, B_QK),
                      ((0, 0), (0, 0), (0, LANES - B_QK))).reshape(B_Q_RANK, B_HEADS * LANES).astype(BF16)
        wukv3 = b_w_ukv[l].reshape(B_KV_RANK, B_HEADS, B_NOPE + B_V)
        wukv = jnp.concatenate(
            [jnp.pad(wukv3[:, :, :B_NOPE], ((0, 0), (0, 0), (0, LANES - B_NOPE))).reshape(B_KV_RANK, -1),
             wukv3[:, :, B_NOPE:].reshape(B_KV_RANK, -1)], axis=1).astype(BF16)
        bound = _mla_logit_bound(b_q_norm[l], b_k_norm[l])
        mla_safe = (bound <= MLA_SAFE_BOUND).astype(jnp.int32).reshape(1)
        gains_b = jnp.stack([b_q_a_norm[l].astype(F32), _pad_lanes(b_kv_a_norm[l], B_Q_RANK),
                             _pad_lanes(b_q_norm[l], B_Q_RANK), _pad_lanes(b_k_norm[l], B_Q_RANK),
                             jnp.full((B_Q_RANK,), -1.0, F32) * bound] + [jnp.zeros((B_Q_RANK,), F32)] * 3)
        qb, kb, vb = _prepb(proj, cos_b, sin_b, wuq, wukv, gains_b, rows=n_all, seq=seq, n_lat=n_lat)

        oa = _wina(prep, proj, a_sink[l].astype(F32), n_batch=n_batch, seq=seq, n_ctx=n_ctx, with_ctx=not last)
        ob = _mla(qb, kb, vb, mla_safe, n_batch=n_batch, seq=seq, n_ctx=n_ctx, with_ctx=not last)
        ld_head = jnp.broadcast_to(c_log_decay[l].astype(F32)[:, :, None], (2, C_HEADS, LANES))
        oc2 = _retention(prep, proj, ld_head, n_batch=n_batch, seq=seq, n_ctx=n_ctx)
        abm = _nbr_bias_tables(d_rpb[l], grid_rows)
        od = _nbr(prep, proj, abm, n_batch=n_batch, seq=seq, n_ctx=n_ctx, with_ctx=not last)
        rows_l = n_lat if last else n_all
        oc = _retention_finish(oc2, proj, gmat, rows=rows_l)
        ymix = _mm([oa, ob, oc, od], w_out[l].astype(BF16), rows=rows_l, tm=tm, tn=1024, out_dtype=BF16,
                   name="proj_out")
        i = l // 2
        if l % 2 == 0:
            xs, h2 = _resnorm(xs, ymix, mod_l, 2, mod_l, norm_ffn[l], 3, 4, rows=rows_l, seq=seq,
                              n_batch=n_batch)
            n_tiles = rows_l // tm
            y = _ffn(h2, ffn_w_gate_up[i:i + 1].astype(BF16), ffn_w_down[i:i + 1].astype(BF16),
                     jnp.zeros((n_tiles,), jnp.int32), jnp.ones((n_tiles,), jnp.int32),
                     rows=rows_l, tm=tm, fc=512, out_dtype=BF16)
            y_gate, y_mod = 5, mod_l
        else:
            xs, h2, h2f, route = _resnorm(xs, ymix, mod_l, 2, mod_l, norm_ffn[l], 3, 4, rows=rows_l, seq=seq,
                                          n_batch=n_batch, router=moe_router[i])
            top_idx = route[:, :TOP_K].astype(jnp.int32)
            row_token, tile_expert, tile_valid, pos0, pos1 = _route_meta(top_idx, moe_router.shape[2], tm)
            xg = _gather_rows(h2f, row_token, out_dtype=BF16)
            yg = _ffn(xg, moe_w_gate_up[i].astype(BF16), moe_w_down[i].astype(BF16), tile_expert, tile_valid,
                      rows=xg.shape[0], tm=tm, fc=512, out_dtype=F32)
            xs = _combine(yg, xs, route, mod_l, 5, pos0, pos1, seq=seq, n_batch=n_batch)
            y = None
    if y is not None:
        xs, _ = _resnorm(xs, y, y_mod, y_gate, y_mod, norm_ffn[depth - 1], 3, 4, rows=xs.shape[0], seq=seq,
                         n_batch=n_batch)
    return xs[:n_lat].reshape(n_batch, seq, d)
```

```python
import functools

import numpy as np
import jax
import jax.numpy as jnp
from jax import lax
from jax.experimental import pallas as pl
from jax.experimental.pallas import tpu as pltpu

F32 = jnp.float32
BF16 = jnp.bfloat16

GRID_W = 64
HEAD_DIM = 64
ROPE_THETA = 10000.0
EPS = 1e-6
NEG_INF = -1e30

A_HEADS = 8
A_KV_HEADS = 2
A_BLOCK = 128
B_HEADS = 8
B_Q_RANK = 384
B_KV_RANK = 128
B_NOPE = 64
B_ROPE = 32
B_QK = B_NOPE + B_ROPE
B_V = 64
C_HEADS = 8
C_CHUNK = 128
D_HEADS = 8
NA_ROWS = 8
NA_COLS = 16
TOP_K = 2

LANES = 128
NA_QROWS = 4
NA_KROWS = NA_ROWS + NA_QROWS
NA_CASES = 4

P_AQ, P_CQ, P_CK, P_DQ, P_DK, P_AK, P_AV, P_CV, P_CG, P_DV, P_BQ, P_BKV, P_BKR, P_END = (
    0, 512, 1024, 1536, 2048, 2560, 2816, 3072, 3584, 4096, 4608, 4992, 5120, 5376)
PREP_W = P_AV

VMEM_LIMIT = 56 * 1024 * 1024

MLA_LOGIT_SCALE = B_QK ** -0.5 * float(np.log2(np.e))
MLA_SAFE_BOUND = 40.0


def _cp(sem, vmem=None):
    return pltpu.CompilerParams(dimension_semantics=sem, vmem_limit_bytes=vmem)


def _nt(a, b):
    return lax.dot_general(a, b, (((1,), (1,)), ((), ())), preferred_element_type=F32)


_HI_HALF = 0xFFFF0000


def _pack_halves(x):
    n = x.shape[1] // 2
    bits = pltpu.bitcast(x.astype(BF16).astype(F32), jnp.uint32)
    return (bits[:, :n] >> 16) | (bits[:, n:] & jnp.uint32(_HI_HALF))


def _unpack_halves(w):
    return pltpu.bitcast(w << 16, F32), pltpu.bitcast(w & jnp.uint32(_HI_HALF), F32)


def _ada_kernel(c_ref, w_ref, b_ref, o_ref):
    c = c_ref[...]
    s = (c * jax.nn.sigmoid(c)).astype(BF16)
    o_ref[0] = jnp.dot(s, w_ref[0].astype(BF16), preferred_element_type=F32) + b_ref[0]


def _ada(cc, w_ada, b_ada):
    depth, d, n = w_ada.shape
    tn = 1024
    return pl.pallas_call(
        _ada_kernel,
        grid=(depth, n // tn),
        in_specs=[pl.BlockSpec((8, d), lambda l, j: (0, 0)),
                  pl.BlockSpec((1, d, tn), lambda l, j: (l, 0, j)),
                  pl.BlockSpec((1, 1, tn), lambda l, j: (l, 0, j))],
        out_specs=pl.BlockSpec((1, 8, tn), lambda l, j: (l, 0, j)),
        out_shape=jax.ShapeDtypeStruct((depth, 8, n), F32),
        compiler_params=_cp(("arbitrary", "arbitrary"), VMEM_LIMIT),
        name="ada",
    )(cc, w_ada, b_ada.reshape(depth, 1, n))


def _resnorm_kernel(*refs, has_res, with_router, n_exp):
    refs = list(refs)
    x_ref = refs.pop(0)
    if has_res:
        y_ref = refs.pop(0)
        gate_ref = refs.pop(0)
    gain_ref, sh_ref, sc_ref = refs[:3]
    refs = refs[3:]
    if with_router:
        router_ref = refs.pop(0)
    if has_res:
        xo_ref = refs.pop(0)
    h_ref = refs.pop(0)
    x = x_ref[...]
    if has_res:
        x = x + gate_ref[0, 0] * y_ref[...].astype(F32)
        xo_ref[...] = x
    ms = jnp.mean(x * x, axis=-1, keepdims=True)
    h = (x * lax.rsqrt(ms + EPS) * gain_ref[...]) * (1.0 + sc_ref[0, 0]) + sh_ref[0, 0]
    if with_router:
        (route_ref,) = refs
        h_ref[...] = _pack_halves(h)
        logits = jnp.dot(h, router_ref[...], preferred_element_type=F32, precision=lax.Precision.HIGHEST)
        lane = lax.broadcasted_iota(jnp.int32, logits.shape, 1)
        logits = jnp.where(lane < n_exp, logits, -jnp.inf)
        m1 = jnp.max(logits, axis=-1, keepdims=True)
        i1 = jnp.min(jnp.where(logits == m1, lane, LANES), axis=-1, keepdims=True)
        rest = jnp.where(lane == i1, -jnp.inf, logits)
        m2 = jnp.max(rest, axis=-1, keepdims=True)
        i2 = jnp.min(jnp.where(rest == m2, lane, LANES), axis=-1, keepdims=True)
        e2 = jnp.exp(m2 - m1)
        g1 = 1.0 / (1.0 + e2)
        g2 = e2 / (1.0 + e2)
        route = jnp.where(lane == 0, i1.astype(F32),
                          jnp.where(lane == 1, i2.astype(F32),
                                    jnp.where(lane == 2, g1, jnp.where(lane == 3, g2, 0.0))))
        route_ref[...] = route
    else:
        h_ref[...] = h.astype(h_ref.dtype)


def _mod_view(mod):
    return mod.reshape(mod.shape[0], 6, 1, mod.shape[1] // 6)


def _resnorm(x, y, gate_mod, gate_k, mod, gain, sh_k, sc_k, *, rows, seq, n_batch, router=None):
    d = x.shape[1]
    tr = 256
    has_res = y is not None
    with_router = router is not None

    def grp(i):
        return jnp.minimum((i * tr) // seq, n_batch)

    def modspec(k):
        return pl.BlockSpec((1, 1, 1, d), lambda i: (grp(i), k, 0, 0))

    row = pl.BlockSpec((tr, d), lambda i: (i, 0))
    in_specs = [row]
    args = [x]
    if has_res:
        in_specs += [row, modspec(gate_k)]
        args += [y, _mod_view(gate_mod)]
    in_specs += [pl.BlockSpec((1, d), lambda i: (0, 0)), modspec(sh_k), modspec(sc_k)]
    args += [gain.reshape(1, d).astype(F32), _mod_view(mod), _mod_view(mod)]
    out_shape, out_specs = [], []
    n_exp = 0
    if with_router:
        n_exp = router.shape[1]
        router = jnp.pad(router.astype(F32), ((0, 0), (0, LANES - n_exp)))
        in_specs.append(pl.BlockSpec(router.shape, lambda i: (0, 0)))
        args.append(router)
    if has_res:
        out_shape.append(jax.ShapeDtypeStruct((rows, d), F32))
        out_specs.append(row)
    if with_router:
        out_shape += [jax.ShapeDtypeStruct((rows, d // 2), jnp.uint32), jax.ShapeDtypeStruct((rows, LANES), F32)]
        out_specs += [pl.BlockSpec((tr, d // 2), lambda i: (i, 0)), pl.BlockSpec((tr, LANES), lambda i: (i, 0))]
    else:
        out_shape.append(jax.ShapeDtypeStruct((rows, d), BF16))
        out_specs.append(row)
    kern = functools.partial(_resnorm_kernel, has_res=has_res, with_router=with_router, n_exp=n_exp)
    return pl.pallas_call(
        kern, grid=(rows // tr,), in_specs=in_specs, out_specs=out_specs, out_shape=out_shape,
        compiler_params=_cp(("arbitrary",), VMEM_LIMIT), name="resnorm",
    )(*args)


def _mm_kernel(*refs, n_x):
    w_ref, o_ref = refs[n_x], refs[n_x + 1]
    if n_x == 1:
        x = refs[0][...]
    else:
        x = jnp.concatenate([r[...] for r in refs[:n_x]], axis=1)
    o_ref[...] = jnp.dot(x, w_ref[...], preferred_element_type=F32).astype(o_ref.dtype)


def _mm(xs, w, *, rows, tm, tn, out_dtype, name):
    n = w.shape[1]
    in_specs = [pl.BlockSpec((tm, a.shape[1]), lambda i, j: (i, 0)) for a in xs]
    in_specs.append(pl.BlockSpec((w.shape[0], tn), lambda i, j: (0, j)))
    return pl.pallas_call(
        functools.partial(_mm_kernel, n_x=len(xs)),
        grid=(rows // tm, n // tn),
        in_specs=in_specs,
        out_specs=pl.BlockSpec((tm, tn), lambda i, j: (i, j)),
        out_shape=jax.ShapeDtypeStruct((rows, n), out_dtype),
        compiler_params=_cp(("arbitrary", "arbitrary"), VMEM_LIMIT), name=name,
    )(*xs, w)


def _group_meansq(x, g_ref, width):
    x2 = x * x
    hi = x2.astype(BF16)
    lo = (x2 - hi.astype(F32)).astype(BF16)
    g = g_ref[:width, :width]
    ss = jnp.dot(hi, g, preferred_element_type=F32) + jnp.dot(lo, g, preferred_element_type=F32)
    return ss * (1.0 / HEAD_DIM)


def _prep_kernel(p_ref, ch_ref, sh_ref, g_ref, gains_ref, o_ref):
    tr = p_ref.shape[0]

    def seg(off, width):
        return p_ref[:, off:off + width].astype(F32)

    def norm(x, row, width):
        return x * lax.rsqrt(_group_meansq(x, g_ref, width) + EPS) * gains_ref[row:row + 1, :width]

    def rope(x, width):
        lane = lax.broadcasted_iota(jnp.int32, (tr, width), 1)
        first = (lane % HEAD_DIM) < (HEAD_DIM // 2)
        cos = jnp.concatenate([ch_ref[...]] * (width // LANES), axis=1)
        sin = jnp.concatenate([sh_ref[...]] * (width // LANES), axis=1)
        swapped = jnp.where(first, pltpu.roll(x, width - HEAD_DIM // 2, 1), pltpu.roll(x, HEAD_DIM // 2, 1))
        return x * cos + swapped * sin

    scale = HEAD_DIM ** -0.5
    o_ref[:, P_AQ:P_AQ + 512] = (rope(norm(seg(P_AQ, 512), 0, 512), 512) * scale).astype(BF16)
    o_ref[:, P_CQ:P_CQ + 512] = (rope(seg(P_CQ, 512), 512) * scale).astype(BF16)
    o_ref[:, P_CK:P_CK + 512] = rope(seg(P_CK, 512), 512).astype(BF16)
    o_ref[:, P_DQ:P_DQ + 512] = (norm(seg(P_DQ, 512), 2, 512) * scale).astype(BF16)
    o_ref[:, P_DK:P_DK + 512] = norm(seg(P_DK, 512), 3, 512).astype(BF16)
    o_ref[:, P_AK:P_AK + 256] = rope(norm(seg(P_AK, 256), 1, 256), 256).astype(BF16)


def _prep(proj, cos_h, sin_h, gmat, gains, *, rows, seq, n_lat):
    tr = 256
    nlat = seq // tr

    def tab(i):
        return (jnp.where(i * tr < n_lat, i % nlat, nlat), 0)

    return pl.pallas_call(
        _prep_kernel,
        grid=(rows // tr,),
        in_specs=[pl.BlockSpec((tr, PREP_W), lambda i: (i, 0)),
                  pl.BlockSpec((tr, LANES), tab),
                  pl.BlockSpec((tr, LANES), tab),
                  pl.BlockSpec((512, 512), lambda i: (0, 0)),
                  pl.BlockSpec((4, 512), lambda i: (0, 0))],
        out_specs=pl.BlockSpec((tr, PREP_W), lambda i: (i, 0)),
        out_shape=jax.ShapeDtypeStruct((rows, PREP_W), BF16),
        compiler_params=_cp(("arbitrary",), VMEM_LIMIT), name="prep",
    )(proj, cos_h, sin_h, gmat, gains)


def _prepb_kernel(bq_ref, bkv_ref, bkr_ref, cb_ref, sb_ref, wuq_ref, wukv_ref, gains_ref, q_ref, k_ref, v_ref):
    tr = bq_ref.shape[0]
    cq = bq_ref[...].astype(F32)
    cqn = cq * lax.rsqrt(jnp.mean(cq * cq, axis=-1, keepdims=True) + EPS) * gains_ref[0:1, :]
    qup = jnp.dot(cqn.astype(BF16), wuq_ref[...], preferred_element_type=F32)
    ckv = bkv_ref[...].astype(F32)
    ckvn = ckv * lax.rsqrt(jnp.mean(ckv * ckv, axis=-1, keepdims=True) + EPS) * gains_ref[1:2, :LANES]
    kvup = jnp.dot(ckvn.astype(BF16), wukv_ref[...], preferred_element_type=F32)
    kr = bkr_ref[...].astype(F32)
    cb, sb = cb_ref[...], sb_ref[...]
    lane = lax.broadcasted_iota(jnp.int32, (tr, LANES), 1)
    half = B_ROPE // 2
    m1 = (lane >= B_NOPE) & (lane < B_NOPE + half)
    m2 = (lane >= B_NOPE + half) & (lane < B_QK)

    def rope(x):
        swapped = jnp.where(m1, pltpu.roll(x, LANES - half, 1), jnp.where(m2, pltpu.roll(x, half, 1), 0.0))
        return x * cb + swapped * sb

    def headnorm(x, row):
        ms = jnp.sum(x * x, axis=-1, keepdims=True) * (1.0 / B_QK)
        return x * lax.rsqrt(ms + EPS) * gains_ref[row:row + 1, :LANES]

    shift_lane = lane == B_QK
    neg_bound = gains_ref[4:5, :LANES]
    for h in range(B_HEADS):
        sl = slice(h * LANES, (h + 1) * LANES)
        q = rope(headnorm(qup[:, sl], 2)) * MLA_LOGIT_SCALE
        q_ref[:, sl] = jnp.where(shift_lane, 1.0, q).astype(BF16)
        k = rope(headnorm(kvup[:, sl] + kr, 3))
        k_ref[:, sl] = jnp.where(shift_lane, neg_bound, k).astype(BF16)
    ones_blk = jnp.where(lane < B_V, 1.0, 0.0).astype(BF16)
    for g in range(B_HEADS // 2):
        v_ref[:, 2 * g * LANES:(2 * g + 1) * LANES] = kvup[:, (B_HEADS + g) * LANES:(B_HEADS + g + 1) * LANES].astype(BF16)
        v_ref[:, (2 * g + 1) * LANES:(2 * g + 2) * LANES] = ones_blk


def _prepb(proj, cos_b, sin_b, wuq, wukv, gains, *, rows, seq, n_lat):
    tr = 256
    nlat = seq // tr

    def tab(i):
        return (jnp.where(i * tr < n_lat, i % nlat, nlat), 0)

    return pl.pallas_call(
        _prepb_kernel,
        grid=(rows // tr,),
        in_specs=[pl.BlockSpec((tr, B_Q_RANK), lambda i: (i, P_BQ // B_Q_RANK)),
                  pl.BlockSpec((tr, LANES), lambda i: (i, P_BKV // LANES)),
                  pl.BlockSpec((tr, LANES), lambda i: (i, P_BKR // LANES)),
                  pl.BlockSpec((tr, LANES), tab),
                  pl.BlockSpec((tr, LANES), tab),
                  pl.BlockSpec(wuq.shape, lambda i: (0, 0)),
                  pl.BlockSpec(wukv.shape, lambda i: (0, 0)),
                  pl.BlockSpec((8, B_Q_RANK), lambda i: (0, 0))],
        out_specs=[pl.BlockSpec((tr, B_HEADS * LANES), lambda i: (i, 0)),
                   pl.BlockSpec((tr, B_HEADS * LANES), lambda i: (i, 0)),
                   pl.BlockSpec((tr, B_HEADS * LANES), lambda i: (i, 0))],
        out_shape=[jax.ShapeDtypeStruct((rows, B_HEADS * LANES), BF16),
                   jax.ShapeDtypeStruct((rows, B_HEADS * LANES), BF16),
                   jax.ShapeDtypeStruct((rows, B_HEADS * LANES), BF16)],
        compiler_params=_cp(("arbitrary",), VMEM_LIMIT), name="prep_mla",
    )(proj, proj, proj, cos_b, sin_b, wuq, wukv, gains)


def _wina_kernel(sink_ref, q_ref, kp_ref, ko_ref, kn_ref, kc_ref, vp_ref, vo_ref, vn_ref, vc_ref, o_ref,
                 *, nb, seq):
    n = pl.program_id(1)
    blk = A_BLOCK
    n_ctx = kc_ref.shape[0]
    kcat = jnp.concatenate([kp_ref[...], ko_ref[...], kn_ref[...], kc_ref[...]], axis=0)
    vcat = jnp.concatenate([vp_ref[...], vo_ref[...], vn_ref[...], vc_ref[...]], axis=0)
    n_keys = 3 * blk + n_ctx
    qi = lax.broadcasted_iota(jnp.int32, (blk, n_keys), 0)
    kj = lax.broadcasted_iota(jnp.int32, (blk, n_keys), 1)
    band = kj - blk
    kpos = n * blk + band
    valid = ((jnp.abs(qi - band) <= blk) & (kpos >= 0) & (kpos < seq) & (n < nb)) | (kj >= 3 * blk)
    lane = lax.broadcasted_iota(jnp.int32, (blk, LANES), 1)
    low = lane < HEAD_DIM
    for j in range(A_HEADS // 2):
        g = j // (A_HEADS // A_KV_HEADS // 2)
        qc = q_ref[:, j * LANES:(j + 1) * LANES]
        kg = kcat[:, g * LANES:(g + 1) * LANES]
        vg = vcat[:, g * LANES:(g + 1) * LANES]
        outs = []
        for e in range(2):
            qm = jnp.where(low if e == 0 else jnp.logical_not(low), qc, jnp.zeros_like(qc))
            s = jnp.where(valid, _nt(qm, kg), NEG_INF)
            sk = sink_ref[2 * j + e]
            m = jnp.maximum(jnp.max(s, axis=-1, keepdims=True), sk)
            p = jnp.exp(s - m)
            den = jnp.sum(p, axis=-1, keepdims=True) + jnp.exp(sk - m)
            outs.append(jnp.dot(p.astype(BF16), vg, preferred_element_type=F32) / den)
        o_ref[:, j * LANES:(j + 1) * LANES] = jnp.where(low, outs[0], outs[1]).astype(o_ref.dtype)


def _wina(prep, proj, sink, *, n_batch, seq, n_ctx, with_ctx):
    blk = A_BLOCK
    nb = seq // blk
    ncb = n_ctx // blk
    steps = nb + (ncb if with_ctx else 0)
    lat_blocks = n_batch * nb
    rows_out = n_batch * seq + (n_batch * n_ctx if with_ctx else 0)

    def qmap(b, n, s):
        return (jnp.where(n < nb, b * nb + n, lat_blocks + b * ncb + (n - nb)), 0)

    def kmap(delta, col):
        def f(b, n, s):
            return (b * nb + jnp.clip(n + delta, 0, nb - 1), col)
        return f

    def cmap(col):
        def f(b, n, s):
            return (n_batch * seq // n_ctx + b, col)
        return f

    kcol, vcol = P_AK // 256, P_AV // 256
    in_specs = [pl.BlockSpec((blk, 512), qmap)]
    in_specs += [pl.BlockSpec((blk, 256), kmap(dl, kcol)) for dl in (-1, 0, 1)]
    in_specs += [pl.BlockSpec((n_ctx, 256), cmap(kcol))]
    in_specs += [pl.BlockSpec((blk, 256), kmap(dl, vcol)) for dl in (-1, 0, 1)]
    in_specs += [pl.BlockSpec((n_ctx, 256), cmap(vcol))]
    return pl.pallas_call(
        functools.partial(_wina_kernel, nb=nb, seq=seq),
        grid_spec=pltpu.PrefetchScalarGridSpec(
            num_scalar_prefetch=1, grid=(n_batch, steps), in_specs=in_specs,
            out_specs=pl.BlockSpec((blk, 512), qmap)),
        out_shape=jax.ShapeDtypeStruct((rows_out, 512), BF16),
        compiler_params=_cp(("arbitrary", "arbitrary"), VMEM_LIMIT), name="mixer_window",
    )(sink, prep, prep, prep, prep, prep, proj, proj, proj, proj)


def _mla_kernel(safe_ref, q_ref, kl_ref, vl_ref, kc_ref, vc_ref, o_ref, *, nq, tk_fast, tk_exact, hps):
    tq = q_ref.shape[0]
    seq = kl_ref.shape[0]
    is_lat = pl.program_id(2) < nq
    safe = safe_ref[0] > 0
    pair_w = 2 * LANES
    lane = lax.broadcasted_iota(jnp.int32, (tq, LANES), 1)

    def q(h):
        return q_ref[:, h * LANES:(h + 1) * LANES]

    def write(nums, dens):
        for g in range(hps // 2):
            o_ref[:, g * LANES:(g + 1) * LANES] = jnp.where(
                lane < B_V, nums[2 * g] / dens[2 * g], nums[2 * g + 1] / dens[2 * g + 1]).astype(o_ref.dtype)

    def fast(chunks):
        accs = [None] * hps
        for k_ref, v_ref, off, n in chunks:
            for h in range(hps):
                g = h // 2
                p = jnp.exp2(_nt(q(h), k_ref[off:off + n, h * LANES:(h + 1) * LANES])).astype(BF16)
                d = jnp.dot(p, v_ref[off:off + n, g * pair_w:(g + 1) * pair_w], preferred_element_type=F32)
                accs[h] = d if accs[h] is None else accs[h] + d
        write([a[:, :LANES] for a in accs], [a[:, LANES:LANES + 1] for a in accs])

    ctx_chunk = (kc_ref, vc_ref, 0, kc_ref.shape[0])

    @pl.when(safe & is_lat)
    def _():
        fast([(kl_ref, vl_ref, c * tk_fast, tk_fast) for c in range(seq // tk_fast)] + [ctx_chunk])

    @pl.when(safe & jnp.logical_not(is_lat))
    def _():
        fast([ctx_chunk])

    @pl.when(jnp.logical_not(safe))
    def _():
        def step(kf, vf, carry):
            out = []
            for h in range(hps):
                m, l, acc = carry[h]
                s = _nt(q(h), kf(h))
                m_new = jnp.maximum(m, jnp.max(s, axis=-1, keepdims=True))
                a = jnp.exp2(m - m_new)
                p = jnp.exp2(s - m_new)
                l = a * l + jnp.sum(p, axis=-1, keepdims=True)
                acc = a * acc + jnp.dot(p.astype(BF16), vf(h // 2), preferred_element_type=F32)
                out.append((m_new, l, acc))
            return tuple(out)

        def body(c, carry):
            off = pl.multiple_of(c * tk_exact, tk_exact)
            return step(lambda h: kl_ref[pl.ds(off, tk_exact), h * LANES:(h + 1) * LANES],
                        lambda g: vl_ref[pl.ds(off, tk_exact), g * pair_w:g * pair_w + LANES], carry)

        init = tuple((jnp.full((tq, 1), -jnp.inf, F32), jnp.zeros((tq, 1), F32), jnp.zeros((tq, LANES), F32))
                     for _ in range(hps))
        carry = lax.fori_loop(0, jnp.where(is_lat, seq // tk_exact, 0), body, init)
        carry = step(lambda h: kc_ref[:, h * LANES:(h + 1) * LANES],
                     lambda g: vc_ref[:, g * pair_w:g * pair_w + LANES], carry)
        write([c[2] for c in carry], [c[1] for c in carry])


def _mla_logit_bound(q_gain, k_gain):
    b = B_QK * MLA_LOGIT_SCALE * jnp.max(jnp.abs(q_gain)) * jnp.max(jnp.abs(k_gain))
    return (1.02 * b + 0.5).astype(F32)


def _mla(qb, kb, vb, safe, *, n_batch, seq, n_ctx, with_ctx):
    hps = 4
    tq = 256
    assert n_ctx == tq
    nq = seq // tq
    ctx_blk = n_batch * seq // n_ctx
    rows_out = n_batch * seq + (n_batch * n_ctx if with_ctx else 0)
    kw, vw = hps * LANES, hps * B_V

    def qmap(b, j, i, s):
        return (jnp.where(i < nq, b * nq + i, ctx_blk + b), j)

    return pl.pallas_call(
        functools.partial(_mla_kernel, nq=nq, tk_fast=min(2048, seq), tk_exact=512, hps=hps),
        grid_spec=pltpu.PrefetchScalarGridSpec(
            num_scalar_prefetch=1,
            grid=(n_batch, B_HEADS // hps, nq + (1 if with_ctx else 0)),
            in_specs=[pl.BlockSpec((tq, kw), qmap),
                      pl.BlockSpec((seq, kw), lambda b, j, i, s: (b, j)),
                      pl.BlockSpec((seq, kw), lambda b, j, i, s: (b, j)),
                      pl.BlockSpec((n_ctx, kw), lambda b, j, i, s: (ctx_blk + b, j)),
                      pl.BlockSpec((n_ctx, kw), lambda b, j, i, s: (ctx_blk + b, j))],
            out_specs=pl.BlockSpec((tq, vw), qmap)),
        out_shape=jax.ShapeDtypeStruct((rows_out, 512), BF16),
        compiler_params=_cp(("arbitrary", "arbitrary", "arbitrary"), VMEM_LIMIT), name="mixer_mla",
    )(safe, qb, kb, vb, kb, vb)


def _ret_kernel(qf_ref, kf_ref, vf_ref, qr_ref, kr_ref, vr_ref, ld_ref, of_ref, or_ref,
                s_ref, dec_ref, qw_ref, kw_ref, cd_ref):
    step = pl.program_id(1)
    cc = C_CHUNK
    ri = lax.broadcasted_iota(jnp.int32, (cc, cc), 0)
    ci = lax.broadcasted_iota(jnp.int32, (cc, cc), 1)
    low = ci < HEAD_DIM

    @pl.when(step == 0)
    def _():
        s_ref[...] = jnp.zeros_like(s_ref)
        pos = ri.astype(F32)
        for d in range(2):
            diff = (ri - ci) if d == 0 else (ci - ri)
            dpos = jnp.maximum(diff, 0).astype(F32)
            qpow = (pos + 1.0) if d == 0 else (cc - pos)
            kpow = (cc - 1.0 - pos) if d == 0 else pos
            for j in range(C_HEADS // 2):
                lg_e = -jnp.exp(ld_ref[d, 2 * j:2 * j + 1, :])
                lg_o = -jnp.exp(ld_ref[d, 2 * j + 1:2 * j + 2, :])
                lgl = jnp.where(low[0:1, :], lg_e, lg_o)
                dec_ref[d, j, :cc] = jnp.where(diff >= 0, jnp.exp(lg_e * dpos), 0.0)
                dec_ref[d, j, cc:] = jnp.where(diff >= 0, jnp.exp(lg_o * dpos), 0.0)
                qw_ref[d, j] = jnp.exp(lgl * qpow)
                kw_ref[d, j] = jnp.exp(lgl * kpow)
                cd_ref[d, j] = jnp.where(ri < HEAD_DIM, jnp.exp(lg_e * cc), jnp.exp(lg_o * cc))

    blockdiag = (ri < HEAD_DIM) == low
    for d, (q_ref, k_ref, v_ref, o_ref) in enumerate(((qf_ref, kf_ref, vf_ref, of_ref),
                                                      (qr_ref, kr_ref, vr_ref, or_ref))):
        for j in range(C_HEADS // 2):
            sl = slice(j * LANES, (j + 1) * LANES)
            q, k, v = q_ref[:, sl], k_ref[:, sl], v_ref[:, sl]
            zero = jnp.zeros_like(q)
            q2 = jnp.concatenate([jnp.where(low, q, zero), jnp.where(low, zero, q)], axis=0)
            sc = (_nt(q2, k) * dec_ref[d, j]).astype(BF16)
            o2 = jnp.dot(sc, v, preferred_element_type=F32)
            o_intra = jnp.where(low, o2[:cc], o2[cc:])
            state = s_ref[d, j]
            qw = (q.astype(F32) * qw_ref[d, j]).astype(BF16)
            o_cross = jnp.dot(qw, state.astype(BF16), preferred_element_type=F32)
            kw = (k.astype(F32) * kw_ref[d, j]).astype(BF16)
            kv = lax.dot_general(kw, v, (((0,), (0,)), ((), ())), preferred_element_type=F32)
            s_ref[d, j] = state * cd_ref[d, j] + jnp.where(blockdiag, kv, 0.0)
            o_ref[:, sl] = o_intra + o_cross


def _retention(prep, proj, ld_head, *, n_batch, seq, n_ctx):
    cc = C_CHUNK
    nl, nc = seq // cc, n_ctx // cc
    steps = nc + nl
    lat_blocks = n_batch * nl
    rows = n_batch * (seq + n_ctx)

    def rowblk(b, d, s):
        c_ctx = jnp.where(d == 0, s, nc - 1 - s)
        c_lat = jnp.where(d == 0, s - nc, nl - 1 - (s - nc))
        return jnp.where(s < nc, lat_blocks + b * nc + c_ctx, b * nl + c_lat)

    def spec(d, col):
        return pl.BlockSpec((cc, 512), lambda b, s: (rowblk(b, d, s), col))

    pairs = C_HEADS // 2
    return pl.pallas_call(
        _ret_kernel,
        grid=(n_batch, steps),
        in_specs=[spec(d, col) for d in (0, 1) for col in (P_CQ // 512, P_CK // 512, P_CV // 512)]
        + [pl.BlockSpec((2, C_HEADS, LANES), lambda b, s: (0, 0, 0))],
        out_specs=[spec(0, 0), spec(1, 0)],
        out_shape=[jax.ShapeDtypeStruct((rows, 512), F32)] * 2,
        scratch_shapes=[pltpu.VMEM((2, pairs, LANES, LANES), F32),
                        pltpu.VMEM((2, pairs, 2 * cc, cc), F32),
                        pltpu.VMEM((2, pairs, cc, LANES), F32),
                        pltpu.VMEM((2, pairs, cc, LANES), F32),
                        pltpu.VMEM((2, pairs, LANES, LANES), F32)],
        compiler_params=_cp(("arbitrary", "arbitrary"), VMEM_LIMIT), name="mixer_retention",
    )(prep, prep, proj, prep, prep, proj, ld_head)


def _retfin_kernel(of_ref, or_ref, g_ref, gm_ref, out_ref):
    o = of_ref[...] + or_ref[...]
    on = o * lax.rsqrt(_group_meansq(o, gm_ref, 512) + EPS)
    g = g_ref[...].astype(F32)
    out_ref[...] = (g * jax.nn.sigmoid(g) * on).astype(out_ref.dtype)


def _retention_finish(o_fwd, o_rev, proj, gmat, *, rows):
    tr = 256
    return pl.pallas_call(
        _retfin_kernel,
        grid=(rows // tr,),
        in_specs=[pl.BlockSpec((tr, 512), lambda i: (i, 0)),
                  pl.BlockSpec((tr, 512), lambda i: (i, 0)),
                  pl.BlockSpec((tr, 512), lambda i: (i, P_CG // 512)),
                  pl.BlockSpec((512, 512), lambda i: (0, 0))],
        out_specs=pl.BlockSpec((tr, 512), lambda i: (i, 0)),
        out_shape=jax.ShapeDtypeStruct((rows, 512), BF16),
        compiler_params=_cp(("arbitrary",), VMEM_LIMIT), name="retention_finish",
    )(o_fwd, o_rev, proj, gmat)


def _nbr_kernel(q_ref, k_ref, v_ref, kc_ref, vc_ref, abm_ref, o_ref, *, rows):
    st = pl.program_id(1)
    start = jnp.clip(NA_QROWS * st - NA_ROWS // 2, 0, rows - NA_KROWS)
    off = pl.multiple_of(start * GRID_W, GRID_W)
    nk = NA_KROWS * GRID_W
    tq = q_ref.shape[0]
    lane = lax.broadcasted_iota(jnp.int32, (tq, LANES), 1)
    low = lane < HEAD_DIM
    for j in range(D_HEADS // 2):
        sl = slice(j * LANES, (j + 1) * LANES)
        qc = q_ref[:, sl]
        kw = k_ref[pl.ds(off, nk), sl]
        vw = v_ref[pl.ds(off, nk), sl]
        kc, vc = kc_ref[:, sl], vc_ref[:, sl]
        outs = []
        for e in range(2):
            qm = jnp.where(low if e == 0 else jnp.logical_not(low), qc, jnp.zeros_like(qc))
            s1 = _nt(qm, kw) + abm_ref[0, 2 * j + e]
            s2 = _nt(qm, kc)
            m = jnp.maximum(jnp.max(s1, axis=-1, keepdims=True), jnp.max(s2, axis=-1, keepdims=True))
            p1 = jnp.exp(s1 - m)
            p2 = jnp.exp(s2 - m)
            den = jnp.sum(p1, axis=-1, keepdims=True) + jnp.sum(p2, axis=-1, keepdims=True)
            o = (jnp.dot(p1.astype(BF16), vw, preferred_element_type=F32)
                 + jnp.dot(p2.astype(BF16), vc, preferred_element_type=F32))
            outs.append(o / den)
        o_ref[:, sl] = jnp.where(low, outs[0], outs[1]).astype(o_ref.dtype)


def _nbr(prep, proj, abm, *, n_batch, seq, n_ctx, with_ctx):
    rows = seq // GRID_W
    tq = NA_QROWS * GRID_W
    assert n_ctx == tq and rows % NA_QROWS == 0 and rows >= NA_KROWS + NA_QROWS
    nst = rows // NA_QROWS
    steps = nst + (1 if with_ctx else 0)
    rows_out = n_batch * seq + (n_batch * n_ctx if with_ctx else 0)

    def qmap(col):
        def f(b, s):
            return (jnp.where(s < nst, b * nst + s, n_batch * nst + b), col)
        return f

    def case(b, s):
        c = jnp.where(s == 0, 0, jnp.where(s == nst - 1, 2, jnp.where(s == nst, 3, 1)))
        return (c, 0, 0, 0)

    ctx_blk = n_batch * seq // n_ctx
    return pl.pallas_call(
        functools.partial(_nbr_kernel, rows=rows),
        grid=(n_batch, steps),
        in_specs=[pl.BlockSpec((tq, 512), qmap(P_DQ // 512)),
                  pl.BlockSpec((seq, 512), lambda b, s: (b, P_DK // 512)),
                  pl.BlockSpec((seq, 512), lambda b, s: (b, P_DV // 512)),
                  pl.BlockSpec((n_ctx, 512), lambda b, s: (ctx_blk + b, P_DK // 512)),
                  pl.BlockSpec((n_ctx, 512), lambda b, s: (ctx_blk + b, P_DV // 512)),
                  pl.BlockSpec((1, D_HEADS, tq, NA_KROWS * GRID_W), case)],
        out_specs=pl.BlockSpec((tq, 512), qmap(0)),
        out_shape=jax.ShapeDtypeStruct((rows_out, 512), BF16),
        compiler_params=_cp(("arbitrary", "arbitrary"), VMEM_LIMIT), name="mixer_neighbourhood",
    )(prep, prep, proj, prep, proj, abm)


def _nbr_bias_tables(rpb, rows):
    w = GRID_W
    cidx = np.arange(w)
    col_start = np.clip(cidx - NA_COLS // 2, 0, w - NA_COLS)
    col_ok = (cidx[None, :] >= col_start[:, None]) & (cidx[None, :] < col_start[:, None] + NA_COLS)
    d_col = np.clip(cidx[None, :] - cidx[:, None] + (NA_COLS - 1), 0, 2 * NA_COLS - 2)
    n_dr = 2 * NA_ROWS - 1
    t = jnp.take(rpb.astype(F32), jnp.asarray(d_col.reshape(-1)), axis=2)
    t = t.reshape(rpb.shape[0], n_dr, w, w).transpose(0, 2, 1, 3)
    t = jnp.where(jnp.asarray(col_ok)[None, :, None, :], t, NEG_INF)
    t = jnp.concatenate([t, jnp.full((rpb.shape[0], w, 1, w), NEG_INF, F32)], axis=2)
    idx = np.full((NA_CASES, NA_QROWS, NA_KROWS), n_dr, np.int32)
    for c, r0 in enumerate((0, NA_QROWS, rows - NA_QROWS)):
        start = int(np.clip(r0 - NA_ROWS // 2, 0, rows - NA_KROWS))
        for i in range(NA_QROWS):
            r = r0 + i
            row_start = int(np.clip(r - NA_ROWS // 2, 0, rows - NA_ROWS))
            for kk in range(NA_KROWS):
                kr = start + kk
                if row_start <= kr < row_start + NA_ROWS:
                    idx[c, i, kk] = kr - r + (NA_ROWS - 1)
    out = jnp.take(t, jnp.asarray(idx.reshape(-1)), axis=2)
    out = out.reshape(rpb.shape[0], w, NA_CASES, NA_QROWS, NA_KROWS, w).transpose(2, 0, 3, 1, 4, 5)
    return out.reshape(NA_CASES, rpb.shape[0], NA_QROWS * w, NA_KROWS * w)


def _swiglu_accumulate(x, wg_ref, wu_ref, wd_ref, acc_ref):
    g = jnp.dot(x, wg_ref[0], preferred_element_type=F32)
    u = jnp.dot(x, wu_ref[0], preferred_element_type=F32)
    a = (g * jax.nn.sigmoid(g) * u).astype(BF16)
    acc_ref[...] += jnp.dot(a, wd_ref[0], preferred_element_type=F32)


def _ffn_kernel(x_ref, wg_ref, wu_ref, wd_ref, o_ref, acc_ref):
    j = pl.program_id(1)

    @pl.when(j == 0)
    def _():
        acc_ref[...] = jnp.zeros_like(acc_ref)

    _swiglu_accumulate(x_ref[...], wg_ref, wu_ref, wd_ref, acc_ref)

    @pl.when(j == pl.num_programs(1) - 1)
    def _():
        o_ref[...] = acc_ref[...].astype(o_ref.dtype)


def _ffn(x, w_gate_up, w_down, *, rows, tm, fc):
    _, d, ff2 = w_gate_up.shape
    nj = ff2 // 2 // fc
    return pl.pallas_call(
        _ffn_kernel,
        grid=(rows // tm, nj),
        in_specs=[pl.BlockSpec((tm, d), lambda i, j: (i, 0)),
                  pl.BlockSpec((1, d, fc), lambda i, j: (0, 0, j)),
                  pl.BlockSpec((1, d, fc), lambda i, j: (0, 0, nj + j)),
                  pl.BlockSpec((1, fc, d), lambda i, j: (0, j, 0))],
        out_specs=pl.BlockSpec((tm, d), lambda i, j: (i, 0)),
        out_shape=jax.ShapeDtypeStruct((rows, d), BF16),
        scratch_shapes=[pltpu.VMEM((tm, d), F32)],
        compiler_params=_cp(("arbitrary", "arbitrary"), VMEM_LIMIT), name="swiglu",
    )(x, w_gate_up, w_gate_up, w_down)


def _moe_ffn_kernel(te_ref, tv_ref, tok_ref, hp_ref, wg_ref, wu_ref, wd_ref, o_ref, gbuf_ref, x_ref, acc_ref, sem,
                    *, rows_per_step):
    i, j = pl.program_id(0), pl.program_id(1)
    n_tiles = pl.num_programs(0)
    tm, d = x_ref.shape
    half = d // 2
    valid = tv_ref[i] > 0
    slot = i % 2

    def issue(tile, first_row, n_rows, to_slot):
        def body(r, carry):
            row = first_row + r
            pltpu.make_async_copy(hp_ref.at[pl.ds(tok_ref[tile * tm + row], 1), :],
                                  gbuf_ref.at[to_slot].at[pl.ds(row, 1), :], sem.at[to_slot]).start()
            return carry

        lax.fori_loop(0, n_rows, body, 0, unroll=8)

    @pl.when((i == 0) & (j == 0) & valid)
    def _():
        issue(0, 0, tm, 0)

    @pl.when((j == 0) & valid)
    def _():
        pltpu.make_async_copy(hp_ref.at[pl.ds(0, tm), :], gbuf_ref.at[slot], sem.at[slot]).wait()
        lo, hi = _unpack_halves(gbuf_ref[slot])
        x_ref[:, :half] = lo.astype(BF16)
        x_ref[:, half:] = hi.astype(BF16)
        acc_ref[...] = jnp.zeros_like(acc_ref)

    nxt = jnp.minimum(i + 1, n_tiles - 1)

    @pl.when((i + 1 < n_tiles) & (tv_ref[nxt] > 0) & (j * rows_per_step < tm))
    def _():
        issue(nxt, j * rows_per_step, rows_per_step, 1 - slot)

    @pl.when(valid)
    def _():
        _swiglu_accumulate(x_ref[...], wg_ref, wu_ref, wd_ref, acc_ref)

    @pl.when(j == pl.num_programs(1) - 1)
    def _():
        o_ref[...] = jnp.where(valid, _pack_halves(acc_ref[...]), jnp.uint32(0))


def _moe_ffn(hp, w_gate_up, w_down, tile_expert, tile_valid, row_token, *, tm, fc):
    _, d, ff2 = w_gate_up.shape
    nj = ff2 // 2 // fc
    n_rows = row_token.shape[0]
    issue_steps = 1 << (nj.bit_length() - 1)

    def jeff(i, j, tv):
        return jnp.where(tv[i] > 0, j, nj - 1)

    return pl.pallas_call(
        functools.partial(_moe_ffn_kernel, rows_per_step=tm // issue_steps),
        grid_spec=pltpu.PrefetchScalarGridSpec(
            num_scalar_prefetch=3, grid=(n_rows // tm, nj),
            in_specs=[pl.BlockSpec(memory_space=pl.ANY),
                      pl.BlockSpec((1, d, fc), lambda i, j, te, tv, tok: (te[i], 0, jeff(i, j, tv))),
                      pl.BlockSpec((1, d, fc), lambda i, j, te, tv, tok: (te[i], 0, nj + jeff(i, j, tv))),
                      pl.BlockSpec((1, fc, d), lambda i, j, te, tv, tok: (te[i], jeff(i, j, tv), 0))],
            out_specs=pl.BlockSpec((tm, d // 2), lambda i, j, te, tv, tok: (i, 0)),
            scratch_shapes=[pltpu.VMEM((2, tm, d // 2), jnp.uint32), pltpu.VMEM((tm, d), BF16),
                            pltpu.VMEM((tm, d), F32), pltpu.SemaphoreType.DMA((2,))]),
        out_shape=jax.ShapeDtypeStruct((n_rows, d // 2), jnp.uint32),
        compiler_params=_cp(("arbitrary", "arbitrary"), VMEM_LIMIT), name="moe_swiglu",
    )(tile_expert, tile_valid, row_token, hp, w_gate_up, w_gate_up, w_down)


def _combine_kernel(p0_ref, p1_ref, y_ref, x_ref, route_ref, gate_ref, o_ref, b0_ref, b1_ref, sem):
    n = b0_ref.shape[0]
    half = b0_ref.shape[1]
    base = pl.program_id(0) * n

    def issue(r, carry):
        pltpu.make_async_copy(y_ref.at[pl.ds(p0_ref[base + r], 1), :], b0_ref.at[pl.ds(r, 1), :], sem.at[0]).start()
        pltpu.make_async_copy(y_ref.at[pl.ds(p1_ref[base + r], 1), :], b1_ref.at[pl.ds(r, 1), :], sem.at[1]).start()
        return carry

    lax.fori_loop(0, n, issue, 0, unroll=8)
    pltpu.make_async_copy(y_ref.at[pl.ds(0, n), :], b0_ref, sem.at[0]).wait()
    pltpu.make_async_copy(y_ref.at[pl.ds(0, n), :], b1_ref, sem.at[1]).wait()
    route = route_ref[...]
    w0, w1 = route[:, 2:3], route[:, 3:4]
    lo0, hi0 = _unpack_halves(b0_ref[...])
    lo1, hi1 = _unpack_halves(b1_ref[...])
    gate = gate_ref[0, 0]
    o_ref[:, :half] = x_ref[:, :half] + gate[:, :half] * (w0 * lo0 + w1 * lo1)
    o_ref[:, half:] = x_ref[:, half:] + gate[:, half:] * (w0 * hi0 + w1 * hi1)


def _combine(y, x, route, mod, gate_k, pos0, pos1, *, seq, n_batch):
    t, d = x.shape
    tr = 256
    return pl.pallas_call(
        _combine_kernel,
        grid_spec=pltpu.PrefetchScalarGridSpec(
            num_scalar_prefetch=2, grid=(t // tr,),
            in_specs=[pl.BlockSpec(memory_space=pl.ANY),
                      pl.BlockSpec((tr, d), lambda i, a, b: (i, 0)),
                      pl.BlockSpec((tr, LANES), lambda i, a, b: (i, 0)),
                      pl.BlockSpec((1, 1, 1, d),
                                   lambda i, a, b: (jnp.minimum((i * tr) // seq, n_batch), gate_k, 0, 0))],
            out_specs=pl.BlockSpec((tr, d), lambda i, a, b: (i, 0)),
            scratch_shapes=[pltpu.VMEM((tr, d // 2), jnp.uint32), pltpu.VMEM((tr, d // 2), jnp.uint32),
                            pltpu.SemaphoreType.DMA((2,))]),
        out_shape=jax.ShapeDtypeStruct((t, d), F32),
        compiler_params=_cp(("arbitrary",), VMEM_LIMIT), name="moe_combine",
    )(pos0, pos1, y, x, route, _mod_view(mod))


def _route_meta(idx, n_exp, tm):
    t = idx.shape[0]
    flat = idx.reshape(-1)
    onehot = (flat[:, None] == jnp.arange(n_exp, dtype=jnp.int32)[None, :]).astype(jnp.int32)
    counts = jnp.sum(onehot, axis=0)
    rank = jnp.take_along_axis(jnp.cumsum(onehot, axis=0) - onehot, flat[:, None], axis=1)[:, 0]
    padded = ((counts + tm - 1) // tm) * tm
    ends = jnp.cumsum(padded)
    starts = ends - padded
    pos = starts[flat] + rank
    n_rows = TOP_K * t + n_exp * tm
    row_token = jnp.zeros((n_rows,), jnp.int32).at[pos].set(jnp.arange(TOP_K * t, dtype=jnp.int32) // TOP_K)
    tile_start = jnp.arange(n_rows // tm, dtype=jnp.int32) * tm
    tile_valid = (tile_start < ends[-1]).astype(jnp.int32)
    tile_expert = jnp.sum((ends[None, :] <= tile_start[:, None]).astype(jnp.int32), axis=1)
    tile_expert = jnp.minimum(tile_expert, n_exp - 1)
    last_valid = jnp.maximum(ends[-1] // tm - 1, 0)
    tile_expert = jnp.where(tile_valid > 0, tile_expert, tile_expert[last_valid])
    pos = pos.reshape(t, TOP_K).astype(jnp.int32)
    return row_token, tile_expert, tile_valid, pos[:, 0], pos[:, 1]


def _rope_tables(seq, pad_rows):
    t = jnp.arange(seq, dtype=jnp.int32)
    row = (t // GRID_W).astype(F32)
    col = (t % GRID_W).astype(F32)

    def angles(rot_dim):
        n_freq = rot_dim // 4
        freqs = ROPE_THETA ** (-jnp.arange(n_freq, dtype=F32) / n_freq)
        ang = jnp.concatenate([row[:, None] * freqs[None, :], col[:, None] * freqs[None, :]], axis=-1)
        return jnp.cos(ang), jnp.sin(ang)

    ch, sh = angles(HEAD_DIM)
    cos_h = jnp.concatenate([ch] * 4, axis=1)
    sin_h = jnp.concatenate([-sh, sh, -sh, sh], axis=1)
    cb, sb = angles(B_ROPE)
    one = jnp.ones((seq, B_NOPE), F32)
    zero = jnp.zeros((seq, B_NOPE), F32)
    tail1 = jnp.ones((seq, LANES - B_QK), F32)
    tail0 = jnp.zeros((seq, LANES - B_QK), F32)
    cos_b = jnp.concatenate([one, cb, cb, tail1], axis=1)
    sin_b = jnp.concatenate([zero, -sb, sb, tail0], axis=1)

    def pad(tab, fill):
        return jnp.concatenate([tab, jnp.full((pad_rows, LANES), fill, F32)], axis=0)

    return pad(cos_h, 1.0), pad(sin_h, 0.0), pad(cos_b, 1.0), pad(sin_b, 0.0)


def _permute_w_in(w):
    d = w.shape[0]
    o = np.cumsum([0, 512, 128, 128, B_Q_RANK, B_KV_RANK, B_ROPE, 512, 512, 512, 512, 512, 512, 512])
    aq, ak, av, bq, bkv, bkr, cq, ck, cv, cg, dq, dk, dv = [w[:, o[i]:o[i + 1]] for i in range(13)]

    def dup(m):
        return jnp.concatenate([m[:, :64], m[:, :64], m[:, 64:], m[:, 64:]], axis=1)

    z = lambda n: jnp.zeros((d, n), w.dtype)
    parts = [aq, cq, ck, dq, dk, dup(ak), dup(av), cv, cg, dv, bq, bkv,
             z(B_NOPE), bkr, z(LANES - B_QK), z(P_END - P_BKR - LANES)]
    return jnp.concatenate(parts, axis=1).astype(BF16)


def _block_diag_ones():
    i = np.arange(512) // HEAD_DIM
    return jnp.asarray((i[:, None] == i[None, :]).astype(np.float32), dtype=BF16)


def _pad_lanes(v, width):
    return jnp.concatenate([v.astype(F32), jnp.zeros((width - v.shape[0],), F32)])


def kernel(x, c, ctx, c_ctx, w_ada, b_ada, norm_mix, norm_ffn, w_in, w_out, a_q_norm, a_k_norm, a_sink,
           b_q_a_norm, b_kv_a_norm, b_w_uq, b_w_ukv, b_q_norm, b_k_norm, c_log_decay, d_q_norm, d_k_norm,
           d_rpb, ffn_w_gate_up, ffn_w_down, moe_router, moe_w_gate_up, moe_w_down):
    n_batch, seq, d = x.shape
    n_ctx = ctx.shape[1]
    depth = w_ada.shape[0]
    n_lat = n_batch * seq
    n_all = n_lat + n_batch * n_ctx
    grid_rows = seq // GRID_W
    tm = next(t for t in (1024, 512, 256) if n_lat % t == 0 and n_all % t == 0)

    cc = jnp.concatenate([c, c_ctx[None, :], jnp.zeros((8 - n_batch - 1, d), F32)], axis=0)
    mod = _ada(cc, w_ada, b_ada)
    xs = jnp.concatenate([x.reshape(n_lat, d), ctx.reshape(n_batch * n_ctx, d)], axis=0)
    cos_h, sin_h, cos_b, sin_b = _rope_tables(seq, 256)
    gmat = _block_diag_ones()

    y = None
    y_gate, y_mod = 0, None
    for l in range(depth):
        last = l == depth - 1
        mod_l = mod[l]
        if y is None:
            (h,) = _resnorm(xs, None, None, 0, mod_l, norm_mix[l], 0, 1, rows=n_all, seq=seq, n_batch=n_batch)
        else:
            xs, h = _resnorm(xs, y, y_mod, y_gate, mod_l, norm_mix[l], 0, 1, rows=n_all, seq=seq,
                             n_batch=n_batch)
        proj = _mm([h], _permute_w_in(w_in[l]), rows=n_all, tm=tm, tn=768, out_dtype=BF16, name="proj_in")
        gains = jnp.stack([jnp.tile(a_q_norm[l], 8), _pad_lanes(jnp.tile(a_k_norm[l], 4), 512),
                           jnp.tile(d_q_norm[l], 8), jnp.tile(d_k_norm[l], 8)]).astype(F32)
        prep = _prep(proj, cos_h, sin_h, gmat, gains, rows=n_all, seq=seq, n_lat=n_lat)
        wuq = jnp.pad(b_w_uq[l].reshape(B_Q_RANK, B_HEADS, B_QK),
                      ((0, 0), (0, 0), (0, LANES - B_QK))).reshape(B_Q_RANK, B_HEADS * LANES).astype(BF16)
        wukv3 = b_w_ukv[l].reshape(B_KV_RANK, B_HEADS, B_NOPE + B_V)
        wukv = jnp.concatenate(
            [jnp.pad(wukv3[:, :, :B_NOPE], ((0, 0), (0, 0), (0, LANES - B_NOPE))).reshape(B_KV_RANK, -1),
             wukv3[:, :, B_NOPE:].reshape(B_KV_RANK, -1)], axis=1).astype(BF16)
        bound = _mla_logit_bound(b_q_norm[l], b_k_norm[l])
        mla_safe = (bound <= MLA_SAFE_BOUND).astype(jnp.int32).reshape(1)
        gains_b = jnp.stack([b_q_a_norm[l].astype(F32), _pad_lanes(b_kv_a_norm[l], B_Q_RANK),
                             _pad_lanes(b_q_norm[l], B_Q_RANK), _pad_lanes(b_k_norm[l], B_Q_RANK),
                             jnp.full((B_Q_RANK,), -1.0, F32) * bound] + [jnp.zeros((B_Q_RANK,), F32)] * 3)
        qb, kb, vb = _prepb(proj, cos_b, sin_b, wuq, wukv, gains_b, rows=n_all, seq=seq, n_lat=n_lat)

        oa = _wina(prep, proj, a_sink[l].astype(F32), n_batch=n_batch, seq=seq, n_ctx=n_ctx, with_ctx=not last)
        ob = _mla(qb, kb, vb, mla_safe, n_batch=n_batch, seq=seq, n_ctx=n_ctx, with_ctx=not last)
        ld_head = jnp.broadcast_to(c_log_decay[l].astype(F32)[:, :, None], (2, C_HEADS, LANES))
        oc_fwd, oc_rev = _retention(prep, proj, ld_head, n_batch=n_batch, seq=seq, n_ctx=n_ctx)
        abm = _nbr_bias_tables(d_rpb[l], grid_rows)
        od = _nbr(prep, proj, abm, n_batch=n_batch, seq=seq, n_ctx=n_ctx, with_ctx=not last)
        rows_l = n_lat if last else n_all
        oc = _retention_finish(oc_fwd, oc_rev, proj, gmat, rows=rows_l)
        ymix = _mm([oa, ob, oc, od], w_out[l].astype(BF16), rows=rows_l, tm=tm, tn=1024, out_dtype=BF16,
                   name="proj_out")
        i = l // 2
        if l % 2 == 0:
            xs, h2 = _resnorm(xs, ymix, mod_l, 2, mod_l, norm_ffn[l], 3, 4, rows=rows_l, seq=seq,
                              n_batch=n_batch)
            y = _ffn(h2, ffn_w_gate_up[i:i + 1].astype(BF16), ffn_w_down[i:i + 1].astype(BF16),
                     rows=rows_l, tm=tm, fc=512)
            y_gate, y_mod = 5, mod_l
        else:
            xs, hp, route = _resnorm(xs, ymix, mod_l, 2, mod_l, norm_ffn[l], 3, 4, rows=rows_l, seq=seq,
                                     n_batch=n_batch, router=moe_router[i])
            top_idx = route[:, :TOP_K].astype(jnp.int32)
            row_token, tile_expert, tile_valid, pos0, pos1 = _route_meta(top_idx, moe_router.shape[2], tm)
            yg = _moe_ffn(hp, moe_w_gate_up[i].astype(BF16), moe_w_down[i].astype(BF16), tile_expert, tile_valid,
                          row_token, tm=tm, fc=512)
            xs = _combine(yg, xs, route, mod_l, 5, pos0, pos1, seq=seq, n_batch=n_batch)
            y = None
    if y is not None:
        xs, _ = _resnorm(xs, y, y_mod, y_gate, y_mod, norm_ffn[depth - 1], 3, 4, rows=xs.shape[0], seq=seq,
                         n_batch=n_batch)
    return xs[:n_lat].reshape(n_batch, seq, d)
```

```python
import functools

import numpy as np
import jax
import jax.numpy as jnp
from jax import lax
from jax.experimental import pallas as pl
from jax.experimental.pallas import tpu as pltpu

F32 = jnp.float32
BF16 = jnp.bfloat16

GRID_W = 64
HEAD_DIM = 64
ROPE_THETA = 10000.0
EPS = 1e-6
NEG_INF = -1e30

A_HEADS = 8
A_KV_HEADS = 2
A_BLOCK = 128
B_HEADS = 8
B_Q_RANK = 384
B_KV_RANK = 128
B_NOPE = 64
B_ROPE = 32
B_QK = B_NOPE + B_ROPE
B_V = 64
C_HEADS = 8
C_CHUNK = 128
D_HEADS = 8
NA_ROWS = 8
NA_COLS = 16
TOP_K = 2

LANES = 128
NA_QROWS = 4
NA_KROWS = NA_ROWS + NA_QROWS
NA_CASES = 4

P_AQ, P_CQ, P_CK, P_DQ, P_DK, P_AK, P_AV, P_CV, P_CG, P_DV, P_BQ, P_BKV, P_BKR, P_END = (
    0, 512, 1024, 1536, 2048, 2560, 2816, 3072, 3584, 4096, 4608, 4992, 5120, 5376)
PREP_W = P_AV

VMEM_LIMIT = 56 * 1024 * 1024
VMEM_LIMIT_MOE = 61 * 1024 * 1024

MLA_LOGIT_SCALE = B_QK ** -0.5 * float(np.log2(np.e))
MLA_SAFE_BOUND = 40.0


def _cp(sem, vmem=None):
    return pltpu.CompilerParams(dimension_semantics=sem, vmem_limit_bytes=vmem)


def _nt(a, b):
    return lax.dot_general(a, b, (((1,), (1,)), ((), ())), preferred_element_type=F32)


_HI_HALF = 0xFFFF0000


def _pack_halves(x):
    n = x.shape[1] // 2
    bits = pltpu.bitcast(x.astype(BF16).astype(F32), jnp.uint32)
    return (bits[:, :n] >> 16) | (bits[:, n:] & jnp.uint32(_HI_HALF))


def _unpack_halves(w):
    return pltpu.bitcast(w << 16, F32), pltpu.bitcast(w & jnp.uint32(_HI_HALF), F32)


def _ada_kernel(c_ref, w_ref, b_ref, o_ref):
    c = c_ref[...]
    s = (c * jax.nn.sigmoid(c)).astype(BF16)
    o_ref[0] = jnp.dot(s, w_ref[0].astype(BF16), preferred_element_type=F32) + b_ref[0]


def _ada(cc, w_ada, b_ada):
    depth, d, n = w_ada.shape
    tn = 1024
    return pl.pallas_call(
        _ada_kernel,
        grid=(depth, n // tn),
        in_specs=[pl.BlockSpec((8, d), lambda l, j: (0, 0)),
                  pl.BlockSpec((1, d, tn), lambda l, j: (l, 0, j)),
                  pl.BlockSpec((1, 1, tn), lambda l, j: (l, 0, j))],
        out_specs=pl.BlockSpec((1, 8, tn), lambda l, j: (l, 0, j)),
        out_shape=jax.ShapeDtypeStruct((depth, 8, n), F32),
        compiler_params=_cp(("arbitrary", "arbitrary"), VMEM_LIMIT),
        name="ada",
    )(cc, w_ada, b_ada.reshape(depth, 1, n))


def _resnorm_kernel(*refs, has_res, with_router, n_exp, lat_tiles):
    refs = list(refs)
    x_ref = refs.pop(0)
    if lat_tiles is not None:
        xc_ref = refs.pop(0)
        x = jnp.where(pl.program_id(0) < lat_tiles, x_ref[...], xc_ref[...])
    else:
        x = x_ref[...]
    if has_res:
        y_ref = refs.pop(0)
        gate_ref = refs.pop(0)
    gain_ref, sh_ref, sc_ref = refs[:3]
    refs = refs[3:]
    if with_router:
        router_ref = refs.pop(0)
    if has_res:
        xo_ref = refs.pop(0)
    h_ref = refs.pop(0)
    if has_res:
        x = x + gate_ref[0, 0] * y_ref[...].astype(F32)
        xo_ref[...] = x
    ms = jnp.mean(x * x, axis=-1, keepdims=True)
    h = (x * lax.rsqrt(ms + EPS) * gain_ref[...]) * (1.0 + sc_ref[0, 0]) + sh_ref[0, 0]
    if with_router:
        (route_ref,) = refs
        h_ref[...] = _pack_halves(h)
        logits = jnp.dot(h, router_ref[...], preferred_element_type=F32, precision=lax.Precision.HIGHEST)
        lane = lax.broadcasted_iota(jnp.int32, logits.shape, 1)
        logits = jnp.where(lane < n_exp, logits, -jnp.inf)
        m1 = jnp.max(logits, axis=-1, keepdims=True)
        i1 = jnp.min(jnp.where(logits == m1, lane, LANES), axis=-1, keepdims=True)
        rest = jnp.where(lane == i1, -jnp.inf, logits)
        m2 = jnp.max(rest, axis=-1, keepdims=True)
        i2 = jnp.min(jnp.where(rest == m2, lane, LANES), axis=-1, keepdims=True)
        e2 = jnp.exp(m2 - m1)
        g1 = 1.0 / (1.0 + e2)
        g2 = e2 / (1.0 + e2)
        route = jnp.where(lane == 0, i1.astype(F32),
                          jnp.where(lane == 1, i2.astype(F32),
                                    jnp.where(lane == 2, g1, jnp.where(lane == 3, g2, 0.0))))
        route_ref[...] = route
    else:
        h_ref[...] = h.astype(h_ref.dtype)


def _mod_view(mod):
    return mod.reshape(mod.shape[0], 6, 1, mod.shape[1] // 6)


def _resnorm(x, y, gate_mod, gate_k, mod, gain, sh_k, sc_k, *, rows, seq, n_batch, router=None):
    tr = 256
    has_res = y is not None
    with_router = router is not None
    split = isinstance(x, tuple)
    d = x[0].shape[1] if split else x.shape[1]

    def grp(i):
        return jnp.minimum((i * tr) // seq, n_batch)

    def modspec(k):
        return pl.BlockSpec((1, 1, 1, d), lambda i: (grp(i), k, 0, 0))

    row = pl.BlockSpec((tr, d), lambda i: (i, 0))
    lat_tiles = None
    if split:
        lat_tiles = x[0].shape[0] // tr
        in_specs = [pl.BlockSpec((tr, d), lambda i: (jnp.minimum(i, lat_tiles - 1), 0)),
                    pl.BlockSpec((tr, d), lambda i: (jnp.maximum(i - lat_tiles, 0), 0))]
        args = list(x)
    else:
        in_specs = [row]
        args = [x]
    if has_res:
        in_specs += [row, modspec(gate_k)]
        args += [y, _mod_view(gate_mod)]
    in_specs += [pl.BlockSpec((1, d), lambda i: (0, 0)), modspec(sh_k), modspec(sc_k)]
    args += [gain.reshape(1, d).astype(F32), _mod_view(mod), _mod_view(mod)]
    out_shape, out_specs = [], []
    n_exp = 0
    if with_router:
        n_exp = router.shape[1]
        router = jnp.pad(router.astype(F32), ((0, 0), (0, LANES - n_exp)))
        in_specs.append(pl.BlockSpec(router.shape, lambda i: (0, 0)))
        args.append(router)
    if has_res:
        out_shape.append(jax.ShapeDtypeStruct((rows, d), F32))
        out_specs.append(row)
    if with_router:
        out_shape += [jax.ShapeDtypeStruct((rows, d // 2), jnp.uint32), jax.ShapeDtypeStruct((rows, LANES), F32)]
        out_specs += [pl.BlockSpec((tr, d // 2), lambda i: (i, 0)), pl.BlockSpec((tr, LANES), lambda i: (i, 0))]
    else:
        out_shape.append(jax.ShapeDtypeStruct((rows, d), BF16))
        out_specs.append(row)
    kern = functools.partial(_resnorm_kernel, has_res=has_res, with_router=with_router, n_exp=n_exp,
                             lat_tiles=lat_tiles)
    return pl.pallas_call(
        kern, grid=(rows // tr,), in_specs=in_specs, out_specs=out_specs, out_shape=out_shape,
        compiler_params=_cp(("arbitrary",), VMEM_LIMIT), name="resnorm",
    )(*args)


def _mm_kernel(*refs, n_x):
    w_ref, o_ref = refs[n_x], refs[n_x + 1]
    if n_x == 1:
        x = refs[0][...]
    else:
        x = jnp.concatenate([r[...] for r in refs[:n_x]], axis=1)
    o_ref[...] = jnp.dot(x, w_ref[...], preferred_element_type=F32).astype(o_ref.dtype)


def _mm(xs, w, *, rows, tm, tn, out_dtype, name):
    n = w.shape[1]
    in_specs = [pl.BlockSpec((tm, a.shape[1]), lambda i, j: (i, 0)) for a in xs]
    in_specs.append(pl.BlockSpec((w.shape[0], tn), lambda i, j: (0, j)))
    return pl.pallas_call(
        functools.partial(_mm_kernel, n_x=len(xs)),
        grid=(rows // tm, n // tn),
        in_specs=in_specs,
        out_specs=pl.BlockSpec((tm, tn), lambda i, j: (i, j)),
        out_shape=jax.ShapeDtypeStruct((rows, n), out_dtype),
        compiler_params=_cp(("arbitrary", "arbitrary"), VMEM_LIMIT), name=name,
    )(*xs, w)


def _group_meansq(x, g_ref, width):
    x2 = x * x
    hi = x2.astype(BF16)
    lo = (x2 - hi.astype(F32)).astype(BF16)
    g = g_ref[:width, :width]
    ss = jnp.dot(hi, g, preferred_element_type=F32) + jnp.dot(lo, g, preferred_element_type=F32)
    return ss * (1.0 / HEAD_DIM)


def _prep_kernel(p_ref, ch_ref, sh_ref, g_ref, gains_ref, o_ref):
    tr = p_ref.shape[0]

    def seg(off, width):
        return p_ref[:, off:off + width].astype(F32)

    def norm(x, row, width):
        return x * lax.rsqrt(_group_meansq(x, g_ref, width) + EPS) * gains_ref[row:row + 1, :width]

    def rope(x, width):
        lane = lax.broadcasted_iota(jnp.int32, (tr, width), 1)
        first = (lane % HEAD_DIM) < (HEAD_DIM // 2)
        cos = jnp.concatenate([ch_ref[...]] * (width // LANES), axis=1)
        sin = jnp.concatenate([sh_ref[...]] * (width // LANES), axis=1)
        swapped = jnp.where(first, pltpu.roll(x, width - HEAD_DIM // 2, 1), pltpu.roll(x, HEAD_DIM // 2, 1))
        return x * cos + swapped * sin

    scale = HEAD_DIM ** -0.5
    o_ref[:, P_AQ:P_AQ + 512] = (rope(norm(seg(P_AQ, 512), 0, 512), 512) * scale).astype(BF16)
    o_ref[:, P_CQ:P_CQ + 512] = (rope(seg(P_CQ, 512), 512) * scale).astype(BF16)
    o_ref[:, P_CK:P_CK + 512] = rope(seg(P_CK, 512), 512).astype(BF16)
    o_ref[:, P_DQ:P_DQ + 512] = (norm(seg(P_DQ, 512), 2, 512) * scale).astype(BF16)
    o_ref[:, P_DK:P_DK + 512] = norm(seg(P_DK, 512), 3, 512).astype(BF16)
    o_ref[:, P_AK:P_AK + 256] = rope(norm(seg(P_AK, 256), 1, 256), 256).astype(BF16)


def _prep(proj, cos_h, sin_h, gmat, gains, *, rows, seq, n_lat):
    tr = 256
    nlat = seq // tr

    def tab(i):
        return (jnp.where(i * tr < n_lat, i % nlat, nlat), 0)

    return pl.pallas_call(
        _prep_kernel,
        grid=(rows // tr,),
        in_specs=[pl.BlockSpec((tr, PREP_W), lambda i: (i, 0)),
                  pl.BlockSpec((tr, LANES), tab),
                  pl.BlockSpec((tr, LANES), tab),
                  pl.BlockSpec((512, 512), lambda i: (0, 0)),
                  pl.BlockSpec((4, 512), lambda i: (0, 0))],
        out_specs=pl.BlockSpec((tr, PREP_W), lambda i: (i, 0)),
        out_shape=jax.ShapeDtypeStruct((rows, PREP_W), BF16),
        compiler_params=_cp(("arbitrary",), VMEM_LIMIT), name="prep",
    )(proj, cos_h, sin_h, gmat, gains)


def _prepb_kernel(bq_ref, bkv_ref, bkr_ref, cb_ref, sb_ref, wuq_ref, wukv_ref, gains_ref, q_ref, k_ref, v_ref):
    tr = bq_ref.shape[0]
    cq = bq_ref[...].astype(F32)
    cqn = cq * lax.rsqrt(jnp.mean(cq * cq, axis=-1, keepdims=True) + EPS) * gains_ref[0:1, :]
    qup = jnp.dot(cqn.astype(BF16), wuq_ref[...], preferred_element_type=F32)
    ckv = bkv_ref[...].astype(F32)
    ckvn = ckv * lax.rsqrt(jnp.mean(ckv * ckv, axis=-1, keepdims=True) + EPS) * gains_ref[1:2, :LANES]
    kvup = jnp.dot(ckvn.astype(BF16), wukv_ref[...], preferred_element_type=F32)
    kr = bkr_ref[...].astype(F32)
    cb, sb = cb_ref[...], sb_ref[...]
    lane = lax.broadcasted_iota(jnp.int32, (tr, LANES), 1)
    half = B_ROPE // 2
    m1 = (lane >= B_NOPE) & (lane < B_NOPE + half)
    m2 = (lane >= B_NOPE + half) & (lane < B_QK)

    def rope(x):
        swapped = jnp.where(m1, pltpu.roll(x, LANES - half, 1), jnp.where(m2, pltpu.roll(x, half, 1), 0.0))
        return x * cb + swapped * sb

    def headnorm(x, row):
        ms = jnp.sum(x * x, axis=-1, keepdims=True) * (1.0 / B_QK)
        return x * lax.rsqrt(ms + EPS) * gains_ref[row:row + 1, :LANES]

    shift_lane = lane == B_QK
    neg_bound = gains_ref[4:5, :LANES]
    for h in range(B_HEADS):
        sl = slice(h * LANES, (h + 1) * LANES)
        q = rope(headnorm(qup[:, sl], 2)) * MLA_LOGIT_SCALE
        q_ref[:, sl] = jnp.where(shift_lane, 1.0, q).astype(BF16)
        k = rope(headnorm(kvup[:, sl] + kr, 3))
        k_ref[:, sl] = jnp.where(shift_lane, neg_bound, k).astype(BF16)
    ones_blk = jnp.where(lane < B_V, 1.0, 0.0).astype(BF16)
    for g in range(B_HEADS // 2):
        v_ref[:, 2 * g * LANES:(2 * g + 1) * LANES] = kvup[:, (B_HEADS + g) * LANES:(B_HEADS + g + 1) * LANES].astype(BF16)
        v_ref[:, (2 * g + 1) * LANES:(2 * g + 2) * LANES] = ones_blk


def _prepb(proj, cos_b, sin_b, wuq, wukv, gains, *, rows, seq, n_lat):
    tr = 256
    nlat = seq // tr

    def tab(i):
        return (jnp.where(i * tr < n_lat, i % nlat, nlat), 0)

    return pl.pallas_call(
        _prepb_kernel,
        grid=(rows // tr,),
        in_specs=[pl.BlockSpec((tr, B_Q_RANK), lambda i: (i, P_BQ // B_Q_RANK)),
                  pl.BlockSpec((tr, LANES), lambda i: (i, P_BKV // LANES)),
                  pl.BlockSpec((tr, LANES), lambda i: (i, P_BKR // LANES)),
                  pl.BlockSpec((tr, LANES), tab),
                  pl.BlockSpec((tr, LANES), tab),
                  pl.BlockSpec(wuq.shape, lambda i: (0, 0)),
                  pl.BlockSpec(wukv.shape, lambda i: (0, 0)),
                  pl.BlockSpec((8, B_Q_RANK), lambda i: (0, 0))],
        out_specs=[pl.BlockSpec((tr, B_HEADS * LANES), lambda i: (i, 0)),
                   pl.BlockSpec((tr, B_HEADS * LANES), lambda i: (i, 0)),
                   pl.BlockSpec((tr, B_HEADS * LANES), lambda i: (i, 0))],
        out_shape=[jax.ShapeDtypeStruct((rows, B_HEADS * LANES), BF16),
                   jax.ShapeDtypeStruct((rows, B_HEADS * LANES), BF16),
                   jax.ShapeDtypeStruct((rows, B_HEADS * LANES), BF16)],
        compiler_params=_cp(("arbitrary",), VMEM_LIMIT), name="prep_mla",
    )(proj, proj, proj, cos_b, sin_b, wuq, wukv, gains)


def _wina_kernel(sink_ref, q_ref, kp_ref, ko_ref, kn_ref, kc_ref, vp_ref, vo_ref, vn_ref, vc_ref, o_ref,
                 *, nb, seq):
    n = pl.program_id(1)
    blk = A_BLOCK
    n_ctx = kc_ref.shape[0]
    kcat = jnp.concatenate([kp_ref[...], ko_ref[...], kn_ref[...], kc_ref[...]], axis=0)
    vcat = jnp.concatenate([vp_ref[...], vo_ref[...], vn_ref[...], vc_ref[...]], axis=0)
    n_keys = 3 * blk + n_ctx
    qi = lax.broadcasted_iota(jnp.int32, (blk, n_keys), 0)
    kj = lax.broadcasted_iota(jnp.int32, (blk, n_keys), 1)
    band = kj - blk
    kpos = n * blk + band
    valid = ((jnp.abs(qi - band) <= blk) & (kpos >= 0) & (kpos < seq) & (n < nb)) | (kj >= 3 * blk)
    lane = lax.broadcasted_iota(jnp.int32, (blk, LANES), 1)
    low = lane < HEAD_DIM
    for j in range(A_HEADS // 2):
        g = j // (A_HEADS // A_KV_HEADS // 2)
        qc = q_ref[:, j * LANES:(j + 1) * LANES]
        kg = kcat[:, g * LANES:(g + 1) * LANES]
        vg = vcat[:, g * LANES:(g + 1) * LANES]
        outs = []
        for e in range(2):
            qm = jnp.where(low if e == 0 else jnp.logical_not(low), qc, jnp.zeros_like(qc))
            s = jnp.where(valid, _nt(qm, kg), NEG_INF)
            sk = sink_ref[2 * j + e]
            m = jnp.maximum(jnp.max(s, axis=-1, keepdims=True), sk)
            p = jnp.exp(s - m)
            den = jnp.sum(p, axis=-1, keepdims=True) + jnp.exp(sk - m)
            outs.append(jnp.dot(p.astype(BF16), vg, preferred_element_type=F32) / den)
        o_ref[:, j * LANES:(j + 1) * LANES] = jnp.where(low, outs[0], outs[1]).astype(o_ref.dtype)


def _wina(prep, proj, sink, *, n_batch, seq, n_ctx, with_ctx):
    blk = A_BLOCK
    nb = seq // blk
    ncb = n_ctx // blk
    steps = nb + (ncb if with_ctx else 0)
    lat_blocks = n_batch * nb
    rows_out = n_batch * seq + (n_batch * n_ctx if with_ctx else 0)

    def qmap(b, n, s):
        return (jnp.where(n < nb, b * nb + n, lat_blocks + b * ncb + (n - nb)), 0)

    def kmap(delta, col):
        def f(b, n, s):
            return (b * nb + jnp.clip(n + delta, 0, nb - 1), col)
        return f

    def cmap(col):
        def f(b, n, s):
            return (n_batch * seq // n_ctx + b, col)
        return f

    kcol, vcol = P_AK // 256, P_AV // 256
    in_specs = [pl.BlockSpec((blk, 512), qmap)]
    in_specs += [pl.BlockSpec((blk, 256), kmap(dl, kcol)) for dl in (-1, 0, 1)]
    in_specs += [pl.BlockSpec((n_ctx, 256), cmap(kcol))]
    in_specs += [pl.BlockSpec((blk, 256), kmap(dl, vcol)) for dl in (-1, 0, 1)]
    in_specs += [pl.BlockSpec((n_ctx, 256), cmap(vcol))]
    return pl.pallas_call(
        functools.partial(_wina_kernel, nb=nb, seq=seq),
        grid_spec=pltpu.PrefetchScalarGridSpec(
            num_scalar_prefetch=1, grid=(n_batch, steps), in_specs=in_specs,
            out_specs=pl.BlockSpec((blk, 512), qmap)),
        out_shape=jax.ShapeDtypeStruct((rows_out, 512), BF16),
        compiler_params=_cp(("arbitrary", "arbitrary"), VMEM_LIMIT), name="mixer_window",
    )(sink, prep, prep, prep, prep, prep, proj, proj, proj, proj)


def _mla_kernel(safe_ref, q_ref, kl_ref, vl_ref, kc_ref, vc_ref, o_ref, *, nq, tk_fast, tk_exact, hps):
    tq = q_ref.shape[0]
    seq = kl_ref.shape[0]
    is_lat = pl.program_id(2) < nq
    safe = safe_ref[0] > 0
    pair_w = 2 * LANES
    lane = lax.broadcasted_iota(jnp.int32, (tq, LANES), 1)

    def q(h):
        return q_ref[:, h * LANES:(h + 1) * LANES]

    def write(nums, dens):
        for g in range(hps // 2):
            o_ref[:, g * LANES:(g + 1) * LANES] = jnp.where(
                lane < B_V, nums[2 * g] / dens[2 * g], nums[2 * g + 1] / dens[2 * g + 1]).astype(o_ref.dtype)

    def fast(chunks):
        accs = [None] * hps
        for k_ref, v_ref, off, n in chunks:
            for h in range(hps):
                g = h // 2
                p = jnp.exp2(_nt(q(h), k_ref[off:off + n, h * LANES:(h + 1) * LANES])).astype(BF16)
                d = jnp.dot(p, v_ref[off:off + n, g * pair_w:(g + 1) * pair_w], preferred_element_type=F32)
                accs[h] = d if accs[h] is None else accs[h] + d
        write([a[:, :LANES] for a in accs], [a[:, LANES:LANES + 1] for a in accs])

    ctx_chunk = (kc_ref, vc_ref, 0, kc_ref.shape[0])

    @pl.when(safe & is_lat)
    def _():
        fast([(kl_ref, vl_ref, c * tk_fast, tk_fast) for c in range(seq // tk_fast)] + [ctx_chunk])

    @pl.when(safe & jnp.logical_not(is_lat))
    def _():
        fast([ctx_chunk])

    @pl.when(jnp.logical_not(safe))
    def _():
        def step(kf, vf, carry):
            out = []
            for h in range(hps):
                m, l, acc = carry[h]
                s = _nt(q(h), kf(h))
                m_new = jnp.maximum(m, jnp.max(s, axis=-1, keepdims=True))
                a = jnp.exp2(m - m_new)
                p = jnp.exp2(s - m_new)
                l = a * l + jnp.sum(p, axis=-1, keepdims=True)
                acc = a * acc + jnp.dot(p.astype(BF16), vf(h // 2), preferred_element_type=F32)
                out.append((m_new, l, acc))
            return tuple(out)

        def body(c, carry):
            off = pl.multiple_of(c * tk_exact, tk_exact)
            return step(lambda h: kl_ref[pl.ds(off, tk_exact), h * LANES:(h + 1) * LANES],
                        lambda g: vl_ref[pl.ds(off, tk_exact), g * pair_w:g * pair_w + LANES], carry)

        init = tuple((jnp.full((tq, 1), -jnp.inf, F32), jnp.zeros((tq, 1), F32), jnp.zeros((tq, LANES), F32))
                     for _ in range(hps))
        carry = lax.fori_loop(0, jnp.where(is_lat, seq // tk_exact, 0), body, init)
        carry = step(lambda h: kc_ref[:, h * LANES:(h + 1) * LANES],
                     lambda g: vc_ref[:, g * pair_w:g * pair_w + LANES], carry)
        write([c[2] for c in carry], [c[1] for c in carry])


def _mla_logit_bound(q_gain, k_gain):
    b = B_QK * MLA_LOGIT_SCALE * jnp.max(jnp.abs(q_gain)) * jnp.max(jnp.abs(k_gain))
    return (1.02 * b + 0.5).astype(F32)


def _mla(qb, kb, vb, safe, *, n_batch, seq, n_ctx, with_ctx):
    hps = 4
    tq = 256
    assert n_ctx == tq
    nq = seq // tq
    ctx_blk = n_batch * seq // n_ctx
    rows_out = n_batch * seq + (n_batch * n_ctx if with_ctx else 0)
    kw, vw = hps * LANES, hps * B_V

    def qmap(b, j, i, s):
        return (jnp.where(i < nq, b * nq + i, ctx_blk + b), j)

    return pl.pallas_call(
        functools.partial(_mla_kernel, nq=nq, tk_fast=min(2048, seq), tk_exact=512, hps=hps),
        grid_spec=pltpu.PrefetchScalarGridSpec(
            num_scalar_prefetch=1,
            grid=(n_batch, B_HEADS // hps, nq + (1 if with_ctx else 0)),
            in_specs=[pl.BlockSpec((tq, kw), qmap),
                      pl.BlockSpec((seq, kw), lambda b, j, i, s: (b, j)),
                      pl.BlockSpec((seq, kw), lambda b, j, i, s: (b, j)),
                      pl.BlockSpec((n_ctx, kw), lambda b, j, i, s: (ctx_blk + b, j)),
                      pl.BlockSpec((n_ctx, kw), lambda b, j, i, s: (ctx_blk + b, j))],
            out_specs=pl.BlockSpec((tq, vw), qmap)),
        out_shape=jax.ShapeDtypeStruct((rows_out, 512), BF16),
        compiler_params=_cp(("arbitrary", "arbitrary", "arbitrary"), VMEM_LIMIT), name="mixer_mla",
    )(safe, qb, kb, vb, kb, vb)


def _ret_kernel(qf_ref, kf_ref, vf_ref, qr_ref, kr_ref, vr_ref, ld_ref, of_ref, or_ref,
                s_ref, dec_ref, qw_ref, kw_ref, cd_ref):
    step = pl.program_id(1)
    cc = C_CHUNK
    ri = lax.broadcasted_iota(jnp.int32, (cc, cc), 0)
    ci = lax.broadcasted_iota(jnp.int32, (cc, cc), 1)
    low = ci < HEAD_DIM

    @pl.when(step == 0)
    def _():
        s_ref[...] = jnp.zeros_like(s_ref)
        pos = ri.astype(F32)
        for d in range(2):
            diff = (ri - ci) if d == 0 else (ci - ri)
            dpos = jnp.maximum(diff, 0).astype(F32)
            qpow = (pos + 1.0) if d == 0 else (cc - pos)
            kpow = (cc - 1.0 - pos) if d == 0 else pos
            for j in range(C_HEADS // 2):
                lg_e = -jnp.exp(ld_ref[d, 2 * j:2 * j + 1, :])
                lg_o = -jnp.exp(ld_ref[d, 2 * j + 1:2 * j + 2, :])
                lgl = jnp.where(low[0:1, :], lg_e, lg_o)
                dec_ref[d, j, :cc] = jnp.where(diff >= 0, jnp.exp(lg_e * dpos), 0.0)
                dec_ref[d, j, cc:] = jnp.where(diff >= 0, jnp.exp(lg_o * dpos), 0.0)
                qw_ref[d, j] = jnp.exp(lgl * qpow)
                kw_ref[d, j] = jnp.exp(lgl * kpow)
                cd_ref[d, j] = jnp.where(ri < HEAD_DIM, jnp.exp(lg_e * cc), jnp.exp(lg_o * cc))

    blockdiag = (ri < HEAD_DIM) == low
    for d, (q_ref, k_ref, v_ref, o_ref) in enumerate(((qf_ref, kf_ref, vf_ref, of_ref),
                                                      (qr_ref, kr_ref, vr_ref, or_ref))):
        for j in range(C_HEADS // 2):
            sl = slice(j * LANES, (j + 1) * LANES)
            q, k, v = q_ref[:, sl], k_ref[:, sl], v_ref[:, sl]
            zero = jnp.zeros_like(q)
            q2 = jnp.concatenate([jnp.where(low, q, zero), jnp.where(low, zero, q)], axis=0)
            sc = (_nt(q2, k) * dec_ref[d, j]).astype(BF16)
            o2 = jnp.dot(sc, v, preferred_element_type=F32)
            o_intra = jnp.where(low, o2[:cc], o2[cc:])
            state = s_ref[d, j]
            qw = (q.astype(F32) * qw_ref[d, j]).astype(BF16)
            o_cross = jnp.dot(qw, state.astype(BF16), preferred_element_type=F32)
            kw = (k.astype(F32) * kw_ref[d, j]).astype(BF16)
            kv = lax.dot_general(kw, v, (((0,), (0,)), ((), ())), preferred_element_type=F32)
            s_ref[d, j] = state * cd_ref[d, j] + jnp.where(blockdiag, kv, 0.0)
            o_ref[:, sl] = o_intra + o_cross


def _retention(prep, proj, ld_head, *, n_batch, seq, n_ctx):
    cc = C_CHUNK
    nl, nc = seq // cc, n_ctx // cc
    steps = nc + nl
    lat_blocks = n_batch * nl
    rows = n_batch * (seq + n_ctx)

    def rowblk(b, d, s):
        c_ctx = jnp.where(d == 0, s, nc - 1 - s)
        c_lat = jnp.where(d == 0, s - nc, nl - 1 - (s - nc))
        return jnp.where(s < nc, lat_blocks + b * nc + c_ctx, b * nl + c_lat)

    def spec(d, col):
        return pl.BlockSpec((cc, 512), lambda b, s: (rowblk(b, d, s), col))

    pairs = C_HEADS // 2
    return pl.pallas_call(
        _ret_kernel,
        grid=(n_batch, steps),
        in_specs=[spec(d, col) for d in (0, 1) for col in (P_CQ // 512, P_CK // 512, P_CV // 512)]
        + [pl.BlockSpec((2, C_HEADS, LANES), lambda b, s: (0, 0, 0))],
        out_specs=[spec(0, 0), spec(1, 0)],
        out_shape=[jax.ShapeDtypeStruct((rows, 512), F32)] * 2,
        scratch_shapes=[pltpu.VMEM((2, pairs, LANES, LANES), F32),
                        pltpu.VMEM((2, pairs, 2 * cc, cc), F32),
                        pltpu.VMEM((2, pairs, cc, LANES), F32),
                        pltpu.VMEM((2, pairs, cc, LANES), F32),
                        pltpu.VMEM((2, pairs, LANES, LANES), F32)],
        compiler_params=_cp(("arbitrary", "arbitrary"), VMEM_LIMIT), name="mixer_retention",
    )(prep, prep, proj, prep, prep, proj, ld_head)


def _retfin_kernel(of_ref, or_ref, g_ref, gm_ref, out_ref):
    o = of_ref[...] + or_ref[...]
    on = o * lax.rsqrt(_group_meansq(o, gm_ref, 512) + EPS)
    g = g_ref[...].astype(F32)
    out_ref[...] = (g * jax.nn.sigmoid(g) * on).astype(out_ref.dtype)


def _retention_finish(o_fwd, o_rev, proj, gmat, *, rows):
    tr = 256
    return pl.pallas_call(
        _retfin_kernel,
        grid=(rows // tr,),
        in_specs=[pl.BlockSpec((tr, 512), lambda i: (i, 0)),
                  pl.BlockSpec((tr, 512), lambda i: (i, 0)),
                  pl.BlockSpec((tr, 512), lambda i: (i, P_CG // 512)),
                  pl.BlockSpec((512, 512), lambda i: (0, 0))],
        out_specs=pl.BlockSpec((tr, 512), lambda i: (i, 0)),
        out_shape=jax.ShapeDtypeStruct((rows, 512), BF16),
        compiler_params=_cp(("arbitrary",), VMEM_LIMIT), name="retention_finish",
    )(o_fwd, o_rev, proj, gmat)


def _nbr_kernel(q_ref, k_ref, v_ref, kc_ref, vc_ref, abm_ref, o_ref, *, rows):
    st = pl.program_id(1)
    start = jnp.clip(NA_QROWS * st - NA_ROWS // 2, 0, rows - NA_KROWS)
    off = pl.multiple_of(start * GRID_W, GRID_W)
    nk = NA_KROWS * GRID_W
    tq = q_ref.shape[0]
    lane = lax.broadcasted_iota(jnp.int32, (tq, LANES), 1)
    low = lane < HEAD_DIM
    for j in range(D_HEADS // 2):
        sl = slice(j * LANES, (j + 1) * LANES)
        qc = q_ref[:, sl]
        kw = k_ref[pl.ds(off, nk), sl]
        vw = v_ref[pl.ds(off, nk), sl]
        kc, vc = kc_ref[:, sl], vc_ref[:, sl]
        outs = []
        for e in range(2):
            qm = jnp.where(low if e == 0 else jnp.logical_not(low), qc, jnp.zeros_like(qc))
            s1 = _nt(qm, kw) + abm_ref[0, 2 * j + e]
            s2 = _nt(qm, kc)
            m = jnp.maximum(jnp.max(s1, axis=-1, keepdims=True), jnp.max(s2, axis=-1, keepdims=True))
            p1 = jnp.exp(s1 - m)
            p2 = jnp.exp(s2 - m)
            den = jnp.sum(p1, axis=-1, keepdims=True) + jnp.sum(p2, axis=-1, keepdims=True)
            o = (jnp.dot(p1.astype(BF16), vw, preferred_element_type=F32)
                 + jnp.dot(p2.astype(BF16), vc, preferred_element_type=F32))
            outs.append(o / den)
        o_ref[:, sl] = jnp.where(low, outs[0], outs[1]).astype(o_ref.dtype)


def _nbr(prep, proj, abm, *, n_batch, seq, n_ctx, with_ctx):
    rows = seq // GRID_W
    tq = NA_QROWS * GRID_W
    assert n_ctx == tq and rows % NA_QROWS == 0 and rows >= NA_KROWS + NA_QROWS
    nst = rows // NA_QROWS
    steps = nst + (1 if with_ctx else 0)
    rows_out = n_batch * seq + (n_batch * n_ctx if with_ctx else 0)

    def qmap(col):
        def f(b, s):
            return (jnp.where(s < nst, b * nst + s, n_batch * nst + b), col)
        return f

    def case(b, s):
        c = jnp.where(s == 0, 0, jnp.where(s == nst - 1, 2, jnp.where(s == nst, 3, 1)))
        return (c, 0, 0, 0)

    ctx_blk = n_batch * seq // n_ctx
    return pl.pallas_call(
        functools.partial(_nbr_kernel, rows=rows),
        grid=(n_batch, steps),
        in_specs=[pl.BlockSpec((tq, 512), qmap(P_DQ // 512)),
                  pl.BlockSpec((seq, 512), lambda b, s: (b, P_DK // 512)),
                  pl.BlockSpec((seq, 512), lambda b, s: (b, P_DV // 512)),
                  pl.BlockSpec((n_ctx, 512), lambda b, s: (ctx_blk + b, P_DK // 512)),
                  pl.BlockSpec((n_ctx, 512), lambda b, s: (ctx_blk + b, P_DV // 512)),
                  pl.BlockSpec((1, D_HEADS, tq, NA_KROWS * GRID_W), case)],
        out_specs=pl.BlockSpec((tq, 512), qmap(0)),
        out_shape=jax.ShapeDtypeStruct((rows_out, 512), BF16),
        compiler_params=_cp(("arbitrary", "arbitrary"), VMEM_LIMIT), name="mixer_neighbourhood",
    )(prep, prep, proj, prep, proj, abm)


def _nbr_bias_tables(rpb, rows):
    w = GRID_W
    cidx = np.arange(w)
    col_start = np.clip(cidx - NA_COLS // 2, 0, w - NA_COLS)
    col_ok = (cidx[None, :] >= col_start[:, None]) & (cidx[None, :] < col_start[:, None] + NA_COLS)
    d_col = np.clip(cidx[None, :] - cidx[:, None] + (NA_COLS - 1), 0, 2 * NA_COLS - 2)
    n_heads, n_dr, n_dc = rpb.shape
    onehot = jnp.asarray((d_col[:, :, None] == np.arange(n_dc)[None, None, :]).astype(np.float32))
    t = jnp.einsum("hrd,qkd->hqrk", rpb.astype(F32), onehot, precision=lax.Precision.HIGHEST)
    t = jnp.where(jnp.asarray(col_ok)[None, :, None, :], t, NEG_INF).reshape(n_heads, w, n_dr * w)
    cases = []
    for r0 in (0, NA_QROWS, rows - NA_QROWS):
        start = int(np.clip(r0 - NA_ROWS // 2, 0, rows - NA_KROWS))
        blocks = []
        for i in range(NA_QROWS):
            r = r0 + i
            row_start = int(np.clip(r - NA_ROWS // 2, 0, rows - NA_ROWS))
            kk0 = row_start - start
            dr0 = row_start - r + (NA_ROWS - 1)
            blk = t[:, :, dr0 * w:(dr0 + NA_ROWS) * w]
            blocks.append(jnp.pad(blk, ((0, 0), (0, 0), (kk0 * w, (NA_KROWS - NA_ROWS - kk0) * w)),
                                  constant_values=NEG_INF))
        cases.append(jnp.concatenate(blocks, axis=1))
    cases.append(jnp.full((n_heads, NA_QROWS * w, NA_KROWS * w), NEG_INF, F32))
    return jnp.stack(cases)


def _swiglu_accumulate(x, wg_ref, wu_ref, wd_ref, acc_ref):
    g = jnp.dot(x, wg_ref[0].astype(BF16), preferred_element_type=F32)
    u = jnp.dot(x, wu_ref[0].astype(BF16), preferred_element_type=F32)
    a = (g * jax.nn.sigmoid(g) * u).astype(BF16)
    acc_ref[...] += jnp.dot(a, wd_ref[0].astype(BF16), preferred_element_type=F32)


def _ffn_kernel(x_ref, wg_ref, wu_ref, wd_ref, o_ref, acc_ref):
    j = pl.program_id(1)

    @pl.when(j == 0)
    def _():
        acc_ref[...] = jnp.zeros_like(acc_ref)

    _swiglu_accumulate(x_ref[...], wg_ref, wu_ref, wd_ref, acc_ref)

    @pl.when(j == pl.num_programs(1) - 1)
    def _():
        o_ref[...] = acc_ref[...].astype(o_ref.dtype)


def _ffn(x, w_gate_up, w_down, layer, *, rows, tm, fc):
    _, d, ff2 = w_gate_up.shape
    nj = ff2 // 2 // fc
    return pl.pallas_call(
        _ffn_kernel,
        grid=(rows // tm, nj),
        in_specs=[pl.BlockSpec((tm, d), lambda i, j: (i, 0)),
                  pl.BlockSpec((1, d, fc), lambda i, j: (layer, 0, j)),
                  pl.BlockSpec((1, d, fc), lambda i, j: (layer, 0, nj + j)),
                  pl.BlockSpec((1, fc, d), lambda i, j: (layer, j, 0))],
        out_specs=pl.BlockSpec((tm, d), lambda i, j: (i, 0)),
        out_shape=jax.ShapeDtypeStruct((rows, d), BF16),
        scratch_shapes=[pltpu.VMEM((tm, d), F32)],
        compiler_params=_cp(("arbitrary", "arbitrary"), VMEM_LIMIT), name="swiglu",
    )(x, w_gate_up, w_gate_up, w_down)


def _moe_ffn_kernel(te_ref, tv_ref, tok_ref, hp_ref, wg_ref, wu_ref, wd_ref, o_ref, gbuf_ref, x_ref, acc_ref, sem,
                    *, rows_per_step):
    i, j = pl.program_id(0), pl.program_id(1)
    n_tiles = pl.num_programs(0)
    tm, d = x_ref.shape
    half = d // 2
    valid = tv_ref[i] > 0
    slot = i % 2

    def issue(tile, first_row, n_rows, to_slot):
        def body(r, carry):
            row = first_row + r
            pltpu.make_async_copy(hp_ref.at[pl.ds(tok_ref[tile * tm + row], 1), :],
                                  gbuf_ref.at[to_slot].at[pl.ds(row, 1), :], sem.at[to_slot]).start()
            return carry

        lax.fori_loop(0, n_rows, body, 0, unroll=8)

    @pl.when((i == 0) & (j == 0) & valid)
    def _():
        issue(0, 0, tm, 0)

    @pl.when((j == 0) & valid)
    def _():
        pltpu.make_async_copy(hp_ref.at[pl.ds(0, tm), :], gbuf_ref.at[slot], sem.at[slot]).wait()
        lo, hi = _unpack_halves(gbuf_ref[slot])
        x_ref[:, :half] = lo.astype(BF16)
        x_ref[:, half:] = hi.astype(BF16)
        acc_ref[...] = jnp.zeros_like(acc_ref)

    nxt = jnp.minimum(i + 1, n_tiles - 1)

    @pl.when((i + 1 < n_tiles) & (tv_ref[nxt] > 0) & (j * rows_per_step < tm))
    def _():
        issue(nxt, j * rows_per_step, rows_per_step, 1 - slot)

    @pl.when(valid)
    def _():
        _swiglu_accumulate(x_ref[...], wg_ref, wu_ref, wd_ref, acc_ref)

    @pl.when(j == pl.num_programs(1) - 1)
    def _():
        o_ref[...] = jnp.where(valid, _pack_halves(acc_ref[...]), jnp.uint32(0))


def _moe_ffn(hp, w_gate_up, w_down, layer, tile_expert, tile_valid, row_token, *, tm, fc):
    _, _, d, ff2 = w_gate_up.shape
    nj = ff2 // 2 // fc
    n_rows = row_token.shape[0]
    issue_steps = 1 << (nj.bit_length() - 1)

    def jeff(i, j, tv):
        return jnp.where(tv[i] > 0, j, nj - 1)

    return pl.pallas_call(
        functools.partial(_moe_ffn_kernel, rows_per_step=tm // issue_steps),
        grid_spec=pltpu.PrefetchScalarGridSpec(
            num_scalar_prefetch=3, grid=(n_rows // tm, nj),
            in_specs=[pl.BlockSpec(memory_space=pl.ANY),
                      pl.BlockSpec((None, 1, d, fc), lambda i, j, te, tv, tok: (layer, te[i], 0, jeff(i, j, tv))),
                      pl.BlockSpec((None, 1, d, fc),
                                   lambda i, j, te, tv, tok: (layer, te[i], 0, nj + jeff(i, j, tv))),
                      pl.BlockSpec((None, 1, fc, d), lambda i, j, te, tv, tok: (layer, te[i], jeff(i, j, tv), 0))],
            out_specs=pl.BlockSpec((tm, d // 2), lambda i, j, te, tv, tok: (i, 0)),
            scratch_shapes=[pltpu.VMEM((2, tm, d // 2), jnp.uint32), pltpu.VMEM((tm, d), BF16),
                            pltpu.VMEM((tm, d), F32), pltpu.SemaphoreType.DMA((2,))]),
        out_shape=jax.ShapeDtypeStruct((n_rows, d // 2), jnp.uint32),
        compiler_params=_cp(("arbitrary", "arbitrary"), VMEM_LIMIT_MOE), name="moe_swiglu",
    )(tile_expert, tile_valid, row_token, hp, w_gate_up, w_gate_up, w_down)


def _combine_kernel(p0_ref, p1_ref, y_ref, x_ref, route_ref, gate_ref, o_ref, b0_ref, b1_ref, sem):
    n = b0_ref.shape[0]
    half = b0_ref.shape[1]
    base = pl.program_id(0) * n

    def issue(r, carry):
        pltpu.make_async_copy(y_ref.at[pl.ds(p0_ref[base + r], 1), :], b0_ref.at[pl.ds(r, 1), :], sem.at[0]).start()
        pltpu.make_async_copy(y_ref.at[pl.ds(p1_ref[base + r], 1), :], b1_ref.at[pl.ds(r, 1), :], sem.at[1]).start()
        return carry

    lax.fori_loop(0, n, issue, 0, unroll=8)
    pltpu.make_async_copy(y_ref.at[pl.ds(0, n), :], b0_ref, sem.at[0]).wait()
    pltpu.make_async_copy(y_ref.at[pl.ds(0, n), :], b1_ref, sem.at[1]).wait()
    route = route_ref[...]
    w0, w1 = route[:, 2:3], route[:, 3:4]
    lo0, hi0 = _unpack_halves(b0_ref[...])
    lo1, hi1 = _unpack_halves(b1_ref[...])
    gate = gate_ref[0, 0]
    o_ref[:, :half] = x_ref[:, :half] + gate[:, :half] * (w0 * lo0 + w1 * lo1)
    o_ref[:, half:] = x_ref[:, half:] + gate[:, half:] * (w0 * hi0 + w1 * hi1)


def _combine(y, x, route, mod, gate_k, pos0, pos1, *, seq, n_batch):
    t, d = x.shape
    tr = 256
    return pl.pallas_call(
        _combine_kernel,
        grid_spec=pltpu.PrefetchScalarGridSpec(
            num_scalar_prefetch=2, grid=(t // tr,),
            in_specs=[pl.BlockSpec(memory_space=pl.ANY),
                      pl.BlockSpec((tr, d), lambda i, a, b: (i, 0)),
                      pl.BlockSpec((tr, LANES), lambda i, a, b: (i, 0)),
                      pl.BlockSpec((1, 1, 1, d),
                                   lambda i, a, b: (jnp.minimum((i * tr) // seq, n_batch), gate_k, 0, 0))],
            out_specs=pl.BlockSpec((tr, d), lambda i, a, b: (i, 0)),
            scratch_shapes=[pltpu.VMEM((tr, d // 2), jnp.uint32), pltpu.VMEM((tr, d // 2), jnp.uint32),
                            pltpu.SemaphoreType.DMA((2,))]),
        out_shape=jax.ShapeDtypeStruct((t, d), F32),
        compiler_params=_cp(("arbitrary",), VMEM_LIMIT), name="moe_combine",
    )(pos0, pos1, y, x, route, _mod_view(mod))


def _route_meta(idx, n_exp, tm):
    t = idx.shape[0]
    flat = idx.reshape(-1)
    onehot = (flat[:, None] == jnp.arange(n_exp, dtype=jnp.int32)[None, :]).astype(jnp.int32)
    counts = jnp.sum(onehot, axis=0)
    rank = jnp.take_along_axis(jnp.cumsum(onehot, axis=0) - onehot, flat[:, None], axis=1)[:, 0]
    padded = ((counts + tm - 1) // tm) * tm
    ends = jnp.cumsum(padded)
    starts = ends - padded
    pos = starts[flat] + rank
    n_rows = TOP_K * t + n_exp * tm
    row_token = jnp.zeros((n_rows,), jnp.int32).at[pos].set(jnp.arange(TOP_K * t, dtype=jnp.int32) // TOP_K)
    tile_start = jnp.arange(n_rows // tm, dtype=jnp.int32) * tm
    tile_valid = (tile_start < ends[-1]).astype(jnp.int32)
    tile_expert = jnp.sum((ends[None, :] <= tile_start[:, None]).astype(jnp.int32), axis=1)
    tile_expert = jnp.minimum(tile_expert, n_exp - 1)
    last_valid = jnp.maximum(ends[-1] // tm - 1, 0)
    tile_expert = jnp.where(tile_valid > 0, tile_expert, tile_expert[last_valid])
    pos = pos.reshape(t, TOP_K).astype(jnp.int32)
    return row_token, tile_expert, tile_valid, pos[:, 0], pos[:, 1]


def _rope_tables(seq, pad_rows):
    t = jnp.arange(seq, dtype=jnp.int32)
    row = (t // GRID_W).astype(F32)
    col = (t % GRID_W).astype(F32)

    def angles(rot_dim):
        n_freq = rot_dim // 4
        freqs = ROPE_THETA ** (-jnp.arange(n_freq, dtype=F32) / n_freq)
        ang = jnp.concatenate([row[:, None] * freqs[None, :], col[:, None] * freqs[None, :]], axis=-1)
        return jnp.cos(ang), jnp.sin(ang)

    ch, sh = angles(HEAD_DIM)
    cos_h = jnp.concatenate([ch] * 4, axis=1)
    sin_h = jnp.concatenate([-sh, sh, -sh, sh], axis=1)
    cb, sb = angles(B_ROPE)
    one = jnp.ones((seq, B_NOPE), F32)
    zero = jnp.zeros((seq, B_NOPE), F32)
    tail1 = jnp.ones((seq, LANES - B_QK), F32)
    tail0 = jnp.zeros((seq, LANES - B_QK), F32)
    cos_b = jnp.concatenate([one, cb, cb, tail1], axis=1)
    sin_b = jnp.concatenate([zero, -sb, sb, tail0], axis=1)

    def pad(tab, fill):
        return jnp.concatenate([tab, jnp.full((pad_rows, LANES), fill, F32)], axis=0)

    return pad(cos_h, 1.0), pad(sin_h, 0.0), pad(cos_b, 1.0), pad(sin_b, 0.0)


def _permute_w_in(w):
    d = w.shape[0]
    o = np.cumsum([0, 512, 128, 128, B_Q_RANK, B_KV_RANK, B_ROPE, 512, 512, 512, 512, 512, 512, 512])
    aq, ak, av, bq, bkv, bkr, cq, ck, cv, cg, dq, dk, dv = [w[:, o[i]:o[i + 1]] for i in range(13)]

    def dup(m):
        return jnp.concatenate([m[:, :64], m[:, :64], m[:, 64:], m[:, 64:]], axis=1)

    z = lambda n: jnp.zeros((d, n), w.dtype)
    parts = [aq, cq, ck, dq, dk, dup(ak), dup(av), cv, cg, dv, bq, bkv,
             z(B_NOPE), bkr, z(LANES - B_QK), z(P_END - P_BKR - LANES)]
    return jnp.concatenate(parts, axis=1).astype(BF16)


def _block_diag_ones():
    i = np.arange(512) // HEAD_DIM
    return jnp.asarray((i[:, None] == i[None, :]).astype(np.float32), dtype=BF16)


def _pad_lanes(v, width):
    return jnp.concatenate([v.astype(F32), jnp.zeros((width - v.shape[0],), F32)])


def kernel(x, c, ctx, c_ctx, w_ada, b_ada, norm_mix, norm_ffn, w_in, w_out, a_q_norm, a_k_norm, a_sink,
           b_q_a_norm, b_kv_a_norm, b_w_uq, b_w_ukv, b_q_norm, b_k_norm, c_log_decay, d_q_norm, d_k_norm,
           d_rpb, ffn_w_gate_up, ffn_w_down, moe_router, moe_w_gate_up, moe_w_down):
    n_batch, seq, d = x.shape
    n_ctx = ctx.shape[1]
    depth = w_ada.shape[0]
    n_lat = n_batch * seq
    n_all = n_lat + n_batch * n_ctx
    grid_rows = seq // GRID_W
    tm = next(t for t in (1024, 512, 256) if n_lat % t == 0 and n_all % t == 0)

    cc = jnp.concatenate([c, c_ctx[None, :], jnp.zeros((8 - n_batch - 1, d), F32)], axis=0)
    mod = _ada(cc, w_ada, b_ada)
    xs = (x.reshape(n_lat, d), ctx.reshape(n_batch * n_ctx, d))
    cos_h, sin_h, cos_b, sin_b = _rope_tables(seq, 256)
    gmat = _block_diag_ones()

    y = None
    y_gate, y_mod = 0, None
    for l in range(depth):
        last = l == depth - 1
        mod_l = mod[l]
        if y is None:
            (h,) = _resnorm(xs, None, None, 0, mod_l, norm_mix[l], 0, 1, rows=n_all, seq=seq, n_batch=n_batch)
        else:
            xs, h = _resnorm(xs, y, y_mod, y_gate, mod_l, norm_mix[l], 0, 1, rows=n_all, seq=seq,
                             n_batch=n_batch)
        proj = _mm([h], _permute_w_in(w_in[l]), rows=n_all, tm=tm, tn=768, out_dtype=BF16, name="proj_in")
        gains = jnp.stack([jnp.tile(a_q_norm[l], 8), _pad_lanes(jnp.tile(a_k_norm[l], 4), 512),
                           jnp.tile(d_q_norm[l], 8), jnp.tile(d_k_norm[l], 8)]).astype(F32)
        prep = _prep(proj, cos_h, sin_h, gmat, gains, rows=n_all, seq=seq, n_lat=n_lat)
        wuq = jnp.pad(b_w_uq[l].reshape(B_Q_RANK, B_HEADS, B_QK),
                      ((0, 0), (0, 0), (0, LANES - B_QK))).reshape(B_Q_RANK, B_HEADS * LANES).astype(BF16)
        wukv3 = b_w_ukv[l].reshape(B_KV_RANK, B_HEADS, B_NOPE + B_V)
        wukv = jnp.concatenate(
            [jnp.pad(wukv3[:, :, :B_NOPE], ((0, 0), (0, 0), (0, LANES - B_NOPE))).reshape(B_KV_RANK, -1),
             wukv3[:, :, B_NOPE:].reshape(B_KV_RANK, -1)], axis=1).astype(BF16)
        bound = _mla_logit_bound(b_q_norm[l], b_k_norm[l])
        mla_safe = (bound <= MLA_SAFE_BOUND).astype(jnp.int32).reshape(1)
        gains_b = jnp.stack([b_q_a_norm[l].astype(F32), _pad_lanes(b_kv_a_norm[l], B_Q_RANK),
                             _pad_lanes(b_q_norm[l], B_Q_RANK), _pad_lanes(b_k_norm[l], B_Q_RANK),
                             jnp.full((B_Q_RANK,), -1.0, F32) * bound] + [jnp.zeros((B_Q_RANK,), F32)] * 3)
        qb, kb, vb = _prepb(proj, cos_b, sin_b, wuq, wukv, gains_b, rows=n_all, seq=seq, n_lat=n_lat)

        oa = _wina(prep, proj, a_sink[l].astype(F32), n_batch=n_batch, seq=seq, n_ctx=n_ctx, with_ctx=not last)
        ob = _mla(qb, kb, vb, mla_safe, n_batch=n_batch, seq=seq, n_ctx=n_ctx, with_ctx=not last)
        ld_head = jnp.broadcast_to(c_log_decay[l].astype(F32)[:, :, None], (2, C_HEADS, LANES))
        oc_fwd, oc_rev = _retention(prep, proj, ld_head, n_batch=n_batch, seq=seq, n_ctx=n_ctx)
        abm = _nbr_bias_tables(d_rpb[l], grid_rows)
        od = _nbr(prep, proj, abm, n_batch=n_batch, seq=seq, n_ctx=n_ctx, with_ctx=not last)
        rows_l = n_lat if last else n_all
        oc = _retention_finish(oc_fwd, oc_rev, proj, gmat, rows=rows_l)
        ymix = _mm([oa, ob, oc, od], w_out[l].astype(BF16), rows=rows_l, tm=tm, tn=1024, out_dtype=BF16,
                   name="proj_out")
        i = l // 2
        if l % 2 == 0:
            xs, h2 = _resnorm(xs, ymix, mod_l, 2, mod_l, norm_ffn[l], 3, 4, rows=rows_l, seq=seq,
                              n_batch=n_batch)
            y = _ffn(h2, ffn_w_gate_up, ffn_w_down, i, rows=rows_l, tm=tm, fc=512)
            y_gate, y_mod = 5, mod_l
        else:
            xs, hp, route = _resnorm(xs, ymix, mod_l, 2, mod_l, norm_ffn[l], 3, 4, rows=rows_l, seq=seq,
                                     n_batch=n_batch, router=moe_router[i])
            top_idx = route[:, :TOP_K].astype(jnp.int32)
            row_token, tile_expert, tile_valid, pos0, pos1 = _route_meta(top_idx, moe_router.shape[2], tm)
            yg = _moe_ffn(hp, moe_w_gate_up, moe_w_down, i, tile_expert, tile_valid, row_token, tm=tm, fc=512)
            xs = _combine(yg, xs, route, mod_l, 5, pos0, pos1, seq=seq, n_batch=n_batch)
            y = None
    if y is not None:
        xs, _ = _resnorm(xs, y, y_mod, y_gate, y_mod, norm_ffn[depth - 1], 3, 4, rows=xs.shape[0], seq=seq,
                         n_batch=n_batch)
    return xs[:n_lat].reshape(n_batch, seq, d)
```

```python
import functools

import numpy as np
import jax
import jax.numpy as jnp
from jax import lax
from jax.experimental import pallas as pl
from jax.experimental.pallas import tpu as pltpu

F32 = jnp.float32
BF16 = jnp.bfloat16

GRID_W = 64
HEAD_DIM = 64
ROPE_THETA = 10000.0
EPS = 1e-6
NEG_INF = -1e30

A_HEADS = 8
A_KV_HEADS = 2
A_BLOCK = 128
B_HEADS = 8
B_Q_RANK = 384
B_KV_RANK = 128
B_NOPE = 64
B_ROPE = 32
B_QK = B_NOPE + B_ROPE
B_V = 64
C_HEADS = 8
C_CHUNK = 128
D_HEADS = 8
NA_ROWS = 8
NA_COLS = 16
TOP_K = 2

LANES = 128
NA_QROWS = 4
NA_KROWS = NA_ROWS + NA_QROWS
NA_CASES = 4

P_AQ, P_CQ, P_CK, P_DQ, P_DK, P_AK, P_AV, P_CV, P_CG, P_DV, P_BQ, P_BKV, P_BKR, P_END = (
    0, 512, 1024, 1536, 2048, 2560, 2816, 3072, 3584, 4096, 4608, 4992, 5120, 5376)
PREP_W = P_AV

VMEM_LIMIT = 56 * 1024 * 1024
VMEM_LIMIT_MOE = 61 * 1024 * 1024

LOG2E = float(np.log2(np.e))
MLA_LOGIT_SCALE = B_QK ** -0.5 * LOG2E
MLA_SAFE_BOUND = 40.0


def _cp(sem, vmem=None):
    return pltpu.CompilerParams(dimension_semantics=sem, vmem_limit_bytes=vmem)


def _nt(a, b):
    return lax.dot_general(a, b, (((1,), (1,)), ((), ())), preferred_element_type=F32)


_HI_HALF = 0xFFFF0000


def _pack_halves(x):
    n = x.shape[1] // 2
    bits = pltpu.bitcast(x.astype(BF16).astype(F32), jnp.uint32)
    return (bits[:, :n] >> 16) | (bits[:, n:] & jnp.uint32(_HI_HALF))


def _unpack_halves(w):
    return pltpu.bitcast(w << 16, F32), pltpu.bitcast(w & jnp.uint32(_HI_HALF), F32)


def _ada_kernel(c_ref, w_ref, b_ref, o_ref):
    c = c_ref[...]
    s = (c * jax.nn.sigmoid(c)).astype(BF16)
    o_ref[0] = jnp.dot(s, w_ref[0].astype(BF16), preferred_element_type=F32) + b_ref[0]


def _ada(cc, w_ada, b_ada):
    depth, d, n = w_ada.shape
    tn = 1024
    return pl.pallas_call(
        _ada_kernel,
        grid=(depth, n // tn),
        in_specs=[pl.BlockSpec((8, d), lambda l, j: (0, 0)),
                  pl.BlockSpec((1, d, tn), lambda l, j: (l, 0, j)),
                  pl.BlockSpec((1, 1, tn), lambda l, j: (l, 0, j))],
        out_specs=pl.BlockSpec((1, 8, tn), lambda l, j: (l, 0, j)),
        out_shape=jax.ShapeDtypeStruct((depth, 8, n), F32),
        compiler_params=_cp(("arbitrary", "arbitrary"), VMEM_LIMIT),
        name="ada",
    )(cc, w_ada, b_ada.reshape(depth, 1, n))


def _resnorm_kernel(*refs, has_res, with_router, n_exp, lat_tiles):
    refs = list(refs)
    x_ref = refs.pop(0)
    if lat_tiles is not None:
        xc_ref = refs.pop(0)
        x = jnp.where(pl.program_id(0) < lat_tiles, x_ref[...], xc_ref[...])
    else:
        x = x_ref[...]
    if has_res:
        y_ref = refs.pop(0)
        gate_ref = refs.pop(0)
    gain_ref, sh_ref, sc_ref = refs[:3]
    refs = refs[3:]
    if with_router:
        router_ref = refs.pop(0)
    if has_res:
        xo_ref = refs.pop(0)
    h_ref = refs.pop(0)
    if has_res:
        x = x + gate_ref[0, 0] * y_ref[...].astype(F32)
        xo_ref[...] = x
    ms = jnp.mean(x * x, axis=-1, keepdims=True)
    h = (x * lax.rsqrt(ms + EPS) * gain_ref[...]) * (1.0 + sc_ref[0, 0]) + sh_ref[0, 0]
    if with_router:
        (route_ref,) = refs
        h_ref[...] = _pack_halves(h)
        logits = jnp.dot(h, router_ref[...], preferred_element_type=F32, precision=lax.Precision.HIGHEST)
        lane = lax.broadcasted_iota(jnp.int32, logits.shape, 1)
        logits = jnp.where(lane < n_exp, logits, -jnp.inf)
        m1 = jnp.max(logits, axis=-1, keepdims=True)
        i1 = jnp.min(jnp.where(logits == m1, lane, LANES), axis=-1, keepdims=True)
        rest = jnp.where(lane == i1, -jnp.inf, logits)
        m2 = jnp.max(rest, axis=-1, keepdims=True)
        i2 = jnp.min(jnp.where(rest == m2, lane, LANES), axis=-1, keepdims=True)
        e2 = jnp.exp(m2 - m1)
        g1 = 1.0 / (1.0 + e2)
        g2 = e2 / (1.0 + e2)
        route = jnp.where(lane == 0, i1.astype(F32),
                          jnp.where(lane == 1, i2.astype(F32),
                                    jnp.where(lane == 2, g1, jnp.where(lane == 3, g2, 0.0))))
        route_ref[...] = route
    else:
        h_ref[...] = h.astype(h_ref.dtype)


def _mod_view(mod):
    return mod.reshape(mod.shape[0], 6, 1, mod.shape[1] // 6)


def _resnorm(x, y, gate_mod, gate_k, mod, gain, sh_k, sc_k, *, rows, seq, n_batch, router=None):
    tr = 256
    has_res = y is not None
    with_router = router is not None
    split = isinstance(x, tuple)
    d = x[0].shape[1] if split else x.shape[1]

    def grp(i):
        return jnp.minimum((i * tr) // seq, n_batch)

    def modspec(k):
        return pl.BlockSpec((1, 1, 1, d), lambda i: (grp(i), k, 0, 0))

    row = pl.BlockSpec((tr, d), lambda i: (i, 0))
    lat_tiles = None
    if split:
        lat_tiles = x[0].shape[0] // tr
        in_specs = [pl.BlockSpec((tr, d), lambda i: (jnp.minimum(i, lat_tiles - 1), 0)),
                    pl.BlockSpec((tr, d), lambda i: (jnp.maximum(i - lat_tiles, 0), 0))]
        args = list(x)
    else:
        in_specs = [row]
        args = [x]
    if has_res:
        in_specs += [row, modspec(gate_k)]
        args += [y, _mod_view(gate_mod)]
    in_specs += [pl.BlockSpec((1, d), lambda i: (0, 0)), modspec(sh_k), modspec(sc_k)]
    args += [gain.reshape(1, d).astype(F32), _mod_view(mod), _mod_view(mod)]
    out_shape, out_specs = [], []
    n_exp = 0
    if with_router:
        n_exp = router.shape[1]
        router = jnp.pad(router.astype(F32), ((0, 0), (0, LANES - n_exp)))
        in_specs.append(pl.BlockSpec(router.shape, lambda i: (0, 0)))
        args.append(router)
    if has_res:
        out_shape.append(jax.ShapeDtypeStruct((rows, d), F32))
        out_specs.append(row)
    if with_router:
        out_shape += [jax.ShapeDtypeStruct((rows, d // 2), jnp.uint32), jax.ShapeDtypeStruct((rows, LANES), F32)]
        out_specs += [pl.BlockSpec((tr, d // 2), lambda i: (i, 0)), pl.BlockSpec((tr, LANES), lambda i: (i, 0))]
    else:
        out_shape.append(jax.ShapeDtypeStruct((rows, d), BF16))
        out_specs.append(row)
    kern = functools.partial(_resnorm_kernel, has_res=has_res, with_router=with_router, n_exp=n_exp,
                             lat_tiles=lat_tiles)
    return pl.pallas_call(
        kern, grid=(rows // tr,), in_specs=in_specs, out_specs=out_specs, out_shape=out_shape,
        compiler_params=_cp(("arbitrary",), VMEM_LIMIT), name="resnorm",
    )(*args)


def _mm_kernel(*refs, n_x):
    w_ref, o_ref = refs[n_x], refs[n_x + 1]
    if n_x == 1:
        x = refs[0][...]
    else:
        x = jnp.concatenate([r[...] for r in refs[:n_x]], axis=1)
    o_ref[...] = jnp.dot(x, w_ref[...], preferred_element_type=F32).astype(o_ref.dtype)


def _mm(xs, w, *, rows, tm, tn, out_dtype, name):
    n = w.shape[1]
    in_specs = [pl.BlockSpec((tm, a.shape[1]), lambda i, j: (i, 0)) for a in xs]
    in_specs.append(pl.BlockSpec((w.shape[0], tn), lambda i, j: (0, j)))
    return pl.pallas_call(
        functools.partial(_mm_kernel, n_x=len(xs)),
        grid=(rows // tm, n // tn),
        in_specs=in_specs,
        out_specs=pl.BlockSpec((tm, tn), lambda i, j: (i, j)),
        out_shape=jax.ShapeDtypeStruct((rows, n), out_dtype),
        compiler_params=_cp(("arbitrary", "arbitrary"), VMEM_LIMIT), name=name,
    )(*xs, w)


def _group_meansq(x, g_ref, width):
    x2 = x * x
    hi = x2.astype(BF16)
    lo = (x2 - hi.astype(F32)).astype(BF16)
    g = g_ref[:width, :width]
    ss = jnp.dot(hi, g, preferred_element_type=F32) + jnp.dot(lo, g, preferred_element_type=F32)
    return ss * (1.0 / HEAD_DIM)


def _prep_kernel(p_ref, dv_ref, ch_ref, sh_ref, g_ref, gains_ref, o_ref, dv2_ref):
    tr = p_ref.shape[0]

    def seg(off, width):
        return p_ref[:, off:off + width].astype(F32)

    def norm(x, row, width):
        return x * lax.rsqrt(_group_meansq(x, g_ref, width) + EPS) * gains_ref[row:row + 1, :width]

    def rope(x, width):
        lane = lax.broadcasted_iota(jnp.int32, (tr, width), 1)
        first = (lane % HEAD_DIM) < (HEAD_DIM // 2)
        cos = jnp.concatenate([ch_ref[...]] * (width // LANES), axis=1)
        sin = jnp.concatenate([sh_ref[...]] * (width // LANES), axis=1)
        swapped = jnp.where(first, pltpu.roll(x, width - HEAD_DIM // 2, 1), pltpu.roll(x, HEAD_DIM // 2, 1))
        return x * cos + swapped * sin

    scale = HEAD_DIM ** -0.5
    scale2 = scale * LOG2E
    o_ref[:, P_AQ:P_AQ + 512] = (rope(norm(seg(P_AQ, 512), 0, 512), 512) * scale2).astype(BF16)
    o_ref[:, P_CQ:P_CQ + 512] = (rope(seg(P_CQ, 512), 512) * scale).astype(BF16)
    o_ref[:, P_CK:P_CK + 512] = rope(seg(P_CK, 512), 512).astype(BF16)
    o_ref[:, P_DQ:P_DQ + 512] = (norm(seg(P_DQ, 512), 2, 512) * scale2).astype(BF16)
    o_ref[:, P_DK:P_DK + 512] = norm(seg(P_DK, 512), 3, 512).astype(BF16)
    o_ref[:, P_AK:P_AK + 256] = rope(norm(seg(P_AK, 256), 1, 256), 256).astype(BF16)
    ones_blk = jnp.where(lax.broadcasted_iota(jnp.int32, (tr, LANES), 1) < HEAD_DIM, 1.0, 0.0).astype(BF16)
    for g in range(D_HEADS // 2):
        dv2_ref[:, 2 * g * LANES:(2 * g + 1) * LANES] = dv_ref[:, g * LANES:(g + 1) * LANES]
        dv2_ref[:, (2 * g + 1) * LANES:(2 * g + 2) * LANES] = ones_blk


def _prep(proj, cos_h, sin_h, gmat, gains, *, rows, seq, n_lat):
    tr = 256
    nlat = seq // tr

    def tab(i):
        return (jnp.where(i * tr < n_lat, i % nlat, nlat), 0)

    return pl.pallas_call(
        _prep_kernel,
        grid=(rows // tr,),
        in_specs=[pl.BlockSpec((tr, PREP_W), lambda i: (i, 0)),
                  pl.BlockSpec((tr, 512), lambda i: (i, P_DV // 512)),
                  pl.BlockSpec((tr, LANES), tab),
                  pl.BlockSpec((tr, LANES), tab),
                  pl.BlockSpec((512, 512), lambda i: (0, 0)),
                  pl.BlockSpec((4, 512), lambda i: (0, 0))],
        out_specs=[pl.BlockSpec((tr, PREP_W), lambda i: (i, 0)),
                   pl.BlockSpec((tr, 2 * 512), lambda i: (i, 0))],
        out_shape=[jax.ShapeDtypeStruct((rows, PREP_W), BF16), jax.ShapeDtypeStruct((rows, 2 * 512), BF16)],
        compiler_params=_cp(("arbitrary",), VMEM_LIMIT), name="prep",
    )(proj, proj, cos_h, sin_h, gmat, gains)


def _prepb_kernel(bq_ref, bkv_ref, bkr_ref, cb_ref, sb_ref, wuq_ref, wukv_ref, gains_ref, q_ref, k_ref, v_ref):
    tr = bq_ref.shape[0]
    cq = bq_ref[...].astype(F32)
    cqn = cq * lax.rsqrt(jnp.mean(cq * cq, axis=-1, keepdims=True) + EPS) * gains_ref[0:1, :]
    qup = jnp.dot(cqn.astype(BF16), wuq_ref[...], preferred_element_type=F32)
    ckv = bkv_ref[...].astype(F32)
    ckvn = ckv * lax.rsqrt(jnp.mean(ckv * ckv, axis=-1, keepdims=True) + EPS) * gains_ref[1:2, :LANES]
    kvup = jnp.dot(ckvn.astype(BF16), wukv_ref[...], preferred_element_type=F32)
    kr = bkr_ref[...].astype(F32)
    cb, sb = cb_ref[...], sb_ref[...]
    lane = lax.broadcasted_iota(jnp.int32, (tr, LANES), 1)
    half = B_ROPE // 2
    m1 = (lane >= B_NOPE) & (lane < B_NOPE + half)
    m2 = (lane >= B_NOPE + half) & (lane < B_QK)

    def rope(x):
        swapped = jnp.where(m1, pltpu.roll(x, LANES - half, 1), jnp.where(m2, pltpu.roll(x, half, 1), 0.0))
        return x * cb + swapped * sb

    def headnorm(x, row):
        ms = jnp.sum(x * x, axis=-1, keepdims=True) * (1.0 / B_QK)
        return x * lax.rsqrt(ms + EPS) * gains_ref[row:row + 1, :LANES]

    shift_lane = lane == B_QK
    neg_bound = gains_ref[4:5, :LANES]
    for h in range(B_HEADS):
        sl = slice(h * LANES, (h + 1) * LANES)
        q = rope(headnorm(qup[:, sl], 2)) * MLA_LOGIT_SCALE
        q_ref[:, sl] = jnp.where(shift_lane, 1.0, q).astype(BF16)
        k = rope(headnorm(kvup[:, sl] + kr, 3))
        k_ref[:, sl] = jnp.where(shift_lane, neg_bound, k).astype(BF16)
    ones_blk = jnp.where(lane < B_V, 1.0, 0.0).astype(BF16)
    for g in range(B_HEADS // 2):
        v_ref[:, 2 * g * LANES:(2 * g + 1) * LANES] = kvup[:, (B_HEADS + g) * LANES:(B_HEADS + g + 1) * LANES].astype(BF16)
        v_ref[:, (2 * g + 1) * LANES:(2 * g + 2) * LANES] = ones_blk


def _prepb(proj, cos_b, sin_b, wuq, wukv, gains, *, rows, seq, n_lat):
    tr = 256
    nlat = seq // tr

    def tab(i):
        return (jnp.where(i * tr < n_lat, i % nlat, nlat), 0)

    return pl.pallas_call(
        _prepb_kernel,
        grid=(rows // tr,),
        in_specs=[pl.BlockSpec((tr, B_Q_RANK), lambda i: (i, P_BQ // B_Q_RANK)),
                  pl.BlockSpec((tr, LANES), lambda i: (i, P_BKV // LANES)),
                  pl.BlockSpec((tr, LANES), lambda i: (i, P_BKR // LANES)),
                  pl.BlockSpec((tr, LANES), tab),
                  pl.BlockSpec((tr, LANES), tab),
                  pl.BlockSpec(wuq.shape, lambda i: (0, 0)),
                  pl.BlockSpec(wukv.shape, lambda i: (0, 0)),
                  pl.BlockSpec((8, B_Q_RANK), lambda i: (0, 0))],
        out_specs=[pl.BlockSpec((tr, B_HEADS * LANES), lambda i: (i, 0)),
                   pl.BlockSpec((tr, B_HEADS * LANES), lambda i: (i, 0)),
                   pl.BlockSpec((tr, B_HEADS * LANES), lambda i: (i, 0))],
        out_shape=[jax.ShapeDtypeStruct((rows, B_HEADS * LANES), BF16),
                   jax.ShapeDtypeStruct((rows, B_HEADS * LANES), BF16),
                   jax.ShapeDtypeStruct((rows, B_HEADS * LANES), BF16)],
        compiler_params=_cp(("arbitrary",), VMEM_LIMIT), name="prep_mla",
    )(proj, proj, proj, cos_b, sin_b, wuq, wukv, gains)


def _wina_kernel(par_ref, q_ref, kp_ref, ko_ref, kn_ref, kc_ref, vp_ref, vo_ref, vn_ref, vc_ref, o_ref,
                 *, nb, seq):
    n = pl.program_id(1)
    blk = A_BLOCK
    rep = A_HEADS // A_KV_HEADS
    n_ctx = kc_ref.shape[0]
    kcat = jnp.concatenate([kp_ref[...], ko_ref[...], kn_ref[...], kc_ref[...]], axis=0)
    vcat = jnp.concatenate([vp_ref[...], vo_ref[...], vn_ref[...], vc_ref[...]], axis=0)
    n_keys = 3 * blk + n_ctx
    qi = lax.broadcasted_iota(jnp.int32, (blk, n_keys), 0)
    kj = lax.broadcasted_iota(jnp.int32, (blk, n_keys), 1)
    band = kj - blk
    kpos = n * blk + band
    valid = ((jnp.abs(qi - band) <= blk) & (kpos >= 0) & (kpos < seq) & (n < nb)) | (kj >= 3 * blk)
    low = lax.broadcasted_iota(jnp.int32, (blk, LANES), 1) < HEAD_DIM
    low_k = lax.broadcasted_iota(jnp.int32, (n_keys, LANES), 1) < HEAD_DIM
    bound = par_ref[A_HEADS]
    safe = par_ref[A_HEADS + 1] > 0.0

    def attend(use_max):
        mask = jnp.where(valid, 0.0 if use_max else -bound, NEG_INF)
        mask = jnp.concatenate([mask] * rep, axis=0)
        for g in range(A_KV_HEADS):
            kg = kcat[:, g * LANES:(g + 1) * LANES]
            v1 = jnp.where(low_k, vcat[:, g * LANES:(g + 1) * LANES], jnp.ones((), BF16))
            qs, sinks = [], []
            for r in range(rep):
                h = g * rep + r
                qc = q_ref[:, (h // 2) * LANES:(h // 2 + 1) * LANES]
                qs.append(jnp.where(low if h % 2 == 0 else jnp.logical_not(low), qc, jnp.zeros_like(qc)))
                sinks.append(jnp.full((blk, 1), par_ref[h], F32))
            s = _nt(jnp.concatenate(qs, axis=0), kg) + mask
            sink = jnp.concatenate(sinks, axis=0)
            if use_max:
                m = jnp.maximum(jnp.max(s, axis=-1, keepdims=True), sink)
                s, sink = s - m, sink - m
            else:
                sink = sink - bound
            o = jnp.dot(jnp.exp2(s).astype(BF16), v1, preferred_element_type=F32)
            res = o / (o[:, HEAD_DIM:HEAD_DIM + 1] + jnp.exp2(sink))
            for c in range(rep // 2):
                even = res[(2 * c) * blk:(2 * c + 1) * blk]
                odd = pltpu.roll(res[(2 * c + 1) * blk:(2 * c + 2) * blk], HEAD_DIM, 1)
                j = (g * rep) // 2 + c
                o_ref[:, j * LANES:(j + 1) * LANES] = jnp.where(low, even, odd).astype(o_ref.dtype)

    @pl.when(safe)
    def _():
        attend(False)

    @pl.when(jnp.logical_not(safe))
    def _():
        attend(True)


def _wina(prep, proj, sink, *, n_batch, seq, n_ctx, with_ctx):
    blk = A_BLOCK
    nb = seq // blk
    ncb = n_ctx // blk
    steps = nb + (ncb if with_ctx else 0)
    lat_blocks = n_batch * nb
    rows_out = n_batch * seq + (n_batch * n_ctx if with_ctx else 0)

    def qmap(b, n, s):
        return (jnp.where(n < nb, b * nb + n, lat_blocks + b * ncb + (n - nb)), 0)

    def kmap(delta, col):
        def f(b, n, s):
            return (b * nb + jnp.clip(n + delta, 0, nb - 1), col)
        return f

    def cmap(col):
        def f(b, n, s):
            return (n_batch * seq // n_ctx + b, col)
        return f

    kcol, vcol = P_AK // 256, P_AV // 256
    in_specs = [pl.BlockSpec((blk, 512), qmap)]
    in_specs += [pl.BlockSpec((blk, 256), kmap(dl, kcol)) for dl in (-1, 0, 1)]
    in_specs += [pl.BlockSpec((n_ctx, 256), cmap(kcol))]
    in_specs += [pl.BlockSpec((blk, 256), kmap(dl, vcol)) for dl in (-1, 0, 1)]
    in_specs += [pl.BlockSpec((n_ctx, 256), cmap(vcol))]
    return pl.pallas_call(
        functools.partial(_wina_kernel, nb=nb, seq=seq),
        grid_spec=pltpu.PrefetchScalarGridSpec(
            num_scalar_prefetch=1, grid=(n_batch, steps), in_specs=in_specs,
            out_specs=pl.BlockSpec((blk, 512), qmap)),
        out_shape=jax.ShapeDtypeStruct((rows_out, 512), BF16),
        compiler_params=_cp(("arbitrary", "arbitrary"), VMEM_LIMIT), name="mixer_window",
    )(sink, prep, prep, prep, prep, prep, proj, proj, proj, proj)


def _mla_kernel(safe_ref, q_ref, kl_ref, vl_ref, kc_ref, vc_ref, o_ref, *, nq, tk_fast, tk_exact, hps):
    tq = q_ref.shape[0]
    seq = kl_ref.shape[0]
    is_lat = pl.program_id(2) < nq
    safe = safe_ref[0] > 0
    pair_w = 2 * LANES
    lane = lax.broadcasted_iota(jnp.int32, (tq, LANES), 1)

    def q(h):
        return q_ref[:, h * LANES:(h + 1) * LANES]

    def write(nums, dens):
        for g in range(hps // 2):
            o_ref[:, g * LANES:(g + 1) * LANES] = jnp.where(
                lane < B_V, nums[2 * g] / dens[2 * g], nums[2 * g + 1] / dens[2 * g + 1]).astype(o_ref.dtype)

    def fast(chunks):
        accs = [None] * hps
        for k_ref, v_ref, off, n in chunks:
            for h in range(hps):
                g = h // 2
                p = jnp.exp2(_nt(q(h), k_ref[off:off + n, h * LANES:(h + 1) * LANES])).astype(BF16)
                d = jnp.dot(p, v_ref[off:off + n, g * pair_w:(g + 1) * pair_w], preferred_element_type=F32)
                accs[h] = d if accs[h] is None else accs[h] + d
        write([a[:, :LANES] for a in accs], [a[:, LANES:LANES + 1] for a in accs])

    ctx_chunk = (kc_ref, vc_ref, 0, kc_ref.shape[0])

    @pl.when(safe & is_lat)
    def _():
        fast([(kl_ref, vl_ref, c * tk_fast, tk_fast) for c in range(seq // tk_fast)] + [ctx_chunk])

    @pl.when(safe & jnp.logical_not(is_lat))
    def _():
        fast([ctx_chunk])

    @pl.when(jnp.logical_not(safe))
    def _():
        def step(kf, vf, carry):
            out = []
            for h in range(hps):
                m, l, acc = carry[h]
                s = _nt(q(h), kf(h))
                m_new = jnp.maximum(m, jnp.max(s, axis=-1, keepdims=True))
                a = jnp.exp2(m - m_new)
                p = jnp.exp2(s - m_new)
                l = a * l + jnp.sum(p, axis=-1, keepdims=True)
                acc = a * acc + jnp.dot(p.astype(BF16), vf(h // 2), preferred_element_type=F32)
                out.append((m_new, l, acc))
            return tuple(out)

        def body(c, carry):
            off = pl.multiple_of(c * tk_exact, tk_exact)
            return step(lambda h: kl_ref[pl.ds(off, tk_exact), h * LANES:(h + 1) * LANES],
                        lambda g: vl_ref[pl.ds(off, tk_exact), g * pair_w:g * pair_w + LANES], carry)

        init = tuple((jnp.full((tq, 1), -jnp.inf, F32), jnp.zeros((tq, 1), F32), jnp.zeros((tq, LANES), F32))
                     for _ in range(hps))
        carry = lax.fori_loop(0, jnp.where(is_lat, seq // tk_exact, 0), body, init)
        carry = step(lambda h: kc_ref[:, h * LANES:(h + 1) * LANES],
                     lambda g: vc_ref[:, g * pair_w:g * pair_w + LANES], carry)
        write([c[2] for c in carry], [c[1] for c in carry])


def _mla_logit_bound(q_gain, k_gain):
    b = B_QK * MLA_LOGIT_SCALE * jnp.max(jnp.abs(q_gain)) * jnp.max(jnp.abs(k_gain))
    return (1.02 * b + 0.5).astype(F32)


def _mla(qb, kb, vb, safe, *, n_batch, seq, n_ctx, with_ctx):
    hps = 4
    tq = 256
    assert n_ctx == tq
    nq = seq // tq
    ctx_blk = n_batch * seq // n_ctx
    rows_out = n_batch * seq + (n_batch * n_ctx if with_ctx else 0)
    kw, vw = hps * LANES, hps * B_V

    def qmap(b, j, i, s):
        return (jnp.where(i < nq, b * nq + i, ctx_blk + b), j)

    return pl.pallas_call(
        functools.partial(_mla_kernel, nq=nq, tk_fast=min(2048, seq), tk_exact=512, hps=hps),
        grid_spec=pltpu.PrefetchScalarGridSpec(
            num_scalar_prefetch=1,
            grid=(n_batch, B_HEADS // hps, nq + (1 if with_ctx else 0)),
            in_specs=[pl.BlockSpec((tq, kw), qmap),
                      pl.BlockSpec((seq, kw), lambda b, j, i, s: (b, j)),
                      pl.BlockSpec((seq, kw), lambda b, j, i, s: (b, j)),
                      pl.BlockSpec((n_ctx, kw), lambda b, j, i, s: (ctx_blk + b, j)),
                      pl.BlockSpec((n_ctx, kw), lambda b, j, i, s: (ctx_blk + b, j))],
            out_specs=pl.BlockSpec((tq, vw), qmap)),
        out_shape=jax.ShapeDtypeStruct((rows_out, 512), BF16),
        compiler_params=_cp(("arbitrary", "arbitrary", "arbitrary"), VMEM_LIMIT), name="mixer_mla",
    )(safe, qb, kb, vb, kb, vb)


def _ret_kernel(qf_ref, kf_ref, vf_ref, qr_ref, kr_ref, vr_ref, ld_ref, of_ref, or_ref,
                s_ref, dec_ref, qw_ref, kw_ref, cd_ref):
    step = pl.program_id(1)
    cc = C_CHUNK
    ri = lax.broadcasted_iota(jnp.int32, (cc, cc), 0)
    ci = lax.broadcasted_iota(jnp.int32, (cc, cc), 1)
    low = ci < HEAD_DIM

    @pl.when(step == 0)
    def _():
        s_ref[...] = jnp.zeros_like(s_ref)
        pos = ri.astype(F32)
        for d in range(2):
            diff = (ri - ci) if d == 0 else (ci - ri)
            dpos = jnp.maximum(diff, 0).astype(F32)
            qpow = (pos + 1.0) if d == 0 else (cc - pos)
            kpow = (cc - 1.0 - pos) if d == 0 else pos
            for j in range(C_HEADS // 2):
                lg_e = -jnp.exp(ld_ref[d, 2 * j:2 * j + 1, :])
                lg_o = -jnp.exp(ld_ref[d, 2 * j + 1:2 * j + 2, :])
                lgl = jnp.where(low[0:1, :], lg_e, lg_o)
                dec_ref[d, j, :cc] = jnp.where(diff >= 0, jnp.exp(lg_e * dpos), 0.0)
                dec_ref[d, j, cc:] = jnp.where(diff >= 0, jnp.exp(lg_o * dpos), 0.0)
                qw_ref[d, j] = jnp.exp(lgl * qpow)
                kw_ref[d, j] = jnp.exp(lgl * kpow)
                cd_ref[d, j] = jnp.where(ri < HEAD_DIM, jnp.exp(lg_e * cc), jnp.exp(lg_o * cc))

    blockdiag = (ri < HEAD_DIM) == low
    for d, (q_ref, k_ref, v_ref, o_ref) in enumerate(((qf_ref, kf_ref, vf_ref, of_ref),
                                                      (qr_ref, kr_ref, vr_ref, or_ref))):
        for j in range(C_HEADS // 2):
            sl = slice(j * LANES, (j + 1) * LANES)
            q, k, v = q_ref[:, sl], k_ref[:, sl], v_ref[:, sl]
            zero = jnp.zeros_like(q)
            q2 = jnp.concatenate([jnp.where(low, q, zero), jnp.where(low, zero, q)], axis=0)
            sc = (_nt(q2, k) * dec_ref[d, j]).astype(BF16)
            o2 = jnp.dot(sc, v, preferred_element_type=F32)
            o_intra = jnp.where(low, o2[:cc], o2[cc:])
            state = s_ref[d, j]
            qw = (q.astype(F32) * qw_ref[d, j]).astype(BF16)
            o_cross = jnp.dot(qw, state.astype(BF16), preferred_element_type=F32)
            kw = (k.astype(F32) * kw_ref[d, j]).astype(BF16)
            kv = lax.dot_general(kw, v, (((0,), (0,)), ((), ())), preferred_element_type=F32)
            s_ref[d, j] = state * cd_ref[d, j] + jnp.where(blockdiag, kv, 0.0)
            o_ref[:, sl] = o_intra + o_cross


def _retention(prep, proj, ld_head, *, n_batch, seq, n_ctx):
    cc = C_CHUNK
    nl, nc = seq // cc, n_ctx // cc
    steps = nc + nl
    lat_blocks = n_batch * nl
    rows = n_batch * (seq + n_ctx)

    def rowblk(b, d, s):
        c_ctx = jnp.where(d == 0, s, nc - 1 - s)
        c_lat = jnp.where(d == 0, s - nc, nl - 1 - (s - nc))
        return jnp.where(s < nc, lat_blocks + b * nc + c_ctx, b * nl + c_lat)

    def spec(d, col):
        return pl.BlockSpec((cc, 512), lambda b, s: (rowblk(b, d, s), col))

    pairs = C_HEADS // 2
    return pl.pallas_call(
        _ret_kernel,
        grid=(n_batch, steps),
        in_specs=[spec(d, col) for d in (0, 1) for col in (P_CQ // 512, P_CK // 512, P_CV // 512)]
        + [pl.BlockSpec((2, C_HEADS, LANES), lambda b, s: (0, 0, 0))],
        out_specs=[spec(0, 0), spec(1, 0)],
        out_shape=[jax.ShapeDtypeStruct((rows, 512), F32)] * 2,
        scratch_shapes=[pltpu.VMEM((2, pairs, LANES, LANES), F32),
                        pltpu.VMEM((2, pairs, 2 * cc, cc), F32),
                        pltpu.VMEM((2, pairs, cc, LANES), F32),
                        pltpu.VMEM((2, pairs, cc, LANES), F32),
                        pltpu.VMEM((2, pairs, LANES, LANES), F32)],
        compiler_params=_cp(("arbitrary", "arbitrary"), VMEM_LIMIT), name="mixer_retention",
    )(prep, prep, proj, prep, prep, proj, ld_head)


def _retfin_kernel(of_ref, or_ref, g_ref, gm_ref, out_ref):
    o = of_ref[...] + or_ref[...]
    on = o * lax.rsqrt(_group_meansq(o, gm_ref, 512) + EPS)
    g = g_ref[...].astype(F32)
    out_ref[...] = (g * jax.nn.sigmoid(g) * on).astype(out_ref.dtype)


def _retention_finish(o_fwd, o_rev, proj, gmat, *, rows):
    tr = 256
    return pl.pallas_call(
        _retfin_kernel,
        grid=(rows // tr,),
        in_specs=[pl.BlockSpec((tr, 512), lambda i: (i, 0)),
                  pl.BlockSpec((tr, 512), lambda i: (i, 0)),
                  pl.BlockSpec((tr, 512), lambda i: (i, P_CG // 512)),
                  pl.BlockSpec((512, 512), lambda i: (0, 0))],
        out_specs=pl.BlockSpec((tr, 512), lambda i: (i, 0)),
        out_shape=jax.ShapeDtypeStruct((rows, 512), BF16),
        compiler_params=_cp(("arbitrary",), VMEM_LIMIT), name="retention_finish",
    )(o_fwd, o_rev, proj, gmat)


def _nbr_kernel(par_ref, q_ref, k_ref, v_ref, kc_ref, vc_ref, abm_ref, o_ref, *, rows):
    st = pl.program_id(1)
    start = jnp.clip(NA_QROWS * st - NA_ROWS // 2, 0, rows - NA_KROWS)
    off = pl.multiple_of(start * GRID_W, GRID_W)
    nk = NA_KROWS * GRID_W
    tq = q_ref.shape[0]
    low = lax.broadcasted_iota(jnp.int32, (tq, LANES), 1) < HEAD_DIM
    bound = par_ref[0]
    safe = par_ref[1] > 0.0
    pair_w = 2 * LANES

    def attend(use_max):
        for j in range(D_HEADS // 2):
            sl = slice(j * LANES, (j + 1) * LANES)
            sl2 = slice(j * pair_w, (j + 1) * pair_w)
            qc = q_ref[:, sl]
            zero = jnp.zeros_like(qc)
            q2 = jnp.concatenate([jnp.where(low, qc, zero), jnp.where(low, zero, qc)], axis=0)
            bias = jnp.concatenate([abm_ref[0, 2 * j], abm_ref[0, 2 * j + 1]], axis=0)
            s1 = _nt(q2, k_ref[pl.ds(off, nk), sl]) + bias
            s2 = _nt(q2, kc_ref[:, sl]) - bound
            if use_max:
                m = jnp.maximum(jnp.max(s1, axis=-1, keepdims=True), jnp.max(s2, axis=-1, keepdims=True))
                s1, s2 = s1 - m, s2 - m
            o = (jnp.dot(jnp.exp2(s1).astype(BF16), v_ref[pl.ds(off, nk), sl2], preferred_element_type=F32)
                 + jnp.dot(jnp.exp2(s2).astype(BF16), vc_ref[:, sl2], preferred_element_type=F32))
            res = o[:, :LANES] / o[:, LANES:LANES + 1]
            o_ref[:, sl] = jnp.where(low, res[:tq], res[tq:]).astype(o_ref.dtype)

    @pl.when(safe)
    def _():
        attend(False)

    @pl.when(jnp.logical_not(safe))
    def _():
        attend(True)


def _nbr(prep, dv2, abm, par, *, n_batch, seq, n_ctx, with_ctx):
    rows = seq // GRID_W
    tq = NA_QROWS * GRID_W
    assert n_ctx == tq and rows % NA_QROWS == 0 and rows >= NA_KROWS + NA_QROWS
    nst = rows // NA_QROWS
    steps = nst + (1 if with_ctx else 0)
    rows_out = n_batch * seq + (n_batch * n_ctx if with_ctx else 0)

    def qmap(col):
        def f(b, s, par):
            return (jnp.where(s < nst, b * nst + s, n_batch * nst + b), col)
        return f

    def case(b, s, par):
        c = jnp.where(s == 0, 0, jnp.where(s == nst - 1, 2, jnp.where(s == nst, 3, 1)))
        return (c, 0, 0, 0)

    ctx_blk = n_batch * seq // n_ctx
    return pl.pallas_call(
        functools.partial(_nbr_kernel, rows=rows),
        grid_spec=pltpu.PrefetchScalarGridSpec(
            num_scalar_prefetch=1, grid=(n_batch, steps),
            in_specs=[pl.BlockSpec((tq, 512), qmap(P_DQ // 512)),
                      pl.BlockSpec((seq, 512), lambda b, s, par: (b, P_DK // 512)),
                      pl.BlockSpec((seq, 2 * 512), lambda b, s, par: (b, 0)),
                      pl.BlockSpec((n_ctx, 512), lambda b, s, par: (ctx_blk + b, P_DK // 512)),
                      pl.BlockSpec((n_ctx, 2 * 512), lambda b, s, par: (ctx_blk + b, 0)),
                      pl.BlockSpec((1, D_HEADS, tq, NA_KROWS * GRID_W), case)],
            out_specs=pl.BlockSpec((tq, 512), qmap(0))),
        out_shape=jax.ShapeDtypeStruct((rows_out, 512), BF16),
        compiler_params=_cp(("arbitrary", "arbitrary"), VMEM_LIMIT), name="mixer_neighbourhood",
    )(par, prep, prep, dv2, prep, dv2, abm)


def _softmax_bound(q_gain, k_gain, extra):
    b = HEAD_DIM ** 0.5 * LOG2E * jnp.max(jnp.abs(q_gain)) * jnp.max(jnp.abs(k_gain)) + LOG2E * extra
    return (1.02 * b + 0.5).astype(F32)


def _nbr_bias_tables(rpb, rows, bound):
    w = GRID_W
    cidx = np.arange(w)
    col_start = np.clip(cidx - NA_COLS // 2, 0, w - NA_COLS)
    col_ok = (cidx[None, :] >= col_start[:, None]) & (cidx[None, :] < col_start[:, None] + NA_COLS)
    d_col = np.clip(cidx[None, :] - cidx[:, None] + (NA_COLS - 1), 0, 2 * NA_COLS - 2)
    n_heads, n_dr, n_dc = rpb.shape
    onehot = jnp.asarray((d_col[:, :, None] == np.arange(n_dc)[None, None, :]).astype(np.float32))
    t = jnp.einsum("hrd,qkd->hqrk", rpb.astype(F32), onehot, precision=lax.Precision.HIGHEST)
    t = t * LOG2E - bound
    t = jnp.where(jnp.asarray(col_ok)[None, :, None, :], t, NEG_INF).reshape(n_heads, w, n_dr * w)
    cases = []
    for r0 in (0, NA_QROWS, rows - NA_QROWS):
        start = int(np.clip(r0 - NA_ROWS // 2, 0, rows - NA_KROWS))
        blocks = []
        for i in range(NA_QROWS):
            r = r0 + i
            row_start = int(np.clip(r - NA_ROWS // 2, 0, rows - NA_ROWS))
            kk0 = row_start - start
            dr0 = row_start - r + (NA_ROWS - 1)
            blk = t[:, :, dr0 * w:(dr0 + NA_ROWS) * w]
            blocks.append(jnp.pad(blk, ((0, 0), (0, 0), (kk0 * w, (NA_KROWS - NA_ROWS - kk0) * w)),
                                  constant_values=NEG_INF))
        cases.append(jnp.concatenate(blocks, axis=1))
    cases.append(jnp.full((n_heads, NA_QROWS * w, NA_KROWS * w), NEG_INF, F32))
    return jnp.stack(cases)


def _swiglu_accumulate(x, wg_ref, wu_ref, wd_ref, acc_ref):
    g = jnp.dot(x, wg_ref[0].astype(BF16), preferred_element_type=F32)
    u = jnp.dot(x, wu_ref[0].astype(BF16), preferred_element_type=F32)
    a = (g * jax.nn.sigmoid(g) * u).astype(BF16)
    acc_ref[...] += jnp.dot(a, wd_ref[0].astype(BF16), preferred_element_type=F32)


def _ffn_kernel(x_ref, wg_ref, wu_ref, wd_ref, o_ref, acc_ref):
    j = pl.program_id(1)

    @pl.when(j == 0)
    def _():
        acc_ref[...] = jnp.zeros_like(acc_ref)

    _swiglu_accumulate(x_ref[...], wg_ref, wu_ref, wd_ref, acc_ref)

    @pl.when(j == pl.num_programs(1) - 1)
    def _():
        o_ref[...] = acc_ref[...].astype(o_ref.dtype)


def _ffn(x, w_gate_up, w_down, layer, *, rows, tm, fc):
    _, d, ff2 = w_gate_up.shape
    nj = ff2 // 2 // fc
    return pl.pallas_call(
        _ffn_kernel,
        grid=(rows // tm, nj),
        in_specs=[pl.BlockSpec((tm, d), lambda i, j: (i, 0)),
                  pl.BlockSpec((1, d, fc), lambda i, j: (layer, 0, j)),
                  pl.BlockSpec((1, d, fc), lambda i, j: (layer, 0, nj + j)),
                  pl.BlockSpec((1, fc, d), lambda i, j: (layer, j, 0))],
        out_specs=pl.BlockSpec((tm, d), lambda i, j: (i, 0)),
        out_shape=jax.ShapeDtypeStruct((rows, d), BF16),
        scratch_shapes=[pltpu.VMEM((tm, d), F32)],
        compiler_params=_cp(("arbitrary", "arbitrary"), VMEM_LIMIT), name="swiglu",
    )(x, w_gate_up, w_gate_up, w_down)


def _moe_ffn_kernel(te_ref, tv_ref, tok_ref, hp_ref, wg_ref, wu_ref, wd_ref, o_ref, gbuf_ref, x_ref, acc_ref, sem,
                    *, rows_per_step):
    i, j = pl.program_id(0), pl.program_id(1)
    n_tiles = pl.num_programs(0)
    tm, d = x_ref.shape
    half = d // 2
    valid = tv_ref[i] > 0
    slot = i % 2

    def issue(tile, first_row, n_rows, to_slot):
        def body(r, carry):
            row = first_row + r
            pltpu.make_async_copy(hp_ref.at[pl.ds(tok_ref[tile * tm + row], 1), :],
                                  gbuf_ref.at[to_slot].at[pl.ds(row, 1), :], sem.at[to_slot]).start()
            return carry

        lax.fori_loop(0, n_rows, body, 0, unroll=8)

    @pl.when((i == 0) & (j == 0) & valid)
    def _():
        issue(0, 0, tm, 0)

    @pl.when((j == 0) & valid)
    def _():
        pltpu.make_async_copy(hp_ref.at[pl.ds(0, tm), :], gbuf_ref.at[slot], sem.at[slot]).wait()
        lo, hi = _unpack_halves(gbuf_ref[slot])
        x_ref[:, :half] = lo.astype(BF16)
        x_ref[:, half:] = hi.astype(BF16)
        acc_ref[...] = jnp.zeros_like(acc_ref)

    nxt = jnp.minimum(i + 1, n_tiles - 1)

    @pl.when((i + 1 < n_tiles) & (tv_ref[nxt] > 0) & (j * rows_per_step < tm))
    def _():
        issue(nxt, j * rows_per_step, rows_per_step, 1 - slot)

    @pl.when(valid)
    def _():
        _swiglu_accumulate(x_ref[...], wg_ref, wu_ref, wd_ref, acc_ref)

    @pl.when(j == pl.num_programs(1) - 1)
    def _():
        o_ref[...] = jnp.where(valid, _pack_halves(acc_ref[...]), jnp.uint32(0))


def _moe_ffn(hp, w_gate_up, w_down, layer, tile_expert, tile_valid, row_token, *, tm, fc):
    _, _, d, ff2 = w_gate_up.shape
    nj = ff2 // 2 // fc
    n_rows = row_token.shape[0]
    issue_steps = 1 << (nj.bit_length() - 1)

    def jeff(i, j, tv):
        return jnp.where(tv[i] > 0, j, nj - 1)

    return pl.pallas_call(
        functools.partial(_moe_ffn_kernel, rows_per_step=tm // issue_steps),
        grid_spec=pltpu.PrefetchScalarGridSpec(
            num_scalar_prefetch=3, grid=(n_rows // tm, nj),
            in_specs=[pl.BlockSpec(memory_space=pl.ANY),
                      pl.BlockSpec((None, 1, d, fc), lambda i, j, te, tv, tok: (layer, te[i], 0, jeff(i, j, tv))),
                      pl.BlockSpec((None, 1, d, fc),
                                   lambda i, j, te, tv, tok: (layer, te[i], 0, nj + jeff(i, j, tv))),
                      pl.BlockSpec((None, 1, fc, d), lambda i, j, te, tv, tok: (layer, te[i], jeff(i, j, tv), 0))],
            out_specs=pl.BlockSpec((tm, d // 2), lambda i, j, te, tv, tok: (i, 0)),
            scratch_shapes=[pltpu.VMEM((2, tm, d // 2), jnp.uint32), pltpu.VMEM((tm, d), BF16),
                            pltpu.VMEM((tm, d), F32), pltpu.SemaphoreType.DMA((2,))]),
        out_shape=jax.ShapeDtypeStruct((n_rows, d // 2), jnp.uint32),
        compiler_params=_cp(("arbitrary", "arbitrary"), VMEM_LIMIT_MOE), name="moe_swiglu",
    )(tile_expert, tile_valid, row_token, hp, w_gate_up, w_gate_up, w_down)


def _combine_kernel(p0_ref, p1_ref, y_ref, x_ref, route_ref, gate_ref, o_ref, b0_ref, b1_ref, sem):
    n = b0_ref.shape[0]
    half = b0_ref.shape[1]
    base = pl.program_id(0) * n

    def issue(r, carry):
        pltpu.make_async_copy(y_ref.at[pl.ds(p0_ref[base + r], 1), :], b0_ref.at[pl.ds(r, 1), :], sem.at[0]).start()
        pltpu.make_async_copy(y_ref.at[pl.ds(p1_ref[base + r], 1), :], b1_ref.at[pl.ds(r, 1), :], sem.at[1]).start()
        return carry

    lax.fori_loop(0, n, issue, 0, unroll=8)
    pltpu.make_async_copy(y_ref.at[pl.ds(0, n), :], b0_ref, sem.at[0]).wait()
    pltpu.make_async_copy(y_ref.at[pl.ds(0, n), :], b1_ref, sem.at[1]).wait()
    route = route_ref[...]
    w0, w1 = route[:, 2:3], route[:, 3:4]
    lo0, hi0 = _unpack_halves(b0_ref[...])
    lo1, hi1 = _unpack_halves(b1_ref[...])
    gate = gate_ref[0, 0]
    o_ref[:, :half] = x_ref[:, :half] + gate[:, :half] * (w0 * lo0 + w1 * lo1)
    o_ref[:, half:] = x_ref[:, half:] + gate[:, half:] * (w0 * hi0 + w1 * hi1)


def _combine(y, x, route, mod, gate_k, pos0, pos1, *, seq, n_batch):
    t, d = x.shape
    tr = 256
    return pl.pallas_call(
        _combine_kernel,
        grid_spec=pltpu.PrefetchScalarGridSpec(
            num_scalar_prefetch=2, grid=(t // tr,),
            in_specs=[pl.BlockSpec(memory_space=pl.ANY),
                      pl.BlockSpec((tr, d), lambda i, a, b: (i, 0)),
                      pl.BlockSpec((tr, LANES), lambda i, a, b: (i, 0)),
                      pl.BlockSpec((1, 1, 1, d),
                                   lambda i, a, b: (jnp.minimum((i * tr) // seq, n_batch), gate_k, 0, 0))],
            out_specs=pl.BlockSpec((tr, d), lambda i, a, b: (i, 0)),
            scratch_shapes=[pltpu.VMEM((tr, d // 2), jnp.uint32), pltpu.VMEM((tr, d // 2), jnp.uint32),
                            pltpu.SemaphoreType.DMA((2,))]),
        out_shape=jax.ShapeDtypeStruct((t, d), F32),
        compiler_params=_cp(("arbitrary",), VMEM_LIMIT), name="moe_combine",
    )(pos0, pos1, y, x, route, _mod_view(mod))


def _route_meta(idx, n_exp, tm):
    t = idx.shape[0]
    flat = idx.reshape(-1)
    onehot = (flat[:, None] == jnp.arange(n_exp, dtype=jnp.int32)[None, :]).astype(jnp.int32)
    counts = jnp.sum(onehot, axis=0)
    rank = jnp.take_along_axis(jnp.cumsum(onehot, axis=0) - onehot, flat[:, None], axis=1)[:, 0]
    padded = ((counts + tm - 1) // tm) * tm
    ends = jnp.cumsum(padded)
    starts = ends - padded
    pos = starts[flat] + rank
    n_rows = TOP_K * t + n_exp * tm
    row_token = jnp.zeros((n_rows,), jnp.int32).at[pos].set(jnp.arange(TOP_K * t, dtype=jnp.int32) // TOP_K)
    tile_start = jnp.arange(n_rows // tm, dtype=jnp.int32) * tm
    tile_valid = (tile_start < ends[-1]).astype(jnp.int32)
    tile_expert = jnp.sum((ends[None, :] <= tile_start[:, None]).astype(jnp.int32), axis=1)
    tile_expert = jnp.minimum(tile_expert, n_exp - 1)
    last_valid = jnp.maximum(ends[-1] // tm - 1, 0)
    tile_expert = jnp.where(tile_valid > 0, tile_expert, tile_expert[last_valid])
    pos = pos.reshape(t, TOP_K).astype(jnp.int32)
    return row_token, tile_expert, tile_valid, pos[:, 0], pos[:, 1]


def _rope_tables(seq, pad_rows):
    t = jnp.arange(seq, dtype=jnp.int32)
    row = (t // GRID_W).astype(F32)
    col = (t % GRID_W).astype(F32)

    def angles(rot_dim):
        n_freq = rot_dim // 4
        freqs = ROPE_THETA ** (-jnp.arange(n_freq, dtype=F32) / n_freq)
        ang = jnp.concatenate([row[:, None] * freqs[None, :], col[:, None] * freqs[None, :]], axis=-1)
        return jnp.cos(ang), jnp.sin(ang)

    ch, sh = angles(HEAD_DIM)
    cos_h = jnp.concatenate([ch] * 4, axis=1)
    sin_h = jnp.concatenate([-sh, sh, -sh, sh], axis=1)
    cb, sb = angles(B_ROPE)
    one = jnp.ones((seq, B_NOPE), F32)
    zero = jnp.zeros((seq, B_NOPE), F32)
    tail1 = jnp.ones((seq, LANES - B_QK), F32)
    tail0 = jnp.zeros((seq, LANES - B_QK), F32)
    cos_b = jnp.concatenate([one, cb, cb, tail1], axis=1)
    sin_b = jnp.concatenate([zero, -sb, sb, tail0], axis=1)

    def pad(tab, fill):
        return jnp.concatenate([tab, jnp.full((pad_rows, LANES), fill, F32)], axis=0)

    return pad(cos_h, 1.0), pad(sin_h, 0.0), pad(cos_b, 1.0), pad(sin_b, 0.0)


def _permute_w_in(w):
    d = w.shape[0]
    o = np.cumsum([0, 512, 128, 128, B_Q_RANK, B_KV_RANK, B_ROPE, 512, 512, 512, 512, 512, 512, 512])
    aq, ak, av, bq, bkv, bkr, cq, ck, cv, cg, dq, dk, dv = [w[:, o[i]:o[i + 1]] for i in range(13)]

    def dup(m):
        return jnp.concatenate([m[:, :64], m[:, :64], m[:, 64:], m[:, 64:]], axis=1)

    z = lambda n: jnp.zeros((d, n), w.dtype)
    parts = [aq, cq, ck, dq, dk, dup(ak), dup(av), cv, cg, dv, bq, bkv,
             z(B_NOPE), bkr, z(LANES - B_QK), z(P_END - P_BKR - LANES)]
    return jnp.concatenate(parts, axis=1).astype(BF16)


def _block_diag_ones():
    i = np.arange(512) // HEAD_DIM
    return jnp.asarray((i[:, None] == i[None, :]).astype(np.float32), dtype=BF16)


def _pad_lanes(v, width):
    return jnp.concatenate([v.astype(F32), jnp.zeros((width - v.shape[0],), F32)])


def kernel(x, c, ctx, c_ctx, w_ada, b_ada, norm_mix, norm_ffn, w_in, w_out, a_q_norm, a_k_norm, a_sink,
           b_q_a_norm, b_kv_a_norm, b_w_uq, b_w_ukv, b_q_norm, b_k_norm, c_log_decay, d_q_norm, d_k_norm,
           d_rpb, ffn_w_gate_up, ffn_w_down, moe_router, moe_w_gate_up, moe_w_down):
    n_batch, seq, d = x.shape
    n_ctx = ctx.shape[1]
    depth = w_ada.shape[0]
    n_lat = n_batch * seq
    n_all = n_lat + n_batch * n_ctx
    grid_rows = seq // GRID_W
    tm = next(t for t in (1024, 512, 256) if n_lat % t == 0 and n_all % t == 0)

    cc = jnp.concatenate([c, c_ctx[None, :], jnp.zeros((8 - n_batch - 1, d), F32)], axis=0)
    mod = _ada(cc, w_ada, b_ada)
    xs = (x.reshape(n_lat, d), ctx.reshape(n_batch * n_ctx, d))
    cos_h, sin_h, cos_b, sin_b = _rope_tables(seq, 256)
    gmat = _block_diag_ones()

    y = None
    y_gate, y_mod = 0, None
    for l in range(depth):
        last = l == depth - 1
        mod_l = mod[l]
        if y is None:
            (h,) = _resnorm(xs, None, None, 0, mod_l, norm_mix[l], 0, 1, rows=n_all, seq=seq, n_batch=n_batch)
        else:
            xs, h = _resnorm(xs, y, y_mod, y_gate, mod_l, norm_mix[l], 0, 1, rows=n_all, seq=seq,
                             n_batch=n_batch)
        proj = _mm([h], _permute_w_in(w_in[l]), rows=n_all, tm=tm, tn=768, out_dtype=BF16, name="proj_in")
        gains = jnp.stack([jnp.tile(a_q_norm[l], 8), _pad_lanes(jnp.tile(a_k_norm[l], 4), 512),
                           jnp.tile(d_q_norm[l], 8), jnp.tile(d_k_norm[l], 8)]).astype(F32)
        prep, dv2 = _prep(proj, cos_h, sin_h, gmat, gains, rows=n_all, seq=seq, n_lat=n_lat)
        wuq = jnp.pad(b_w_uq[l].reshape(B_Q_RANK, B_HEADS, B_QK),
                      ((0, 0), (0, 0), (0, LANES - B_QK))).reshape(B_Q_RANK, B_HEADS * LANES).astype(BF16)
        wukv3 = b_w_ukv[l].reshape(B_KV_RANK, B_HEADS, B_NOPE + B_V)
        wukv = jnp.concatenate(
            [jnp.pad(wukv3[:, :, :B_NOPE], ((0, 0), (0, 0), (0, LANES - B_NOPE))).reshape(B_KV_RANK, -1),
             wukv3[:, :, B_NOPE:].reshape(B_KV_RANK, -1)], axis=1).astype(BF16)
        bound = _mla_logit_bound(b_q_norm[l], b_k_norm[l])
        mla_safe = (bound <= MLA_SAFE_BOUND).astype(jnp.int32).reshape(1)
        gains_b = jnp.stack([b_q_a_norm[l].astype(F32), _pad_lanes(b_kv_a_norm[l], B_Q_RANK),
                             _pad_lanes(b_q_norm[l], B_Q_RANK), _pad_lanes(b_k_norm[l], B_Q_RANK),
                             jnp.full((B_Q_RANK,), -1.0, F32) * bound] + [jnp.zeros((B_Q_RANK,), F32)] * 3)
        qb, kb, vb = _prepb(proj, cos_b, sin_b, wuq, wukv, gains_b, rows=n_all, seq=seq, n_lat=n_lat)

        sink = a_sink[l].astype(F32)
        bound_a = _softmax_bound(a_q_norm[l], a_k_norm[l], jnp.maximum(jnp.max(sink), 0.0))
        par_a = jnp.concatenate([sink * LOG2E, bound_a[None], (bound_a <= MLA_SAFE_BOUND).astype(F32)[None]])
        oa = _wina(prep, proj, par_a, n_batch=n_batch, seq=seq, n_ctx=n_ctx, with_ctx=not last)
        ob = _mla(qb, kb, vb, mla_safe, n_batch=n_batch, seq=seq, n_ctx=n_ctx, with_ctx=not last)
        ld_head = jnp.broadcast_to(c_log_decay[l].astype(F32)[:, :, None], (2, C_HEADS, LANES))
        oc_fwd, oc_rev = _retention(prep, proj, ld_head, n_batch=n_batch, seq=seq, n_ctx=n_ctx)
        bound_d = _softmax_bound(d_q_norm[l], d_k_norm[l], jnp.max(jnp.abs(d_rpb[l])))
        par_d = jnp.stack([bound_d, (bound_d <= MLA_SAFE_BOUND).astype(F32)])
        abm = _nbr_bias_tables(d_rpb[l], grid_rows, bound_d)
        od = _nbr(prep, dv2, abm, par_d, n_batch=n_batch, seq=seq, n_ctx=n_ctx, with_ctx=not last)
        rows_l = n_lat if last else n_all
        oc = _retention_finish(oc_fwd, oc_rev, proj, gmat, rows=rows_l)
        ymix = _mm([oa, ob, oc, od], w_out[l].astype(BF16), rows=rows_l, tm=tm, tn=1024, out_dtype=BF16,
                   name="proj_out")
        i = l // 2
        if l % 2 == 0:
            xs, h2 = _resnorm(xs, ymix, mod_l, 2, mod_l, norm_ffn[l], 3, 4, rows=rows_l, seq=seq,
                              n_batch=n_batch)
            y = _ffn(h2, ffn_w_gate_up, ffn_w_down, i, rows=rows_l, tm=tm, fc=512)
            y_gate, y_mod = 5, mod_l
        else:
            xs, hp, route = _resnorm(xs, ymix, mod_l, 2, mod_l, norm_ffn[l], 3, 4, rows=rows_l, seq=seq,
                                     n_batch=n_batch, router=moe_router[i])
            top_idx = route[:, :TOP_K].astype(jnp.int32)
            row_token, tile_expert, tile_valid, pos0, pos1 = _route_meta(top_idx, moe_router.shape[2], tm)
            yg = _moe_ffn(hp, moe_w_gate_up, moe_w_down, i, tile_expert, tile_valid, row_token, tm=tm, fc=512)
            xs = _combine(yg, xs, route, mod_l, 5, pos0, pos1, seq=seq, n_batch=n_batch)
            y = None
    if y is not None:
        xs, _ = _resnorm(xs, y, y_mod, y_gate, y_mod, norm_ffn[depth - 1], 3, 4, rows=xs.shape[0], seq=seq,
                         n_batch=n_batch)
    return xs[:n_lat].reshape(n_batch, seq, d)
```

```python
import functools

import numpy as np
import jax
import jax.numpy as jnp
from jax import lax
from jax.experimental import pallas as pl
from jax.experimental.pallas import tpu as pltpu

F32 = jnp.float32
BF16 = jnp.bfloat16

GRID_W = 64
HEAD_DIM = 64
ROPE_THETA = 10000.0
EPS = 1e-6
NEG_INF = -1e30

A_HEADS = 8
A_KV_HEADS = 2
A_BLOCK = 128
B_HEADS = 8
B_Q_RANK = 384
B_KV_RANK = 128
B_NOPE = 64
B_ROPE = 32
B_QK = B_NOPE + B_ROPE
B_V = 64
C_HEADS = 8
C_CHUNK = 128
D_HEADS = 8
NA_ROWS = 8
NA_COLS = 16
TOP_K = 2

LANES = 128
NA_QROWS = 4
NA_KROWS = NA_ROWS + NA_QROWS
NA_CASES = 4
MOE_ROW_SPLITS = 4

P_AQ, P_CQ, P_CK, P_DQ, P_DK, P_AK, P_AV, P_CV, P_CG, P_DV, P_BQ, P_BKV, P_BKR, P_END = (
    0, 512, 1024, 1536, 2048, 2560, 2816, 3072, 3584, 4096, 4608, 4992, 5120, 5376)
PREP_W = P_AV

VMEM_LIMIT = 56 * 1024 * 1024
VMEM_LIMIT_MOE = 61 * 1024 * 1024

LOG2E = float(np.log2(np.e))
MLA_LOGIT_SCALE = B_QK ** -0.5 * LOG2E
MLA_SAFE_BOUND = 40.0


def _cp(sem, vmem=None):
    return pltpu.CompilerParams(dimension_semantics=sem, vmem_limit_bytes=vmem)


def _nt(a, b):
    return lax.dot_general(a, b, (((1,), (1,)), ((), ())), preferred_element_type=F32)


_HI_HALF = 0xFFFF0000


def _pack_halves(x):
    n = x.shape[1] // 2
    bits = pltpu.bitcast(x.astype(BF16).astype(F32), jnp.uint32)
    return (bits[:, :n] >> 16) | (bits[:, n:] & jnp.uint32(_HI_HALF))


def _unpack_halves(w):
    return pltpu.bitcast(w << 16, F32), pltpu.bitcast(w & jnp.uint32(_HI_HALF), F32)


def _ada_kernel(c_ref, w_ref, b_ref, o_ref):
    c = c_ref[...]
    s = (c * jax.nn.sigmoid(c)).astype(BF16)
    o_ref[0] = jnp.dot(s, w_ref[0].astype(BF16), preferred_element_type=F32) + b_ref[0]


def _ada(cc, w_ada, b_ada):
    depth, d, n = w_ada.shape
    tn = 1024
    return pl.pallas_call(
        _ada_kernel,
        grid=(depth, n // tn),
        in_specs=[pl.BlockSpec((8, d), lambda l, j: (0, 0)),
                  pl.BlockSpec((1, d, tn), lambda l, j: (l, 0, j)),
                  pl.BlockSpec((1, 1, tn), lambda l, j: (l, 0, j))],
        out_specs=pl.BlockSpec((1, 8, tn), lambda l, j: (l, 0, j)),
        out_shape=jax.ShapeDtypeStruct((depth, 8, n), F32),
        compiler_params=_cp(("arbitrary", "arbitrary"), VMEM_LIMIT),
        name="ada",
    )(cc, w_ada, b_ada.reshape(depth, 1, n))


def _resnorm_kernel(*refs, has_res, with_router, n_exp, lat_tiles):
    refs = list(refs)
    x_ref = refs.pop(0)
    if lat_tiles is not None:
        xc_ref = refs.pop(0)
        x = jnp.where(pl.program_id(0) < lat_tiles, x_ref[...], xc_ref[...])
    else:
        x = x_ref[...]
    if has_res:
        y_ref = refs.pop(0)
        gate_ref = refs.pop(0)
    gain_ref, sh_ref, sc_ref = refs[:3]
    refs = refs[3:]
    if with_router:
        router_ref = refs.pop(0)
    if has_res:
        xo_ref = refs.pop(0)
    h_ref = refs.pop(0)
    if has_res:
        x = x + gate_ref[0, 0] * y_ref[...].astype(F32)
        xo_ref[...] = x
    ms = jnp.mean(x * x, axis=-1, keepdims=True)
    h = (x * lax.rsqrt(ms + EPS) * gain_ref[...]) * (1.0 + sc_ref[0, 0]) + sh_ref[0, 0]
    if with_router:
        (route_ref,) = refs
        h_ref[...] = _pack_halves(h)
        logits = jnp.dot(h, router_ref[...], preferred_element_type=F32, precision=lax.Precision.HIGHEST)
        lane = lax.broadcasted_iota(jnp.int32, logits.shape, 1)
        logits = jnp.where(lane < n_exp, logits, -jnp.inf)
        m1 = jnp.max(logits, axis=-1, keepdims=True)
        i1 = jnp.min(jnp.where(logits == m1, lane, LANES), axis=-1, keepdims=True)
        rest = jnp.where(lane == i1, -jnp.inf, logits)
        m2 = jnp.max(rest, axis=-1, keepdims=True)
        i2 = jnp.min(jnp.where(rest == m2, lane, LANES), axis=-1, keepdims=True)
        e2 = jnp.exp(m2 - m1)
        g1 = 1.0 / (1.0 + e2)
        g2 = e2 / (1.0 + e2)
        route = jnp.where(lane == 0, i1.astype(F32),
                          jnp.where(lane == 1, i2.astype(F32),
                                    jnp.where(lane == 2, g1, jnp.where(lane == 3, g2, 0.0))))
        route_ref[...] = route
    else:
        h_ref[...] = h.astype(h_ref.dtype)


def _mod_view(mod):
    return mod.reshape(mod.shape[0], 6, 1, mod.shape[1] // 6)


def _resnorm(x, y, gate_mod, gate_k, mod, gain, sh_k, sc_k, *, rows, seq, n_batch, router=None):
    tr = 256
    has_res = y is not None
    with_router = router is not None
    split = isinstance(x, tuple)
    d = x[0].shape[1] if split else x.shape[1]

    def grp(i):
        return jnp.minimum((i * tr) // seq, n_batch)

    def modspec(k):
        return pl.BlockSpec((1, 1, 1, d), lambda i: (grp(i), k, 0, 0))

    row = pl.BlockSpec((tr, d), lambda i: (i, 0))
    lat_tiles = None
    if split:
        lat_tiles = x[0].shape[0] // tr
        in_specs = [pl.BlockSpec((tr, d), lambda i: (jnp.minimum(i, lat_tiles - 1), 0)),
                    pl.BlockSpec((tr, d), lambda i: (jnp.maximum(i - lat_tiles, 0), 0))]
        args = list(x)
    else:
        in_specs = [row]
        args = [x]
    if has_res:
        in_specs += [row, modspec(gate_k)]
        args += [y, _mod_view(gate_mod)]
    in_specs += [pl.BlockSpec((1, d), lambda i: (0, 0)), modspec(sh_k), modspec(sc_k)]
    args += [gain.reshape(1, d).astype(F32), _mod_view(mod), _mod_view(mod)]
    out_shape, out_specs = [], []
    n_exp = 0
    if with_router:
        n_exp = router.shape[1]
        router = jnp.pad(router.astype(F32), ((0, 0), (0, LANES - n_exp)))
        in_specs.append(pl.BlockSpec(router.shape, lambda i: (0, 0)))
        args.append(router)
    if has_res:
        out_shape.append(jax.ShapeDtypeStruct((rows, d), F32))
        out_specs.append(row)
    if with_router:
        out_shape += [jax.ShapeDtypeStruct((rows, d // 2), jnp.uint32), jax.ShapeDtypeStruct((rows, LANES), F32)]
        out_specs += [pl.BlockSpec((tr, d // 2), lambda i: (i, 0)), pl.BlockSpec((tr, LANES), lambda i: (i, 0))]
    else:
        out_shape.append(jax.ShapeDtypeStruct((rows, d), BF16))
        out_specs.append(row)
    kern = functools.partial(_resnorm_kernel, has_res=has_res, with_router=with_router, n_exp=n_exp,
                             lat_tiles=lat_tiles)
    return pl.pallas_call(
        kern, grid=(rows // tr,), in_specs=in_specs, out_specs=out_specs, out_shape=out_shape,
        compiler_params=_cp(("arbitrary",), VMEM_LIMIT), name="resnorm",
    )(*args)


def _mm_kernel(*refs, n_x):
    w_ref, o_ref = refs[n_x], refs[n_x + 1]
    if n_x == 1:
        x = refs[0][...]
    else:
        x = jnp.concatenate([r[...] for r in refs[:n_x]], axis=1)
    o_ref[...] = jnp.dot(x, w_ref[...], preferred_element_type=F32).astype(o_ref.dtype)


def _mm(xs, w, *, rows, tm, tn, out_dtype, name):
    n = w.shape[1]
    in_specs = [pl.BlockSpec((tm, a.shape[1]), lambda i, j: (i, 0)) for a in xs]
    in_specs.append(pl.BlockSpec((w.shape[0], tn), lambda i, j: (0, j)))
    return pl.pallas_call(
        functools.partial(_mm_kernel, n_x=len(xs)),
        grid=(rows // tm, n // tn),
        in_specs=in_specs,
        out_specs=pl.BlockSpec((tm, tn), lambda i, j: (i, j)),
        out_shape=jax.ShapeDtypeStruct((rows, n), out_dtype),
        compiler_params=_cp(("arbitrary", "arbitrary"), VMEM_LIMIT), name=name,
    )(*xs, w)


def _group_meansq(x, g_ref, width):
    x2 = x * x
    hi = x2.astype(BF16)
    lo = (x2 - hi.astype(F32)).astype(BF16)
    g = g_ref[:width, :width]
    ss = jnp.dot(hi, g, preferred_element_type=F32) + jnp.dot(lo, g, preferred_element_type=F32)
    return ss * (1.0 / HEAD_DIM)


def _prep_kernel(p_ref, dv_ref, ch_ref, sh_ref, g_ref, gains_ref, o_ref, dv2_ref):
    tr = p_ref.shape[0]

    def seg(off, width):
        return p_ref[:, off:off + width].astype(F32)

    def norm(x, row, width):
        return x * lax.rsqrt(_group_meansq(x, g_ref, width) + EPS) * gains_ref[row:row + 1, :width]

    def rope(x, width):
        lane = lax.broadcasted_iota(jnp.int32, (tr, width), 1)
        first = (lane % HEAD_DIM) < (HEAD_DIM // 2)
        cos = jnp.concatenate([ch_ref[...]] * (width // LANES), axis=1)
        sin = jnp.concatenate([sh_ref[...]] * (width // LANES), axis=1)
        swapped = jnp.where(first, pltpu.roll(x, width - HEAD_DIM // 2, 1), pltpu.roll(x, HEAD_DIM // 2, 1))
        return x * cos + swapped * sin

    scale = HEAD_DIM ** -0.5
    scale2 = scale * LOG2E
    o_ref[:, P_AQ:P_AQ + 512] = (rope(norm(seg(P_AQ, 512), 0, 512), 512) * scale2).astype(BF16)
    o_ref[:, P_CQ:P_CQ + 512] = (rope(seg(P_CQ, 512), 512) * scale).astype(BF16)
    o_ref[:, P_CK:P_CK + 512] = rope(seg(P_CK, 512), 512).astype(BF16)
    o_ref[:, P_DQ:P_DQ + 512] = (norm(seg(P_DQ, 512), 2, 512) * scale2).astype(BF16)
    o_ref[:, P_DK:P_DK + 512] = norm(seg(P_DK, 512), 3, 512).astype(BF16)
    o_ref[:, P_AK:P_AK + 256] = rope(norm(seg(P_AK, 256), 1, 256), 256).astype(BF16)
    ones_blk = jnp.where(lax.broadcasted_iota(jnp.int32, (tr, LANES), 1) < HEAD_DIM, 1.0, 0.0).astype(BF16)
    for g in range(D_HEADS // 2):
        dv2_ref[:, 2 * g * LANES:(2 * g + 1) * LANES] = dv_ref[:, g * LANES:(g + 1) * LANES]
        dv2_ref[:, (2 * g + 1) * LANES:(2 * g + 2) * LANES] = ones_blk


def _prep(proj, cos_h, sin_h, gmat, gains, *, rows, seq, n_lat):
    tr = 256
    nlat = seq // tr

    def tab(i):
        return (jnp.where(i * tr < n_lat, i % nlat, nlat), 0)

    return pl.pallas_call(
        _prep_kernel,
        grid=(rows // tr,),
        in_specs=[pl.BlockSpec((tr, PREP_W), lambda i: (i, 0)),
                  pl.BlockSpec((tr, 512), lambda i: (i, P_DV // 512)),
                  pl.BlockSpec((tr, LANES), tab),
                  pl.BlockSpec((tr, LANES), tab),
                  pl.BlockSpec((512, 512), lambda i: (0, 0)),
                  pl.BlockSpec((4, 512), lambda i: (0, 0))],
        out_specs=[pl.BlockSpec((tr, PREP_W), lambda i: (i, 0)),
                   pl.BlockSpec((tr, 2 * 512), lambda i: (i, 0))],
        out_shape=[jax.ShapeDtypeStruct((rows, PREP_W), BF16), jax.ShapeDtypeStruct((rows, 2 * 512), BF16)],
        compiler_params=_cp(("arbitrary",), VMEM_LIMIT), name="prep",
    )(proj, proj, cos_h, sin_h, gmat, gains)


def _prepb_kernel(bq_ref, bkv_ref, bkr_ref, cb_ref, sb_ref, wuq_ref, wukv_ref, gains_ref, q_ref, k_ref, v_ref):
    tr = bq_ref.shape[0]
    cq = bq_ref[...].astype(F32)
    cqn = cq * lax.rsqrt(jnp.mean(cq * cq, axis=-1, keepdims=True) + EPS) * gains_ref[0:1, :]
    qup = jnp.dot(cqn.astype(BF16), wuq_ref[...], preferred_element_type=F32)
    ckv = bkv_ref[...].astype(F32)
    ckvn = ckv * lax.rsqrt(jnp.mean(ckv * ckv, axis=-1, keepdims=True) + EPS) * gains_ref[1:2, :LANES]
    kvup = jnp.dot(ckvn.astype(BF16), wukv_ref[...], preferred_element_type=F32)
    kr = bkr_ref[...].astype(F32)
    cb, sb = cb_ref[...], sb_ref[...]
    lane = lax.broadcasted_iota(jnp.int32, (tr, LANES), 1)
    half = B_ROPE // 2
    m1 = (lane >= B_NOPE) & (lane < B_NOPE + half)
    m2 = (lane >= B_NOPE + half) & (lane < B_QK)

    def rope(x):
        swapped = jnp.where(m1, pltpu.roll(x, LANES - half, 1), jnp.where(m2, pltpu.roll(x, half, 1), 0.0))
        return x * cb + swapped * sb

    def headnorm(x, row):
        ms = jnp.sum(x * x, axis=-1, keepdims=True) * (1.0 / B_QK)
        return x * lax.rsqrt(ms + EPS) * gains_ref[row:row + 1, :LANES]

    shift_lane = lane == B_QK
    neg_bound = gains_ref[4:5, :LANES]
    for h in range(B_HEADS):
        sl = slice(h * LANES, (h + 1) * LANES)
        q = rope(headnorm(qup[:, sl], 2)) * MLA_LOGIT_SCALE
        q_ref[:, sl] = jnp.where(shift_lane, 1.0, q).astype(BF16)
        k = rope(headnorm(kvup[:, sl] + kr, 3))
        k_ref[:, sl] = jnp.where(shift_lane, neg_bound, k).astype(BF16)
    ones_blk = jnp.where(lane < B_V, 1.0, 0.0).astype(BF16)
    for g in range(B_HEADS // 2):
        v_ref[:, 2 * g * LANES:(2 * g + 1) * LANES] = kvup[:, (B_HEADS + g) * LANES:(B_HEADS + g + 1) * LANES].astype(BF16)
        v_ref[:, (2 * g + 1) * LANES:(2 * g + 2) * LANES] = ones_blk


def _prepb(proj, cos_b, sin_b, wuq, wukv, gains, *, rows, seq, n_lat):
    tr = 256
    nlat = seq // tr

    def tab(i):
        return (jnp.where(i * tr < n_lat, i % nlat, nlat), 0)

    return pl.pallas_call(
        _prepb_kernel,
        grid=(rows // tr,),
        in_specs=[pl.BlockSpec((tr, B_Q_RANK), lambda i: (i, P_BQ // B_Q_RANK)),
                  pl.BlockSpec((tr, LANES), lambda i: (i, P_BKV // LANES)),
                  pl.BlockSpec((tr, LANES), lambda i: (i, P_BKR // LANES)),
                  pl.BlockSpec((tr, LANES), tab),
                  pl.BlockSpec((tr, LANES), tab),
                  pl.BlockSpec(wuq.shape, lambda i: (0, 0)),
                  pl.BlockSpec(wukv.shape, lambda i: (0, 0)),
                  pl.BlockSpec((8, B_Q_RANK), lambda i: (0, 0))],
        out_specs=[pl.BlockSpec((tr, B_HEADS * LANES), lambda i: (i, 0)),
                   pl.BlockSpec((tr, B_HEADS * LANES), lambda i: (i, 0)),
                   pl.BlockSpec((tr, B_HEADS * LANES), lambda i: (i, 0))],
        out_shape=[jax.ShapeDtypeStruct((rows, B_HEADS * LANES), BF16),
                   jax.ShapeDtypeStruct((rows, B_HEADS * LANES), BF16),
                   jax.ShapeDtypeStruct((rows, B_HEADS * LANES), BF16)],
        compiler_params=_cp(("arbitrary",), VMEM_LIMIT), name="prep_mla",
    )(proj, proj, proj, cos_b, sin_b, wuq, wukv, gains)


def _wina_kernel(par_ref, q_ref, kp_ref, ko_ref, kn_ref, kc_ref, vp_ref, vo_ref, vn_ref, vc_ref, o_ref,
                 *, nb, seq):
    n = pl.program_id(1)
    blk = A_BLOCK
    rep = A_HEADS // A_KV_HEADS
    n_ctx = kc_ref.shape[0]
    kcat = jnp.concatenate([kp_ref[...], ko_ref[...], kn_ref[...], kc_ref[...]], axis=0)
    vcat = jnp.concatenate([vp_ref[...], vo_ref[...], vn_ref[...], vc_ref[...]], axis=0)
    n_keys = 3 * blk + n_ctx
    qi = lax.broadcasted_iota(jnp.int32, (blk, n_keys), 0)
    kj = lax.broadcasted_iota(jnp.int32, (blk, n_keys), 1)
    band = kj - blk
    kpos = n * blk + band
    valid = ((jnp.abs(qi - band) <= blk) & (kpos >= 0) & (kpos < seq) & (n < nb)) | (kj >= 3 * blk)
    low = lax.broadcasted_iota(jnp.int32, (blk, LANES), 1) < HEAD_DIM
    low_k = lax.broadcasted_iota(jnp.int32, (n_keys, LANES), 1) < HEAD_DIM
    bound = par_ref[A_HEADS]
    safe = par_ref[A_HEADS + 1] > 0.0

    def attend(use_max):
        mask = jnp.where(valid, 0.0 if use_max else -bound, NEG_INF)
        mask = jnp.concatenate([mask] * rep, axis=0)
        for g in range(A_KV_HEADS):
            kg = kcat[:, g * LANES:(g + 1) * LANES]
            v1 = jnp.where(low_k, vcat[:, g * LANES:(g + 1) * LANES], jnp.ones((), BF16))
            qs, sinks = [], []
            for r in range(rep):
                h = g * rep + r
                qc = q_ref[:, (h // 2) * LANES:(h // 2 + 1) * LANES]
                qs.append(jnp.where(low if h % 2 == 0 else jnp.logical_not(low), qc, jnp.zeros_like(qc)))
                sinks.append(jnp.full((blk, 1), par_ref[h], F32))
            s = _nt(jnp.concatenate(qs, axis=0), kg) + mask
            sink = jnp.concatenate(sinks, axis=0)
            if use_max:
                m = jnp.maximum(jnp.max(s, axis=-1, keepdims=True), sink)
                s, sink = s - m, sink - m
            else:
                sink = sink - bound
            o = jnp.dot(jnp.exp2(s).astype(BF16), v1, preferred_element_type=F32)
            res = o / (o[:, HEAD_DIM:HEAD_DIM + 1] + jnp.exp2(sink))
            for c in range(rep // 2):
                even = res[(2 * c) * blk:(2 * c + 1) * blk]
                odd = pltpu.roll(res[(2 * c + 1) * blk:(2 * c + 2) * blk], HEAD_DIM, 1)
                j = (g * rep) // 2 + c
                o_ref[:, j * LANES:(j + 1) * LANES] = jnp.where(low, even, odd).astype(o_ref.dtype)

    @pl.when(safe)
    def _():
        attend(False)

    @pl.when(jnp.logical_not(safe))
    def _():
        attend(True)


def _wina(prep, proj, sink, *, n_batch, seq, n_ctx, with_ctx):
    blk = A_BLOCK
    nb = seq // blk
    ncb = n_ctx // blk
    steps = nb + (ncb if with_ctx else 0)
    lat_blocks = n_batch * nb
    rows_out = n_batch * seq + (n_batch * n_ctx if with_ctx else 0)

    def qmap(b, n, s):
        return (jnp.where(n < nb, b * nb + n, lat_blocks + b * ncb + (n - nb)), 0)

    def kmap(delta, col):
        def f(b, n, s):
            return (b * nb + jnp.clip(n + delta, 0, nb - 1), col)
        return f

    def cmap(col):
        def f(b, n, s):
            return (n_batch * seq // n_ctx + b, col)
        return f

    kcol, vcol = P_AK // 256, P_AV // 256
    in_specs = [pl.BlockSpec((blk, 512), qmap)]
    in_specs += [pl.BlockSpec((blk, 256), kmap(dl, kcol)) for dl in (-1, 0, 1)]
    in_specs += [pl.BlockSpec((n_ctx, 256), cmap(kcol))]
    in_specs += [pl.BlockSpec((blk, 256), kmap(dl, vcol)) for dl in (-1, 0, 1)]
    in_specs += [pl.BlockSpec((n_ctx, 256), cmap(vcol))]
    return pl.pallas_call(
        functools.partial(_wina_kernel, nb=nb, seq=seq),
        grid_spec=pltpu.PrefetchScalarGridSpec(
            num_scalar_prefetch=1, grid=(n_batch, steps), in_specs=in_specs,
            out_specs=pl.BlockSpec((blk, 512), qmap)),
        out_shape=jax.ShapeDtypeStruct((rows_out, 512), BF16),
        compiler_params=_cp(("arbitrary", "arbitrary"), VMEM_LIMIT), name="mixer_window",
    )(sink, prep, prep, prep, prep, prep, proj, proj, proj, proj)


def _mla_kernel(safe_ref, q_ref, kl_ref, vl_ref, kc_ref, vc_ref, o_ref, *, nq, tk_fast, tk_exact, hps):
    tq = q_ref.shape[0]
    seq = kl_ref.shape[0]
    is_lat = pl.program_id(2) < nq
    safe = safe_ref[0] > 0
    pair_w = 2 * LANES
    lane = lax.broadcasted_iota(jnp.int32, (tq, LANES), 1)

    def q(h):
        return q_ref[:, h * LANES:(h + 1) * LANES]

    def write(nums, dens):
        for g in range(hps // 2):
            o_ref[:, g * LANES:(g + 1) * LANES] = jnp.where(
                lane < B_V, nums[2 * g] / dens[2 * g], nums[2 * g + 1] / dens[2 * g + 1]).astype(o_ref.dtype)

    def fast(chunks):
        accs = [None] * hps
        for k_ref, v_ref, off, n in chunks:
            for h in range(hps):
                g = h // 2
                p = jnp.exp2(_nt(q(h), k_ref[off:off + n, h * LANES:(h + 1) * LANES])).astype(BF16)
                d = jnp.dot(p, v_ref[off:off + n, g * pair_w:(g + 1) * pair_w], preferred_element_type=F32)
                accs[h] = d if accs[h] is None else accs[h] + d
        write([a[:, :LANES] for a in accs], [a[:, LANES:LANES + 1] for a in accs])

    ctx_chunk = (kc_ref, vc_ref, 0, kc_ref.shape[0])

    @pl.when(safe & is_lat)
    def _():
        fast([(kl_ref, vl_ref, c * tk_fast, tk_fast) for c in range(seq // tk_fast)] + [ctx_chunk])

    @pl.when(safe & jnp.logical_not(is_lat))
    def _():
        fast([ctx_chunk])

    @pl.when(jnp.logical_not(safe))
    def _():
        def step(kf, vf, carry):
            out = []
            for h in range(hps):
                m, l, acc = carry[h]
                s = _nt(q(h), kf(h))
                m_new = jnp.maximum(m, jnp.max(s, axis=-1, keepdims=True))
                a = jnp.exp2(m - m_new)
                p = jnp.exp2(s - m_new)
                l = a * l + jnp.sum(p, axis=-1, keepdims=True)
                acc = a * acc + jnp.dot(p.astype(BF16), vf(h // 2), preferred_element_type=F32)
                out.append((m_new, l, acc))
            return tuple(out)

        def body(c, carry):
            off = pl.multiple_of(c * tk_exact, tk_exact)
            return step(lambda h: kl_ref[pl.ds(off, tk_exact), h * LANES:(h + 1) * LANES],
                        lambda g: vl_ref[pl.ds(off, tk_exact), g * pair_w:g * pair_w + LANES], carry)

        init = tuple((jnp.full((tq, 1), -jnp.inf, F32), jnp.zeros((tq, 1), F32), jnp.zeros((tq, LANES), F32))
                     for _ in range(hps))
        carry = lax.fori_loop(0, jnp.where(is_lat, seq // tk_exact, 0), body, init)
        carry = step(lambda h: kc_ref[:, h * LANES:(h + 1) * LANES],
                     lambda g: vc_ref[:, g * pair_w:g * pair_w + LANES], carry)
        write([c[2] for c in carry], [c[1] for c in carry])


def _mla_logit_bound(q_gain, k_gain):
    b = B_QK * MLA_LOGIT_SCALE * jnp.max(jnp.abs(q_gain)) * jnp.max(jnp.abs(k_gain))
    return (1.02 * b + 0.5).astype(F32)


def _mla(qb, kb, vb, safe, *, n_batch, seq, n_ctx, with_ctx):
    hps = 4
    tq = 256
    assert n_ctx == tq
    nq = seq // tq
    ctx_blk = n_batch * seq // n_ctx
    rows_out = n_batch * seq + (n_batch * n_ctx if with_ctx else 0)
    kw, vw = hps * LANES, hps * B_V

    def qmap(b, j, i, s):
        return (jnp.where(i < nq, b * nq + i, ctx_blk + b), j)

    return pl.pallas_call(
        functools.partial(_mla_kernel, nq=nq, tk_fast=min(2048, seq), tk_exact=512, hps=hps),
        grid_spec=pltpu.PrefetchScalarGridSpec(
            num_scalar_prefetch=1,
            grid=(n_batch, B_HEADS // hps, nq + (1 if with_ctx else 0)),
            in_specs=[pl.BlockSpec((tq, kw), qmap),
                      pl.BlockSpec((seq, kw), lambda b, j, i, s: (b, j)),
                      pl.BlockSpec((seq, kw), lambda b, j, i, s: (b, j)),
                      pl.BlockSpec((n_ctx, kw), lambda b, j, i, s: (ctx_blk + b, j)),
                      pl.BlockSpec((n_ctx, kw), lambda b, j, i, s: (ctx_blk + b, j))],
            out_specs=pl.BlockSpec((tq, vw), qmap)),
        out_shape=jax.ShapeDtypeStruct((rows_out, 512), BF16),
        compiler_params=_cp(("arbitrary", "arbitrary", "arbitrary"), VMEM_LIMIT), name="mixer_mla",
    )(safe, qb, kb, vb, kb, vb)


def _ret_kernel(qf_ref, kf_ref, vf_ref, qr_ref, kr_ref, vr_ref, ld_ref, of_ref, or_ref,
                s_ref, dec_ref, qw_ref, kw_ref, cd_ref):
    step = pl.program_id(1)
    cc = C_CHUNK
    ri = lax.broadcasted_iota(jnp.int32, (cc, cc), 0)
    ci = lax.broadcasted_iota(jnp.int32, (cc, cc), 1)
    low = ci < HEAD_DIM

    @pl.when(step == 0)
    def _():
        s_ref[...] = jnp.zeros_like(s_ref)
        pos = ri.astype(F32)
        for d in range(2):
            diff = (ri - ci) if d == 0 else (ci - ri)
            dpos = jnp.maximum(diff, 0).astype(F32)
            qpow = (pos + 1.0) if d == 0 else (cc - pos)
            kpow = (cc - 1.0 - pos) if d == 0 else pos
            for j in range(C_HEADS // 2):
                lg_e = -jnp.exp(ld_ref[d, 2 * j:2 * j + 1, :])
                lg_o = -jnp.exp(ld_ref[d, 2 * j + 1:2 * j + 2, :])
                lgl = jnp.where(low[0:1, :], lg_e, lg_o)
                dec_ref[d, j, :cc] = jnp.where(diff >= 0, jnp.exp(lg_e * dpos), 0.0)
                dec_ref[d, j, cc:] = jnp.where(diff >= 0, jnp.exp(lg_o * dpos), 0.0)
                qw_ref[d, j] = jnp.exp(lgl * qpow)
                kw_ref[d, j] = jnp.exp(lgl * kpow)
                cd_ref[d, j] = jnp.where(ri < HEAD_DIM, jnp.exp(lg_e * cc), jnp.exp(lg_o * cc))

    blockdiag = (ri < HEAD_DIM) == low
    for d, (q_ref, k_ref, v_ref, o_ref) in enumerate(((qf_ref, kf_ref, vf_ref, of_ref),
                                                      (qr_ref, kr_ref, vr_ref, or_ref))):
        for j in range(C_HEADS // 2):
            sl = slice(j * LANES, (j + 1) * LANES)
            q, k, v = q_ref[:, sl], k_ref[:, sl], v_ref[:, sl]
            zero = jnp.zeros_like(q)
            q2 = jnp.concatenate([jnp.where(low, q, zero), jnp.where(low, zero, q)], axis=0)
            sc = (_nt(q2, k) * dec_ref[d, j]).astype(BF16)
            o2 = jnp.dot(sc, v, preferred_element_type=F32)
            o_intra = jnp.where(low, o2[:cc], o2[cc:])
            state = s_ref[d, j]
            qw = (q.astype(F32) * qw_ref[d, j]).astype(BF16)
            o_cross = jnp.dot(qw, state.astype(BF16), preferred_element_type=F32)
            kw = (k.astype(F32) * kw_ref[d, j]).astype(BF16)
            kv = lax.dot_general(kw, v, (((0,), (0,)), ((), ())), preferred_element_type=F32)
            s_ref[d, j] = state * cd_ref[d, j] + jnp.where(blockdiag, kv, 0.0)
            o_ref[:, sl] = o_intra + o_cross


def _retention(prep, proj, ld_head, *, n_batch, seq, n_ctx):
    cc = C_CHUNK
    nl, nc = seq // cc, n_ctx // cc
    steps = nc + nl
    lat_blocks = n_batch * nl
    rows = n_batch * (seq + n_ctx)

    def rowblk(b, d, s):
        c_ctx = jnp.where(d == 0, s, nc - 1 - s)
        c_lat = jnp.where(d == 0, s - nc, nl - 1 - (s - nc))
        return jnp.where(s < nc, lat_blocks + b * nc + c_ctx, b * nl + c_lat)

    def spec(d, col):
        return pl.BlockSpec((cc, 512), lambda b, s: (rowblk(b, d, s), col))

    pairs = C_HEADS // 2
    return pl.pallas_call(
        _ret_kernel,
        grid=(n_batch, steps),
        in_specs=[spec(d, col) for d in (0, 1) for col in (P_CQ // 512, P_CK // 512, P_CV // 512)]
        + [pl.BlockSpec((2, C_HEADS, LANES), lambda b, s: (0, 0, 0))],
        out_specs=[spec(0, 0), spec(1, 0)],
        out_shape=[jax.ShapeDtypeStruct((rows, 512), F32)] * 2,
        scratch_shapes=[pltpu.VMEM((2, pairs, LANES, LANES), F32),
                        pltpu.VMEM((2, pairs, 2 * cc, cc), F32),
                        pltpu.VMEM((2, pairs, cc, LANES), F32),
                        pltpu.VMEM((2, pairs, cc, LANES), F32),
                        pltpu.VMEM((2, pairs, LANES, LANES), F32)],
        compiler_params=_cp(("arbitrary", "arbitrary"), VMEM_LIMIT), name="mixer_retention",
    )(prep, prep, proj, prep, prep, proj, ld_head)


def _retfin_kernel(of_ref, or_ref, g_ref, gm_ref, out_ref):
    o = of_ref[...] + or_ref[...]
    on = o * lax.rsqrt(_group_meansq(o, gm_ref, 512) + EPS)
    g = g_ref[...].astype(F32)
    out_ref[...] = (g * jax.nn.sigmoid(g) * on).astype(out_ref.dtype)


def _retention_finish(o_fwd, o_rev, proj, gmat, *, rows):
    tr = 256
    return pl.pallas_call(
        _retfin_kernel,
        grid=(rows // tr,),
        in_specs=[pl.BlockSpec((tr, 512), lambda i: (i, 0)),
                  pl.BlockSpec((tr, 512), lambda i: (i, 0)),
                  pl.BlockSpec((tr, 512), lambda i: (i, P_CG // 512)),
                  pl.BlockSpec((512, 512), lambda i: (0, 0))],
        out_specs=pl.BlockSpec((tr, 512), lambda i: (i, 0)),
        out_shape=jax.ShapeDtypeStruct((rows, 512), BF16),
        compiler_params=_cp(("arbitrary",), VMEM_LIMIT), name="retention_finish",
    )(o_fwd, o_rev, proj, gmat)


def _nbr_kernel(par_ref, q_ref, k_ref, v_ref, kc_ref, vc_ref, abm_ref, o_ref, *, rows):
    st = pl.program_id(1)
    start = jnp.clip(NA_QROWS * st - NA_ROWS // 2, 0, rows - NA_KROWS)
    off = pl.multiple_of(start * GRID_W, GRID_W)
    nk = NA_KROWS * GRID_W
    tq = q_ref.shape[0]
    low = lax.broadcasted_iota(jnp.int32, (tq, LANES), 1) < HEAD_DIM
    bound = par_ref[0]
    safe = par_ref[1] > 0.0
    pair_w = 2 * LANES

    def attend(use_max):
        for j in range(D_HEADS // 2):
            sl = slice(j * LANES, (j + 1) * LANES)
            sl2 = slice(j * pair_w, (j + 1) * pair_w)
            qc = q_ref[:, sl]
            zero = jnp.zeros_like(qc)
            q2 = jnp.concatenate([jnp.where(low, qc, zero), jnp.where(low, zero, qc)], axis=0)
            bias = jnp.concatenate([abm_ref[0, 2 * j], abm_ref[0, 2 * j + 1]], axis=0)
            s1 = _nt(q2, k_ref[pl.ds(off, nk), sl]) + bias
            s2 = _nt(q2, kc_ref[:, sl]) - bound
            if use_max:
                m = jnp.maximum(jnp.max(s1, axis=-1, keepdims=True), jnp.max(s2, axis=-1, keepdims=True))
                s1, s2 = s1 - m, s2 - m
            o = (jnp.dot(jnp.exp2(s1).astype(BF16), v_ref[pl.ds(off, nk), sl2], preferred_element_type=F32)
                 + jnp.dot(jnp.exp2(s2).astype(BF16), vc_ref[:, sl2], preferred_element_type=F32))
            res = o[:, :LANES] / o[:, LANES:LANES + 1]
            o_ref[:, sl] = jnp.where(low, res[:tq], res[tq:]).astype(o_ref.dtype)

    @pl.when(safe)
    def _():
        attend(False)

    @pl.when(jnp.logical_not(safe))
    def _():
        attend(True)


def _nbr(prep, dv2, abm, par, *, n_batch, seq, n_ctx, with_ctx):
    rows = seq // GRID_W
    tq = NA_QROWS * GRID_W
    assert n_ctx == tq and rows % NA_QROWS == 0 and rows >= NA_KROWS + NA_QROWS
    nst = rows // NA_QROWS
    steps = nst + (1 if with_ctx else 0)
    rows_out = n_batch * seq + (n_batch * n_ctx if with_ctx else 0)

    def qmap(col):
        def f(b, s, par):
            return (jnp.where(s < nst, b * nst + s, n_batch * nst + b), col)
        return f

    def case(b, s, par):
        c = jnp.where(s == 0, 0, jnp.where(s == nst - 1, 2, jnp.where(s == nst, 3, 1)))
        return (c, 0, 0, 0)

    ctx_blk = n_batch * seq // n_ctx
    return pl.pallas_call(
        functools.partial(_nbr_kernel, rows=rows),
        grid_spec=pltpu.PrefetchScalarGridSpec(
            num_scalar_prefetch=1, grid=(n_batch, steps),
            in_specs=[pl.BlockSpec((tq, 512), qmap(P_DQ // 512)),
                      pl.BlockSpec((seq, 512), lambda b, s, par: (b, P_DK // 512)),
                      pl.BlockSpec((seq, 2 * 512), lambda b, s, par: (b, 0)),
                      pl.BlockSpec((n_ctx, 512), lambda b, s, par: (ctx_blk + b, P_DK // 512)),
                      pl.BlockSpec((n_ctx, 2 * 512), lambda b, s, par: (ctx_blk + b, 0)),
                      pl.BlockSpec((1, D_HEADS, tq, NA_KROWS * GRID_W), case)],
            out_specs=pl.BlockSpec((tq, 512), qmap(0))),
        out_shape=jax.ShapeDtypeStruct((rows_out, 512), BF16),
        compiler_params=_cp(("arbitrary", "arbitrary"), VMEM_LIMIT), name="mixer_neighbourhood",
    )(par, prep, prep, dv2, prep, dv2, abm)


def _softmax_bound(q_gain, k_gain, extra):
    b = HEAD_DIM ** 0.5 * LOG2E * jnp.max(jnp.abs(q_gain)) * jnp.max(jnp.abs(k_gain)) + LOG2E * extra
    return (1.02 * b + 0.5).astype(F32)


def _nbr_bias_tables(rpb, rows, bound):
    w = GRID_W
    cidx = np.arange(w)
    col_start = np.clip(cidx - NA_COLS // 2, 0, w - NA_COLS)
    col_ok = (cidx[None, :] >= col_start[:, None]) & (cidx[None, :] < col_start[:, None] + NA_COLS)
    d_col = np.clip(cidx[None, :] - cidx[:, None] + (NA_COLS - 1), 0, 2 * NA_COLS - 2)
    n_heads, n_dr, n_dc = rpb.shape
    onehot = jnp.asarray((d_col[:, :, None] == np.arange(n_dc)[None, None, :]).astype(np.float32))
    t = jnp.einsum("hrd,qkd->hqrk", rpb.astype(F32), onehot, precision=lax.Precision.HIGHEST)
    t = t * LOG2E - bound
    t = jnp.where(jnp.asarray(col_ok)[None, :, None, :], t, NEG_INF).reshape(n_heads, w, n_dr * w)
    cases = []
    for r0 in (0, NA_QROWS, rows - NA_QROWS):
        start = int(np.clip(r0 - NA_ROWS // 2, 0, rows - NA_KROWS))
        blocks = []
        for i in range(NA_QROWS):
            r = r0 + i
            row_start = int(np.clip(r - NA_ROWS // 2, 0, rows - NA_ROWS))
            kk0 = row_start - start
            dr0 = row_start - r + (NA_ROWS - 1)
            blk = t[:, :, dr0 * w:(dr0 + NA_ROWS) * w]
            blocks.append(jnp.pad(blk, ((0, 0), (0, 0), (kk0 * w, (NA_KROWS - NA_ROWS - kk0) * w)),
                                  constant_values=NEG_INF))
        cases.append(jnp.concatenate(blocks, axis=1))
    cases.append(jnp.full((n_heads, NA_QROWS * w, NA_KROWS * w), NEG_INF, F32))
    return jnp.stack(cases)


def _swiglu_accumulate(x_ref, wg_ref, wu_ref, wd_ref, acc_ref, n_rows):
    x = x_ref[:n_rows]
    g = jnp.dot(x, wg_ref[0].astype(BF16), preferred_element_type=F32)
    u = jnp.dot(x, wu_ref[0].astype(BF16), preferred_element_type=F32)
    a = (g * jax.nn.sigmoid(g) * u).astype(BF16)
    acc_ref[:n_rows] += jnp.dot(a, wd_ref[0].astype(BF16), preferred_element_type=F32)


def _ffn_kernel(x_ref, wg_ref, wu_ref, wd_ref, o_ref, acc_ref):
    j = pl.program_id(1)

    @pl.when(j == 0)
    def _():
        acc_ref[...] = jnp.zeros_like(acc_ref)

    _swiglu_accumulate(x_ref, wg_ref, wu_ref, wd_ref, acc_ref, x_ref.shape[0])

    @pl.when(j == pl.num_programs(1) - 1)
    def _():
        o_ref[...] = acc_ref[...].astype(o_ref.dtype)


def _ffn(x, w_gate_up, w_down, layer, *, rows, tm, fc):
    _, d, ff2 = w_gate_up.shape
    nj = ff2 // 2 // fc
    return pl.pallas_call(
        _ffn_kernel,
        grid=(rows // tm, nj),
        in_specs=[pl.BlockSpec((tm, d), lambda i, j: (i, 0)),
                  pl.BlockSpec((1, d, fc), lambda i, j: (layer, 0, j)),
                  pl.BlockSpec((1, d, fc), lambda i, j: (layer, 0, nj + j)),
                  pl.BlockSpec((1, fc, d), lambda i, j: (layer, j, 0))],
        out_specs=pl.BlockSpec((tm, d), lambda i, j: (i, 0)),
        out_shape=jax.ShapeDtypeStruct((rows, d), BF16),
        scratch_shapes=[pltpu.VMEM((tm, d), F32)],
        compiler_params=_cp(("arbitrary", "arbitrary"), VMEM_LIMIT), name="swiglu",
    )(x, w_gate_up, w_gate_up, w_down)


def _moe_ffn_kernel(te_ref, tv_ref, tok_ref, hp_ref, wg_ref, wu_ref, wd_ref, o_ref, gbuf_ref, x_ref, acc_ref, sem,
                    *, rows_per_step):
    i, j = pl.program_id(0), pl.program_id(1)
    n_tiles = pl.num_programs(0)
    tm, d = x_ref.shape
    half = d // 2
    valid = tv_ref[i] > 0
    slot = i % 2

    def issue(tile, first_row, n_rows, to_slot):
        def body(r, carry):
            row = first_row + r
            pltpu.make_async_copy(hp_ref.at[pl.ds(tok_ref[tile * tm + row], 1), :],
                                  gbuf_ref.at[to_slot].at[pl.ds(row, 1), :], sem.at[to_slot]).start()
            return carry

        lax.fori_loop(0, n_rows, body, 0, unroll=8)

    @pl.when((i == 0) & (j == 0) & valid)
    def _():
        issue(0, 0, tm, 0)

    @pl.when((j == 0) & valid)
    def _():
        pltpu.make_async_copy(hp_ref.at[pl.ds(0, tm), :], gbuf_ref.at[slot], sem.at[slot]).wait()
        lo, hi = _unpack_halves(gbuf_ref[slot])
        x_ref[:, :half] = lo.astype(BF16)
        x_ref[:, half:] = hi.astype(BF16)
        acc_ref[...] = jnp.zeros_like(acc_ref)

    nxt = jnp.minimum(i + 1, n_tiles - 1)

    @pl.when((i + 1 < n_tiles) & (tv_ref[nxt] > 0) & (j * rows_per_step < tm))
    def _():
        issue(nxt, j * rows_per_step, rows_per_step, 1 - slot)

    for q in range(1, MOE_ROW_SPLITS + 1):
        n_rows = q * (tm // MOE_ROW_SPLITS)

        @pl.when(tv_ref[i] == n_rows)
        def _():
            _swiglu_accumulate(x_ref, wg_ref, wu_ref, wd_ref, acc_ref, n_rows)

    @pl.when(j == pl.num_programs(1) - 1)
    def _():
        o_ref[...] = jnp.where(valid, _pack_halves(acc_ref[...]), jnp.uint32(0))


def _moe_ffn(hp, w_gate_up, w_down, layer, tile_expert, tile_valid, row_token, *, tm, fc):
    _, _, d, ff2 = w_gate_up.shape
    nj = ff2 // 2 // fc
    n_rows = row_token.shape[0]
    issue_steps = 1 << (nj.bit_length() - 1)

    def jeff(i, j, tv):
        return jnp.where(tv[i] > 0, j, nj - 1)

    return pl.pallas_call(
        functools.partial(_moe_ffn_kernel, rows_per_step=tm // issue_steps),
        grid_spec=pltpu.PrefetchScalarGridSpec(
            num_scalar_prefetch=3, grid=(n_rows // tm, nj),
            in_specs=[pl.BlockSpec(memory_space=pl.ANY),
                      pl.BlockSpec((None, 1, d, fc), lambda i, j, te, tv, tok: (layer, te[i], 0, jeff(i, j, tv))),
                      pl.BlockSpec((None, 1, d, fc),
                                   lambda i, j, te, tv, tok: (layer, te[i], 0, nj + jeff(i, j, tv))),
                      pl.BlockSpec((None, 1, fc, d), lambda i, j, te, tv, tok: (layer, te[i], jeff(i, j, tv), 0))],
            out_specs=pl.BlockSpec((tm, d // 2), lambda i, j, te, tv, tok: (i, 0)),
            scratch_shapes=[pltpu.VMEM((2, tm, d // 2), jnp.uint32), pltpu.VMEM((tm, d), BF16),
                            pltpu.VMEM((tm, d), F32), pltpu.SemaphoreType.DMA((2,))]),
        out_shape=jax.ShapeDtypeStruct((n_rows, d // 2), jnp.uint32),
        compiler_params=_cp(("arbitrary", "arbitrary"), VMEM_LIMIT_MOE), name="moe_swiglu",
    )(tile_expert, tile_valid, row_token, hp, w_gate_up, w_gate_up, w_down)


def _combine_kernel(p0_ref, p1_ref, y_ref, x_ref, route_ref, gate_ref, o_ref, b0_ref, b1_ref, sem):
    n = b0_ref.shape[0]
    half = b0_ref.shape[1]
    base = pl.program_id(0) * n

    def issue(r, carry):
        pltpu.make_async_copy(y_ref.at[pl.ds(p0_ref[base + r], 1), :], b0_ref.at[pl.ds(r, 1), :], sem.at[0]).start()
        pltpu.make_async_copy(y_ref.at[pl.ds(p1_ref[base + r], 1), :], b1_ref.at[pl.ds(r, 1), :], sem.at[1]).start()
        return carry

    lax.fori_loop(0, n, issue, 0, unroll=8)
    pltpu.make_async_copy(y_ref.at[pl.ds(0, n), :], b0_ref, sem.at[0]).wait()
    pltpu.make_async_copy(y_ref.at[pl.ds(0, n), :], b1_ref, sem.at[1]).wait()
    route = route_ref[...]
    w0, w1 = route[:, 2:3], route[:, 3:4]
    lo0, hi0 = _unpack_halves(b0_ref[...])
    lo1, hi1 = _unpack_halves(b1_ref[...])
    gate = gate_ref[0, 0]
    o_ref[:, :half] = x_ref[:, :half] + gate[:, :half] * (w0 * lo0 + w1 * lo1)
    o_ref[:, half:] = x_ref[:, half:] + gate[:, half:] * (w0 * hi0 + w1 * hi1)


def _combine(y, x, route, mod, gate_k, pos0, pos1, *, seq, n_batch):
    t, d = x.shape
    tr = 256
    return pl.pallas_call(
        _combine_kernel,
        grid_spec=pltpu.PrefetchScalarGridSpec(
            num_scalar_prefetch=2, grid=(t // tr,),
            in_specs=[pl.BlockSpec(memory_space=pl.ANY),
                      pl.BlockSpec((tr, d), lambda i, a, b: (i, 0)),
                      pl.BlockSpec((tr, LANES), lambda i, a, b: (i, 0)),
                      pl.BlockSpec((1, 1, 1, d),
                                   lambda i, a, b: (jnp.minimum((i * tr) // seq, n_batch), gate_k, 0, 0))],
            out_specs=pl.BlockSpec((tr, d), lambda i, a, b: (i, 0)),
            scratch_shapes=[pltpu.VMEM((tr, d // 2), jnp.uint32), pltpu.VMEM((tr, d // 2), jnp.uint32),
                            pltpu.SemaphoreType.DMA((2,))]),
        out_shape=jax.ShapeDtypeStruct((t, d), F32),
        compiler_params=_cp(("arbitrary",), VMEM_LIMIT), name="moe_combine",
    )(pos0, pos1, y, x, route, _mod_view(mod))


def _route_meta(idx, n_exp, tm):
    t = idx.shape[0]
    flat = idx.reshape(-1)
    onehot = (flat[:, None] == jnp.arange(n_exp, dtype=jnp.int32)[None, :]).astype(jnp.int32)
    counts = jnp.sum(onehot, axis=0)
    rank = jnp.take_along_axis(jnp.cumsum(onehot, axis=0) - onehot, flat[:, None], axis=1)[:, 0]
    padded = ((counts + tm - 1) // tm) * tm
    ends = jnp.cumsum(padded)
    starts = ends - padded
    pos = starts[flat] + rank
    n_rows = TOP_K * t + n_exp * tm
    row_token = jnp.zeros((n_rows,), jnp.int32).at[pos].set(jnp.arange(TOP_K * t, dtype=jnp.int32) // TOP_K)
    tile_start = jnp.arange(n_rows // tm, dtype=jnp.int32) * tm
    tile_valid = tile_start < ends[-1]
    tile_expert = jnp.sum((ends[None, :] <= tile_start[:, None]).astype(jnp.int32), axis=1)
    tile_expert = jnp.minimum(tile_expert, n_exp - 1)
    last_valid = jnp.maximum(ends[-1] // tm - 1, 0)
    tile_expert = jnp.where(tile_valid, tile_expert, tile_expert[last_valid])
    sub = tm // MOE_ROW_SPLITS
    used = jnp.clip(counts[tile_expert] - (tile_start - starts[tile_expert]), 0, tm)
    tile_rows = jnp.where(tile_valid, ((used + sub - 1) // sub) * sub, 0).astype(jnp.int32)
    pos = pos.reshape(t, TOP_K).astype(jnp.int32)
    return row_token, tile_expert, tile_rows, pos[:, 0], pos[:, 1]


def _rope_tables(seq, pad_rows):
    t = jnp.arange(seq, dtype=jnp.int32)
    row = (t // GRID_W).astype(F32)
    col = (t % GRID_W).astype(F32)

    def angles(rot_dim):
        n_freq = rot_dim // 4
        freqs = ROPE_THETA ** (-jnp.arange(n_freq, dtype=F32) / n_freq)
        ang = jnp.concatenate([row[:, None] * freqs[None, :], col[:, None] * freqs[None, :]], axis=-1)
        return jnp.cos(ang), jnp.sin(ang)

    ch, sh = angles(HEAD_DIM)
    cos_h = jnp.concatenate([ch] * 4, axis=1)
    sin_h = jnp.concatenate([-sh, sh, -sh, sh], axis=1)
    cb, sb = angles(B_ROPE)
    one = jnp.ones((seq, B_NOPE), F32)
    zero = jnp.zeros((seq, B_NOPE), F32)
    tail1 = jnp.ones((seq, LANES - B_QK), F32)
    tail0 = jnp.zeros((seq, LANES - B_QK), F32)
    cos_b = jnp.concatenate([one, cb, cb, tail1], axis=1)
    sin_b = jnp.concatenate([zero, -sb, sb, tail0], axis=1)

    def pad(tab, fill):
        return jnp.concatenate([tab, jnp.full((pad_rows, LANES), fill, F32)], axis=0)

    return pad(cos_h, 1.0), pad(sin_h, 0.0), pad(cos_b, 1.0), pad(sin_b, 0.0)


def _permute_w_in(w):
    d = w.shape[0]
    o = np.cumsum([0, 512, 128, 128, B_Q_RANK, B_KV_RANK, B_ROPE, 512, 512, 512, 512, 512, 512, 512])
    aq, ak, av, bq, bkv, bkr, cq, ck, cv, cg, dq, dk, dv = [w[:, o[i]:o[i + 1]] for i in range(13)]

    def dup(m):
        return jnp.concatenate([m[:, :64], m[:, :64], m[:, 64:], m[:, 64:]], axis=1)

    z = lambda n: jnp.zeros((d, n), w.dtype)
    parts = [aq, cq, ck, dq, dk, dup(ak), dup(av), cv, cg, dv, bq, bkv,
             z(B_NOPE), bkr, z(LANES - B_QK), z(P_END - P_BKR - LANES)]
    return jnp.concatenate(parts, axis=1).astype(BF16)


def _block_diag_ones():
    i = np.arange(512) // HEAD_DIM
    return jnp.asarray((i[:, None] == i[None, :]).astype(np.float32), dtype=BF16)


def _pad_lanes(v, width):
    return jnp.concatenate([v.astype(F32), jnp.zeros((width - v.shape[0],), F32)])


def kernel(x, c, ctx, c_ctx, w_ada, b_ada, norm_mix, norm_ffn, w_in, w_out, a_q_norm, a_k_norm, a_sink,
           b_q_a_norm, b_kv_a_norm, b_w_uq, b_w_ukv, b_q_norm, b_k_norm, c_log_decay, d_q_norm, d_k_norm,
           d_rpb, ffn_w_gate_up, ffn_w_down, moe_router, moe_w_gate_up, moe_w_down):
    n_batch, seq, d = x.shape
    n_ctx = ctx.shape[1]
    depth = w_ada.shape[0]
    n_lat = n_batch * seq
    n_all = n_lat + n_batch * n_ctx
    grid_rows = seq // GRID_W
    tm = next(t for t in (1024, 512, 256) if n_lat % t == 0 and n_all % t == 0)

    cc = jnp.concatenate([c, c_ctx[None, :], jnp.zeros((8 - n_batch - 1, d), F32)], axis=0)
    mod = _ada(cc, w_ada, b_ada)
    xs = (x.reshape(n_lat, d), ctx.reshape(n_batch * n_ctx, d))
    cos_h, sin_h, cos_b, sin_b = _rope_tables(seq, 256)
    gmat = _block_diag_ones()

    y = None
    y_gate, y_mod = 0, None
    for l in range(depth):
        last = l == depth - 1
        mod_l = mod[l]
        if y is None:
            (h,) = _resnorm(xs, None, None, 0, mod_l, norm_mix[l], 0, 1, rows=n_all, seq=seq, n_batch=n_batch)
        else:
            xs, h = _resnorm(xs, y, y_mod, y_gate, mod_l, norm_mix[l], 0, 1, rows=n_all, seq=seq,
                             n_batch=n_batch)
        proj = _mm([h], _permute_w_in(w_in[l]), rows=n_all, tm=tm, tn=768, out_dtype=BF16, name="proj_in")
        gains = jnp.stack([jnp.tile(a_q_norm[l], 8), _pad_lanes(jnp.tile(a_k_norm[l], 4), 512),
                           jnp.tile(d_q_norm[l], 8), jnp.tile(d_k_norm[l], 8)]).astype(F32)
        prep, dv2 = _prep(proj, cos_h, sin_h, gmat, gains, rows=n_all, seq=seq, n_lat=n_lat)
        wuq = jnp.pad(b_w_uq[l].reshape(B_Q_RANK, B_HEADS, B_QK),
                      ((0, 0), (0, 0), (0, LANES - B_QK))).reshape(B_Q_RANK, B_HEADS * LANES).astype(BF16)
        wukv3 = b_w_ukv[l].reshape(B_KV_RANK, B_HEADS, B_NOPE + B_V)
        wukv = jnp.concatenate(
            [jnp.pad(wukv3[:, :, :B_NOPE], ((0, 0), (0, 0), (0, LANES - B_NOPE))).reshape(B_KV_RANK, -1),
             wukv3[:, :, B_NOPE:].reshape(B_KV_RANK, -1)], axis=1).astype(BF16)
        bound = _mla_logit_bound(b_q_norm[l], b_k_norm[l])
        mla_safe = (bound <= MLA_SAFE_BOUND).astype(jnp.int32).reshape(1)
        gains_b = jnp.stack([b_q_a_norm[l].astype(F32), _pad_lanes(b_kv_a_norm[l], B_Q_RANK),
                             _pad_lanes(b_q_norm[l], B_Q_RANK), _pad_lanes(b_k_norm[l], B_Q_RANK),
                             jnp.full((B_Q_RANK,), -1.0, F32) * bound] + [jnp.zeros((B_Q_RANK,), F32)] * 3)
        qb, kb, vb = _prepb(proj, cos_b, sin_b, wuq, wukv, gains_b, rows=n_all, seq=seq, n_lat=n_lat)

        sink = a_sink[l].astype(F32)
        bound_a = _softmax_bound(a_q_norm[l], a_k_norm[l], jnp.maximum(jnp.max(sink), 0.0))
        par_a = jnp.concatenate([sink * LOG2E, bound_a[None], (bound_a <= MLA_SAFE_BOUND).astype(F32)[None]])
        oa = _wina(prep, proj, par_a, n_batch=n_batch, seq=seq, n_ctx=n_ctx, with_ctx=not last)
        ob = _mla(qb, kb, vb, mla_safe, n_batch=n_batch, seq=seq, n_ctx=n_ctx, with_ctx=not last)
        ld_head = jnp.broadcast_to(c_log_decay[l].astype(F32)[:, :, None], (2, C_HEADS, LANES))
        oc_fwd, oc_rev = _retention(prep, proj, ld_head, n_batch=n_batch, seq=seq, n_ctx=n_ctx)
        bound_d = _softmax_bound(d_q_norm[l], d_k_norm[l], jnp.max(jnp.abs(d_rpb[l])))
        par_d = jnp.stack([bound_d, (bound_d <= MLA_SAFE_BOUND).astype(F32)])
        abm = _nbr_bias_tables(d_rpb[l], grid_rows, bound_d)
        od = _nbr(prep, dv2, abm, par_d, n_batch=n_batch, seq=seq, n_ctx=n_ctx, with_ctx=not last)
        rows_l = n_lat if last else n_all
        oc = _retention_finish(oc_fwd, oc_rev, proj, gmat, rows=rows_l)
        ymix = _mm([oa, ob, oc, od], w_out[l].astype(BF16), rows=rows_l, tm=tm, tn=1024, out_dtype=BF16,
                   name="proj_out")
        i = l // 2
        if l % 2 == 0:
            xs, h2 = _resnorm(xs, ymix, mod_l, 2, mod_l, norm_ffn[l], 3, 4, rows=rows_l, seq=seq,
                              n_batch=n_batch)
            y = _ffn(h2, ffn_w_gate_up, ffn_w_down, i, rows=rows_l, tm=tm, fc=512)
            y_gate, y_mod = 5, mod_l
        else:
            xs, hp, route = _resnorm(xs, ymix, mod_l, 2, mod_l, norm_ffn[l], 3, 4, rows=rows_l, seq=seq,
                                     n_batch=n_batch, router=moe_router[i])
            top_idx = route[:, :TOP_K].astype(jnp.int32)
            row_token, tile_expert, tile_valid, pos0, pos1 = _route_meta(top_idx, moe_router.shape[2], tm)
            yg = _moe_ffn(hp, moe_w_gate_up, moe_w_down, i, tile_expert, tile_valid, row_token, tm=tm, fc=512)
            xs = _combine(yg, xs, route, mod_l, 5, pos0, pos1, seq=seq, n_batch=n_batch)
            y = None
    if y is not None:
        xs, _ = _resnorm(xs, y, y_mod, y_gate, y_mod, norm_ffn[depth - 1], 3, 4, rows=xs.shape[0], seq=seq,
                         n_batch=n_batch)
    return xs[:n_lat].reshape(n_batch, seq, d)
```

```python
import functools

import numpy as np
import jax
import jax.numpy as jnp
from jax import lax
from jax.experimental import pallas as pl
from jax.experimental.pallas import tpu as pltpu

F32 = jnp.float32
BF16 = jnp.bfloat16

GRID_W = 64
HEAD_DIM = 64
ROPE_THETA = 10000.0
EPS = 1e-6
NEG_INF = -1e30

A_HEADS = 8
A_KV_HEADS = 2
A_BLOCK = 128
B_HEADS = 8
B_Q_RANK = 384
B_KV_RANK = 128
B_NOPE = 64
B_ROPE = 32
B_QK = B_NOPE + B_ROPE
B_V = 64
C_HEADS = 8
D_HEADS = 8
NA_ROWS = 8
NA_COLS = 16
TOP_K = 2

LANES = 128
NA_QROWS = 4
NA_KROWS = NA_ROWS + NA_QROWS
NA_CASES = 4
MOE_ROW_SPLITS = 4
RET_CHUNK = 256

P_AQ, P_CQ, P_CK, P_DQ, P_DK, P_AK, P_AV, P_CV, P_CG, P_DV, P_BQ, P_BKV, P_BKR, P_END = (
    0, 512, 1024, 1536, 2048, 2560, 2816, 3072, 3584, 4096, 4608, 4992, 5120, 5376)
PREP_W = P_AV

VMEM_LIMIT = 56 * 1024 * 1024
VMEM_LIMIT_MOE = 61 * 1024 * 1024

LOG2E = float(np.log2(np.e))
MLA_LOGIT_SCALE = B_QK ** -0.5 * LOG2E
MLA_SAFE_BOUND = 40.0


def _cp(sem, vmem=None):
    return pltpu.CompilerParams(dimension_semantics=sem, vmem_limit_bytes=vmem)


def _nt(a, b):
    return lax.dot_general(a, b, (((1,), (1,)), ((), ())), preferred_element_type=F32)


_HI_HALF = 0xFFFF0000


def _pack_halves(x):
    n = x.shape[1] // 2
    bits = pltpu.bitcast(x.astype(BF16).astype(F32), jnp.uint32)
    return (bits[:, :n] >> 16) | (bits[:, n:] & jnp.uint32(_HI_HALF))


def _unpack_halves(w):
    return pltpu.bitcast(w << 16, F32), pltpu.bitcast(w & jnp.uint32(_HI_HALF), F32)


def _ada_kernel(c_ref, w_ref, b_ref, o_ref):
    c = c_ref[...]
    s = (c * jax.nn.sigmoid(c)).astype(BF16)
    o_ref[0] = jnp.dot(s, w_ref[0].astype(BF16), preferred_element_type=F32) + b_ref[0]


def _ada(cc, w_ada, b_ada):
    depth, d, n = w_ada.shape
    tn = 1024
    return pl.pallas_call(
        _ada_kernel,
        grid=(depth, n // tn),
        in_specs=[pl.BlockSpec((8, d), lambda l, j: (0, 0)),
                  pl.BlockSpec((1, d, tn), lambda l, j: (l, 0, j)),
                  pl.BlockSpec((1, 1, tn), lambda l, j: (l, 0, j))],
        out_specs=pl.BlockSpec((1, 8, tn), lambda l, j: (l, 0, j)),
        out_shape=jax.ShapeDtypeStruct((depth, 8, n), F32),
        compiler_params=_cp(("arbitrary", "arbitrary"), VMEM_LIMIT),
        name="ada",
    )(cc, w_ada, b_ada.reshape(depth, 1, n))


def _resnorm_kernel(*refs, has_res, with_router, n_exp, lat_tiles):
    refs = list(refs)
    x_ref = refs.pop(0)
    if lat_tiles is not None:
        xc_ref = refs.pop(0)
        x = jnp.where(pl.program_id(0) < lat_tiles, x_ref[...], xc_ref[...])
    else:
        x = x_ref[...]
    if has_res:
        y_ref = refs.pop(0)
        gate_ref = refs.pop(0)
    gain_ref, sh_ref, sc_ref = refs[:3]
    refs = refs[3:]
    if with_router:
        router_ref = refs.pop(0)
    if has_res:
        xo_ref = refs.pop(0)
    h_ref = refs.pop(0)
    if has_res:
        x = x + gate_ref[0, 0] * y_ref[...].astype(F32)
        xo_ref[...] = x
    ms = jnp.mean(x * x, axis=-1, keepdims=True)
    h = (x * lax.rsqrt(ms + EPS) * gain_ref[...]) * (1.0 + sc_ref[0, 0]) + sh_ref[0, 0]
    if with_router:
        (route_ref,) = refs
        h_ref[...] = _pack_halves(h)
        logits = jnp.dot(h, router_ref[...], preferred_element_type=F32, precision=lax.Precision.HIGHEST)
        lane = lax.broadcasted_iota(jnp.int32, logits.shape, 1)
        logits = jnp.where(lane < n_exp, logits, -jnp.inf)
        m1 = jnp.max(logits, axis=-1, keepdims=True)
        i1 = jnp.min(jnp.where(logits == m1, lane, LANES), axis=-1, keepdims=True)
        rest = jnp.where(lane == i1, -jnp.inf, logits)
        m2 = jnp.max(rest, axis=-1, keepdims=True)
        i2 = jnp.min(jnp.where(rest == m2, lane, LANES), axis=-1, keepdims=True)
        e2 = jnp.exp(m2 - m1)
        g1 = 1.0 / (1.0 + e2)
        g2 = e2 / (1.0 + e2)
        route = jnp.where(lane == 0, i1.astype(F32),
                          jnp.where(lane == 1, i2.astype(F32),
                                    jnp.where(lane == 2, g1, jnp.where(lane == 3, g2, 0.0))))
        route_ref[...] = route
    else:
        h_ref[...] = h.astype(h_ref.dtype)


def _mod_view(mod):
    return mod.reshape(mod.shape[0], 6, 1, mod.shape[1] // 6)


def _resnorm(x, y, gate_mod, gate_k, mod, gain, sh_k, sc_k, *, rows, seq, n_batch, router=None):
    tr = 256
    has_res = y is not None
    with_router = router is not None
    split = isinstance(x, tuple)
    d = x[0].shape[1] if split else x.shape[1]

    def grp(i):
        return jnp.minimum((i * tr) // seq, n_batch)

    def modspec(k):
        return pl.BlockSpec((1, 1, 1, d), lambda i: (grp(i), k, 0, 0))

    row = pl.BlockSpec((tr, d), lambda i: (i, 0))
    lat_tiles = None
    if split:
        lat_tiles = x[0].shape[0] // tr
        in_specs = [pl.BlockSpec((tr, d), lambda i: (jnp.minimum(i, lat_tiles - 1), 0)),
                    pl.BlockSpec((tr, d), lambda i: (jnp.maximum(i - lat_tiles, 0), 0))]
        args = list(x)
    else:
        in_specs = [row]
        args = [x]
    if has_res:
        in_specs += [row, modspec(gate_k)]
        args += [y, _mod_view(gate_mod)]
    in_specs += [pl.BlockSpec((1, d), lambda i: (0, 0)), modspec(sh_k), modspec(sc_k)]
    args += [gain.reshape(1, d).astype(F32), _mod_view(mod), _mod_view(mod)]
    out_shape, out_specs = [], []
    n_exp = 0
    if with_router:
        n_exp = router.shape[1]
        router = jnp.pad(router.astype(F32), ((0, 0), (0, LANES - n_exp)))
        in_specs.append(pl.BlockSpec(router.shape, lambda i: (0, 0)))
        args.append(router)
    if has_res:
        out_shape.append(jax.ShapeDtypeStruct((rows, d), F32))
        out_specs.append(row)
    if with_router:
        out_shape += [jax.ShapeDtypeStruct((rows, d // 2), jnp.uint32), jax.ShapeDtypeStruct((rows, LANES), F32)]
        out_specs += [pl.BlockSpec((tr, d // 2), lambda i: (i, 0)), pl.BlockSpec((tr, LANES), lambda i: (i, 0))]
    else:
        out_shape.append(jax.ShapeDtypeStruct((rows, d), BF16))
        out_specs.append(row)
    kern = functools.partial(_resnorm_kernel, has_res=has_res, with_router=with_router, n_exp=n_exp,
                             lat_tiles=lat_tiles)
    return pl.pallas_call(
        kern, grid=(rows // tr,), in_specs=in_specs, out_specs=out_specs, out_shape=out_shape,
        compiler_params=_cp(("arbitrary",), VMEM_LIMIT), name="resnorm",
    )(*args)


def _mm_kernel(*refs, n_x):
    w_ref, o_ref = refs[n_x], refs[n_x + 1]
    if n_x == 1:
        x = refs[0][...]
    else:
        x = jnp.concatenate([r[...] for r in refs[:n_x]], axis=1)
    o_ref[...] = jnp.dot(x, w_ref[...], preferred_element_type=F32).astype(o_ref.dtype)


def _mm(xs, w, *, rows, tm, tn, out_dtype, name):
    n = w.shape[1]
    in_specs = [pl.BlockSpec((tm, a.shape[1]), lambda i, j: (i, 0)) for a in xs]
    in_specs.append(pl.BlockSpec((w.shape[0], tn), lambda i, j: (0, j)))
    return pl.pallas_call(
        functools.partial(_mm_kernel, n_x=len(xs)),
        grid=(rows // tm, n // tn),
        in_specs=in_specs,
        out_specs=pl.BlockSpec((tm, tn), lambda i, j: (i, j)),
        out_shape=jax.ShapeDtypeStruct((rows, n), out_dtype),
        compiler_params=_cp(("arbitrary", "arbitrary"), VMEM_LIMIT), name=name,
    )(*xs, w)


def _group_meansq(x, g_ref, width):
    x2 = x * x
    hi = x2.astype(BF16)
    lo = (x2 - hi.astype(F32)).astype(BF16)
    g = g_ref[:width, :width]
    ss = jnp.dot(hi, g, preferred_element_type=F32) + jnp.dot(lo, g, preferred_element_type=F32)
    return ss * (1.0 / HEAD_DIM)


def _prep_kernel(p_ref, dv_ref, ch_ref, sh_ref, g_ref, gains_ref, o_ref, dv2_ref):
    tr = p_ref.shape[0]

    def seg(off, width):
        return p_ref[:, off:off + width].astype(F32)

    def norm(x, row, width):
        return x * lax.rsqrt(_group_meansq(x, g_ref, width) + EPS) * gains_ref[row:row + 1, :width]

    def rope(x, width):
        lane = lax.broadcasted_iota(jnp.int32, (tr, width), 1)
        first = (lane % HEAD_DIM) < (HEAD_DIM // 2)
        cos = jnp.concatenate([ch_ref[...]] * (width // LANES), axis=1)
        sin = jnp.concatenate([sh_ref[...]] * (width // LANES), axis=1)
        swapped = jnp.where(first, pltpu.roll(x, width - HEAD_DIM // 2, 1), pltpu.roll(x, HEAD_DIM // 2, 1))
        return x * cos + swapped * sin

    scale = HEAD_DIM ** -0.5
    scale2 = scale * LOG2E
    o_ref[:, P_AQ:P_AQ + 512] = (rope(norm(seg(P_AQ, 512), 0, 512), 512) * scale2).astype(BF16)
    o_ref[:, P_CQ:P_CQ + 512] = (rope(seg(P_CQ, 512), 512) * scale).astype(BF16)
    o_ref[:, P_CK:P_CK + 512] = rope(seg(P_CK, 512), 512).astype(BF16)
    o_ref[:, P_DQ:P_DQ + 512] = (norm(seg(P_DQ, 512), 2, 512) * scale2).astype(BF16)
    o_ref[:, P_DK:P_DK + 512] = norm(seg(P_DK, 512), 3, 512).astype(BF16)
    o_ref[:, P_AK:P_AK + 256] = rope(norm(seg(P_AK, 256), 1, 256), 256).astype(BF16)
    ones_blk = jnp.where(lax.broadcasted_iota(jnp.int32, (tr, LANES), 1) < HEAD_DIM, 1.0, 0.0).astype(BF16)
    for g in range(D_HEADS // 2):
        dv2_ref[:, 2 * g * LANES:(2 * g + 1) * LANES] = dv_ref[:, g * LANES:(g + 1) * LANES]
        dv2_ref[:, (2 * g + 1) * LANES:(2 * g + 2) * LANES] = ones_blk


def _prep(proj, cos_h, sin_h, gmat, gains, *, rows, seq, n_lat):
    tr = 256
    nlat = seq // tr

    def tab(i):
        return (jnp.where(i * tr < n_lat, i % nlat, nlat), 0)

    return pl.pallas_call(
        _prep_kernel,
        grid=(rows // tr,),
        in_specs=[pl.BlockSpec((tr, PREP_W), lambda i: (i, 0)),
                  pl.BlockSpec((tr, 512), lambda i: (i, P_DV // 512)),
                  pl.BlockSpec((tr, LANES), tab),
                  pl.BlockSpec((tr, LANES), tab),
                  pl.BlockSpec((512, 512), lambda i: (0, 0)),
                  pl.BlockSpec((4, 512), lambda i: (0, 0))],
        out_specs=[pl.BlockSpec((tr, PREP_W), lambda i: (i, 0)),
                   pl.BlockSpec((tr, 2 * 512), lambda i: (i, 0))],
        out_shape=[jax.ShapeDtypeStruct((rows, PREP_W), BF16), jax.ShapeDtypeStruct((rows, 2 * 512), BF16)],
        compiler_params=_cp(("arbitrary",), VMEM_LIMIT), name="prep",
    )(proj, proj, cos_h, sin_h, gmat, gains)


def _prepb_kernel(bq_ref, bkv_ref, bkr_ref, cb_ref, sb_ref, wuq_ref, wukv_ref, gains_ref, q_ref, k_ref, v_ref):
    tr = bq_ref.shape[0]
    cq = bq_ref[...].astype(F32)
    cqn = cq * lax.rsqrt(jnp.mean(cq * cq, axis=-1, keepdims=True) + EPS) * gains_ref[0:1, :]
    qup = jnp.dot(cqn.astype(BF16), wuq_ref[...], preferred_element_type=F32)
    ckv = bkv_ref[...].astype(F32)
    ckvn = ckv * lax.rsqrt(jnp.mean(ckv * ckv, axis=-1, keepdims=True) + EPS) * gains_ref[1:2, :LANES]
    kvup = jnp.dot(ckvn.astype(BF16), wukv_ref[...], preferred_element_type=F32)
    kr = bkr_ref[...].astype(F32)
    cb, sb = cb_ref[...], sb_ref[...]
    lane = lax.broadcasted_iota(jnp.int32, (tr, LANES), 1)
    half = B_ROPE // 2
    m1 = (lane >= B_NOPE) & (lane < B_NOPE + half)
    m2 = (lane >= B_NOPE + half) & (lane < B_QK)

    def rope(x):
        swapped = jnp.where(m1, pltpu.roll(x, LANES - half, 1), jnp.where(m2, pltpu.roll(x, half, 1), 0.0))
        return x * cb + swapped * sb

    def headnorm(x, row):
        ms = jnp.sum(x * x, axis=-1, keepdims=True) * (1.0 / B_QK)
        return x * lax.rsqrt(ms + EPS) * gains_ref[row:row + 1, :LANES]

    shift_lane = lane == B_QK
    neg_bound = gains_ref[4:5, :LANES]
    for h in range(B_HEADS):
        sl = slice(h * LANES, (h + 1) * LANES)
        q = rope(headnorm(qup[:, sl], 2)) * MLA_LOGIT_SCALE
        q_ref[:, sl] = jnp.where(shift_lane, 1.0, q).astype(BF16)
        k = rope(headnorm(kvup[:, sl] + kr, 3))
        k_ref[:, sl] = jnp.where(shift_lane, neg_bound, k).astype(BF16)
    ones_blk = jnp.where(lane < B_V, 1.0, 0.0).astype(BF16)
    for g in range(B_HEADS // 2):
        v_ref[:, 2 * g * LANES:(2 * g + 1) * LANES] = kvup[:, (B_HEADS + g) * LANES:(B_HEADS + g + 1) * LANES].astype(BF16)
        v_ref[:, (2 * g + 1) * LANES:(2 * g + 2) * LANES] = ones_blk


def _prepb(proj, cos_b, sin_b, wuq, wukv, gains, *, rows, seq, n_lat):
    tr = 256
    nlat = seq // tr

    def tab(i):
        return (jnp.where(i * tr < n_lat, i % nlat, nlat), 0)

    return pl.pallas_call(
        _prepb_kernel,
        grid=(rows // tr,),
        in_specs=[pl.BlockSpec((tr, B_Q_RANK), lambda i: (i, P_BQ // B_Q_RANK)),
                  pl.BlockSpec((tr, LANES), lambda i: (i, P_BKV // LANES)),
                  pl.BlockSpec((tr, LANES), lambda i: (i, P_BKR // LANES)),
                  pl.BlockSpec((tr, LANES), tab),
                  pl.BlockSpec((tr, LANES), tab),
                  pl.BlockSpec(wuq.shape, lambda i: (0, 0)),
                  pl.BlockSpec(wukv.shape, lambda i: (0, 0)),
                  pl.BlockSpec((8, B_Q_RANK), lambda i: (0, 0))],
        out_specs=[pl.BlockSpec((tr, B_HEADS * LANES), lambda i: (i, 0)),
                   pl.BlockSpec((tr, B_HEADS * LANES), lambda i: (i, 0)),
                   pl.BlockSpec((tr, B_HEADS * LANES), lambda i: (i, 0))],
        out_shape=[jax.ShapeDtypeStruct((rows, B_HEADS * LANES), BF16),
                   jax.ShapeDtypeStruct((rows, B_HEADS * LANES), BF16),
                   jax.ShapeDtypeStruct((rows, B_HEADS * LANES), BF16)],
        compiler_params=_cp(("arbitrary",), VMEM_LIMIT), name="prep_mla",
    )(proj, proj, proj, cos_b, sin_b, wuq, wukv, gains)


def _wina_kernel(par_ref, q_ref, kp_ref, ko_ref, kn_ref, kc_ref, vp_ref, vo_ref, vn_ref, vc_ref, o_ref,
                 *, nb, seq):
    n = pl.program_id(1)
    blk = A_BLOCK
    rep = A_HEADS // A_KV_HEADS
    n_ctx = kc_ref.shape[0]
    kcat = jnp.concatenate([kp_ref[...], ko_ref[...], kn_ref[...], kc_ref[...]], axis=0)
    vcat = jnp.concatenate([vp_ref[...], vo_ref[...], vn_ref[...], vc_ref[...]], axis=0)
    n_keys = 3 * blk + n_ctx
    qi = lax.broadcasted_iota(jnp.int32, (blk, n_keys), 0)
    kj = lax.broadcasted_iota(jnp.int32, (blk, n_keys), 1)
    band = kj - blk
    kpos = n * blk + band
    valid = ((jnp.abs(qi - band) <= blk) & (kpos >= 0) & (kpos < seq) & (n < nb)) | (kj >= 3 * blk)
    low = lax.broadcasted_iota(jnp.int32, (blk, LANES), 1) < HEAD_DIM
    low_k = lax.broadcasted_iota(jnp.int32, (n_keys, LANES), 1) < HEAD_DIM
    bound = par_ref[A_HEADS]
    safe = par_ref[A_HEADS + 1] > 0.0

    def attend(use_max):
        mask = jnp.where(valid, 0.0 if use_max else -bound, NEG_INF)
        mask = jnp.concatenate([mask] * rep, axis=0)
        for g in range(A_KV_HEADS):
            kg = kcat[:, g * LANES:(g + 1) * LANES]
            v1 = jnp.where(low_k, vcat[:, g * LANES:(g + 1) * LANES], jnp.ones((), BF16))
            qs, sinks = [], []
            for r in range(rep):
                h = g * rep + r
                qc = q_ref[:, (h // 2) * LANES:(h // 2 + 1) * LANES]
                qs.append(jnp.where(low if h % 2 == 0 else jnp.logical_not(low), qc, jnp.zeros_like(qc)))
                sinks.append(jnp.full((blk, 1), par_ref[h], F32))
            s = _nt(jnp.concatenate(qs, axis=0), kg) + mask
            sink = jnp.concatenate(sinks, axis=0)
            if use_max:
                m = jnp.maximum(jnp.max(s, axis=-1, keepdims=True), sink)
                s, sink = s - m, sink - m
            else:
                sink = sink - bound
            o = jnp.dot(jnp.exp2(s).astype(BF16), v1, preferred_element_type=F32)
            res = o / (o[:, HEAD_DIM:HEAD_DIM + 1] + jnp.exp2(sink))
            for c in range(rep // 2):
                even = res[(2 * c) * blk:(2 * c + 1) * blk]
                odd = pltpu.roll(res[(2 * c + 1) * blk:(2 * c + 2) * blk], HEAD_DIM, 1)
                j = (g * rep) // 2 + c
                o_ref[:, j * LANES:(j + 1) * LANES] = jnp.where(low, even, odd).astype(o_ref.dtype)

    @pl.when(safe)
    def _():
        attend(False)

    @pl.when(jnp.logical_not(safe))
    def _():
        attend(True)


def _wina(prep, proj, sink, *, n_batch, seq, n_ctx, with_ctx):
    blk = A_BLOCK
    nb = seq // blk
    ncb = n_ctx // blk
    steps = nb + (ncb if with_ctx else 0)
    lat_blocks = n_batch * nb
    rows_out = n_batch * seq + (n_batch * n_ctx if with_ctx else 0)

    def qmap(b, n, s):
        return (jnp.where(n < nb, b * nb + n, lat_blocks + b * ncb + (n - nb)), 0)

    def kmap(delta, col):
        def f(b, n, s):
            return (b * nb + jnp.clip(n + delta, 0, nb - 1), col)
        return f

    def cmap(col):
        def f(b, n, s):
            return (n_batch * seq // n_ctx + b, col)
        return f

    kcol, vcol = P_AK // 256, P_AV // 256
    in_specs = [pl.BlockSpec((blk, 512), qmap)]
    in_specs += [pl.BlockSpec((blk, 256), kmap(dl, kcol)) for dl in (-1, 0, 1)]
    in_specs += [pl.BlockSpec((n_ctx, 256), cmap(kcol))]
    in_specs += [pl.BlockSpec((blk, 256), kmap(dl, vcol)) for dl in (-1, 0, 1)]
    in_specs += [pl.BlockSpec((n_ctx, 256), cmap(vcol))]
    return pl.pallas_call(
        functools.partial(_wina_kernel, nb=nb, seq=seq),
        grid_spec=pltpu.PrefetchScalarGridSpec(
            num_scalar_prefetch=1, grid=(n_batch, steps), in_specs=in_specs,
            out_specs=pl.BlockSpec((blk, 512), qmap)),
        out_shape=jax.ShapeDtypeStruct((rows_out, 512), BF16),
        compiler_params=_cp(("arbitrary", "arbitrary"), VMEM_LIMIT), name="mixer_window",
    )(sink, prep, prep, prep, prep, prep, proj, proj, proj, proj)


def _mla_kernel(safe_ref, q_ref, kl_ref, vl_ref, kc_ref, vc_ref, o_ref, *, nq, tk_fast, tk_exact, hps):
    tq = q_ref.shape[0]
    seq = kl_ref.shape[0]
    is_lat = pl.program_id(2) < nq
    safe = safe_ref[0] > 0
    pair_w = 2 * LANES
    lane = lax.broadcasted_iota(jnp.int32, (tq, LANES), 1)

    def q(h):
        return q_ref[:, h * LANES:(h + 1) * LANES]

    def write(nums, dens):
        for g in range(hps // 2):
            o_ref[:, g * LANES:(g + 1) * LANES] = jnp.where(
                lane < B_V, nums[2 * g] / dens[2 * g], nums[2 * g + 1] / dens[2 * g + 1]).astype(o_ref.dtype)

    def fast(chunks):
        accs = [None] * hps
        for k_ref, v_ref, off, n in chunks:
            for h in range(hps):
                g = h // 2
                p = jnp.exp2(_nt(q(h), k_ref[off:off + n, h * LANES:(h + 1) * LANES])).astype(BF16)
                d = jnp.dot(p, v_ref[off:off + n, g * pair_w:(g + 1) * pair_w], preferred_element_type=F32)
                accs[h] = d if accs[h] is None else accs[h] + d
        write([a[:, :LANES] for a in accs], [a[:, LANES:LANES + 1] for a in accs])

    ctx_chunk = (kc_ref, vc_ref, 0, kc_ref.shape[0])

    @pl.when(safe & is_lat)
    def _():
        fast([(kl_ref, vl_ref, c * tk_fast, tk_fast) for c in range(seq // tk_fast)] + [ctx_chunk])

    @pl.when(safe & jnp.logical_not(is_lat))
    def _():
        fast([ctx_chunk])

    @pl.when(jnp.logical_not(safe))
    def _():
        def step(kf, vf, carry):
            out = []
            for h in range(hps):
                m, l, acc = carry[h]
                s = _nt(q(h), kf(h))
                m_new = jnp.maximum(m, jnp.max(s, axis=-1, keepdims=True))
                a = jnp.exp2(m - m_new)
                p = jnp.exp2(s - m_new)
                l = a * l + jnp.sum(p, axis=-1, keepdims=True)
                acc = a * acc + jnp.dot(p.astype(BF16), vf(h // 2), preferred_element_type=F32)
                out.append((m_new, l, acc))
            return tuple(out)

        def body(c, carry):
            off = pl.multiple_of(c * tk_exact, tk_exact)
            return step(lambda h: kl_ref[pl.ds(off, tk_exact), h * LANES:(h + 1) * LANES],
                        lambda g: vl_ref[pl.ds(off, tk_exact), g * pair_w:g * pair_w + LANES], carry)

        init = tuple((jnp.full((tq, 1), -jnp.inf, F32), jnp.zeros((tq, 1), F32), jnp.zeros((tq, LANES), F32))
                     for _ in range(hps))
        carry = lax.fori_loop(0, jnp.where(is_lat, seq // tk_exact, 0), body, init)
        carry = step(lambda h: kc_ref[:, h * LANES:(h + 1) * LANES],
                     lambda g: vc_ref[:, g * pair_w:g * pair_w + LANES], carry)
        write([c[2] for c in carry], [c[1] for c in carry])


def _mla_logit_bound(q_gain, k_gain):
    b = B_QK * MLA_LOGIT_SCALE * jnp.max(jnp.abs(q_gain)) * jnp.max(jnp.abs(k_gain))
    return (1.02 * b + 0.5).astype(F32)


def _mla(qb, kb, vb, safe, *, n_batch, seq, n_ctx, with_ctx):
    hps = 4
    tq = 256
    assert n_ctx == tq
    nq = seq // tq
    ctx_blk = n_batch * seq // n_ctx
    rows_out = n_batch * seq + (n_batch * n_ctx if with_ctx else 0)
    kw, vw = hps * LANES, hps * B_V

    def qmap(b, j, i, s):
        return (jnp.where(i < nq, b * nq + i, ctx_blk + b), j)

    return pl.pallas_call(
        functools.partial(_mla_kernel, nq=nq, tk_fast=min(2048, seq), tk_exact=512, hps=hps),
        grid_spec=pltpu.PrefetchScalarGridSpec(
            num_scalar_prefetch=1,
            grid=(n_batch, B_HEADS // hps, nq + (1 if with_ctx else 0)),
            in_specs=[pl.BlockSpec((tq, kw), qmap),
                      pl.BlockSpec((seq, kw), lambda b, j, i, s: (b, j)),
                      pl.BlockSpec((seq, kw), lambda b, j, i, s: (b, j)),
                      pl.BlockSpec((n_ctx, kw), lambda b, j, i, s: (ctx_blk + b, j)),
                      pl.BlockSpec((n_ctx, kw), lambda b, j, i, s: (ctx_blk + b, j))],
            out_specs=pl.BlockSpec((tq, vw), qmap)),
        out_shape=jax.ShapeDtypeStruct((rows_out, 512), BF16),
        compiler_params=_cp(("arbitrary", "arbitrary", "arbitrary"), VMEM_LIMIT), name="mixer_mla",
    )(safe, qb, kb, vb, kb, vb)


def _ret_kernel(qf_ref, kf_ref, vf_ref, qr_ref, kr_ref, vr_ref, ld_ref, of_ref, or_ref,
                s_ref, dec_ref, qw_ref, kw_ref, cd_ref):
    step = pl.program_id(1)
    cc = qf_ref.shape[0]
    low = lax.broadcasted_iota(jnp.int32, (cc, LANES), 1) < HEAD_DIM
    sr = lax.broadcasted_iota(jnp.int32, (LANES, LANES), 0)
    blockdiag = (sr < HEAD_DIM) == (lax.broadcasted_iota(jnp.int32, (LANES, LANES), 1) < HEAD_DIM)

    @pl.when(step == 0)
    def _():
        s_ref[...] = jnp.zeros_like(s_ref)
        qi = lax.broadcasted_iota(jnp.int32, (cc, cc), 0)
        ki = lax.broadcasted_iota(jnp.int32, (cc, cc), 1)
        pos = lax.broadcasted_iota(jnp.int32, (cc, LANES), 0).astype(F32)
        for d in range(2):
            diff = (qi - ki) if d == 0 else (ki - qi)
            dpos = jnp.maximum(diff, 0).astype(F32)
            qpow = (pos + 1.0) if d == 0 else (cc - pos)
            kpow = (cc - 1.0 - pos) if d == 0 else pos
            for j in range(C_HEADS // 2):
                lg_e = -jnp.exp(ld_ref[d, 2 * j:2 * j + 1, :])
                lg_o = -jnp.exp(ld_ref[d, 2 * j + 1:2 * j + 2, :])
                lgl = jnp.where(low[0:1, :], lg_e, lg_o)
                wide_e = jnp.concatenate([lg_e] * (cc // LANES), axis=1)
                wide_o = jnp.concatenate([lg_o] * (cc // LANES), axis=1)
                dec_ref[d, j, :cc] = jnp.where(diff >= 0, jnp.exp(wide_e * dpos), 0.0)
                dec_ref[d, j, cc:] = jnp.where(diff >= 0, jnp.exp(wide_o * dpos), 0.0)
                qw_ref[d, j] = jnp.exp(lgl * qpow)
                kw_ref[d, j] = jnp.exp(lgl * kpow)
                cd_ref[d, j] = jnp.where(sr < HEAD_DIM, jnp.exp(lg_e * cc), jnp.exp(lg_o * cc))

    pairs = C_HEADS // 2
    dirs = ((qf_ref, kf_ref, vf_ref, of_ref), (qr_ref, kr_ref, vr_ref, or_ref))
    states = [[s_ref[d, j] for j in range(pairs)] for d in range(2)]
    outs, new_states = [], []
    for d, (q_ref, k_ref, v_ref, _) in enumerate(dirs):
        for j in range(pairs):
            sl = slice(j * LANES, (j + 1) * LANES)
            q, k, v = q_ref[:, sl], k_ref[:, sl], v_ref[:, sl]
            zero = jnp.zeros_like(q)
            q2 = jnp.concatenate([jnp.where(low, q, zero), jnp.where(low, zero, q)], axis=0)
            sc = (_nt(q2, k) * dec_ref[d, j]).astype(BF16)
            o2 = jnp.dot(sc, v, preferred_element_type=F32)
            o_intra = jnp.where(low, o2[:cc], o2[cc:])
            state = states[d][j]
            qw = (q.astype(F32) * qw_ref[d, j]).astype(BF16)
            o_cross = jnp.dot(qw, state.astype(BF16), preferred_element_type=F32)
            kw = (k.astype(F32) * kw_ref[d, j]).astype(BF16)
            kv = lax.dot_general(kw, v, (((0,), (0,)), ((), ())), preferred_element_type=F32)
            new_states.append(state * cd_ref[d, j] + jnp.where(blockdiag, kv, 0.0))
            outs.append(o_intra + o_cross)
    for d in range(2):
        for j in range(pairs):
            s_ref[d, j] = new_states[d * pairs + j]
            dirs[d][3][:, j * LANES:(j + 1) * LANES] = outs[d * pairs + j]


def _retention(prep, proj, ld_head, *, n_batch, seq, n_ctx):
    cc = RET_CHUNK
    nl, nc = seq // cc, n_ctx // cc
    steps = nc + nl
    lat_blocks = n_batch * nl
    rows = n_batch * (seq + n_ctx)

    def rowblk(b, d, s):
        c_ctx = jnp.where(d == 0, s, nc - 1 - s)
        c_lat = jnp.where(d == 0, s - nc, nl - 1 - (s - nc))
        return jnp.where(s < nc, lat_blocks + b * nc + c_ctx, b * nl + c_lat)

    def spec(d, col):
        return pl.BlockSpec((cc, 512), lambda b, s: (rowblk(b, d, s), col))

    pairs = C_HEADS // 2
    return pl.pallas_call(
        _ret_kernel,
        grid=(n_batch, steps),
        in_specs=[spec(d, col) for d in (0, 1) for col in (P_CQ // 512, P_CK // 512, P_CV // 512)]
        + [pl.BlockSpec((2, C_HEADS, LANES), lambda b, s: (0, 0, 0))],
        out_specs=[spec(0, 0), spec(1, 0)],
        out_shape=[jax.ShapeDtypeStruct((rows, 512), F32)] * 2,
        scratch_shapes=[pltpu.VMEM((2, pairs, LANES, LANES), F32),
                        pltpu.VMEM((2, pairs, 2 * cc, cc), F32),
                        pltpu.VMEM((2, pairs, cc, LANES), F32),
                        pltpu.VMEM((2, pairs, cc, LANES), F32),
                        pltpu.VMEM((2, pairs, LANES, LANES), F32)],
        compiler_params=_cp(("arbitrary", "arbitrary"), VMEM_LIMIT), name="mixer_retention",
    )(prep, prep, proj, prep, prep, proj, ld_head)


def _retfin_kernel(of_ref, or_ref, g_ref, gm_ref, out_ref):
    o = of_ref[...] + or_ref[...]
    on = o * lax.rsqrt(_group_meansq(o, gm_ref, 512) + EPS)
    g = g_ref[...].astype(F32)
    out_ref[...] = (g * jax.nn.sigmoid(g) * on).astype(out_ref.dtype)


def _retention_finish(o_fwd, o_rev, proj, gmat, *, rows):
    tr = 256
    return pl.pallas_call(
        _retfin_kernel,
        grid=(rows // tr,),
        in_specs=[pl.BlockSpec((tr, 512), lambda i: (i, 0)),
                  pl.BlockSpec((tr, 512), lambda i: (i, 0)),
                  pl.BlockSpec((tr, 512), lambda i: (i, P_CG // 512)),
                  pl.BlockSpec((512, 512), lambda i: (0, 0))],
        out_specs=pl.BlockSpec((tr, 512), lambda i: (i, 0)),
        out_shape=jax.ShapeDtypeStruct((rows, 512), BF16),
        compiler_params=_cp(("arbitrary",), VMEM_LIMIT), name="retention_finish",
    )(o_fwd, o_rev, proj, gmat)


def _nbr_kernel(par_ref, q_ref, k_ref, v_ref, kc_ref, vc_ref, abm_ref, o_ref, *, rows):
    st = pl.program_id(1)
    start = jnp.clip(NA_QROWS * st - NA_ROWS // 2, 0, rows - NA_KROWS)
    off = pl.multiple_of(start * GRID_W, GRID_W)
    nk = NA_KROWS * GRID_W
    tq = q_ref.shape[0]
    low = lax.broadcasted_iota(jnp.int32, (tq, LANES), 1) < HEAD_DIM
    bound = par_ref[0]
    safe = par_ref[1] > 0.0
    pair_w = 2 * LANES

    def attend(use_max):
        for j in range(D_HEADS // 2):
            sl = slice(j * LANES, (j + 1) * LANES)
            sl2 = slice(j * pair_w, (j + 1) * pair_w)
            qc = q_ref[:, sl]
            zero = jnp.zeros_like(qc)
            q2 = jnp.concatenate([jnp.where(low, qc, zero), jnp.where(low, zero, qc)], axis=0)
            bias = jnp.concatenate([abm_ref[0, 2 * j], abm_ref[0, 2 * j + 1]], axis=0)
            s1 = _nt(q2, k_ref[pl.ds(off, nk), sl]) + bias
            s2 = _nt(q2, kc_ref[:, sl]) - bound
            if use_max:
                m = jnp.maximum(jnp.max(s1, axis=-1, keepdims=True), jnp.max(s2, axis=-1, keepdims=True))
                s1, s2 = s1 - m, s2 - m
            o = (jnp.dot(jnp.exp2(s1).astype(BF16), v_ref[pl.ds(off, nk), sl2], preferred_element_type=F32)
                 + jnp.dot(jnp.exp2(s2).astype(BF16), vc_ref[:, sl2], preferred_element_type=F32))
            res = o[:, :LANES] / o[:, LANES:LANES + 1]
            o_ref[:, sl] = jnp.where(low, res[:tq], res[tq:]).astype(o_ref.dtype)

    @pl.when(safe)
    def _():
        attend(False)

    @pl.when(jnp.logical_not(safe))
    def _():
        attend(True)


def _nbr(prep, dv2, abm, par, *, n_batch, seq, n_ctx, with_ctx):
    rows = seq // GRID_W
    tq = NA_QROWS * GRID_W
    assert n_ctx == tq and rows % NA_QROWS == 0 and rows >= NA_KROWS + NA_QROWS
    nst = rows // NA_QROWS
    steps = nst + (1 if with_ctx else 0)
    rows_out = n_batch * seq + (n_batch * n_ctx if with_ctx else 0)

    def qmap(col):
        def f(b, s, par):
            return (jnp.where(s < nst, b * nst + s, n_batch * nst + b), col)
        return f

    def case(b, s, par):
        c = jnp.where(s == 0, 0, jnp.where(s == nst - 1, 2, jnp.where(s == nst, 3, 1)))
        return (c, 0, 0, 0)

    ctx_blk = n_batch * seq // n_ctx
    return pl.pallas_call(
        functools.partial(_nbr_kernel, rows=rows),
        grid_spec=pltpu.PrefetchScalarGridSpec(
            num_scalar_prefetch=1, grid=(n_batch, steps),
            in_specs=[pl.BlockSpec((tq, 512), qmap(P_DQ // 512)),
                      pl.BlockSpec((seq, 512), lambda b, s, par: (b, P_DK // 512)),
                      pl.BlockSpec((seq, 2 * 512), lambda b, s, par: (b, 0)),
                      pl.BlockSpec((n_ctx, 512), lambda b, s, par: (ctx_blk + b, P_DK // 512)),
                      pl.BlockSpec((n_ctx, 2 * 512), lambda b, s, par: (ctx_blk + b, 0)),
                      pl.BlockSpec((1, D_HEADS, tq, NA_KROWS * GRID_W), case)],
            out_specs=pl.BlockSpec((tq, 512), qmap(0))),
        out_shape=jax.ShapeDtypeStruct((rows_out, 512), BF16),
        compiler_params=_cp(("arbitrary", "arbitrary"), VMEM_LIMIT), name="mixer_neighbourhood",
    )(par, prep, prep, dv2, prep, dv2, abm)


def _softmax_bound(q_gain, k_gain, extra):
    b = HEAD_DIM ** 0.5 * LOG2E * jnp.max(jnp.abs(q_gain)) * jnp.max(jnp.abs(k_gain)) + LOG2E * extra
    return (1.02 * b + 0.5).astype(F32)


def _nbr_bias_tables(rpb, rows, bound):
    w = GRID_W
    cidx = np.arange(w)
    col_start = np.clip(cidx - NA_COLS // 2, 0, w - NA_COLS)
    col_ok = (cidx[None, :] >= col_start[:, None]) & (cidx[None, :] < col_start[:, None] + NA_COLS)
    d_col = np.clip(cidx[None, :] - cidx[:, None] + (NA_COLS - 1), 0, 2 * NA_COLS - 2)
    n_heads, n_dr, n_dc = rpb.shape
    onehot = jnp.asarray((d_col[:, :, None] == np.arange(n_dc)[None, None, :]).astype(np.float32))
    t = jnp.einsum("hrd,qkd->hqrk", rpb.astype(F32), onehot, precision=lax.Precision.HIGHEST)
    t = t * LOG2E - bound
    t = jnp.where(jnp.asarray(col_ok)[None, :, None, :], t, NEG_INF).reshape(n_heads, w, n_dr * w)
    cases = []
    for r0 in (0, NA_QROWS, rows - NA_QROWS):
        start = int(np.clip(r0 - NA_ROWS // 2, 0, rows - NA_KROWS))
        blocks = []
        for i in range(NA_QROWS):
            r = r0 + i
            row_start = int(np.clip(r - NA_ROWS // 2, 0, rows - NA_ROWS))
            kk0 = row_start - start
            dr0 = row_start - r + (NA_ROWS - 1)
            blk = t[:, :, dr0 * w:(dr0 + NA_ROWS) * w]
            blocks.append(jnp.pad(blk, ((0, 0), (0, 0), (kk0 * w, (NA_KROWS - NA_ROWS - kk0) * w)),
                                  constant_values=NEG_INF))
        cases.append(jnp.concatenate(blocks, axis=1))
    cases.append(jnp.full((n_heads, NA_QROWS * w, NA_KROWS * w), NEG_INF, F32))
    return jnp.stack(cases)


def _swiglu_accumulate(x_ref, wg_ref, wu_ref, wd_ref, acc_ref, n_rows):
    x = x_ref[:n_rows]
    g = jnp.dot(x, wg_ref[0].astype(BF16), preferred_element_type=F32)
    u = jnp.dot(x, wu_ref[0].astype(BF16), preferred_element_type=F32)
    a = (g * jax.nn.sigmoid(g) * u).astype(BF16)
    acc_ref[:n_rows] += jnp.dot(a, wd_ref[0].astype(BF16), preferred_element_type=F32)


def _ffn_kernel(x_ref, wg_ref, wu_ref, wd_ref, o_ref, acc_ref):
    j = pl.program_id(1)

    @pl.when(j == 0)
    def _():
        acc_ref[...] = jnp.zeros_like(acc_ref)

    _swiglu_accumulate(x_ref, wg_ref, wu_ref, wd_ref, acc_ref, x_ref.shape[0])

    @pl.when(j == pl.num_programs(1) - 1)
    def _():
        o_ref[...] = acc_ref[...].astype(o_ref.dtype)


def _ffn(x, w_gate_up, w_down, layer, *, rows, tm, fc):
    _, d, ff2 = w_gate_up.shape
    nj = ff2 // 2 // fc
    return pl.pallas_call(
        _ffn_kernel,
        grid=(rows // tm, nj),
        in_specs=[pl.BlockSpec((tm, d), lambda i, j: (i, 0)),
                  pl.BlockSpec((1, d, fc), lambda i, j: (layer, 0, j)),
                  pl.BlockSpec((1, d, fc), lambda i, j: (layer, 0, nj + j)),
                  pl.BlockSpec((1, fc, d), lambda i, j: (layer, j, 0))],
        out_specs=pl.BlockSpec((tm, d), lambda i, j: (i, 0)),
        out_shape=jax.ShapeDtypeStruct((rows, d), BF16),
        scratch_shapes=[pltpu.VMEM((tm, d), F32)],
        compiler_params=_cp(("arbitrary", "arbitrary"), VMEM_LIMIT), name="swiglu",
    )(x, w_gate_up, w_gate_up, w_down)


def _moe_ffn_kernel(te_ref, tv_ref, tok_ref, hp_ref, wg_ref, wu_ref, wd_ref, o_ref, gbuf_ref, x_ref, acc_ref, sem,
                    *, rows_per_step):
    i, j = pl.program_id(0), pl.program_id(1)
    n_tiles = pl.num_programs(0)
    tm, d = x_ref.shape
    half = d // 2
    valid = tv_ref[i] > 0
    slot = i % 2

    def issue(tile, first_row, n_rows, to_slot):
        def body(r, carry):
            row = first_row + r
            pltpu.make_async_copy(hp_ref.at[pl.ds(tok_ref[tile * tm + row], 1), :],
                                  gbuf_ref.at[to_slot].at[pl.ds(row, 1), :], sem.at[to_slot]).start()
            return carry

        lax.fori_loop(0, n_rows, body, 0, unroll=8)

    @pl.when((i == 0) & (j == 0) & valid)
    def _():
        issue(0, 0, tm, 0)

    @pl.when((j == 0) & valid)
    def _():
        pltpu.make_async_copy(hp_ref.at[pl.ds(0, tm), :], gbuf_ref.at[slot], sem.at[slot]).wait()
        lo, hi = _unpack_halves(gbuf_ref[slot])
        x_ref[:, :half] = lo.astype(BF16)
        x_ref[:, half:] = hi.astype(BF16)
        acc_ref[...] = jnp.zeros_like(acc_ref)

    nxt = jnp.minimum(i + 1, n_tiles - 1)

    @pl.when((i + 1 < n_tiles) & (tv_ref[nxt] > 0) & (j * rows_per_step < tm))
    def _():
        issue(nxt, j * rows_per_step, rows_per_step, 1 - slot)

    for q in range(1, MOE_ROW_SPLITS + 1):
        n_rows = q * (tm // MOE_ROW_SPLITS)

        @pl.when(tv_ref[i] == n_rows)
        def _():
            _swiglu_accumulate(x_ref, wg_ref, wu_ref, wd_ref, acc_ref, n_rows)

    @pl.when(j == pl.num_programs(1) - 1)
    def _():
        o_ref[...] = jnp.where(valid, _pack_halves(acc_ref[...]), jnp.uint32(0))


def _moe_ffn(hp, w_gate_up, w_down, layer, tile_expert, tile_valid, row_token, *, tm, fc):
    _, _, d, ff2 = w_gate_up.shape
    nj = ff2 // 2 // fc
    n_rows = row_token.shape[0]
    issue_steps = 1 << (nj.bit_length() - 1)

    def jeff(i, j, tv):
        return jnp.where(tv[i] > 0, j, nj - 1)

    return pl.pallas_call(
        functools.partial(_moe_ffn_kernel, rows_per_step=tm // issue_steps),
        grid_spec=pltpu.PrefetchScalarGridSpec(
            num_scalar_prefetch=3, grid=(n_rows // tm, nj),
            in_specs=[pl.BlockSpec(memory_space=pl.ANY),
                      pl.BlockSpec((None, 1, d, fc), lambda i, j, te, tv, tok: (layer, te[i], 0, jeff(i, j, tv))),
                      pl.BlockSpec((None, 1, d, fc),
                                   lambda i, j, te, tv, tok: (layer, te[i], 0, nj + jeff(i, j, tv))),
                      pl.BlockSpec((None, 1, fc, d), lambda i, j, te, tv, tok: (layer, te[i], jeff(i, j, tv), 0))],
            out_specs=pl.BlockSpec((tm, d // 2), lambda i, j, te, tv, tok: (i, 0)),
            scratch_shapes=[pltpu.VMEM((2, tm, d // 2), jnp.uint32), pltpu.VMEM((tm, d), BF16),
                            pltpu.VMEM((tm, d), F32), pltpu.SemaphoreType.DMA((2,))]),
        out_shape=jax.ShapeDtypeStruct((n_rows, d // 2), jnp.uint32),
        compiler_params=_cp(("arbitrary", "arbitrary"), VMEM_LIMIT_MOE), name="moe_swiglu",
    )(tile_expert, tile_valid, row_token, hp, w_gate_up, w_gate_up, w_down)


def _combine_kernel(p0_ref, p1_ref, y_ref, x_ref, route_ref, gate_ref, o_ref, b0_ref, b1_ref, sem):
    _, n, half = b0_ref.shape
    i = pl.program_id(0)
    slot = i % 2

    def issue(tile, to_slot):
        def body(r, carry):
            k = tile * n + r
            pltpu.make_async_copy(y_ref.at[pl.ds(p0_ref[k], 1), :], b0_ref.at[to_slot].at[pl.ds(r, 1), :],
                                  sem.at[0, to_slot]).start()
            pltpu.make_async_copy(y_ref.at[pl.ds(p1_ref[k], 1), :], b1_ref.at[to_slot].at[pl.ds(r, 1), :],
                                  sem.at[1, to_slot]).start()
            return carry

        lax.fori_loop(0, n, body, 0, unroll=8)

    @pl.when(i == 0)
    def _():
        issue(0, 0)

    pltpu.make_async_copy(y_ref.at[pl.ds(0, n), :], b0_ref.at[slot], sem.at[0, slot]).wait()
    pltpu.make_async_copy(y_ref.at[pl.ds(0, n), :], b1_ref.at[slot], sem.at[1, slot]).wait()

    @pl.when(i + 1 < pl.num_programs(0))
    def _():
        issue(i + 1, 1 - slot)

    route = route_ref[...]
    w0, w1 = route[:, 2:3], route[:, 3:4]
    lo0, hi0 = _unpack_halves(b0_ref[slot])
    lo1, hi1 = _unpack_halves(b1_ref[slot])
    gate = gate_ref[0, 0]
    o_ref[:, :half] = x_ref[:, :half] + gate[:, :half] * (w0 * lo0 + w1 * lo1)
    o_ref[:, half:] = x_ref[:, half:] + gate[:, half:] * (w0 * hi0 + w1 * hi1)


def _combine(y, x, route, mod, gate_k, pos0, pos1, *, seq, n_batch):
    t, d = x.shape
    tr = 256
    return pl.pallas_call(
        _combine_kernel,
        grid_spec=pltpu.PrefetchScalarGridSpec(
            num_scalar_prefetch=2, grid=(t // tr,),
            in_specs=[pl.BlockSpec(memory_space=pl.ANY),
                      pl.BlockSpec((tr, d), lambda i, a, b: (i, 0)),
                      pl.BlockSpec((tr, LANES), lambda i, a, b: (i, 0)),
                      pl.BlockSpec((1, 1, 1, d),
                                   lambda i, a, b: (jnp.minimum((i * tr) // seq, n_batch), gate_k, 0, 0))],
            out_specs=pl.BlockSpec((tr, d), lambda i, a, b: (i, 0)),
            scratch_shapes=[pltpu.VMEM((2, tr, d // 2), jnp.uint32), pltpu.VMEM((2, tr, d // 2), jnp.uint32),
                            pltpu.SemaphoreType.DMA((2, 2))]),
        out_shape=jax.ShapeDtypeStruct((t, d), F32),
        compiler_params=_cp(("arbitrary",), VMEM_LIMIT), name="moe_combine",
    )(pos0, pos1, y, x, route, _mod_view(mod))


def _route_meta(idx, n_exp, tm):
    t = idx.shape[0]
    flat = idx.reshape(-1)
    onehot = (flat[:, None] == jnp.arange(n_exp, dtype=jnp.int32)[None, :]).astype(jnp.int32)
    counts = jnp.sum(onehot, axis=0)
    rank = jnp.take_along_axis(jnp.cumsum(onehot, axis=0) - onehot, flat[:, None], axis=1)[:, 0]
    padded = ((counts + tm - 1) // tm) * tm
    ends = jnp.cumsum(padded)
    starts = ends - padded
    pos = starts[flat] + rank
    n_rows = TOP_K * t + n_exp * tm
    row_token = jnp.zeros((n_rows,), jnp.int32).at[pos].set(jnp.arange(TOP_K * t, dtype=jnp.int32) // TOP_K)
    tile_start = jnp.arange(n_rows // tm, dtype=jnp.int32) * tm
    tile_valid = tile_start < ends[-1]
    tile_expert = jnp.sum((ends[None, :] <= tile_start[:, None]).astype(jnp.int32), axis=1)
    tile_expert = jnp.minimum(tile_expert, n_exp - 1)
    last_valid = jnp.maximum(ends[-1] // tm - 1, 0)
    tile_expert = jnp.where(tile_valid, tile_expert, tile_expert[last_valid])
    sub = tm // MOE_ROW_SPLITS
    used = jnp.clip(counts[tile_expert] - (tile_start - starts[tile_expert]), 0, tm)
    tile_rows = jnp.where(tile_valid, ((used + sub - 1) // sub) * sub, 0).astype(jnp.int32)
    pos = pos.reshape(t, TOP_K).astype(jnp.int32)
    return row_token, tile_expert, tile_rows, pos[:, 0], pos[:, 1]


def _rope_tables(seq, pad_rows):
    t = jnp.arange(seq, dtype=jnp.int32)
    row = (t // GRID_W).astype(F32)
    col = (t % GRID_W).astype(F32)

    def angles(rot_dim):
        n_freq = rot_dim // 4
        freqs = ROPE_THETA ** (-jnp.arange(n_freq, dtype=F32) / n_freq)
        ang = jnp.concatenate([row[:, None] * freqs[None, :], col[:, None] * freqs[None, :]], axis=-1)
        return jnp.cos(ang), jnp.sin(ang)

    ch, sh = angles(HEAD_DIM)
    cos_h = jnp.concatenate([ch] * 4, axis=1)
    sin_h = jnp.concatenate([-sh, sh, -sh, sh], axis=1)
    cb, sb = angles(B_ROPE)
    one = jnp.ones((seq, B_NOPE), F32)
    zero = jnp.zeros((seq, B_NOPE), F32)
    tail1 = jnp.ones((seq, LANES - B_QK), F32)
    tail0 = jnp.zeros((seq, LANES - B_QK), F32)
    cos_b = jnp.concatenate([one, cb, cb, tail1], axis=1)
    sin_b = jnp.concatenate([zero, -sb, sb, tail0], axis=1)

    def pad(tab, fill):
        return jnp.concatenate([tab, jnp.full((pad_rows, LANES), fill, F32)], axis=0)

    return pad(cos_h, 1.0), pad(sin_h, 0.0), pad(cos_b, 1.0), pad(sin_b, 0.0)


def _permute_w_in(w):
    d = w.shape[0]
    o = np.cumsum([0, 512, 128, 128, B_Q_RANK, B_KV_RANK, B_ROPE, 512, 512, 512, 512, 512, 512, 512])
    aq, ak, av, bq, bkv, bkr, cq, ck, cv, cg, dq, dk, dv = [w[:, o[i]:o[i + 1]] for i in range(13)]

    def dup(m):
        return jnp.concatenate([m[:, :64], m[:, :64], m[:, 64:], m[:, 64:]], axis=1)

    z = lambda n: jnp.zeros((d, n), w.dtype)
    parts = [aq, cq, ck, dq, dk, dup(ak), dup(av), cv, cg, dv, bq, bkv,
             z(B_NOPE), bkr, z(LANES - B_QK), z(P_END - P_BKR - LANES)]
    return jnp.concatenate(parts, axis=1).astype(BF16)


def _block_diag_ones():
    i = np.arange(512) // HEAD_DIM
    return jnp.asarray((i[:, None] == i[None, :]).astype(np.float32), dtype=BF16)


def _pad_lanes(v, width):
    return jnp.concatenate([v.astype(F32), jnp.zeros((width - v.shape[0],), F32)])


def kernel(x, c, ctx, c_ctx, w_ada, b_ada, norm_mix, norm_ffn, w_in, w_out, a_q_norm, a_k_norm, a_sink,
           b_q_a_norm, b_kv_a_norm, b_w_uq, b_w_ukv, b_q_norm, b_k_norm, c_log_decay, d_q_norm, d_k_norm,
           d_rpb, ffn_w_gate_up, ffn_w_down, moe_router, moe_w_gate_up, moe_w_down):
    n_batch, seq, d = x.shape
    n_ctx = ctx.shape[1]
    depth = w_ada.shape[0]
    n_lat = n_batch * seq
    n_all = n_lat + n_batch * n_ctx
    grid_rows = seq // GRID_W
    tm = next(t for t in (1024, 512, 256) if n_lat % t == 0 and n_all % t == 0)

    cc = jnp.concatenate([c, c_ctx[None, :], jnp.zeros((8 - n_batch - 1, d), F32)], axis=0)
    mod = _ada(cc, w_ada, b_ada)
    xs = (x.reshape(n_lat, d), ctx.reshape(n_batch * n_ctx, d))
    cos_h, sin_h, cos_b, sin_b = _rope_tables(seq, 256)
    gmat = _block_diag_ones()

    y = None
    y_gate, y_mod = 0, None
    for l in range(depth):
        last = l == depth - 1
        mod_l = mod[l]
        if y is None:
            (h,) = _resnorm(xs, None, None, 0, mod_l, norm_mix[l], 0, 1, rows=n_all, seq=seq, n_batch=n_batch)
        else:
            xs, h = _resnorm(xs, y, y_mod, y_gate, mod_l, norm_mix[l], 0, 1, rows=n_all, seq=seq,
                             n_batch=n_batch)
        proj = _mm([h], _permute_w_in(w_in[l]), rows=n_all, tm=tm, tn=768, out_dtype=BF16, name="proj_in")
        gains = jnp.stack([jnp.tile(a_q_norm[l], 8), _pad_lanes(jnp.tile(a_k_norm[l], 4), 512),
                           jnp.tile(d_q_norm[l], 8), jnp.tile(d_k_norm[l], 8)]).astype(F32)
        prep, dv2 = _prep(proj, cos_h, sin_h, gmat, gains, rows=n_all, seq=seq, n_lat=n_lat)
        wuq = jnp.pad(b_w_uq[l].reshape(B_Q_RANK, B_HEADS, B_QK),
                      ((0, 0), (0, 0), (0, LANES - B_QK))).reshape(B_Q_RANK, B_HEADS * LANES).astype(BF16)
        wukv3 = b_w_ukv[l].reshape(B_KV_RANK, B_HEADS, B_NOPE + B_V)
        wukv = jnp.concatenate(
            [jnp.pad(wukv3[:, :, :B_NOPE], ((0, 0), (0, 0), (0, LANES - B_NOPE))).reshape(B_KV_RANK, -1),
             wukv3[:, :, B_NOPE:].reshape(B_KV_RANK, -1)], axis=1).astype(BF16)
        bound = _mla_logit_bound(b_q_norm[l], b_k_norm[l])
        mla_safe = (bound <= MLA_SAFE_BOUND).astype(jnp.int32).reshape(1)
        gains_b = jnp.stack([b_q_a_norm[l].astype(F32), _pad_lanes(b_kv_a_norm[l], B_Q_RANK),
                             _pad_lanes(b_q_norm[l], B_Q_RANK), _pad_lanes(b_k_norm[l], B_Q_RANK),
                             jnp.full((B_Q_RANK,), -1.0, F32) * bound] + [jnp.zeros((B_Q_RANK,), F32)] * 3)
        qb, kb, vb = _prepb(proj, cos_b, sin_b, wuq, wukv, gains_b, rows=n_all, seq=seq, n_lat=n_lat)

        sink = a_sink[l].astype(F32)
        bound_a = _softmax_bound(a_q_norm[l], a_k_norm[l], jnp.maximum(jnp.max(sink), 0.0))
        par_a = jnp.concatenate([sink * LOG2E, bound_a[None], (bound_a <= MLA_SAFE_BOUND).astype(F32)[None]])
        oa = _wina(prep, proj, par_a, n_batch=n_batch, seq=seq, n_ctx=n_ctx, with_ctx=not last)
        ob = _mla(qb, kb, vb, mla_safe, n_batch=n_batch, seq=seq, n_ctx=n_ctx, with_ctx=not last)
        ld_head = jnp.broadcast_to(c_log_decay[l].astype(F32)[:, :, None], (2, C_HEADS, LANES))
        oc_fwd, oc_rev = _retention(prep, proj, ld_head, n_batch=n_batch, seq=seq, n_ctx=n_ctx)
        bound_d = _softmax_bound(d_q_norm[l], d_k_norm[l], jnp.max(jnp.abs(d_rpb[l])))
        par_d = jnp.stack([bound_d, (bound_d <= MLA_SAFE_BOUND).astype(F32)])
        abm = _nbr_bias_tables(d_rpb[l], grid_rows, bound_d)
        od = _nbr(prep, dv2, abm, par_d, n_batch=n_batch, seq=seq, n_ctx=n_ctx, with_ctx=not last)
        rows_l = n_lat if last else n_all
        oc = _retention_finish(oc_fwd, oc_rev, proj, gmat, rows=rows_l)
        ymix = _mm([oa, ob, oc, od], w_out[l].astype(BF16), rows=rows_l, tm=tm, tn=1024, out_dtype=BF16,
                   name="proj_out")
        i = l // 2
        if l % 2 == 0:
            xs, h2 = _resnorm(xs, ymix, mod_l, 2, mod_l, norm_ffn[l], 3, 4, rows=rows_l, seq=seq,
                              n_batch=n_batch)
            y = _ffn(h2, ffn_w_gate_up, ffn_w_down, i, rows=rows_l, tm=tm, fc=512)
            y_gate, y_mod = 5, mod_l
        else:
            xs, hp, route = _resnorm(xs, ymix, mod_l, 2, mod_l, norm_ffn[l], 3, 4, rows=rows_l, seq=seq,
                                     n_batch=n_batch, router=moe_router[i])
            top_idx = route[:, :TOP_K].astype(jnp.int32)
            row_token, tile_expert, tile_valid, pos0, pos1 = _route_meta(top_idx, moe_router.shape[2], tm)
            yg = _moe_ffn(hp, moe_w_gate_up, moe_w_down, i, tile_expert, tile_valid, row_token, tm=tm, fc=512)
            xs = _combine(yg, xs, route, mod_l, 5, pos0, pos1, seq=seq, n_batch=n_batch)
            y = None
    if y is not None:
        xs, _ = _resnorm(xs, y, y_mod, y_gate, y_mod, norm_ffn[depth - 1], 3, 4, rows=xs.shape[0], seq=seq,
                         n_batch=n_batch)
    return xs[:n_lat].reshape(n_batch, seq, d)
```

```python
import functools

import numpy as np
import jax
import jax.numpy as jnp
from jax import lax
from jax.experimental import pallas as pl
from jax.experimental.pallas import tpu as pltpu

F32 = jnp.float32
BF16 = jnp.bfloat16

GRID_W = 64
HEAD_DIM = 64
ROPE_THETA = 10000.0
EPS = 1e-6
NEG_INF = -1e30

A_HEADS = 8
A_KV_HEADS = 2
A_BLOCK = 128
B_HEADS = 8
B_Q_RANK = 384
B_KV_RANK = 128
B_NOPE = 64
B_ROPE = 32
B_QK = B_NOPE + B_ROPE
B_V = 64
C_HEADS = 8
D_HEADS = 8
NA_ROWS = 8
NA_COLS = 16
TOP_K = 2

LANES = 128
NA_QROWS = 4
NA_KROWS = NA_ROWS + NA_QROWS
NA_CASES = 4
MOE_ROW_SPLITS = 4
RET_CHUNK = 256

P_AQ, P_CQ, P_CK, P_DQ, P_DK, P_AK, P_AV, P_CV, P_CG, P_DV, P_BQ, P_BKV, P_BKR, P_BKR_SW, P_END = (
    0, 512, 1024, 1536, 2048, 2560, 2816, 3072, 3584, 4096, 4608, 4992, 5120, 5248, 5376)
PREP_W = P_AV

VMEM_LIMIT = 56 * 1024 * 1024
VMEM_LIMIT_MOE = 61 * 1024 * 1024

LOG2E = float(np.log2(np.e))
MLA_LOGIT_SCALE = B_QK ** -0.5 * LOG2E
MLA_SAFE_BOUND = 40.0


def _cp(sem, vmem=None):
    return pltpu.CompilerParams(dimension_semantics=sem, vmem_limit_bytes=vmem)


def _nt(a, b):
    return lax.dot_general(a, b, (((1,), (1,)), ((), ())), preferred_element_type=F32)


_HI_HALF = 0xFFFF0000


def _pack_halves(x):
    n = x.shape[1] // 2
    bits = pltpu.bitcast(x.astype(BF16).astype(F32), jnp.uint32)
    return (bits[:, :n] >> 16) | (bits[:, n:] & jnp.uint32(_HI_HALF))


def _unpack_halves(w):
    return pltpu.bitcast(w << 16, F32), pltpu.bitcast(w & jnp.uint32(_HI_HALF), F32)


def _ada_kernel(c_ref, w_ref, b_ref, o_ref):
    c = c_ref[...]
    s = (c * jax.nn.sigmoid(c)).astype(BF16)
    o_ref[0] = jnp.dot(s, w_ref[0].astype(BF16), preferred_element_type=F32) + b_ref[0]


def _ada(cc, w_ada, b_ada):
    depth, d, n = w_ada.shape
    tn = 1024
    return pl.pallas_call(
        _ada_kernel,
        grid=(depth, n // tn),
        in_specs=[pl.BlockSpec((8, d), lambda l, j: (0, 0)),
                  pl.BlockSpec((1, d, tn), lambda l, j: (l, 0, j)),
                  pl.BlockSpec((1, 1, tn), lambda l, j: (l, 0, j))],
        out_specs=pl.BlockSpec((1, 8, tn), lambda l, j: (l, 0, j)),
        out_shape=jax.ShapeDtypeStruct((depth, 8, n), F32),
        compiler_params=_cp(("arbitrary", "arbitrary"), VMEM_LIMIT),
        name="ada",
    )(cc, w_ada, b_ada.reshape(depth, 1, n))


def _resnorm_kernel(*refs, has_res, with_router, n_exp, lat_tiles):
    refs = list(refs)
    x_ref = refs.pop(0)
    if lat_tiles is not None:
        xc_ref = refs.pop(0)
        x = jnp.where(pl.program_id(0) < lat_tiles, x_ref[...], xc_ref[...])
    else:
        x = x_ref[...]
    if has_res:
        y_ref = refs.pop(0)
        gate_ref = refs.pop(0)
    gain_ref, sh_ref, sc_ref = refs[:3]
    refs = refs[3:]
    if with_router:
        router_ref = refs.pop(0)
    if has_res:
        xo_ref = refs.pop(0)
    h_ref = refs.pop(0)
    if has_res:
        x = x + gate_ref[0, 0] * y_ref[...].astype(F32)
        xo_ref[...] = x
    ms = jnp.mean(x * x, axis=-1, keepdims=True)
    h = (x * lax.rsqrt(ms + EPS) * gain_ref[...]) * (1.0 + sc_ref[0, 0]) + sh_ref[0, 0]
    if with_router:
        (route_ref,) = refs
        h_ref[...] = _pack_halves(h)
        logits = jnp.dot(h, router_ref[...], preferred_element_type=F32, precision=lax.Precision.HIGHEST)
        lane = lax.broadcasted_iota(jnp.int32, logits.shape, 1)
        logits = jnp.where(lane < n_exp, logits, -jnp.inf)
        m1 = jnp.max(logits, axis=-1, keepdims=True)
        i1 = jnp.min(jnp.where(logits == m1, lane, LANES), axis=-1, keepdims=True)
        rest = jnp.where(lane == i1, -jnp.inf, logits)
        m2 = jnp.max(rest, axis=-1, keepdims=True)
        i2 = jnp.min(jnp.where(rest == m2, lane, LANES), axis=-1, keepdims=True)
        e2 = jnp.exp(m2 - m1)
        g1 = 1.0 / (1.0 + e2)
        g2 = e2 / (1.0 + e2)
        route = jnp.where(lane == 0, i1.astype(F32),
                          jnp.where(lane == 1, i2.astype(F32),
                                    jnp.where(lane == 2, g1, jnp.where(lane == 3, g2, 0.0))))
        route_ref[...] = route
    else:
        h_ref[...] = h.astype(h_ref.dtype)


def _mod_view(mod):
    return mod.reshape(mod.shape[0], 6, 1, mod.shape[1] // 6)


def _resnorm(x, y, gate_mod, gate_k, mod, gain, sh_k, sc_k, *, rows, seq, n_batch, router=None):
    tr = 256
    has_res = y is not None
    with_router = router is not None
    split = isinstance(x, tuple)
    d = x[0].shape[1] if split else x.shape[1]

    def grp(i):
        return jnp.minimum((i * tr) // seq, n_batch)

    def modspec(k):
        return pl.BlockSpec((1, 1, 1, d), lambda i: (grp(i), k, 0, 0))

    row = pl.BlockSpec((tr, d), lambda i: (i, 0))
    lat_tiles = None
    if split:
        lat_tiles = x[0].shape[0] // tr
        in_specs = [pl.BlockSpec((tr, d), lambda i: (jnp.minimum(i, lat_tiles - 1), 0)),
                    pl.BlockSpec((tr, d), lambda i: (jnp.maximum(i - lat_tiles, 0), 0))]
        args = list(x)
    else:
        in_specs = [row]
        args = [x]
    if has_res:
        in_specs += [row, modspec(gate_k)]
        args += [y, _mod_view(gate_mod)]
    in_specs += [pl.BlockSpec((1, d), lambda i: (0, 0)), modspec(sh_k), modspec(sc_k)]
    args += [gain.reshape(1, d).astype(F32), _mod_view(mod), _mod_view(mod)]
    out_shape, out_specs = [], []
    n_exp = 0
    if with_router:
        n_exp = router.shape[1]
        router = jnp.pad(router.astype(F32), ((0, 0), (0, LANES - n_exp)))
        in_specs.append(pl.BlockSpec(router.shape, lambda i: (0, 0)))
        args.append(router)
    if has_res:
        out_shape.append(jax.ShapeDtypeStruct((rows, d), F32))
        out_specs.append(row)
    if with_router:
        out_shape += [jax.ShapeDtypeStruct((rows, d // 2), jnp.uint32), jax.ShapeDtypeStruct((rows, LANES), F32)]
        out_specs += [pl.BlockSpec((tr, d // 2), lambda i: (i, 0)), pl.BlockSpec((tr, LANES), lambda i: (i, 0))]
    else:
        out_shape.append(jax.ShapeDtypeStruct((rows, d), BF16))
        out_specs.append(row)
    kern = functools.partial(_resnorm_kernel, has_res=has_res, with_router=with_router, n_exp=n_exp,
                             lat_tiles=lat_tiles)
    return pl.pallas_call(
        kern, grid=(rows // tr,), in_specs=in_specs, out_specs=out_specs, out_shape=out_shape,
        compiler_params=_cp(("arbitrary",), VMEM_LIMIT), name="resnorm",
    )(*args)


def _mm_kernel(*refs, n_x):
    w_ref, o_ref = refs[n_x], refs[n_x + 1]
    if n_x == 1:
        x = refs[0][...]
    else:
        x = jnp.concatenate([r[...] for r in refs[:n_x]], axis=1)
    o_ref[...] = jnp.dot(x, w_ref[...], preferred_element_type=F32).astype(o_ref.dtype)


def _mm(xs, w, *, rows, tm, tn, out_dtype, name):
    n = w.shape[1]
    in_specs = [pl.BlockSpec((tm, a.shape[1]), lambda i, j: (i, 0)) for a in xs]
    in_specs.append(pl.BlockSpec((w.shape[0], tn), lambda i, j: (0, j)))
    return pl.pallas_call(
        functools.partial(_mm_kernel, n_x=len(xs)),
        grid=(rows // tm, n // tn),
        in_specs=in_specs,
        out_specs=pl.BlockSpec((tm, tn), lambda i, j: (i, j)),
        out_shape=jax.ShapeDtypeStruct((rows, n), out_dtype),
        compiler_params=_cp(("arbitrary", "arbitrary"), VMEM_LIMIT), name=name,
    )(*xs, w)


def _group_meansq(x, g_ref, width):
    x2 = x * x
    hi = x2.astype(BF16)
    lo = (x2 - hi.astype(F32)).astype(BF16)
    g = g_ref[:width, :width]
    ss = jnp.dot(hi, g, preferred_element_type=F32) + jnp.dot(lo, g, preferred_element_type=F32)
    return ss * (1.0 / HEAD_DIM)


def _prep_kernel(p_ref, dv_ref, ch_ref, sh_ref, g_ref, gains_ref, o_ref, dv2_ref):
    tr = p_ref.shape[0]

    def seg(off, width):
        return p_ref[:, off:off + width].astype(F32)

    def norm(x, row, width):
        return x * lax.rsqrt(_group_meansq(x, g_ref, width) + EPS) * gains_ref[row:row + 1, :width]

    def rope(x, width):
        lane = lax.broadcasted_iota(jnp.int32, (tr, width), 1)
        first = (lane % HEAD_DIM) < (HEAD_DIM // 2)
        cos = jnp.concatenate([ch_ref[...]] * (width // LANES), axis=1)
        sin = jnp.concatenate([sh_ref[...]] * (width // LANES), axis=1)
        swapped = jnp.where(first, pltpu.roll(x, width - HEAD_DIM // 2, 1), pltpu.roll(x, HEAD_DIM // 2, 1))
        return x * cos + swapped * sin

    scale = HEAD_DIM ** -0.5
    scale2 = scale * LOG2E
    o_ref[:, P_AQ:P_AQ + 512] = (rope(norm(seg(P_AQ, 512), 0, 512), 512) * scale2).astype(BF16)
    o_ref[:, P_CQ:P_CQ + 512] = (rope(seg(P_CQ, 512), 512) * scale).astype(BF16)
    o_ref[:, P_CK:P_CK + 512] = rope(seg(P_CK, 512), 512).astype(BF16)
    o_ref[:, P_DQ:P_DQ + 512] = (norm(seg(P_DQ, 512), 2, 512) * scale2).astype(BF16)
    o_ref[:, P_DK:P_DK + 512] = norm(seg(P_DK, 512), 3, 512).astype(BF16)
    o_ref[:, P_AK:P_AK + 256] = rope(norm(seg(P_AK, 256), 1, 256), 256).astype(BF16)
    ones_blk = jnp.where(lax.broadcasted_iota(jnp.int32, (tr, LANES), 1) < HEAD_DIM, 1.0, 0.0).astype(BF16)
    for g in range(D_HEADS // 2):
        dv2_ref[:, 2 * g * LANES:(2 * g + 1) * LANES] = dv_ref[:, g * LANES:(g + 1) * LANES]
        dv2_ref[:, (2 * g + 1) * LANES:(2 * g + 2) * LANES] = ones_blk


def _prep(proj, cos_h, sin_h, gmat, gains, *, rows, seq, n_lat):
    tr = 256
    nlat = seq // tr

    def tab(i):
        return (jnp.where(i * tr < n_lat, i % nlat, nlat), 0)

    return pl.pallas_call(
        _prep_kernel,
        grid=(rows // tr,),
        in_specs=[pl.BlockSpec((tr, PREP_W), lambda i: (i, 0)),
                  pl.BlockSpec((tr, 512), lambda i: (i, P_DV // 512)),
                  pl.BlockSpec((tr, LANES), tab),
                  pl.BlockSpec((tr, LANES), tab),
                  pl.BlockSpec((512, 512), lambda i: (0, 0)),
                  pl.BlockSpec((4, 512), lambda i: (0, 0))],
        out_specs=[pl.BlockSpec((tr, PREP_W), lambda i: (i, 0)),
                   pl.BlockSpec((tr, 2 * 512), lambda i: (i, 0))],
        out_shape=[jax.ShapeDtypeStruct((rows, PREP_W), BF16), jax.ShapeDtypeStruct((rows, 2 * 512), BF16)],
        compiler_params=_cp(("arbitrary",), VMEM_LIMIT), name="prep",
    )(proj, proj, cos_h, sin_h, gmat, gains)


def _prepb_kernel(bq_ref, bkv_ref, bkr_ref, bkrs_ref, cb_ref, sb_ref, wuq_ref, wuqs_ref, wukv_ref, gains_ref,
                  q_ref, k_ref, v_ref):
    tr = bq_ref.shape[0]
    cq = bq_ref[...].astype(F32)
    cqn = (cq * lax.rsqrt(jnp.mean(cq * cq, axis=-1, keepdims=True) + EPS) * gains_ref[0:1, :]).astype(BF16)
    qup = jnp.dot(cqn, wuq_ref[...], preferred_element_type=F32)
    qup_sw = jnp.dot(cqn, wuqs_ref[...], preferred_element_type=F32)
    ckv = bkv_ref[...].astype(F32)
    ckvn = ckv * lax.rsqrt(jnp.mean(ckv * ckv, axis=-1, keepdims=True) + EPS) * gains_ref[1:2, :LANES]
    kvup = jnp.dot(ckvn.astype(BF16), wukv_ref[...], preferred_element_type=F32)
    kr = bkr_ref[...].astype(F32)
    cb, sb = cb_ref[...], sb_ref[...]
    lane = lax.broadcasted_iota(jnp.int32, (tr, LANES), 1)
    q_cos = cb * (gains_ref[2:3, :LANES] * MLA_LOGIT_SCALE)
    q_sin = sb * (gains_ref[5:6, :LANES] * MLA_LOGIT_SCALE)
    k_cos = cb * gains_ref[3:4, :LANES]
    k_rot = bkrs_ref[...].astype(F32) * (sb * gains_ref[6:7, :LANES])

    def inv_rms(x):
        return lax.rsqrt(jnp.sum(x * x, axis=-1, keepdims=True) * (1.0 / B_QK) + EPS)

    shift_lane = lane == B_QK
    neg_bound = gains_ref[4:5, :LANES]
    for h in range(B_HEADS):
        sl = slice(h * LANES, (h + 1) * LANES)
        x = qup[:, sl]
        q = (x * q_cos + qup_sw[:, sl] * q_sin) * inv_rms(x)
        q_ref[:, sl] = jnp.where(shift_lane, 1.0, q).astype(BF16)
        x = kvup[:, sl] + kr
        k = (x * k_cos + k_rot) * inv_rms(x)
        k_ref[:, sl] = jnp.where(shift_lane, neg_bound, k).astype(BF16)
    ones_blk = jnp.where(lane < B_V, 1.0, 0.0).astype(BF16)
    for g in range(B_HEADS // 2):
        v_ref[:, 2 * g * LANES:(2 * g + 1) * LANES] = kvup[:, (B_HEADS + g) * LANES:(B_HEADS + g + 1) * LANES].astype(BF16)
        v_ref[:, (2 * g + 1) * LANES:(2 * g + 2) * LANES] = ones_blk


def _prepb(proj, cos_b, sin_b, wuq, wuq_sw, wukv, gains, *, rows, seq, n_lat):
    tr = 256
    nlat = seq // tr

    def tab(i):
        return (jnp.where(i * tr < n_lat, i % nlat, nlat), 0)

    return pl.pallas_call(
        _prepb_kernel,
        grid=(rows // tr,),
        in_specs=[pl.BlockSpec((tr, B_Q_RANK), lambda i: (i, P_BQ // B_Q_RANK)),
                  pl.BlockSpec((tr, LANES), lambda i: (i, P_BKV // LANES)),
                  pl.BlockSpec((tr, LANES), lambda i: (i, P_BKR // LANES)),
                  pl.BlockSpec((tr, LANES), lambda i: (i, P_BKR_SW // LANES)),
                  pl.BlockSpec((tr, LANES), tab),
                  pl.BlockSpec((tr, LANES), tab),
                  pl.BlockSpec(wuq.shape, lambda i: (0, 0)),
                  pl.BlockSpec(wuq.shape, lambda i: (0, 0)),
                  pl.BlockSpec(wukv.shape, lambda i: (0, 0)),
                  pl.BlockSpec((8, B_Q_RANK), lambda i: (0, 0))],
        out_specs=[pl.BlockSpec((tr, B_HEADS * LANES), lambda i: (i, 0)),
                   pl.BlockSpec((tr, B_HEADS * LANES), lambda i: (i, 0)),
                   pl.BlockSpec((tr, B_HEADS * LANES), lambda i: (i, 0))],
        out_shape=[jax.ShapeDtypeStruct((rows, B_HEADS * LANES), BF16),
                   jax.ShapeDtypeStruct((rows, B_HEADS * LANES), BF16),
                   jax.ShapeDtypeStruct((rows, B_HEADS * LANES), BF16)],
        compiler_params=_cp(("arbitrary",), VMEM_LIMIT), name="prep_mla",
    )(proj, proj, proj, proj, cos_b, sin_b, wuq, wuq_sw, wukv, gains)


def _wina_kernel(par_ref, q_ref, kp_ref, ko_ref, kn_ref, kc_ref, vp_ref, vo_ref, vn_ref, vc_ref, o_ref,
                 *, nb, seq):
    n = pl.program_id(1)
    blk = A_BLOCK
    rep = A_HEADS // A_KV_HEADS
    n_ctx = kc_ref.shape[0]
    kcat = jnp.concatenate([kp_ref[...], ko_ref[...], kn_ref[...], kc_ref[...]], axis=0)
    vcat = jnp.concatenate([vp_ref[...], vo_ref[...], vn_ref[...], vc_ref[...]], axis=0)
    n_keys = 3 * blk + n_ctx
    qi = lax.broadcasted_iota(jnp.int32, (blk, n_keys), 0)
    kj = lax.broadcasted_iota(jnp.int32, (blk, n_keys), 1)
    band = kj - blk
    kpos = n * blk + band
    valid = ((jnp.abs(qi - band) <= blk) & (kpos >= 0) & (kpos < seq) & (n < nb)) | (kj >= 3 * blk)
    low = lax.broadcasted_iota(jnp.int32, (blk, LANES), 1) < HEAD_DIM
    low_k = lax.broadcasted_iota(jnp.int32, (n_keys, LANES), 1) < HEAD_DIM
    bound = par_ref[A_HEADS]
    safe = par_ref[A_HEADS + 1] > 0.0

    def attend(use_max):
        mask = jnp.where(valid, 0.0 if use_max else -bound, NEG_INF)
        mask = jnp.concatenate([mask] * rep, axis=0)
        for g in range(A_KV_HEADS):
            kg = kcat[:, g * LANES:(g + 1) * LANES]
            v1 = jnp.where(low_k, vcat[:, g * LANES:(g + 1) * LANES], jnp.ones((), BF16))
            qs, sinks = [], []
            for r in range(rep):
                h = g * rep + r
                qc = q_ref[:, (h // 2) * LANES:(h // 2 + 1) * LANES]
                qs.append(jnp.where(low if h % 2 == 0 else jnp.logical_not(low), qc, jnp.zeros_like(qc)))
                sinks.append(jnp.full((blk, 1), par_ref[h], F32))
            s = _nt(jnp.concatenate(qs, axis=0), kg) + mask
            sink = jnp.concatenate(sinks, axis=0)
            if use_max:
                m = jnp.maximum(jnp.max(s, axis=-1, keepdims=True), sink)
                s, sink = s - m, sink - m
            else:
                sink = sink - bound
            o = jnp.dot(jnp.exp2(s).astype(BF16), v1, preferred_element_type=F32)
            res = o / (o[:, HEAD_DIM:HEAD_DIM + 1] + jnp.exp2(sink))
            for c in range(rep // 2):
                even = res[(2 * c) * blk:(2 * c + 1) * blk]
                odd = pltpu.roll(res[(2 * c + 1) * blk:(2 * c + 2) * blk], HEAD_DIM, 1)
                j = (g * rep) // 2 + c
                o_ref[:, j * LANES:(j + 1) * LANES] = jnp.where(low, even, odd).astype(o_ref.dtype)

    @pl.when(safe)
    def _():
        attend(False)

    @pl.when(jnp.logical_not(safe))
    def _():
        attend(True)


def _wina(prep, proj, sink, *, n_batch, seq, n_ctx, with_ctx):
    blk = A_BLOCK
    nb = seq // blk
    ncb = n_ctx // blk
    steps = nb + (ncb if with_ctx else 0)
    lat_blocks = n_batch * nb
    rows_out = n_batch * seq + (n_batch * n_ctx if with_ctx else 0)

    def qmap(b, n, s):
        return (jnp.where(n < nb, b * nb + n, lat_blocks + b * ncb + (n - nb)), 0)

    def kmap(delta, col):
        def f(b, n, s):
            return (b * nb + jnp.clip(n + delta, 0, nb - 1), col)
        return f

    def cmap(col):
        def f(b, n, s):
            return (n_batch * seq // n_ctx + b, col)
        return f

    kcol, vcol = P_AK // 256, P_AV // 256
    in_specs = [pl.BlockSpec((blk, 512), qmap)]
    in_specs += [pl.BlockSpec((blk, 256), kmap(dl, kcol)) for dl in (-1, 0, 1)]
    in_specs += [pl.BlockSpec((n_ctx, 256), cmap(kcol))]
    in_specs += [pl.BlockSpec((blk, 256), kmap(dl, vcol)) for dl in (-1, 0, 1)]
    in_specs += [pl.BlockSpec((n_ctx, 256), cmap(vcol))]
    return pl.pallas_call(
        functools.partial(_wina_kernel, nb=nb, seq=seq),
        grid_spec=pltpu.PrefetchScalarGridSpec(
            num_scalar_prefetch=1, grid=(n_batch, steps), in_specs=in_specs,
            out_specs=pl.BlockSpec((blk, 512), qmap)),
        out_shape=jax.ShapeDtypeStruct((rows_out, 512), BF16),
        compiler_params=_cp(("arbitrary", "arbitrary"), VMEM_LIMIT), name="mixer_window",
    )(sink, prep, prep, prep, prep, prep, proj, proj, proj, proj)


def _mla_kernel(safe_ref, q_ref, kl_ref, vl_ref, kc_ref, vc_ref, o_ref, *, nq, tk_fast, tk_exact, hps):
    tq = q_ref.shape[0]
    seq = kl_ref.shape[0]
    is_lat = pl.program_id(2) < nq
    safe = safe_ref[0] > 0
    pair_w = 2 * LANES
    lane = lax.broadcasted_iota(jnp.int32, (tq, LANES), 1)

    def q(h):
        return q_ref[:, h * LANES:(h + 1) * LANES]

    def write(nums, dens):
        for g in range(hps // 2):
            o_ref[:, g * LANES:(g + 1) * LANES] = jnp.where(
                lane < B_V, nums[2 * g] / dens[2 * g], nums[2 * g + 1] / dens[2 * g + 1]).astype(o_ref.dtype)

    def fast(chunks):
        accs = [None] * hps
        for k_ref, v_ref, off, n in chunks:
            for h in range(hps):
                g = h // 2
                p = jnp.exp2(_nt(q(h), k_ref[off:off + n, h * LANES:(h + 1) * LANES])).astype(BF16)
                d = jnp.dot(p, v_ref[off:off + n, g * pair_w:(g + 1) * pair_w], preferred_element_type=F32)
                accs[h] = d if accs[h] is None else accs[h] + d
        write([a[:, :LANES] for a in accs], [a[:, LANES:LANES + 1] for a in accs])

    ctx_chunk = (kc_ref, vc_ref, 0, kc_ref.shape[0])

    @pl.when(safe & is_lat)
    def _():
        fast([(kl_ref, vl_ref, c * tk_fast, tk_fast) for c in range(seq // tk_fast)] + [ctx_chunk])

    @pl.when(safe & jnp.logical_not(is_lat))
    def _():
        fast([ctx_chunk])

    @pl.when(jnp.logical_not(safe))
    def _():
        def step(kf, vf, carry):
            out = []
            for h in range(hps):
                m, l, acc = carry[h]
                s = _nt(q(h), kf(h))
                m_new = jnp.maximum(m, jnp.max(s, axis=-1, keepdims=True))
                a = jnp.exp2(m - m_new)
                p = jnp.exp2(s - m_new)
                l = a * l + jnp.sum(p, axis=-1, keepdims=True)
                acc = a * acc + jnp.dot(p.astype(BF16), vf(h // 2), preferred_element_type=F32)
                out.append((m_new, l, acc))
            return tuple(out)

        def body(c, carry):
            off = pl.multiple_of(c * tk_exact, tk_exact)
            return step(lambda h: kl_ref[pl.ds(off, tk_exact), h * LANES:(h + 1) * LANES],
                        lambda g: vl_ref[pl.ds(off, tk_exact), g * pair_w:g * pair_w + LANES], carry)

        init = tuple((jnp.full((tq, 1), -jnp.inf, F32), jnp.zeros((tq, 1), F32), jnp.zeros((tq, LANES), F32))
                     for _ in range(hps))
        carry = lax.fori_loop(0, jnp.where(is_lat, seq // tk_exact, 0), body, init)
        carry = step(lambda h: kc_ref[:, h * LANES:(h + 1) * LANES],
                     lambda g: vc_ref[:, g * pair_w:g * pair_w + LANES], carry)
        write([c[2] for c in carry], [c[1] for c in carry])


def _mla_logit_bound(q_gain, k_gain):
    b = B_QK * MLA_LOGIT_SCALE * jnp.max(jnp.abs(q_gain)) * jnp.max(jnp.abs(k_gain))
    return (1.02 * b + 0.5).astype(F32)


def _mla(qb, kb, vb, safe, *, n_batch, seq, n_ctx, with_ctx):
    hps = 4
    tq = 256
    assert n_ctx == tq
    nq = seq // tq
    ctx_blk = n_batch * seq // n_ctx
    rows_out = n_batch * seq + (n_batch * n_ctx if with_ctx else 0)
    kw, vw = hps * LANES, hps * B_V

    def qmap(b, j, i, s):
        return (jnp.where(i < nq, b * nq + i, ctx_blk + b), j)

    return pl.pallas_call(
        functools.partial(_mla_kernel, nq=nq, tk_fast=min(2048, seq), tk_exact=512, hps=hps),
        grid_spec=pltpu.PrefetchScalarGridSpec(
            num_scalar_prefetch=1,
            grid=(n_batch, B_HEADS // hps, nq + (1 if with_ctx else 0)),
            in_specs=[pl.BlockSpec((tq, kw), qmap),
                      pl.BlockSpec((seq, kw), lambda b, j, i, s: (b, j)),
                      pl.BlockSpec((seq, kw), lambda b, j, i, s: (b, j)),
                      pl.BlockSpec((n_ctx, kw), lambda b, j, i, s: (ctx_blk + b, j)),
                      pl.BlockSpec((n_ctx, kw), lambda b, j, i, s: (ctx_blk + b, j))],
            out_specs=pl.BlockSpec((tq, vw), qmap)),
        out_shape=jax.ShapeDtypeStruct((rows_out, 512), BF16),
        compiler_params=_cp(("arbitrary", "arbitrary", "arbitrary"), VMEM_LIMIT), name="mixer_mla",
    )(safe, qb, kb, vb, kb, vb)


def _ret_kernel(qf_ref, kf_ref, vf_ref, qr_ref, kr_ref, vr_ref, ld_ref, of_ref, or_ref,
                s_ref, dec_ref, qw_ref, kw_ref, cd_ref):
    step = pl.program_id(1)
    cc = qf_ref.shape[0]
    low = lax.broadcasted_iota(jnp.int32, (cc, LANES), 1) < HEAD_DIM
    sr = lax.broadcasted_iota(jnp.int32, (LANES, LANES), 0)
    blockdiag = (sr < HEAD_DIM) == (lax.broadcasted_iota(jnp.int32, (LANES, LANES), 1) < HEAD_DIM)

    @pl.when(step == 0)
    def _():
        s_ref[...] = jnp.zeros_like(s_ref)
        qi = lax.broadcasted_iota(jnp.int32, (cc, cc), 0)
        ki = lax.broadcasted_iota(jnp.int32, (cc, cc), 1)
        pos = lax.broadcasted_iota(jnp.int32, (cc, LANES), 0).astype(F32)
        for d in range(2):
            diff = (qi - ki) if d == 0 else (ki - qi)
            dpos = jnp.maximum(diff, 0).astype(F32)
            qpow = (pos + 1.0) if d == 0 else (cc - pos)
            kpow = (cc - 1.0 - pos) if d == 0 else pos
            for j in range(C_HEADS // 2):
                lg_e = -jnp.exp(ld_ref[d, 2 * j:2 * j + 1, :])
                lg_o = -jnp.exp(ld_ref[d, 2 * j + 1:2 * j + 2, :])
                lgl = jnp.where(low[0:1, :], lg_e, lg_o)
                wide_e = jnp.concatenate([lg_e] * (cc // LANES), axis=1)
                wide_o = jnp.concatenate([lg_o] * (cc // LANES), axis=1)
                dec_ref[d, j, :cc] = jnp.where(diff >= 0, jnp.exp(wide_e * dpos), 0.0)
                dec_ref[d, j, cc:] = jnp.where(diff >= 0, jnp.exp(wide_o * dpos), 0.0)
                qw_ref[d, j] = jnp.exp(lgl * qpow)
                kw_ref[d, j] = jnp.exp(lgl * kpow)
                cd_ref[d, j] = jnp.where(sr < HEAD_DIM, jnp.exp(lg_e * cc), jnp.exp(lg_o * cc))

    pairs = C_HEADS // 2
    dirs = ((qf_ref, kf_ref, vf_ref, of_ref), (qr_ref, kr_ref, vr_ref, or_ref))
    states = [[s_ref[d, j] for j in range(pairs)] for d in range(2)]
    outs, new_states = [], []
    for d, (q_ref, k_ref, v_ref, _) in enumerate(dirs):
        for j in range(pairs):
            sl = slice(j * LANES, (j + 1) * LANES)
            q, k, v = q_ref[:, sl], k_ref[:, sl], v_ref[:, sl]
            zero = jnp.zeros_like(q)
            q2 = jnp.concatenate([jnp.where(low, q, zero), jnp.where(low, zero, q)], axis=0)
            sc = (_nt(q2, k) * dec_ref[d, j]).astype(BF16)
            o2 = jnp.dot(sc, v, preferred_element_type=F32)
            o_intra = jnp.where(low, o2[:cc], o2[cc:])
            state = states[d][j]
            qw = (q.astype(F32) * qw_ref[d, j]).astype(BF16)
            o_cross = jnp.dot(qw, state.astype(BF16), preferred_element_type=F32)
            kw = (k.astype(F32) * kw_ref[d, j]).astype(BF16)
            kv = lax.dot_general(kw, v, (((0,), (0,)), ((), ())), preferred_element_type=F32)
            new_states.append(state * cd_ref[d, j] + jnp.where(blockdiag, kv, 0.0))
            outs.append(o_intra + o_cross)
    for d in range(2):
        for j in range(pairs):
            s_ref[d, j] = new_states[d * pairs + j]
            dirs[d][3][:, j * LANES:(j + 1) * LANES] = outs[d * pairs + j]


def _retention(prep, proj, ld_head, *, n_batch, seq, n_ctx):
    cc = RET_CHUNK
    nl, nc = seq // cc, n_ctx // cc
    steps = nc + nl
    lat_blocks = n_batch * nl
    rows = n_batch * (seq + n_ctx)

    def rowblk(b, d, s):
        c_ctx = jnp.where(d == 0, s, nc - 1 - s)
        c_lat = jnp.where(d == 0, s - nc, nl - 1 - (s - nc))
        return jnp.where(s < nc, lat_blocks + b * nc + c_ctx, b * nl + c_lat)

    def spec(d, col):
        return pl.BlockSpec((cc, 512), lambda b, s: (rowblk(b, d, s), col))

    pairs = C_HEADS // 2
    return pl.pallas_call(
        _ret_kernel,
        grid=(n_batch, steps),
        in_specs=[spec(d, col) for d in (0, 1) for col in (P_CQ // 512, P_CK // 512, P_CV // 512)]
        + [pl.BlockSpec((2, C_HEADS, LANES), lambda b, s: (0, 0, 0))],
        out_specs=[spec(0, 0), spec(1, 0)],
        out_shape=[jax.ShapeDtypeStruct((rows, 512), F32)] * 2,
        scratch_shapes=[pltpu.VMEM((2, pairs, LANES, LANES), F32),
                        pltpu.VMEM((2, pairs, 2 * cc, cc), F32),
                        pltpu.VMEM((2, pairs, cc, LANES), F32),
                        pltpu.VMEM((2, pairs, cc, LANES), F32),
                        pltpu.VMEM((2, pairs, LANES, LANES), F32)],
        compiler_params=_cp(("arbitrary", "arbitrary"), VMEM_LIMIT), name="mixer_retention",
    )(prep, prep, proj, prep, prep, proj, ld_head)


def _retfin_kernel(of_ref, or_ref, g_ref, gm_ref, out_ref):
    o = of_ref[...] + or_ref[...]
    on = o * lax.rsqrt(_group_meansq(o, gm_ref, 512) + EPS)
    g = g_ref[...].astype(F32)
    out_ref[...] = (g * jax.nn.sigmoid(g) * on).astype(out_ref.dtype)


def _retention_finish(o_fwd, o_rev, proj, gmat, *, rows):
    tr = 256
    return pl.pallas_call(
        _retfin_kernel,
        grid=(rows // tr,),
        in_specs=[pl.BlockSpec((tr, 512), lambda i: (i, 0)),
                  pl.BlockSpec((tr, 512), lambda i: (i, 0)),
                  pl.BlockSpec((tr, 512), lambda i: (i, P_CG // 512)),
                  pl.BlockSpec((512, 512), lambda i: (0, 0))],
        out_specs=pl.BlockSpec((tr, 512), lambda i: (i, 0)),
        out_shape=jax.ShapeDtypeStruct((rows, 512), BF16),
        compiler_params=_cp(("arbitrary",), VMEM_LIMIT), name="retention_finish",
    )(o_fwd, o_rev, proj, gmat)


def _nbr_kernel(par_ref, q_ref, k_ref, v_ref, kc_ref, vc_ref, abm_ref, o_ref, *, rows):
    st = pl.program_id(1)
    start = jnp.clip(NA_QROWS * st - NA_ROWS // 2, 0, rows - NA_KROWS)
    off = pl.multiple_of(start * GRID_W, GRID_W)
    nk = NA_KROWS * GRID_W
    tq = q_ref.shape[0]
    low = lax.broadcasted_iota(jnp.int32, (tq, LANES), 1) < HEAD_DIM
    bound = par_ref[0]
    safe = par_ref[1] > 0.0
    pair_w = 2 * LANES

    def attend(use_max):
        for j in range(D_HEADS // 2):
            sl = slice(j * LANES, (j + 1) * LANES)
            sl2 = slice(j * pair_w, (j + 1) * pair_w)
            qc = q_ref[:, sl]
            zero = jnp.zeros_like(qc)
            q2 = jnp.concatenate([jnp.where(low, qc, zero), jnp.where(low, zero, qc)], axis=0)
            bias = jnp.concatenate([abm_ref[0, 2 * j], abm_ref[0, 2 * j + 1]], axis=0)
            s1 = _nt(q2, k_ref[pl.ds(off, nk), sl]) + bias
            s2 = _nt(q2, kc_ref[:, sl]) - bound
            if use_max:
                m = jnp.maximum(jnp.max(s1, axis=-1, keepdims=True), jnp.max(s2, axis=-1, keepdims=True))
                s1, s2 = s1 - m, s2 - m
            o = (jnp.dot(jnp.exp2(s1).astype(BF16), v_ref[pl.ds(off, nk), sl2], preferred_element_type=F32)
                 + jnp.dot(jnp.exp2(s2).astype(BF16), vc_ref[:, sl2], preferred_element_type=F32))
            res = o[:, :LANES] / o[:, LANES:LANES + 1]
            o_ref[:, sl] = jnp.where(low, res[:tq], res[tq:]).astype(o_ref.dtype)

    @pl.when(safe)
    def _():
        attend(False)

    @pl.when(jnp.logical_not(safe))
    def _():
        attend(True)


def _nbr(prep, dv2, abm, par, *, n_batch, seq, n_ctx, with_ctx):
    rows = seq // GRID_W
    tq = NA_QROWS * GRID_W
    assert n_ctx == tq and rows % NA_QROWS == 0 and rows >= NA_KROWS + NA_QROWS
    nst = rows // NA_QROWS
    steps = nst + (1 if with_ctx else 0)
    rows_out = n_batch * seq + (n_batch * n_ctx if with_ctx else 0)

    def qmap(col):
        def f(b, s, par):
            return (jnp.where(s < nst, b * nst + s, n_batch * nst + b), col)
        return f

    def case(b, s, par):
        c = jnp.where(s == 0, 0, jnp.where(s == nst - 1, 2, jnp.where(s == nst, 3, 1)))
        return (c, 0, 0, 0)

    ctx_blk = n_batch * seq // n_ctx
    return pl.pallas_call(
        functools.partial(_nbr_kernel, rows=rows),
        grid_spec=pltpu.PrefetchScalarGridSpec(
            num_scalar_prefetch=1, grid=(n_batch, steps),
            in_specs=[pl.BlockSpec((tq, 512), qmap(P_DQ // 512)),
                      pl.BlockSpec((seq, 512), lambda b, s, par: (b, P_DK // 512)),
                      pl.BlockSpec((seq, 2 * 512), lambda b, s, par: (b, 0)),
                      pl.BlockSpec((n_ctx, 512), lambda b, s, par: (ctx_blk + b, P_DK // 512)),
                      pl.BlockSpec((n_ctx, 2 * 512), lambda b, s, par: (ctx_blk + b, 0)),
                      pl.BlockSpec((1, D_HEADS, tq, NA_KROWS * GRID_W), case)],
            out_specs=pl.BlockSpec((tq, 512), qmap(0))),
        out_shape=jax.ShapeDtypeStruct((rows_out, 512), BF16),
        compiler_params=_cp(("arbitrary", "arbitrary"), VMEM_LIMIT), name="mixer_neighbourhood",
    )(par, prep, prep, dv2, prep, dv2, abm)


def _softmax_bound(q_gain, k_gain, extra):
    b = HEAD_DIM ** 0.5 * LOG2E * jnp.max(jnp.abs(q_gain)) * jnp.max(jnp.abs(k_gain)) + LOG2E * extra
    return (1.02 * b + 0.5).astype(F32)


def _nbr_bias_tables(rpb, rows, bound):
    w = GRID_W
    cidx = np.arange(w)
    col_start = np.clip(cidx - NA_COLS // 2, 0, w - NA_COLS)
    col_ok = (cidx[None, :] >= col_start[:, None]) & (cidx[None, :] < col_start[:, None] + NA_COLS)
    d_col = np.clip(cidx[None, :] - cidx[:, None] + (NA_COLS - 1), 0, 2 * NA_COLS - 2)
    n_heads, n_dr, n_dc = rpb.shape
    onehot = jnp.asarray((d_col[:, :, None] == np.arange(n_dc)[None, None, :]).astype(np.float32))
    t = jnp.einsum("hrd,qkd->hqrk", rpb.astype(F32), onehot, precision=lax.Precision.HIGHEST)
    t = t * LOG2E - bound
    t = jnp.where(jnp.asarray(col_ok)[None, :, None, :], t, NEG_INF).reshape(n_heads, w, n_dr * w)
    cases = []
    for r0 in (0, NA_QROWS, rows - NA_QROWS):
        start = int(np.clip(r0 - NA_ROWS // 2, 0, rows - NA_KROWS))
        blocks = []
        for i in range(NA_QROWS):
            r = r0 + i
            row_start = int(np.clip(r - NA_ROWS // 2, 0, rows - NA_ROWS))
            kk0 = row_start - start
            dr0 = row_start - r + (NA_ROWS - 1)
            blk = t[:, :, dr0 * w:(dr0 + NA_ROWS) * w]
            blocks.append(jnp.pad(blk, ((0, 0), (0, 0), (kk0 * w, (NA_KROWS - NA_ROWS - kk0) * w)),
                                  constant_values=NEG_INF))
        cases.append(jnp.concatenate(blocks, axis=1))
    cases.append(jnp.full((n_heads, NA_QROWS * w, NA_KROWS * w), NEG_INF, F32))
    return jnp.stack(cases)


def _swiglu_accumulate(x_ref, wg_ref, wu_ref, wd_ref, acc_ref, n_rows):
    x = x_ref[:n_rows]
    g = jnp.dot(x, wg_ref[0].astype(BF16), preferred_element_type=F32)
    u = jnp.dot(x, wu_ref[0].astype(BF16), preferred_element_type=F32)
    a = (g * jax.nn.sigmoid(g) * u).astype(BF16)
    acc_ref[:n_rows] += jnp.dot(a, wd_ref[0].astype(BF16), preferred_element_type=F32)


def _ffn_kernel(x_ref, wg_ref, wu_ref, wd_ref, o_ref, acc_ref):
    j = pl.program_id(1)

    @pl.when(j == 0)
    def _():
        acc_ref[...] = jnp.zeros_like(acc_ref)

    _swiglu_accumulate(x_ref, wg_ref, wu_ref, wd_ref, acc_ref, x_ref.shape[0])

    @pl.when(j == pl.num_programs(1) - 1)
    def _():
        o_ref[...] = acc_ref[...].astype(o_ref.dtype)


def _ffn(x, w_gate_up, w_down, layer, *, rows, tm, fc):
    _, d, ff2 = w_gate_up.shape
    nj = ff2 // 2 // fc
    return pl.pallas_call(
        _ffn_kernel,
        grid=(rows // tm, nj),
        in_specs=[pl.BlockSpec((tm, d), lambda i, j: (i, 0)),
                  pl.BlockSpec((1, d, fc), lambda i, j: (layer, 0, j)),
                  pl.BlockSpec((1, d, fc), lambda i, j: (layer, 0, nj + j)),
                  pl.BlockSpec((1, fc, d), lambda i, j: (layer, j, 0))],
        out_specs=pl.BlockSpec((tm, d), lambda i, j: (i, 0)),
        out_shape=jax.ShapeDtypeStruct((rows, d), BF16),
        scratch_shapes=[pltpu.VMEM((tm, d), F32)],
        compiler_params=_cp(("arbitrary", "arbitrary"), VMEM_LIMIT), name="swiglu",
    )(x, w_gate_up, w_gate_up, w_down)


def _moe_ffn_kernel(te_ref, tv_ref, tok_ref, hp_ref, wg_ref, wu_ref, wd_ref, o_ref, gbuf_ref, x_ref, acc_ref, sem,
                    *, rows_per_step):
    i, j = pl.program_id(0), pl.program_id(1)
    n_tiles = pl.num_programs(0)
    tm, d = x_ref.shape
    half = d // 2
    valid = tv_ref[i] > 0
    slot = i % 2

    def issue(tile, first_row, n_rows, to_slot):
        def body(r, carry):
            row = first_row + r
            pltpu.make_async_copy(hp_ref.at[pl.ds(tok_ref[tile * tm + row], 1), :],
                                  gbuf_ref.at[to_slot].at[pl.ds(row, 1), :], sem.at[to_slot]).start()
            return carry

        lax.fori_loop(0, n_rows, body, 0, unroll=8)

    @pl.when((i == 0) & (j == 0) & valid)
    def _():
        issue(0, 0, tm, 0)

    @pl.when((j == 0) & valid)
    def _():
        pltpu.make_async_copy(hp_ref.at[pl.ds(0, tm), :], gbuf_ref.at[slot], sem.at[slot]).wait()
        lo, hi = _unpack_halves(gbuf_ref[slot])
        x_ref[:, :half] = lo.astype(BF16)
        x_ref[:, half:] = hi.astype(BF16)
        acc_ref[...] = jnp.zeros_like(acc_ref)

    nxt = jnp.minimum(i + 1, n_tiles - 1)

    @pl.when((i + 1 < n_tiles) & (tv_ref[nxt] > 0) & (j * rows_per_step < tm))
    def _():
        issue(nxt, j * rows_per_step, rows_per_step, 1 - slot)

    for q in range(1, MOE_ROW_SPLITS + 1):
        n_rows = q * (tm // MOE_ROW_SPLITS)

        @pl.when(tv_ref[i] == n_rows)
        def _():
            _swiglu_accumulate(x_ref, wg_ref, wu_ref, wd_ref, acc_ref, n_rows)

    @pl.when(j == pl.num_programs(1) - 1)
    def _():
        o_ref[...] = jnp.where(valid, _pack_halves(acc_ref[...]), jnp.uint32(0))


def _moe_ffn(hp, w_gate_up, w_down, layer, tile_expert, tile_valid, row_token, *, tm, fc):
    _, _, d, ff2 = w_gate_up.shape
    nj = ff2 // 2 // fc
    n_rows = row_token.shape[0]
    issue_steps = 1 << (nj.bit_length() - 1)

    def jeff(i, j, tv):
        return jnp.where(tv[i] > 0, j, nj - 1)

    return pl.pallas_call(
        functools.partial(_moe_ffn_kernel, rows_per_step=tm // issue_steps),
        grid_spec=pltpu.PrefetchScalarGridSpec(
            num_scalar_prefetch=3, grid=(n_rows // tm, nj),
            in_specs=[pl.BlockSpec(memory_space=pl.ANY),
                      pl.BlockSpec((None, 1, d, fc), lambda i, j, te, tv, tok: (layer, te[i], 0, jeff(i, j, tv))),
                      pl.BlockSpec((None, 1, d, fc),
                                   lambda i, j, te, tv, tok: (layer, te[i], 0, nj + jeff(i, j, tv))),
                      pl.BlockSpec((None, 1, fc, d), lambda i, j, te, tv, tok: (layer, te[i], jeff(i, j, tv), 0))],
            out_specs=pl.BlockSpec((tm, d // 2), lambda i, j, te, tv, tok: (i, 0)),
            scratch_shapes=[pltpu.VMEM((2, tm, d // 2), jnp.uint32), pltpu.VMEM((tm, d), BF16),
                            pltpu.VMEM((tm, d), F32), pltpu.SemaphoreType.DMA((2,))]),
        out_shape=jax.ShapeDtypeStruct((n_rows, d // 2), jnp.uint32),
        compiler_params=_cp(("arbitrary", "arbitrary"), VMEM_LIMIT_MOE), name="moe_swiglu",
    )(tile_expert, tile_valid, row_token, hp, w_gate_up, w_gate_up, w_down)


def _combine_kernel(p0_ref, p1_ref, y_ref, x_ref, route_ref, gate_ref, o_ref, b0_ref, b1_ref, sem):
    _, n, half = b0_ref.shape
    i = pl.program_id(0)
    slot = i % 2

    def issue(tile, to_slot):
        def body(r, carry):
            k = tile * n + r
            pltpu.make_async_copy(y_ref.at[pl.ds(p0_ref[k], 1), :], b0_ref.at[to_slot].at[pl.ds(r, 1), :],
                                  sem.at[0, to_slot]).start()
            pltpu.make_async_copy(y_ref.at[pl.ds(p1_ref[k], 1), :], b1_ref.at[to_slot].at[pl.ds(r, 1), :],
                                  sem.at[1, to_slot]).start()
            return carry

        lax.fori_loop(0, n, body, 0, unroll=8)

    @pl.when(i == 0)
    def _():
        issue(0, 0)

    pltpu.make_async_copy(y_ref.at[pl.ds(0, n), :], b0_ref.at[slot], sem.at[0, slot]).wait()
    pltpu.make_async_copy(y_ref.at[pl.ds(0, n), :], b1_ref.at[slot], sem.at[1, slot]).wait()

    @pl.when(i + 1 < pl.num_programs(0))
    def _():
        issue(i + 1, 1 - slot)

    route = route_ref[...]
    w0, w1 = route[:, 2:3], route[:, 3:4]
    lo0, hi0 = _unpack_halves(b0_ref[slot])
    lo1, hi1 = _unpack_halves(b1_ref[slot])
    gate = gate_ref[0, 0]
    o_ref[:, :half] = x_ref[:, :half] + gate[:, :half] * (w0 * lo0 + w1 * lo1)
    o_ref[:, half:] = x_ref[:, half:] + gate[:, half:] * (w0 * hi0 + w1 * hi1)


def _combine(y, x, route, mod, gate_k, pos0, pos1, *, seq, n_batch):
    t, d = x.shape
    tr = 256
    return pl.pallas_call(
        _combine_kernel,
        grid_spec=pltpu.PrefetchScalarGridSpec(
            num_scalar_prefetch=2, grid=(t // tr,),
            in_specs=[pl.BlockSpec(memory_space=pl.ANY),
                      pl.BlockSpec((tr, d), lambda i, a, b: (i, 0)),
                      pl.BlockSpec((tr, LANES), lambda i, a, b: (i, 0)),
                      pl.BlockSpec((1, 1, 1, d),
                                   lambda i, a, b: (jnp.minimum((i * tr) // seq, n_batch), gate_k, 0, 0))],
            out_specs=pl.BlockSpec((tr, d), lambda i, a, b: (i, 0)),
            scratch_shapes=[pltpu.VMEM((2, tr, d // 2), jnp.uint32), pltpu.VMEM((2, tr, d // 2), jnp.uint32),
                            pltpu.SemaphoreType.DMA((2, 2))]),
        out_shape=jax.ShapeDtypeStruct((t, d), F32),
        compiler_params=_cp(("arbitrary",), VMEM_LIMIT), name="moe_combine",
    )(pos0, pos1, y, x, route, _mod_view(mod))


def _route_meta(idx, n_exp, tm):
    t = idx.shape[0]
    flat = idx.reshape(-1)
    onehot = (flat[:, None] == jnp.arange(n_exp, dtype=jnp.int32)[None, :]).astype(jnp.int32)
    counts = jnp.sum(onehot, axis=0)
    rank = jnp.take_along_axis(jnp.cumsum(onehot, axis=0) - onehot, flat[:, None], axis=1)[:, 0]
    padded = ((counts + tm - 1) // tm) * tm
    ends = jnp.cumsum(padded)
    starts = ends - padded
    pos = starts[flat] + rank
    n_rows = TOP_K * t + n_exp * tm
    row_token = jnp.zeros((n_rows,), jnp.int32).at[pos].set(jnp.arange(TOP_K * t, dtype=jnp.int32) // TOP_K)
    tile_start = jnp.arange(n_rows // tm, dtype=jnp.int32) * tm
    tile_valid = tile_start < ends[-1]
    tile_expert = jnp.sum((ends[None, :] <= tile_start[:, None]).astype(jnp.int32), axis=1)
    tile_expert = jnp.minimum(tile_expert, n_exp - 1)
    last_valid = jnp.maximum(ends[-1] // tm - 1, 0)
    tile_expert = jnp.where(tile_valid, tile_expert, tile_expert[last_valid])
    sub = tm // MOE_ROW_SPLITS
    used = jnp.clip(counts[tile_expert] - (tile_start - starts[tile_expert]), 0, tm)
    tile_rows = jnp.where(tile_valid, ((used + sub - 1) // sub) * sub, 0).astype(jnp.int32)
    pos = pos.reshape(t, TOP_K).astype(jnp.int32)
    return row_token, tile_expert, tile_rows, pos[:, 0], pos[:, 1]


def _rope_tables(seq, pad_rows):
    t = jnp.arange(seq, dtype=jnp.int32)
    row = (t // GRID_W).astype(F32)
    col = (t % GRID_W).astype(F32)

    def angles(rot_dim):
        n_freq = rot_dim // 4
        freqs = ROPE_THETA ** (-jnp.arange(n_freq, dtype=F32) / n_freq)
        ang = jnp.concatenate([row[:, None] * freqs[None, :], col[:, None] * freqs[None, :]], axis=-1)
        return jnp.cos(ang), jnp.sin(ang)

    ch, sh = angles(HEAD_DIM)
    cos_h = jnp.concatenate([ch] * 4, axis=1)
    sin_h = jnp.concatenate([-sh, sh, -sh, sh], axis=1)
    cb, sb = angles(B_ROPE)
    one = jnp.ones((seq, B_NOPE), F32)
    zero = jnp.zeros((seq, B_NOPE), F32)
    tail1 = jnp.ones((seq, LANES - B_QK), F32)
    tail0 = jnp.zeros((seq, LANES - B_QK), F32)
    cos_b = jnp.concatenate([one, cb, cb, tail1], axis=1)
    sin_b = jnp.concatenate([zero, -sb, sb, tail0], axis=1)

    def pad(tab, fill):
        return jnp.concatenate([tab, jnp.full((pad_rows, LANES), fill, F32)], axis=0)

    return pad(cos_h, 1.0), pad(sin_h, 0.0), pad(cos_b, 1.0), pad(sin_b, 0.0)


def _permute_w_in(w):
    d = w.shape[0]
    o = np.cumsum([0, 512, 128, 128, B_Q_RANK, B_KV_RANK, B_ROPE, 512, 512, 512, 512, 512, 512, 512])
    aq, ak, av, bq, bkv, bkr, cq, ck, cv, cg, dq, dk, dv = [w[:, o[i]:o[i + 1]] for i in range(13)]

    def dup(m):
        return jnp.concatenate([m[:, :64], m[:, :64], m[:, 64:], m[:, 64:]], axis=1)

    z = lambda n: jnp.zeros((d, n), w.dtype)
    half = B_ROPE // 2
    bkr_swapped = jnp.concatenate([bkr[:, half:], bkr[:, :half]], axis=1)
    parts = [aq, cq, ck, dq, dk, dup(ak), dup(av), cv, cg, dv, bq, bkv,
             z(B_NOPE), bkr, z(LANES - B_QK), z(B_NOPE), bkr_swapped, z(LANES - B_QK)]
    return jnp.concatenate(parts, axis=1).astype(BF16)


def _block_diag_ones():
    i = np.arange(512) // HEAD_DIM
    return jnp.asarray((i[:, None] == i[None, :]).astype(np.float32), dtype=BF16)


def _rope_partner(m):
    half = B_ROPE // 2
    return jnp.concatenate([m[..., :B_NOPE], m[..., B_NOPE + half:], m[..., B_NOPE:B_NOPE + half]], axis=-1)


def _pad_lanes(v, width):
    return jnp.concatenate([v.astype(F32), jnp.zeros((width - v.shape[0],), F32)])


def kernel(x, c, ctx, c_ctx, w_ada, b_ada, norm_mix, norm_ffn, w_in, w_out, a_q_norm, a_k_norm, a_sink,
           b_q_a_norm, b_kv_a_norm, b_w_uq, b_w_ukv, b_q_norm, b_k_norm, c_log_decay, d_q_norm, d_k_norm,
           d_rpb, ffn_w_gate_up, ffn_w_down, moe_router, moe_w_gate_up, moe_w_down):
    n_batch, seq, d = x.shape
    n_ctx = ctx.shape[1]
    depth = w_ada.shape[0]
    n_lat = n_batch * seq
    n_all = n_lat + n_batch * n_ctx
    grid_rows = seq // GRID_W
    tm = next(t for t in (1024, 512, 256) if n_lat % t == 0 and n_all % t == 0)

    cc = jnp.concatenate([c, c_ctx[None, :], jnp.zeros((8 - n_batch - 1, d), F32)], axis=0)
    mod = _ada(cc, w_ada, b_ada)
    xs = (x.reshape(n_lat, d), ctx.reshape(n_batch * n_ctx, d))
    cos_h, sin_h, cos_b, sin_b = _rope_tables(seq, 256)
    gmat = _block_diag_ones()

    y = None
    y_gate, y_mod = 0, None
    for l in range(depth):
        last = l == depth - 1
        mod_l = mod[l]
        if y is None:
            (h,) = _resnorm(xs, None, None, 0, mod_l, norm_mix[l], 0, 1, rows=n_all, seq=seq, n_batch=n_batch)
        else:
            xs, h = _resnorm(xs, y, y_mod, y_gate, mod_l, norm_mix[l], 0, 1, rows=n_all, seq=seq,
                             n_batch=n_batch)
        proj = _mm([h], _permute_w_in(w_in[l]), rows=n_all, tm=tm, tn=768, out_dtype=BF16, name="proj_in")
        gains = jnp.stack([jnp.tile(a_q_norm[l], 8), _pad_lanes(jnp.tile(a_k_norm[l], 4), 512),
                           jnp.tile(d_q_norm[l], 8), jnp.tile(d_k_norm[l], 8)]).astype(F32)
        prep, dv2 = _prep(proj, cos_h, sin_h, gmat, gains, rows=n_all, seq=seq, n_lat=n_lat)
        wuq = jnp.pad(b_w_uq[l].reshape(B_Q_RANK, B_HEADS, B_QK),
                      ((0, 0), (0, 0), (0, LANES - B_QK))).reshape(B_Q_RANK, B_HEADS * LANES).astype(BF16)
        wukv3 = b_w_ukv[l].reshape(B_KV_RANK, B_HEADS, B_NOPE + B_V)
        wukv = jnp.concatenate(
            [jnp.pad(wukv3[:, :, :B_NOPE], ((0, 0), (0, 0), (0, LANES - B_NOPE))).reshape(B_KV_RANK, -1),
             wukv3[:, :, B_NOPE:].reshape(B_KV_RANK, -1)], axis=1).astype(BF16)
        bound = _mla_logit_bound(b_q_norm[l], b_k_norm[l])
        mla_safe = (bound <= MLA_SAFE_BOUND).astype(jnp.int32).reshape(1)
        gains_b = jnp.stack([b_q_a_norm[l].astype(F32), _pad_lanes(b_kv_a_norm[l], B_Q_RANK),
                             _pad_lanes(b_q_norm[l], B_Q_RANK), _pad_lanes(b_k_norm[l], B_Q_RANK),
                             jnp.full((B_Q_RANK,), -1.0, F32) * bound,
                             _pad_lanes(_rope_partner(b_q_norm[l]), B_Q_RANK),
                             _pad_lanes(_rope_partner(b_k_norm[l]), B_Q_RANK), jnp.zeros((B_Q_RANK,), F32)])
        wuq3 = b_w_uq[l].reshape(B_Q_RANK, B_HEADS, B_QK)
        wuq_sw = jnp.pad(_rope_partner(wuq3), ((0, 0), (0, 0), (0, LANES - B_QK))).reshape(wuq.shape).astype(BF16)
        qb, kb, vb = _prepb(proj, cos_b, sin_b, wuq, wuq_sw, wukv, gains_b, rows=n_all, seq=seq, n_lat=n_lat)

        sink = a_sink[l].astype(F32)
        bound_a = _softmax_bound(a_q_norm[l], a_k_norm[l], jnp.maximum(jnp.max(sink), 0.0))
        par_a = jnp.concatenate([sink * LOG2E, bound_a[None], (bound_a <= MLA_SAFE_BOUND).astype(F32)[None]])
        oa = _wina(prep, proj, par_a, n_batch=n_batch, seq=seq, n_ctx=n_ctx, with_ctx=not last)
        ob = _mla(qb, kb, vb, mla_safe, n_batch=n_batch, seq=seq, n_ctx=n_ctx, with_ctx=not last)
        ld_head = jnp.broadcast_to(c_log_decay[l].astype(F32)[:, :, None], (2, C_HEADS, LANES))
        oc_fwd, oc_rev = _retention(prep, proj, ld_head, n_batch=n_batch, seq=seq, n_ctx=n_ctx)
        bound_d = _softmax_bound(d_q_norm[l], d_k_norm[l], jnp.max(jnp.abs(d_rpb[l])))
        par_d = jnp.stack([bound_d, (bound_d <= MLA_SAFE_BOUND).astype(F32)])
        abm = _nbr_bias_tables(d_rpb[l], grid_rows, bound_d)
        od = _nbr(prep, dv2, abm, par_d, n_batch=n_batch, seq=seq, n_ctx=n_ctx, with_ctx=not last)
        rows_l = n_lat if last else n_all
        oc = _retention_finish(oc_fwd, oc_rev, proj, gmat, rows=rows_l)
        ymix = _mm([oa, ob, oc, od], w_out[l].astype(BF16), rows=rows_l, tm=tm, tn=1024, out_dtype=BF16,
                   name="proj_out")
        i = l // 2
        if l % 2 == 0:
            xs, h2 = _resnorm(xs, ymix, mod_l, 2, mod_l, norm_ffn[l], 3, 4, rows=rows_l, seq=seq,
                              n_batch=n_batch)
            y = _ffn(h2, ffn_w_gate_up, ffn_w_down, i, rows=rows_l, tm=tm, fc=512)
            y_gate, y_mod = 5, mod_l
        else:
            xs, hp, route = _resnorm(xs, ymix, mod_l, 2, mod_l, norm_ffn[l], 3, 4, rows=rows_l, seq=seq,
                                     n_batch=n_batch, router=moe_router[i])
            top_idx = route[:, :TOP_K].astype(jnp.int32)
            row_token, tile_expert, tile_valid, pos0, pos1 = _route_meta(top_idx, moe_router.shape[2], tm)
            yg = _moe_ffn(hp, moe_w_gate_up, moe_w_down, i, tile_expert, tile_valid, row_token, tm=tm, fc=512)
            xs = _combine(yg, xs, route, mod_l, 5, pos0, pos1, seq=seq, n_batch=n_batch)
            y = None
    if y is not None:
        xs, _ = _resnorm(xs, y, y_mod, y_gate, y_mod, norm_ffn[depth - 1], 3, 4, rows=xs.shape[0], seq=seq,
                         n_batch=n_batch)
    return xs[:n_lat].reshape(n_batch, seq, d)
```

```python
import functools

import numpy as np
import jax
import jax.numpy as jnp
from jax import lax
from jax.experimental import pallas as pl
from jax.experimental.pallas import tpu as pltpu

F32 = jnp.float32
BF16 = jnp.bfloat16

GRID_W = 64
HEAD_DIM = 64
ROPE_THETA = 10000.0
EPS = 1e-6
NEG_INF = -1e30

A_HEADS = 8
A_KV_HEADS = 2
A_BLOCK = 128
B_HEADS = 8
B_Q_RANK = 384
B_KV_RANK = 128
B_NOPE = 64
B_ROPE = 32
B_QK = B_NOPE + B_ROPE
B_V = 64
C_HEADS = 8
D_HEADS = 8
NA_ROWS = 8
NA_COLS = 16
TOP_K = 2

LANES = 128
NA_QROWS = 4
NA_KROWS = NA_ROWS + NA_QROWS
NA_CASES = 4
MOE_ROW_SPLITS = 4
RET_CHUNK = 256

P_AQ, P_CQ, P_CK, P_DQ, P_DK, P_AK, P_AV, P_CV, P_CG, P_DV, P_BQ, P_BKV, P_BKR, P_BKR_SW, P_END = (
    0, 512, 1024, 1536, 2048, 2560, 2816, 3072, 3584, 4096, 4608, 4992, 5120, 5248, 5376)
PREP_W = P_AV

VMEM_LIMIT = 56 * 1024 * 1024
VMEM_LIMIT_MOE = 61 * 1024 * 1024

LOG2E = float(np.log2(np.e))
MLA_LOGIT_SCALE = B_QK ** -0.5 * LOG2E
MLA_SAFE_BOUND = 40.0


def _cp(sem, vmem=None):
    return pltpu.CompilerParams(dimension_semantics=sem, vmem_limit_bytes=vmem)


def _nt(a, b):
    return lax.dot_general(a, b, (((1,), (1,)), ((), ())), preferred_element_type=F32)


_HI_HALF = 0xFFFF0000


def _pack_halves(x):
    n = x.shape[1] // 2
    bits = pltpu.bitcast(x.astype(BF16).astype(F32), jnp.uint32)
    return (bits[:, :n] >> 16) | (bits[:, n:] & jnp.uint32(_HI_HALF))


def _unpack_halves(w):
    return pltpu.bitcast(w << 16, F32), pltpu.bitcast(w & jnp.uint32(_HI_HALF), F32)


def _ada_kernel(c_ref, w_ref, b_ref, o_ref):
    c = c_ref[...]
    s = (c * jax.nn.sigmoid(c)).astype(BF16)
    o_ref[0] = jnp.dot(s, w_ref[0].astype(BF16), preferred_element_type=F32) + b_ref[0]


def _ada(cc, w_ada, b_ada):
    depth, d, n = w_ada.shape
    tn = 1024
    return pl.pallas_call(
        _ada_kernel,
        grid=(depth, n // tn),
        in_specs=[pl.BlockSpec((8, d), lambda l, j: (0, 0)),
                  pl.BlockSpec((1, d, tn), lambda l, j: (l, 0, j)),
                  pl.BlockSpec((1, 1, tn), lambda l, j: (l, 0, j))],
        out_specs=pl.BlockSpec((1, 8, tn), lambda l, j: (l, 0, j)),
        out_shape=jax.ShapeDtypeStruct((depth, 8, n), F32),
        compiler_params=_cp(("arbitrary", "arbitrary"), VMEM_LIMIT),
        name="ada",
    )(cc, w_ada, b_ada.reshape(depth, 1, n))


def _resnorm_kernel(*refs, has_res, with_router, n_exp, lat_tiles):
    refs = list(refs)
    x_ref = refs.pop(0)
    if lat_tiles is not None:
        xc_ref = refs.pop(0)
        x = jnp.where(pl.program_id(0) < lat_tiles, x_ref[...], xc_ref[...])
    else:
        x = x_ref[...]
    if has_res:
        y_ref = refs.pop(0)
        gate_ref = refs.pop(0)
    gain_ref, sh_ref, sc_ref = refs[:3]
    refs = refs[3:]
    if with_router:
        router_ref = refs.pop(0)
    if has_res:
        xo_ref = refs.pop(0)
    h_ref = refs.pop(0)
    if has_res:
        x = x + gate_ref[0, 0] * y_ref[...].astype(F32)
        xo_ref[...] = x
    ms = jnp.mean(x * x, axis=-1, keepdims=True)
    h = (x * lax.rsqrt(ms + EPS) * gain_ref[...]) * (1.0 + sc_ref[0, 0]) + sh_ref[0, 0]
    if with_router:
        (route_ref,) = refs
        h_ref[...] = _pack_halves(h)
        h_hi = h.astype(BF16)
        h_lo = (h - h_hi.astype(F32)).astype(BF16)
        r_hi, r_lo = router_ref[0], router_ref[1]
        logits = (jnp.dot(h_hi, r_hi, preferred_element_type=F32) + jnp.dot(h_lo, r_hi, preferred_element_type=F32)
                  + jnp.dot(h_hi, r_lo, preferred_element_type=F32))
        lane = lax.broadcasted_iota(jnp.int32, logits.shape, 1)
        logits = jnp.where(lane < n_exp, logits, -jnp.inf)
        m1 = jnp.max(logits, axis=-1, keepdims=True)
        i1 = jnp.min(jnp.where(logits == m1, lane, LANES), axis=-1, keepdims=True)
        rest = jnp.where(lane == i1, -jnp.inf, logits)
        m2 = jnp.max(rest, axis=-1, keepdims=True)
        i2 = jnp.min(jnp.where(rest == m2, lane, LANES), axis=-1, keepdims=True)
        e2 = jnp.exp(m2 - m1)
        g1 = 1.0 / (1.0 + e2)
        g2 = e2 / (1.0 + e2)
        route = jnp.where(lane == 0, i1.astype(F32),
                          jnp.where(lane == 1, i2.astype(F32),
                                    jnp.where(lane == 2, g1, jnp.where(lane == 3, g2, 0.0))))
        route_ref[...] = route
    else:
        h_ref[...] = h.astype(h_ref.dtype)


def _mod_view(mod):
    return mod.reshape(mod.shape[0], 6, 1, mod.shape[1] // 6)


def _resnorm(x, y, gate_mod, gate_k, mod, gain, sh_k, sc_k, *, rows, seq, n_batch, router=None):
    tr = 256
    has_res = y is not None
    with_router = router is not None
    split = isinstance(x, tuple)
    d = x[0].shape[1] if split else x.shape[1]

    def grp(i):
        return jnp.minimum((i * tr) // seq, n_batch)

    def modspec(k):
        return pl.BlockSpec((1, 1, 1, d), lambda i: (grp(i), k, 0, 0))

    row = pl.BlockSpec((tr, d), lambda i: (i, 0))
    lat_tiles = None
    if split:
        lat_tiles = x[0].shape[0] // tr
        in_specs = [pl.BlockSpec((tr, d), lambda i: (jnp.minimum(i, lat_tiles - 1), 0)),
                    pl.BlockSpec((tr, d), lambda i: (jnp.maximum(i - lat_tiles, 0), 0))]
        args = list(x)
    else:
        in_specs = [row]
        args = [x]
    if has_res:
        in_specs += [row, modspec(gate_k)]
        args += [y, _mod_view(gate_mod)]
    in_specs += [pl.BlockSpec((1, d), lambda i: (0, 0)), modspec(sh_k), modspec(sc_k)]
    args += [gain.reshape(1, d).astype(F32), _mod_view(mod), _mod_view(mod)]
    out_shape, out_specs = [], []
    n_exp = 0
    if with_router:
        n_exp = router.shape[1]
        router = jnp.pad(router.astype(F32), ((0, 0), (0, LANES - n_exp)))
        r_hi = router.astype(BF16)
        router = jnp.stack([r_hi, (router - r_hi.astype(F32)).astype(BF16)])
        in_specs.append(pl.BlockSpec(router.shape, lambda i: (0, 0, 0)))
        args.append(router)
    if has_res:
        out_shape.append(jax.ShapeDtypeStruct((rows, d), F32))
        out_specs.append(row)
    if with_router:
        out_shape += [jax.ShapeDtypeStruct((rows, d // 2), jnp.uint32), jax.ShapeDtypeStruct((rows, LANES), F32)]
        out_specs += [pl.BlockSpec((tr, d // 2), lambda i: (i, 0)), pl.BlockSpec((tr, LANES), lambda i: (i, 0))]
    else:
        out_shape.append(jax.ShapeDtypeStruct((rows, d), BF16))
        out_specs.append(row)
    kern = functools.partial(_resnorm_kernel, has_res=has_res, with_router=with_router, n_exp=n_exp,
                             lat_tiles=lat_tiles)
    return pl.pallas_call(
        kern, grid=(rows // tr,), in_specs=in_specs, out_specs=out_specs, out_shape=out_shape,
        compiler_params=_cp(("arbitrary",), VMEM_LIMIT), name="resnorm",
    )(*args)


def _mm_kernel(*refs, n_x):
    w_ref, o_ref = refs[n_x], refs[n_x + 1]
    if n_x == 1:
        x = refs[0][...]
    else:
        x = jnp.concatenate([r[...] for r in refs[:n_x]], axis=1)
    o_ref[...] = jnp.dot(x, w_ref[...], preferred_element_type=F32).astype(o_ref.dtype)


def _mm(xs, w, *, rows, tm, tn, out_dtype, name):
    n = w.shape[1]
    in_specs = [pl.BlockSpec((tm, a.shape[1]), lambda i, j: (i, 0)) for a in xs]
    in_specs.append(pl.BlockSpec((w.shape[0], tn), lambda i, j: (0, j)))
    return pl.pallas_call(
        functools.partial(_mm_kernel, n_x=len(xs)),
        grid=(rows // tm, n // tn),
        in_specs=in_specs,
        out_specs=pl.BlockSpec((tm, tn), lambda i, j: (i, j)),
        out_shape=jax.ShapeDtypeStruct((rows, n), out_dtype),
        compiler_params=_cp(("arbitrary", "arbitrary"), VMEM_LIMIT), name=name,
    )(*xs, w)


def _group_meansq(x, g_ref, width):
    x2 = x * x
    hi = x2.astype(BF16)
    lo = (x2 - hi.astype(F32)).astype(BF16)
    g = g_ref[:width, :width]
    ss = jnp.dot(hi, g, preferred_element_type=F32) + jnp.dot(lo, g, preferred_element_type=F32)
    return ss * (1.0 / HEAD_DIM)


def _prep_kernel(p_ref, dv_ref, ch_ref, sh_ref, g_ref, gains_ref, o_ref, dv2_ref):
    tr = p_ref.shape[0]

    def seg(off, width):
        return p_ref[:, off:off + width].astype(F32)

    def norm(x, row, width):
        return x * lax.rsqrt(_group_meansq(x, g_ref, width) + EPS) * gains_ref[row:row + 1, :width]

    def rope(x, width):
        lane = lax.broadcasted_iota(jnp.int32, (tr, width), 1)
        first = (lane % HEAD_DIM) < (HEAD_DIM // 2)
        cos = jnp.concatenate([ch_ref[...]] * (width // LANES), axis=1)
        sin = jnp.concatenate([sh_ref[...]] * (width // LANES), axis=1)
        swapped = jnp.where(first, pltpu.roll(x, width - HEAD_DIM // 2, 1), pltpu.roll(x, HEAD_DIM // 2, 1))
        return x * cos + swapped * sin

    scale = HEAD_DIM ** -0.5
    scale2 = scale * LOG2E
    o_ref[:, P_AQ:P_AQ + 512] = (rope(norm(seg(P_AQ, 512), 0, 512), 512) * scale2).astype(BF16)
    o_ref[:, P_CQ:P_CQ + 512] = (rope(seg(P_CQ, 512), 512) * scale).astype(BF16)
    o_ref[:, P_CK:P_CK + 512] = rope(seg(P_CK, 512), 512).astype(BF16)
    o_ref[:, P_DQ:P_DQ + 512] = (norm(seg(P_DQ, 512), 2, 512) * scale2).astype(BF16)
    o_ref[:, P_DK:P_DK + 512] = norm(seg(P_DK, 512), 3, 512).astype(BF16)
    o_ref[:, P_AK:P_AK + 256] = rope(norm(seg(P_AK, 256), 1, 256), 256).astype(BF16)
    ones_blk = jnp.where(lax.broadcasted_iota(jnp.int32, (tr, LANES), 1) < HEAD_DIM, 1.0, 0.0).astype(BF16)
    for g in range(D_HEADS // 2):
        dv2_ref[:, 2 * g * LANES:(2 * g + 1) * LANES] = dv_ref[:, g * LANES:(g + 1) * LANES]
        dv2_ref[:, (2 * g + 1) * LANES:(2 * g + 2) * LANES] = ones_blk


def _prep(proj, cos_h, sin_h, gmat, gains, *, rows, seq, n_lat):
    tr = 256
    nlat = seq // tr

    def tab(i):
        return (jnp.where(i * tr < n_lat, i % nlat, nlat), 0)

    return pl.pallas_call(
        _prep_kernel,
        grid=(rows // tr,),
        in_specs=[pl.BlockSpec((tr, PREP_W), lambda i: (i, 0)),
                  pl.BlockSpec((tr, 512), lambda i: (i, P_DV // 512)),
                  pl.BlockSpec((tr, LANES), tab),
                  pl.BlockSpec((tr, LANES), tab),
                  pl.BlockSpec((512, 512), lambda i: (0, 0)),
                  pl.BlockSpec((4, 512), lambda i: (0, 0))],
        out_specs=[pl.BlockSpec((tr, PREP_W), lambda i: (i, 0)),
                   pl.BlockSpec((tr, 2 * 512), lambda i: (i, 0))],
        out_shape=[jax.ShapeDtypeStruct((rows, PREP_W), BF16), jax.ShapeDtypeStruct((rows, 2 * 512), BF16)],
        compiler_params=_cp(("arbitrary",), VMEM_LIMIT), name="prep",
    )(proj, proj, cos_h, sin_h, gmat, gains)


def _prepb_kernel(bq_ref, bkv_ref, bkr_ref, bkrs_ref, cb_ref, sb_ref, wuq_ref, wuqs_ref, wukv_ref, gains_ref,
                  q_ref, k_ref, v_ref):
    tr = bq_ref.shape[0]
    cq = bq_ref[...].astype(F32)
    cqn = (cq * lax.rsqrt(jnp.mean(cq * cq, axis=-1, keepdims=True) + EPS) * gains_ref[0:1, :]).astype(BF16)
    qup = jnp.dot(cqn, wuq_ref[...], preferred_element_type=F32)
    qup_sw = jnp.dot(cqn, wuqs_ref[...], preferred_element_type=F32)
    ckv = bkv_ref[...].astype(F32)
    ckvn = ckv * lax.rsqrt(jnp.mean(ckv * ckv, axis=-1, keepdims=True) + EPS) * gains_ref[1:2, :LANES]
    kvup = jnp.dot(ckvn.astype(BF16), wukv_ref[...], preferred_element_type=F32)
    kr = bkr_ref[...].astype(F32)
    cb, sb = cb_ref[...], sb_ref[...]
    lane = lax.broadcasted_iota(jnp.int32, (tr, LANES), 1)
    q_cos = cb * (gains_ref[2:3, :LANES] * MLA_LOGIT_SCALE)
    q_sin = sb * (gains_ref[5:6, :LANES] * MLA_LOGIT_SCALE)
    k_cos = cb * gains_ref[3:4, :LANES]
    k_rot = bkrs_ref[...].astype(F32) * (sb * gains_ref[6:7, :LANES])

    def inv_rms(x):
        return lax.rsqrt(jnp.sum(x * x, axis=-1, keepdims=True) * (1.0 / B_QK) + EPS)

    shift_lane = lane == B_QK
    neg_bound = gains_ref[4:5, :LANES]
    for h in range(B_HEADS):
        sl = slice(h * LANES, (h + 1) * LANES)
        x = qup[:, sl]
        q = (x * q_cos + qup_sw[:, sl] * q_sin) * inv_rms(x)
        q_ref[:, sl] = jnp.where(shift_lane, 1.0, q).astype(BF16)
        x = kvup[:, sl] + kr
        k = (x * k_cos + k_rot) * inv_rms(x)
        k_ref[:, sl] = jnp.where(shift_lane, neg_bound, k).astype(BF16)
    ones_blk = jnp.where(lane < B_V, 1.0, 0.0).astype(BF16)
    for g in range(B_HEADS // 2):
        v_ref[:, 2 * g * LANES:(2 * g + 1) * LANES] = kvup[:, (B_HEADS + g) * LANES:(B_HEADS + g + 1) * LANES].astype(BF16)
        v_ref[:, (2 * g + 1) * LANES:(2 * g + 2) * LANES] = ones_blk


def _prepb(proj, cos_b, sin_b, wuq, wuq_sw, wukv, gains, *, rows, seq, n_lat):
    tr = 256
    nlat = seq // tr

    def tab(i):
        return (jnp.where(i * tr < n_lat, i % nlat, nlat), 0)

    return pl.pallas_call(
        _prepb_kernel,
        grid=(rows // tr,),
        in_specs=[pl.BlockSpec((tr, B_Q_RANK), lambda i: (i, P_BQ // B_Q_RANK)),
                  pl.BlockSpec((tr, LANES), lambda i: (i, P_BKV // LANES)),
                  pl.BlockSpec((tr, LANES), lambda i: (i, P_BKR // LANES)),
                  pl.BlockSpec((tr, LANES), lambda i: (i, P_BKR_SW // LANES)),
                  pl.BlockSpec((tr, LANES), tab),
                  pl.BlockSpec((tr, LANES), tab),
                  pl.BlockSpec(wuq.shape, lambda i: (0, 0)),
                  pl.BlockSpec(wuq.shape, lambda i: (0, 0)),
                  pl.BlockSpec(wukv.shape, lambda i: (0, 0)),
                  pl.BlockSpec((8, B_Q_RANK), lambda i: (0, 0))],
        out_specs=[pl.BlockSpec((tr, B_HEADS * LANES), lambda i: (i, 0)),
                   pl.BlockSpec((tr, B_HEADS * LANES), lambda i: (i, 0)),
                   pl.BlockSpec((tr, B_HEADS * LANES), lambda i: (i, 0))],
        out_shape=[jax.ShapeDtypeStruct((rows, B_HEADS * LANES), BF16),
                   jax.ShapeDtypeStruct((rows, B_HEADS * LANES), BF16),
                   jax.ShapeDtypeStruct((rows, B_HEADS * LANES), BF16)],
        compiler_params=_cp(("arbitrary",), VMEM_LIMIT), name="prep_mla",
    )(proj, proj, proj, proj, cos_b, sin_b, wuq, wuq_sw, wukv, gains)


def _wina_kernel(par_ref, q_ref, kp_ref, ko_ref, kn_ref, kc_ref, vp_ref, vo_ref, vn_ref, vc_ref, o_ref,
                 *, nb, seq):
    n = pl.program_id(1)
    blk = A_BLOCK
    rep = A_HEADS // A_KV_HEADS
    n_ctx = kc_ref.shape[0]
    kcat = jnp.concatenate([kp_ref[...], ko_ref[...], kn_ref[...], kc_ref[...]], axis=0)
    vcat = jnp.concatenate([vp_ref[...], vo_ref[...], vn_ref[...], vc_ref[...]], axis=0)
    n_keys = 3 * blk + n_ctx
    qi = lax.broadcasted_iota(jnp.int32, (blk, n_keys), 0)
    kj = lax.broadcasted_iota(jnp.int32, (blk, n_keys), 1)
    band = kj - blk
    kpos = n * blk + band
    valid = ((jnp.abs(qi - band) <= blk) & (kpos >= 0) & (kpos < seq) & (n < nb)) | (kj >= 3 * blk)
    low = lax.broadcasted_iota(jnp.int32, (blk, LANES), 1) < HEAD_DIM
    low_k = lax.broadcasted_iota(jnp.int32, (n_keys, LANES), 1) < HEAD_DIM
    bound = par_ref[A_HEADS]
    safe = par_ref[A_HEADS + 1] > 0.0

    def attend(use_max):
        mask = jnp.where(valid, 0.0 if use_max else -bound, NEG_INF)
        mask = jnp.concatenate([mask] * rep, axis=0)
        for g in range(A_KV_HEADS):
            kg = kcat[:, g * LANES:(g + 1) * LANES]
            v1 = jnp.where(low_k, vcat[:, g * LANES:(g + 1) * LANES], jnp.ones((), BF16))
            qs, sinks = [], []
            for r in range(rep):
                h = g * rep + r
                qc = q_ref[:, (h // 2) * LANES:(h // 2 + 1) * LANES]
                qs.append(jnp.where(low if h % 2 == 0 else jnp.logical_not(low), qc, jnp.zeros_like(qc)))
                sinks.append(jnp.full((blk, 1), par_ref[h], F32))
            s = _nt(jnp.concatenate(qs, axis=0), kg) + mask
            sink = jnp.concatenate(sinks, axis=0)
            if use_max:
                m = jnp.maximum(jnp.max(s, axis=-1, keepdims=True), sink)
                s, sink = s - m, sink - m
            else:
                sink = sink - bound
            o = jnp.dot(jnp.exp2(s).astype(BF16), v1, preferred_element_type=F32)
            res = o / (o[:, HEAD_DIM:HEAD_DIM + 1] + jnp.exp2(sink))
            for c in range(rep // 2):
                even = res[(2 * c) * blk:(2 * c + 1) * blk]
                odd = pltpu.roll(res[(2 * c + 1) * blk:(2 * c + 2) * blk], HEAD_DIM, 1)
                j = (g * rep) // 2 + c
                o_ref[:, j * LANES:(j + 1) * LANES] = jnp.where(low, even, odd).astype(o_ref.dtype)

    @pl.when(safe)
    def _():
        attend(False)

    @pl.when(jnp.logical_not(safe))
    def _():
        attend(True)


def _wina(prep, proj, sink, *, n_batch, seq, n_ctx, with_ctx):
    blk = A_BLOCK
    nb = seq // blk
    ncb = n_ctx // blk
    steps = nb + (ncb if with_ctx else 0)
    lat_blocks = n_batch * nb
    rows_out = n_batch * seq + (n_batch * n_ctx if with_ctx else 0)

    def qmap(b, n, s):
        return (jnp.where(n < nb, b * nb + n, lat_blocks + b * ncb + (n - nb)), 0)

    def kmap(delta, col):
        def f(b, n, s):
            return (b * nb + jnp.clip(n + delta, 0, nb - 1), col)
        return f

    def cmap(col):
        def f(b, n, s):
            return (n_batch * seq // n_ctx + b, col)
        return f

    kcol, vcol = P_AK // 256, P_AV // 256
    in_specs = [pl.BlockSpec((blk, 512), qmap)]
    in_specs += [pl.BlockSpec((blk, 256), kmap(dl, kcol)) for dl in (-1, 0, 1)]
    in_specs += [pl.BlockSpec((n_ctx, 256), cmap(kcol))]
    in_specs += [pl.BlockSpec((blk, 256), kmap(dl, vcol)) for dl in (-1, 0, 1)]
    in_specs += [pl.BlockSpec((n_ctx, 256), cmap(vcol))]
    return pl.pallas_call(
        functools.partial(_wina_kernel, nb=nb, seq=seq),
        grid_spec=pltpu.PrefetchScalarGridSpec(
            num_scalar_prefetch=1, grid=(n_batch, steps), in_specs=in_specs,
            out_specs=pl.BlockSpec((blk, 512), qmap)),
        out_shape=jax.ShapeDtypeStruct((rows_out, 512), BF16),
        compiler_params=_cp(("arbitrary", "arbitrary"), VMEM_LIMIT), name="mixer_window",
    )(sink, prep, prep, prep, prep, prep, proj, proj, proj, proj)


def _mla_kernel(safe_ref, q_ref, kl_ref, vl_ref, kc_ref, vc_ref, o_ref, *, nq, tk_fast, tk_exact, hps):
    tq = q_ref.shape[0]
    seq = kl_ref.shape[0]
    is_lat = pl.program_id(2) < nq
    safe = safe_ref[0] > 0
    pair_w = 2 * LANES
    lane = lax.broadcasted_iota(jnp.int32, (tq, LANES), 1)

    def q(h):
        return q_ref[:, h * LANES:(h + 1) * LANES]

    def write(nums, dens):
        for g in range(hps // 2):
            o_ref[:, g * LANES:(g + 1) * LANES] = jnp.where(
                lane < B_V, nums[2 * g] / dens[2 * g], nums[2 * g + 1] / dens[2 * g + 1]).astype(o_ref.dtype)

    def fast(chunks):
        accs = [None] * hps
        for k_ref, v_ref, off, n in chunks:
            for h in range(hps):
                g = h // 2
                p = jnp.exp2(_nt(q(h), k_ref[off:off + n, h * LANES:(h + 1) * LANES])).astype(BF16)
                d = jnp.dot(p, v_ref[off:off + n, g * pair_w:(g + 1) * pair_w], preferred_element_type=F32)
                accs[h] = d if accs[h] is None else accs[h] + d
        write([a[:, :LANES] for a in accs], [a[:, LANES:LANES + 1] for a in accs])

    ctx_chunk = (kc_ref, vc_ref, 0, kc_ref.shape[0])

    @pl.when(safe & is_lat)
    def _():
        fast([(kl_ref, vl_ref, c * tk_fast, tk_fast) for c in range(seq // tk_fast)] + [ctx_chunk])

    @pl.when(safe & jnp.logical_not(is_lat))
    def _():
        fast([ctx_chunk])

    @pl.when(jnp.logical_not(safe))
    def _():
        def step(kf, vf, carry):
            out = []
            for h in range(hps):
                m, l, acc = carry[h]
                s = _nt(q(h), kf(h))
                m_new = jnp.maximum(m, jnp.max(s, axis=-1, keepdims=True))
                a = jnp.exp2(m - m_new)
                p = jnp.exp2(s - m_new)
                l = a * l + jnp.sum(p, axis=-1, keepdims=True)
                acc = a * acc + jnp.dot(p.astype(BF16), vf(h // 2), preferred_element_type=F32)
                out.append((m_new, l, acc))
            return tuple(out)

        def body(c, carry):
            off = pl.multiple_of(c * tk_exact, tk_exact)
            return step(lambda h: kl_ref[pl.ds(off, tk_exact), h * LANES:(h + 1) * LANES],
                        lambda g: vl_ref[pl.ds(off, tk_exact), g * pair_w:g * pair_w + LANES], carry)

        init = tuple((jnp.full((tq, 1), -jnp.inf, F32), jnp.zeros((tq, 1), F32), jnp.zeros((tq, LANES), F32))
                     for _ in range(hps))
        carry = lax.fori_loop(0, jnp.where(is_lat, seq // tk_exact, 0), body, init)
        carry = step(lambda h: kc_ref[:, h * LANES:(h + 1) * LANES],
                     lambda g: vc_ref[:, g * pair_w:g * pair_w + LANES], carry)
        write([c[2] for c in carry], [c[1] for c in carry])


def _mla_logit_bound(q_gain, k_gain):
    b = B_QK * MLA_LOGIT_SCALE * jnp.max(jnp.abs(q_gain)) * jnp.max(jnp.abs(k_gain))
    return (1.02 * b + 0.5).astype(F32)


def _mla(qb, kb, vb, safe, *, n_batch, seq, n_ctx, with_ctx):
    hps = 4
    tq = 256
    assert n_ctx == tq
    nq = seq // tq
    ctx_blk = n_batch * seq // n_ctx
    rows_out = n_batch * seq + (n_batch * n_ctx if with_ctx else 0)
    kw, vw = hps * LANES, hps * B_V

    def qmap(b, j, i, s):
        return (jnp.where(i < nq, b * nq + i, ctx_blk + b), j)

    return pl.pallas_call(
        functools.partial(_mla_kernel, nq=nq, tk_fast=min(2048, seq), tk_exact=512, hps=hps),
        grid_spec=pltpu.PrefetchScalarGridSpec(
            num_scalar_prefetch=1,
            grid=(n_batch, B_HEADS // hps, nq + (1 if with_ctx else 0)),
            in_specs=[pl.BlockSpec((tq, kw), qmap),
                      pl.BlockSpec((seq, kw), lambda b, j, i, s: (b, j)),
                      pl.BlockSpec((seq, kw), lambda b, j, i, s: (b, j)),
                      pl.BlockSpec((n_ctx, kw), lambda b, j, i, s: (ctx_blk + b, j)),
                      pl.BlockSpec((n_ctx, kw), lambda b, j, i, s: (ctx_blk + b, j))],
            out_specs=pl.BlockSpec((tq, vw), qmap)),
        out_shape=jax.ShapeDtypeStruct((rows_out, 512), BF16),
        compiler_params=_cp(("arbitrary", "arbitrary", "arbitrary"), VMEM_LIMIT), name="mixer_mla",
    )(safe, qb, kb, vb, kb, vb)


def _ret_kernel(qf_ref, kf_ref, vf_ref, qr_ref, kr_ref, vr_ref, ld_ref, of_ref, or_ref,
                s_ref, dec_ref, qw_ref, kw_ref, cd_ref):
    step = pl.program_id(1)
    cc = qf_ref.shape[0]
    low = lax.broadcasted_iota(jnp.int32, (cc, LANES), 1) < HEAD_DIM
    sr = lax.broadcasted_iota(jnp.int32, (LANES, LANES), 0)
    blockdiag = (sr < HEAD_DIM) == (lax.broadcasted_iota(jnp.int32, (LANES, LANES), 1) < HEAD_DIM)

    @pl.when(step == 0)
    def _():
        s_ref[...] = jnp.zeros_like(s_ref)
        qi = lax.broadcasted_iota(jnp.int32, (cc, cc), 0)
        ki = lax.broadcasted_iota(jnp.int32, (cc, cc), 1)
        pos = lax.broadcasted_iota(jnp.int32, (cc, LANES), 0).astype(F32)
        for d in range(2):
            diff = (qi - ki) if d == 0 else (ki - qi)
            dpos = jnp.maximum(diff, 0).astype(F32)
            qpow = (pos + 1.0) if d == 0 else (cc - pos)
            kpow = (cc - 1.0 - pos) if d == 0 else pos
            for j in range(C_HEADS // 2):
                lg_e = -jnp.exp(ld_ref[d, 2 * j:2 * j + 1, :])
                lg_o = -jnp.exp(ld_ref[d, 2 * j + 1:2 * j + 2, :])
                lgl = jnp.where(low[0:1, :], lg_e, lg_o)
                wide_e = jnp.concatenate([lg_e] * (cc // LANES), axis=1)
                wide_o = jnp.concatenate([lg_o] * (cc // LANES), axis=1)
                dec_ref[d, j, :cc] = jnp.where(diff >= 0, jnp.exp(wide_e * dpos), 0.0)
                dec_ref[d, j, cc:] = jnp.where(diff >= 0, jnp.exp(wide_o * dpos), 0.0)
                qw_ref[d, j] = jnp.exp(lgl * qpow)
                kw_ref[d, j] = jnp.exp(lgl * kpow)
                cd_ref[d, j] = jnp.where(sr < HEAD_DIM, jnp.exp(lg_e * cc), jnp.exp(lg_o * cc))

    pairs = C_HEADS // 2
    dirs = ((qf_ref, kf_ref, vf_ref, of_ref), (qr_ref, kr_ref, vr_ref, or_ref))
    states = [[s_ref[d, j] for j in range(pairs)] for d in range(2)]
    outs, new_states = [], []
    for d, (q_ref, k_ref, v_ref, _) in enumerate(dirs):
        for j in range(pairs):
            sl = slice(j * LANES, (j + 1) * LANES)
            q, k, v = q_ref[:, sl], k_ref[:, sl], v_ref[:, sl]
            zero = jnp.zeros_like(q)
            q2 = jnp.concatenate([jnp.where(low, q, zero), jnp.where(low, zero, q)], axis=0)
            sc = (_nt(q2, k) * dec_ref[d, j]).astype(BF16)
            o2 = jnp.dot(sc, v, preferred_element_type=F32)
            o_intra = jnp.where(low, o2[:cc], o2[cc:])
            state = states[d][j]
            qw = (q.astype(F32) * qw_ref[d, j]).astype(BF16)
            o_cross = jnp.dot(qw, state.astype(BF16), preferred_element_type=F32)
            kw = (k.astype(F32) * kw_ref[d, j]).astype(BF16)
            kv = lax.dot_general(kw, v, (((0,), (0,)), ((), ())), preferred_element_type=F32)
            new_states.append(state * cd_ref[d, j] + jnp.where(blockdiag, kv, 0.0))
            outs.append(o_intra + o_cross)
    for d in range(2):
        for j in range(pairs):
            s_ref[d, j] = new_states[d * pairs + j]
            dirs[d][3][:, j * LANES:(j + 1) * LANES] = outs[d * pairs + j]


def _retention(prep, proj, ld_head, *, n_batch, seq, n_ctx):
    cc = RET_CHUNK
    nl, nc = seq // cc, n_ctx // cc
    steps = nc + nl
    lat_blocks = n_batch * nl
    rows = n_batch * (seq + n_ctx)

    def rowblk(b, d, s):
        c_ctx = jnp.where(d == 0, s, nc - 1 - s)
        c_lat = jnp.where(d == 0, s - nc, nl - 1 - (s - nc))
        return jnp.where(s < nc, lat_blocks + b * nc + c_ctx, b * nl + c_lat)

    def spec(d, col):
        return pl.BlockSpec((cc, 512), lambda b, s: (rowblk(b, d, s), col))

    pairs = C_HEADS // 2
    return pl.pallas_call(
        _ret_kernel,
        grid=(n_batch, steps),
        in_specs=[spec(d, col) for d in (0, 1) for col in (P_CQ // 512, P_CK // 512, P_CV // 512)]
        + [pl.BlockSpec((2, C_HEADS, LANES), lambda b, s: (0, 0, 0))],
        out_specs=[spec(0, 0), spec(1, 0)],
        out_shape=[jax.ShapeDtypeStruct((rows, 512), F32)] * 2,
        scratch_shapes=[pltpu.VMEM((2, pairs, LANES, LANES), F32),
                        pltpu.VMEM((2, pairs, 2 * cc, cc), F32),
                        pltpu.VMEM((2, pairs, cc, LANES), F32),
                        pltpu.VMEM((2, pairs, cc, LANES), F32),
                        pltpu.VMEM((2, pairs, LANES, LANES), F32)],
        compiler_params=_cp(("arbitrary", "arbitrary"), VMEM_LIMIT), name="mixer_retention",
    )(prep, prep, proj, prep, prep, proj, ld_head)


def _retfin_kernel(of_ref, or_ref, g_ref, gm_ref, out_ref):
    o = of_ref[...] + or_ref[...]
    on = o * lax.rsqrt(_group_meansq(o, gm_ref, 512) + EPS)
    g = g_ref[...].astype(F32)
    out_ref[...] = (g * jax.nn.sigmoid(g) * on).astype(out_ref.dtype)


def _retention_finish(o_fwd, o_rev, proj, gmat, *, rows):
    tr = 256
    return pl.pallas_call(
        _retfin_kernel,
        grid=(rows // tr,),
        in_specs=[pl.BlockSpec((tr, 512), lambda i: (i, 0)),
                  pl.BlockSpec((tr, 512), lambda i: (i, 0)),
                  pl.BlockSpec((tr, 512), lambda i: (i, P_CG // 512)),
                  pl.BlockSpec((512, 512), lambda i: (0, 0))],
        out_specs=pl.BlockSpec((tr, 512), lambda i: (i, 0)),
        out_shape=jax.ShapeDtypeStruct((rows, 512), BF16),
        compiler_params=_cp(("arbitrary",), VMEM_LIMIT), name="retention_finish",
    )(o_fwd, o_rev, proj, gmat)


def _nbr_kernel(par_ref, q_ref, k_ref, v_ref, kc_ref, vc_ref, abm_ref, o_ref, *, rows):
    st = pl.program_id(1)
    start = jnp.clip(NA_QROWS * st - NA_ROWS // 2, 0, rows - NA_KROWS)
    off = pl.multiple_of(start * GRID_W, GRID_W)
    nk = NA_KROWS * GRID_W
    tq = q_ref.shape[0]
    low = lax.broadcasted_iota(jnp.int32, (tq, LANES), 1) < HEAD_DIM
    bound = par_ref[0]
    safe = par_ref[1] > 0.0
    pair_w = 2 * LANES

    def attend(use_max):
        for j in range(D_HEADS // 2):
            sl = slice(j * LANES, (j + 1) * LANES)
            sl2 = slice(j * pair_w, (j + 1) * pair_w)
            qc = q_ref[:, sl]
            zero = jnp.zeros_like(qc)
            q2 = jnp.concatenate([jnp.where(low, qc, zero), jnp.where(low, zero, qc)], axis=0)
            bias = jnp.concatenate([abm_ref[0, 2 * j], abm_ref[0, 2 * j + 1]], axis=0)
            s1 = _nt(q2, k_ref[pl.ds(off, nk), sl]) + bias
            s2 = _nt(q2, kc_ref[:, sl]) - bound
            if use_max:
                m = jnp.maximum(jnp.max(s1, axis=-1, keepdims=True), jnp.max(s2, axis=-1, keepdims=True))
                s1, s2 = s1 - m, s2 - m
            o = (jnp.dot(jnp.exp2(s1).astype(BF16), v_ref[pl.ds(off, nk), sl2], preferred_element_type=F32)
                 + jnp.dot(jnp.exp2(s2).astype(BF16), vc_ref[:, sl2], preferred_element_type=F32))
            res = o[:, :LANES] / o[:, LANES:LANES + 1]
            o_ref[:, sl] = jnp.where(low, res[:tq], res[tq:]).astype(o_ref.dtype)

    @pl.when(safe)
    def _():
        attend(False)

    @pl.when(jnp.logical_not(safe))
    def _():
        attend(True)


def _nbr(prep, dv2, abm, par, *, n_batch, seq, n_ctx, with_ctx):
    rows = seq // GRID_W
    tq = NA_QROWS * GRID_W
    assert n_ctx == tq and rows % NA_QROWS == 0 and rows >= NA_KROWS + NA_QROWS
    nst = rows // NA_QROWS
    steps = nst + (1 if with_ctx else 0)
    rows_out = n_batch * seq + (n_batch * n_ctx if with_ctx else 0)

    def qmap(col):
        def f(b, s, par):
            return (jnp.where(s < nst, b * nst + s, n_batch * nst + b), col)
        return f

    def case(b, s, par):
        c = jnp.where(s == 0, 0, jnp.where(s == nst - 1, 2, jnp.where(s == nst, 3, 1)))
        return (c, 0, 0, 0)

    ctx_blk = n_batch * seq // n_ctx
    return pl.pallas_call(
        functools.partial(_nbr_kernel, rows=rows),
        grid_spec=pltpu.PrefetchScalarGridSpec(
            num_scalar_prefetch=1, grid=(n_batch, steps),
            in_specs=[pl.BlockSpec((tq, 512), qmap(P_DQ // 512)),
                      pl.BlockSpec((seq, 512), lambda b, s, par: (b, P_DK // 512)),
                      pl.BlockSpec((seq, 2 * 512), lambda b, s, par: (b, 0)),
                      pl.BlockSpec((n_ctx, 512), lambda b, s, par: (ctx_blk + b, P_DK // 512)),
                      pl.BlockSpec((n_ctx, 2 * 512), lambda b, s, par: (ctx_blk + b, 0)),
                      pl.BlockSpec((1, D_HEADS, tq, NA_KROWS * GRID_W), case)],
            out_specs=pl.BlockSpec((tq, 512), qmap(0))),
        out_shape=jax.ShapeDtypeStruct((rows_out, 512), BF16),
        compiler_params=_cp(("arbitrary", "arbitrary"), VMEM_LIMIT), name="mixer_neighbourhood",
    )(par, prep, prep, dv2, prep, dv2, abm)


def _softmax_bound(q_gain, k_gain, extra):
    b = HEAD_DIM ** 0.5 * LOG2E * jnp.max(jnp.abs(q_gain)) * jnp.max(jnp.abs(k_gain)) + LOG2E * extra
    return (1.02 * b + 0.5).astype(F32)


def _nbr_bias_tables(rpb, rows, bound):
    w = GRID_W
    cidx = np.arange(w)
    col_start = np.clip(cidx - NA_COLS // 2, 0, w - NA_COLS)
    col_ok = (cidx[None, :] >= col_start[:, None]) & (cidx[None, :] < col_start[:, None] + NA_COLS)
    d_col = np.clip(cidx[None, :] - cidx[:, None] + (NA_COLS - 1), 0, 2 * NA_COLS - 2)
    n_heads, n_dr, n_dc = rpb.shape
    onehot = jnp.asarray((d_col[:, :, None] == np.arange(n_dc)[None, None, :]).astype(np.float32))
    t = jnp.einsum("hrd,qkd->hqrk", rpb.astype(F32), onehot, precision=lax.Precision.HIGHEST)
    t = t * LOG2E - bound
    t = jnp.where(jnp.asarray(col_ok)[None, :, None, :], t, NEG_INF).reshape(n_heads, w, n_dr * w)
    cases = []
    for r0 in (0, NA_QROWS, rows - NA_QROWS):
        start = int(np.clip(r0 - NA_ROWS // 2, 0, rows - NA_KROWS))
        blocks = []
        for i in range(NA_QROWS):
            r = r0 + i
            row_start = int(np.clip(r - NA_ROWS // 2, 0, rows - NA_ROWS))
            kk0 = row_start - start
            dr0 = row_start - r + (NA_ROWS - 1)
            blk = t[:, :, dr0 * w:(dr0 + NA_ROWS) * w]
            blocks.append(jnp.pad(blk, ((0, 0), (0, 0), (kk0 * w, (NA_KROWS - NA_ROWS - kk0) * w)),
                                  constant_values=NEG_INF))
        cases.append(jnp.concatenate(blocks, axis=1))
    cases.append(jnp.full((n_heads, NA_QROWS * w, NA_KROWS * w), NEG_INF, F32))
    return jnp.stack(cases)


def _swiglu_accumulate(x_ref, wg_ref, wu_ref, wd_ref, acc_ref, n_rows):
    x = x_ref[:n_rows]
    g = jnp.dot(x, wg_ref[0].astype(BF16), preferred_element_type=F32)
    u = jnp.dot(x, wu_ref[0].astype(BF16), preferred_element_type=F32)
    a = (g * jax.nn.sigmoid(g) * u).astype(BF16)
    acc_ref[:n_rows] += jnp.dot(a, wd_ref[0].astype(BF16), preferred_element_type=F32)


def _ffn_kernel(x_ref, wg_ref, wu_ref, wd_ref, o_ref, acc_ref):
    j = pl.program_id(1)

    @pl.when(j == 0)
    def _():
        acc_ref[...] = jnp.zeros_like(acc_ref)

    _swiglu_accumulate(x_ref, wg_ref, wu_ref, wd_ref, acc_ref, x_ref.shape[0])

    @pl.when(j == pl.num_programs(1) - 1)
    def _():
        o_ref[...] = acc_ref[...].astype(o_ref.dtype)


def _ffn(x, w_gate_up, w_down, layer, *, rows, tm, fc):
    _, d, ff2 = w_gate_up.shape
    nj = ff2 // 2 // fc
    return pl.pallas_call(
        _ffn_kernel,
        grid=(rows // tm, nj),
        in_specs=[pl.BlockSpec((tm, d), lambda i, j: (i, 0)),
                  pl.BlockSpec((1, d, fc), lambda i, j: (layer, 0, j)),
                  pl.BlockSpec((1, d, fc), lambda i, j: (layer, 0, nj + j)),
                  pl.BlockSpec((1, fc, d), lambda i, j: (layer, j, 0))],
        out_specs=pl.BlockSpec((tm, d), lambda i, j: (i, 0)),
        out_shape=jax.ShapeDtypeStruct((rows, d), BF16),
        scratch_shapes=[pltpu.VMEM((tm, d), F32)],
        compiler_params=_cp(("arbitrary", "arbitrary"), VMEM_LIMIT), name="swiglu",
    )(x, w_gate_up, w_gate_up, w_down)


def _moe_ffn_kernel(te_ref, tv_ref, tok_ref, hp_ref, wg_ref, wu_ref, wd_ref, o_ref, gbuf_ref, x_ref, acc_ref, sem,
                    *, rows_per_step):
    i, j = pl.program_id(0), pl.program_id(1)
    n_tiles = pl.num_programs(0)
    tm, d = x_ref.shape
    half = d // 2
    valid = tv_ref[i] > 0
    slot = i % 2

    def issue(tile, first_row, n_rows, to_slot):
        def body(r, carry):
            row = first_row + r
            pltpu.make_async_copy(hp_ref.at[pl.ds(tok_ref[tile * tm + row], 1), :],
                                  gbuf_ref.at[to_slot].at[pl.ds(row, 1), :], sem.at[to_slot]).start()
            return carry

        lax.fori_loop(0, n_rows, body, 0, unroll=8)

    @pl.when((i == 0) & (j == 0) & valid)
    def _():
        issue(0, 0, tm, 0)

    @pl.when((j == 0) & valid)
    def _():
        pltpu.make_async_copy(hp_ref.at[pl.ds(0, tm), :], gbuf_ref.at[slot], sem.at[slot]).wait()
        lo, hi = _unpack_halves(gbuf_ref[slot])
        x_ref[:, :half] = lo.astype(BF16)
        x_ref[:, half:] = hi.astype(BF16)
        acc_ref[...] = jnp.zeros_like(acc_ref)

    nxt = jnp.minimum(i + 1, n_tiles - 1)

    @pl.when((i + 1 < n_tiles) & (tv_ref[nxt] > 0) & (j * rows_per_step < tm))
    def _():
        issue(nxt, j * rows_per_step, rows_per_step, 1 - slot)

    for q in range(1, MOE_ROW_SPLITS + 1):
        n_rows = q * (tm // MOE_ROW_SPLITS)

        @pl.when(tv_ref[i] == n_rows)
        def _():
            _swiglu_accumulate(x_ref, wg_ref, wu_ref, wd_ref, acc_ref, n_rows)

    @pl.when(j == pl.num_programs(1) - 1)
    def _():
        o_ref[...] = jnp.where(valid, _pack_halves(acc_ref[...]), jnp.uint32(0))


def _moe_ffn(hp, w_gate_up, w_down, layer, tile_expert, tile_valid, row_token, *, tm, fc):
    _, _, d, ff2 = w_gate_up.shape
    nj = ff2 // 2 // fc
    n_rows = row_token.shape[0]
    issue_steps = 1 << (nj.bit_length() - 1)

    def jeff(i, j, tv):
        return jnp.where(tv[i] > 0, j, nj - 1)

    return pl.pallas_call(
        functools.partial(_moe_ffn_kernel, rows_per_step=tm // issue_steps),
        grid_spec=pltpu.PrefetchScalarGridSpec(
            num_scalar_prefetch=3, grid=(n_rows // tm, nj),
            in_specs=[pl.BlockSpec(memory_space=pl.ANY),
                      pl.BlockSpec((None, 1, d, fc), lambda i, j, te, tv, tok: (layer, te[i], 0, jeff(i, j, tv))),
                      pl.BlockSpec((None, 1, d, fc),
                                   lambda i, j, te, tv, tok: (layer, te[i], 0, nj + jeff(i, j, tv))),
                      pl.BlockSpec((None, 1, fc, d), lambda i, j, te, tv, tok: (layer, te[i], jeff(i, j, tv), 0))],
            out_specs=pl.BlockSpec((tm, d // 2), lambda i, j, te, tv, tok: (i, 0)),
            scratch_shapes=[pltpu.VMEM((2, tm, d // 2), jnp.uint32), pltpu.VMEM((tm, d), BF16),
                            pltpu.VMEM((tm, d), F32), pltpu.SemaphoreType.DMA((2,))]),
        out_shape=jax.ShapeDtypeStruct((n_rows, d // 2), jnp.uint32),
        compiler_params=_cp(("arbitrary", "arbitrary"), VMEM_LIMIT_MOE), name="moe_swiglu",
    )(tile_expert, tile_valid, row_token, hp, w_gate_up, w_gate_up, w_down)


def _combine_kernel(p0_ref, p1_ref, y_ref, x_ref, route_ref, gate_ref, o_ref, b0_ref, b1_ref, sem):
    _, n, half = b0_ref.shape
    i = pl.program_id(0)
    slot = i % 2

    def issue(tile, to_slot):
        def body(r, carry):
            k = tile * n + r
            pltpu.make_async_copy(y_ref.at[pl.ds(p0_ref[k], 1), :], b0_ref.at[to_slot].at[pl.ds(r, 1), :],
                                  sem.at[0, to_slot]).start()
            pltpu.make_async_copy(y_ref.at[pl.ds(p1_ref[k], 1), :], b1_ref.at[to_slot].at[pl.ds(r, 1), :],
                                  sem.at[1, to_slot]).start()
            return carry

        lax.fori_loop(0, n, body, 0, unroll=8)

    @pl.when(i == 0)
    def _():
        issue(0, 0)

    pltpu.make_async_copy(y_ref.at[pl.ds(0, n), :], b0_ref.at[slot], sem.at[0, slot]).wait()
    pltpu.make_async_copy(y_ref.at[pl.ds(0, n), :], b1_ref.at[slot], sem.at[1, slot]).wait()

    @pl.when(i + 1 < pl.num_programs(0))
    def _():
        issue(i + 1, 1 - slot)

    route = route_ref[...]
    w0, w1 = route[:, 2:3], route[:, 3:4]
    lo0, hi0 = _unpack_halves(b0_ref[slot])
    lo1, hi1 = _unpack_halves(b1_ref[slot])
    gate = gate_ref[0, 0]
    o_ref[:, :half] = x_ref[:, :half] + gate[:, :half] * (w0 * lo0 + w1 * lo1)
    o_ref[:, half:] = x_ref[:, half:] + gate[:, half:] * (w0 * hi0 + w1 * hi1)


def _combine(y, x, route, mod, gate_k, pos0, pos1, *, seq, n_batch):
    t, d = x.shape
    tr = 256
    return pl.pallas_call(
        _combine_kernel,
        grid_spec=pltpu.PrefetchScalarGridSpec(
            num_scalar_prefetch=2, grid=(t // tr,),
            in_specs=[pl.BlockSpec(memory_space=pl.ANY),
                      pl.BlockSpec((tr, d), lambda i, a, b: (i, 0)),
                      pl.BlockSpec((tr, LANES), lambda i, a, b: (i, 0)),
                      pl.BlockSpec((1, 1, 1, d),
                                   lambda i, a, b: (jnp.minimum((i * tr) // seq, n_batch), gate_k, 0, 0))],
            out_specs=pl.BlockSpec((tr, d), lambda i, a, b: (i, 0)),
            scratch_shapes=[pltpu.VMEM((2, tr, d // 2), jnp.uint32), pltpu.VMEM((2, tr, d // 2), jnp.uint32),
                            pltpu.SemaphoreType.DMA((2, 2))]),
        out_shape=jax.ShapeDtypeStruct((t, d), F32),
        compiler_params=_cp(("arbitrary",), VMEM_LIMIT), name="moe_combine",
    )(pos0, pos1, y, x, route, _mod_view(mod))


def _route_meta(idx, n_exp, tm):
    t = idx.shape[0]
    flat = idx.reshape(-1)
    onehot = (flat[:, None] == jnp.arange(n_exp, dtype=jnp.int32)[None, :]).astype(jnp.int32)
    counts = jnp.sum(onehot, axis=0)
    rank = jnp.take_along_axis(jnp.cumsum(onehot, axis=0) - onehot, flat[:, None], axis=1)[:, 0]
    padded = ((counts + tm - 1) // tm) * tm
    ends = jnp.cumsum(padded)
    starts = ends - padded
    pos = starts[flat] + rank
    n_rows = TOP_K * t + n_exp * tm
    row_token = jnp.zeros((n_rows,), jnp.int32).at[pos].set(jnp.arange(TOP_K * t, dtype=jnp.int32) // TOP_K)
    tile_start = jnp.arange(n_rows // tm, dtype=jnp.int32) * tm
    tile_valid = tile_start < ends[-1]
    tile_expert = jnp.sum((ends[None, :] <= tile_start[:, None]).astype(jnp.int32), axis=1)
    tile_expert = jnp.minimum(tile_expert, n_exp - 1)
    last_valid = jnp.maximum(ends[-1] // tm - 1, 0)
    tile_expert = jnp.where(tile_valid, tile_expert, tile_expert[last_valid])
    sub = tm // MOE_ROW_SPLITS
    used = jnp.clip(counts[tile_expert] - (tile_start - starts[tile_expert]), 0, tm)
    tile_rows = jnp.where(tile_valid, ((used + sub - 1) // sub) * sub, 0).astype(jnp.int32)
    pos = pos.reshape(t, TOP_K).astype(jnp.int32)
    return row_token, tile_expert, tile_rows, pos[:, 0], pos[:, 1]


def _rope_tables(seq, pad_rows):
    t = jnp.arange(seq, dtype=jnp.int32)
    row = (t // GRID_W).astype(F32)
    col = (t % GRID_W).astype(F32)

    def angles(rot_dim):
        n_freq = rot_dim // 4
        freqs = ROPE_THETA ** (-jnp.arange(n_freq, dtype=F32) / n_freq)
        ang = jnp.concatenate([row[:, None] * freqs[None, :], col[:, None] * freqs[None, :]], axis=-1)
        return jnp.cos(ang), jnp.sin(ang)

    ch, sh = angles(HEAD_DIM)
    cos_h = jnp.concatenate([ch] * 4, axis=1)
    sin_h = jnp.concatenate([-sh, sh, -sh, sh], axis=1)
    cb, sb = angles(B_ROPE)
    one = jnp.ones((seq, B_NOPE), F32)
    zero = jnp.zeros((seq, B_NOPE), F32)
    tail1 = jnp.ones((seq, LANES - B_QK), F32)
    tail0 = jnp.zeros((seq, LANES - B_QK), F32)
    cos_b = jnp.concatenate([one, cb, cb, tail1], axis=1)
    sin_b = jnp.concatenate([zero, -sb, sb, tail0], axis=1)

    def pad(tab, fill):
        return jnp.concatenate([tab, jnp.full((pad_rows, LANES), fill, F32)], axis=0)

    return pad(cos_h, 1.0), pad(sin_h, 0.0), pad(cos_b, 1.0), pad(sin_b, 0.0)


def _permute_w_in(w):
    d = w.shape[0]
    o = np.cumsum([0, 512, 128, 128, B_Q_RANK, B_KV_RANK, B_ROPE, 512, 512, 512, 512, 512, 512, 512])
    aq, ak, av, bq, bkv, bkr, cq, ck, cv, cg, dq, dk, dv = [w[:, o[i]:o[i + 1]] for i in range(13)]

    def dup(m):
        return jnp.concatenate([m[:, :64], m[:, :64], m[:, 64:], m[:, 64:]], axis=1)

    z = lambda n: jnp.zeros((d, n), w.dtype)
    half = B_ROPE // 2
    bkr_swapped = jnp.concatenate([bkr[:, half:], bkr[:, :half]], axis=1)
    parts = [aq, cq, ck, dq, dk, dup(ak), dup(av), cv, cg, dv, bq, bkv,
             z(B_NOPE), bkr, z(LANES - B_QK), z(B_NOPE), bkr_swapped, z(LANES - B_QK)]
    return jnp.concatenate(parts, axis=1).astype(BF16)


def _block_diag_ones():
    i = np.arange(512) // HEAD_DIM
    return jnp.asarray((i[:, None] == i[None, :]).astype(np.float32), dtype=BF16)


def _rope_partner(m):
    half = B_ROPE // 2
    return jnp.concatenate([m[..., :B_NOPE], m[..., B_NOPE + half:], m[..., B_NOPE:B_NOPE + half]], axis=-1)


def _pad_lanes(v, width):
    return jnp.concatenate([v.astype(F32), jnp.zeros((width - v.shape[0],), F32)])


def kernel(x, c, ctx, c_ctx, w_ada, b_ada, norm_mix, norm_ffn, w_in, w_out, a_q_norm, a_k_norm, a_sink,
           b_q_a_norm, b_kv_a_norm, b_w_uq, b_w_ukv, b_q_norm, b_k_norm, c_log_decay, d_q_norm, d_k_norm,
           d_rpb, ffn_w_gate_up, ffn_w_down, moe_router, moe_w_gate_up, moe_w_down):
    n_batch, seq, d = x.shape
    n_ctx = ctx.shape[1]
    depth = w_ada.shape[0]
    n_lat = n_batch * seq
    n_all = n_lat + n_batch * n_ctx
    grid_rows = seq // GRID_W
    tm = next(t for t in (1024, 512, 256) if n_lat % t == 0 and n_all % t == 0)

    cc = jnp.concatenate([c, c_ctx[None, :], jnp.zeros((8 - n_batch - 1, d), F32)], axis=0)
    mod = _ada(cc, w_ada, b_ada)
    xs = (x.reshape(n_lat, d), ctx.reshape(n_batch * n_ctx, d))
    cos_h, sin_h, cos_b, sin_b = _rope_tables(seq, 256)
    gmat = _block_diag_ones()

    y = None
    y_gate, y_mod = 0, None
    for l in range(depth):
        last = l == depth - 1
        mod_l = mod[l]
        if y is None:
            (h,) = _resnorm(xs, None, None, 0, mod_l, norm_mix[l], 0, 1, rows=n_all, seq=seq, n_batch=n_batch)
        else:
            xs, h = _resnorm(xs, y, y_mod, y_gate, mod_l, norm_mix[l], 0, 1, rows=n_all, seq=seq,
                             n_batch=n_batch)
        proj = _mm([h], _permute_w_in(w_in[l]), rows=n_all, tm=tm, tn=768, out_dtype=BF16, name="proj_in")
        gains = jnp.stack([jnp.tile(a_q_norm[l], 8), _pad_lanes(jnp.tile(a_k_norm[l], 4), 512),
                           jnp.tile(d_q_norm[l], 8), jnp.tile(d_k_norm[l], 8)]).astype(F32)
        prep, dv2 = _prep(proj, cos_h, sin_h, gmat, gains, rows=n_all, seq=seq, n_lat=n_lat)
        wuq = jnp.pad(b_w_uq[l].reshape(B_Q_RANK, B_HEADS, B_QK),
                      ((0, 0), (0, 0), (0, LANES - B_QK))).reshape(B_Q_RANK, B_HEADS * LANES).astype(BF16)
        wukv3 = b_w_ukv[l].reshape(B_KV_RANK, B_HEADS, B_NOPE + B_V)
        wukv = jnp.concatenate(
            [jnp.pad(wukv3[:, :, :B_NOPE], ((0, 0), (0, 0), (0, LANES - B_NOPE))).reshape(B_KV_RANK, -1),
             wukv3[:, :, B_NOPE:].reshape(B_KV_RANK, -1)], axis=1).astype(BF16)
        bound = _mla_logit_bound(b_q_norm[l], b_k_norm[l])
        mla_safe = (bound <= MLA_SAFE_BOUND).astype(jnp.int32).reshape(1)
        gains_b = jnp.stack([b_q_a_norm[l].astype(F32), _pad_lanes(b_kv_a_norm[l], B_Q_RANK),
                             _pad_lanes(b_q_norm[l], B_Q_RANK), _pad_lanes(b_k_norm[l], B_Q_RANK),
                             jnp.full((B_Q_RANK,), -1.0, F32) * bound,
                             _pad_lanes(_rope_partner(b_q_norm[l]), B_Q_RANK),
                             _pad_lanes(_rope_partner(b_k_norm[l]), B_Q_RANK), jnp.zeros((B_Q_RANK,), F32)])
        wuq3 = b_w_uq[l].reshape(B_Q_RANK, B_HEADS, B_QK)
        wuq_sw = jnp.pad(_rope_partner(wuq3), ((0, 0), (0, 0), (0, LANES - B_QK))).reshape(wuq.shape).astype(BF16)
        qb, kb, vb = _prepb(proj, cos_b, sin_b, wuq, wuq_sw, wukv, gains_b, rows=n_all, seq=seq, n_lat=n_lat)

        sink = a_sink[l].astype(F32)
        bound_a = _softmax_bound(a_q_norm[l], a_k_norm[l], jnp.maximum(jnp.max(sink), 0.0))
        par_a = jnp.concatenate([sink * LOG2E, bound_a[None], (bound_a <= MLA_SAFE_BOUND).astype(F32)[None]])
        oa = _wina(prep, proj, par_a, n_batch=n_batch, seq=seq, n_ctx=n_ctx, with_ctx=not last)
        ob = _mla(qb, kb, vb, mla_safe, n_batch=n_batch, seq=seq, n_ctx=n_ctx, with_ctx=not last)
        ld_head = jnp.broadcast_to(c_log_decay[l].astype(F32)[:, :, None], (2, C_HEADS, LANES))
        oc_fwd, oc_rev = _retention(prep, proj, ld_head, n_batch=n_batch, seq=seq, n_ctx=n_ctx)
        bound_d = _softmax_bound(d_q_norm[l], d_k_norm[l], jnp.max(jnp.abs(d_rpb[l])))
        par_d = jnp.stack([bound_d, (bound_d <= MLA_SAFE_BOUND).astype(F32)])
        abm = _nbr_bias_tables(d_rpb[l], grid_rows, bound_d)
        od = _nbr(prep, dv2, abm, par_d, n_batch=n_batch, seq=seq, n_ctx=n_ctx, with_ctx=not last)
        rows_l = n_lat if last else n_all
        oc = _retention_finish(oc_fwd, oc_rev, proj, gmat, rows=rows_l)
        ymix = _mm([oa, ob, oc, od], w_out[l].astype(BF16), rows=rows_l, tm=tm, tn=1024, out_dtype=BF16,
                   name="proj_out")
        i = l // 2
        if l % 2 == 0:
            xs, h2 = _resnorm(xs, ymix, mod_l, 2, mod_l, norm_ffn[l], 3, 4, rows=rows_l, seq=seq,
                              n_batch=n_batch)
            y = _ffn(h2, ffn_w_gate_up, ffn_w_down, i, rows=rows_l, tm=tm, fc=512)
            y_gate, y_mod = 5, mod_l
        else:
            xs, hp, route = _resnorm(xs, ymix, mod_l, 2, mod_l, norm_ffn[l], 3, 4, rows=rows_l, seq=seq,
                                     n_batch=n_batch, router=moe_router[i])
            top_idx = route[:, :TOP_K].astype(jnp.int32)
            row_token, tile_expert, tile_valid, pos0, pos1 = _route_meta(top_idx, moe_router.shape[2], tm)
            yg = _moe_ffn(hp, moe_w_gate_up, moe_w_down, i, tile_expert, tile_valid, row_token, tm=tm, fc=512)
            xs = _combine(yg, xs, route, mod_l, 5, pos0, pos1, seq=seq, n_batch=n_batch)
            y = None
    if y is not None:
        xs, _ = _resnorm(xs, y, y_mod, y_gate, y_mod, norm_ffn[depth - 1], 3, 4, rows=xs.shape[0], seq=seq,
                         n_batch=n_batch)
    return xs[:n_lat].reshape(n_batch, seq, d)
```

```python
import functools

import numpy as np
import jax
import jax.numpy as jnp
from jax import lax
from jax.experimental import pallas as pl
from jax.experimental.pallas import tpu as pltpu

F32 = jnp.float32
BF16 = jnp.bfloat16

GRID_W = 64
HEAD_DIM = 64
ROPE_THETA = 10000.0
EPS = 1e-6
NEG_INF = -1e30

A_HEADS = 8
A_KV_HEADS = 2
A_BLOCK = 128
B_HEADS = 8
B_Q_RANK = 384
B_KV_RANK = 128
B_NOPE = 64
B_ROPE = 32
B_QK = B_NOPE + B_ROPE
B_V = 64
C_HEADS = 8
D_HEADS = 8
NA_ROWS = 8
NA_COLS = 16
TOP_K = 2

LANES = 128
NA_QROWS = 4
NA_KROWS = NA_ROWS + NA_QROWS
NA_CASES = 4
MOE_ROW_SPLITS = 4
RET_CHUNK = 256

P_AQ, P_CQ, P_CK, P_DQ, P_DK, P_AK, P_AV, P_CV, P_CG, P_DV, P_BQ, P_BKV, P_BKR, P_BKR_SW, P_END = (
    0, 512, 1024, 1536, 2048, 2560, 2816, 3072, 3584, 4096, 4608, 4992, 5120, 5248, 5376)
PREP_W = P_AV

VMEM_LIMIT = 56 * 1024 * 1024
VMEM_LIMIT_MOE = 61 * 1024 * 1024

LOG2E = float(np.log2(np.e))
MLA_LOGIT_SCALE = B_QK ** -0.5 * LOG2E
SOFTMAX_SAFE_BOUND = 40.0


def _cp(sem, vmem=None):
    return pltpu.CompilerParams(dimension_semantics=sem, vmem_limit_bytes=vmem)


def _nt(a, b):
    return lax.dot_general(a, b, (((1,), (1,)), ((), ())), preferred_element_type=F32)


_HI_HALF = 0xFFFF0000


def _pack_halves(x):
    n = x.shape[1] // 2
    bits = pltpu.bitcast(x.astype(BF16).astype(F32), jnp.uint32)
    return (bits[:, :n] >> 16) | (bits[:, n:] & jnp.uint32(_HI_HALF))


def _unpack_halves(w):
    return pltpu.bitcast(w << 16, F32), pltpu.bitcast(w & jnp.uint32(_HI_HALF), F32)


def _ada_kernel(c_ref, w_ref, b_ref, o_ref):
    c = c_ref[...]
    s = (c * jax.nn.sigmoid(c)).astype(BF16)
    o_ref[0] = jnp.dot(s, w_ref[0].astype(BF16), preferred_element_type=F32) + b_ref[0]


def _ada(cc, w_ada, b_ada):
    depth, d, n = w_ada.shape
    tn = 1024
    return pl.pallas_call(
        _ada_kernel,
        grid=(depth, n // tn),
        in_specs=[pl.BlockSpec((8, d), lambda l, j: (0, 0)),
                  pl.BlockSpec((1, d, tn), lambda l, j: (l, 0, j)),
                  pl.BlockSpec((1, 1, tn), lambda l, j: (l, 0, j))],
        out_specs=pl.BlockSpec((1, 8, tn), lambda l, j: (l, 0, j)),
        out_shape=jax.ShapeDtypeStruct((depth, 8, n), F32),
        compiler_params=_cp(("arbitrary", "arbitrary"), VMEM_LIMIT),
        name="ada",
    )(cc, w_ada, b_ada.reshape(depth, 1, n))


def _resnorm_kernel(*refs, has_res, with_router, n_exp, lat_tiles):
    refs = list(refs)
    x_ref = refs.pop(0)
    if lat_tiles is not None:
        xc_ref = refs.pop(0)
        x = jnp.where(pl.program_id(0) < lat_tiles, x_ref[...], xc_ref[...])
    else:
        x = x_ref[...]
    if has_res:
        y_ref = refs.pop(0)
        gate_ref = refs.pop(0)
    gain_ref, sh_ref, sc_ref = refs[:3]
    refs = refs[3:]
    if with_router:
        router_ref = refs.pop(0)
    if has_res:
        xo_ref = refs.pop(0)
    h_ref = refs.pop(0)
    if has_res:
        x = x + gate_ref[0, 0] * y_ref[...].astype(F32)
        xo_ref[...] = x
    ms = jnp.mean(x * x, axis=-1, keepdims=True)
    h = (x * lax.rsqrt(ms + EPS) * gain_ref[...]) * (1.0 + sc_ref[0, 0]) + sh_ref[0, 0]
    if with_router:
        (route_ref,) = refs
        h_ref[...] = _pack_halves(h)
        h_hi = h.astype(BF16)
        h_lo = (h - h_hi.astype(F32)).astype(BF16)
        r_hi, r_lo = router_ref[0], router_ref[1]
        logits = (jnp.dot(h_hi, r_hi, preferred_element_type=F32) + jnp.dot(h_lo, r_hi, preferred_element_type=F32)
                  + jnp.dot(h_hi, r_lo, preferred_element_type=F32))
        lane = lax.broadcasted_iota(jnp.int32, logits.shape, 1)
        logits = jnp.where(lane < n_exp, logits, -jnp.inf)
        m1 = jnp.max(logits, axis=-1, keepdims=True)
        i1 = jnp.min(jnp.where(logits == m1, lane, LANES), axis=-1, keepdims=True)
        rest = jnp.where(lane == i1, -jnp.inf, logits)
        m2 = jnp.max(rest, axis=-1, keepdims=True)
        i2 = jnp.min(jnp.where(rest == m2, lane, LANES), axis=-1, keepdims=True)
        e2 = jnp.exp(m2 - m1)
        g1 = 1.0 / (1.0 + e2)
        g2 = e2 / (1.0 + e2)
        route = jnp.where(lane == 0, i1.astype(F32),
                          jnp.where(lane == 1, i2.astype(F32),
                                    jnp.where(lane == 2, g1, jnp.where(lane == 3, g2, 0.0))))
        route_ref[...] = route
    else:
        h_ref[...] = h.astype(h_ref.dtype)


def _mod_view(mod):
    return mod.reshape(mod.shape[0], 6, 1, mod.shape[1] // 6)


def _resnorm(x, y, gate_mod, gate_k, mod, gain, sh_k, sc_k, *, rows, seq, n_batch, router=None):
    tr = 512
    has_res = y is not None
    with_router = router is not None
    split = isinstance(x, tuple)
    d = x[0].shape[1] if split else x.shape[1]

    def grp(i):
        return jnp.minimum((i * tr) // seq, n_batch)

    def modspec(k):
        return pl.BlockSpec((1, 1, 1, d), lambda i: (grp(i), k, 0, 0))

    row = pl.BlockSpec((tr, d), lambda i: (i, 0))
    lat_tiles = None
    if split:
        lat_tiles = x[0].shape[0] // tr
        in_specs = [pl.BlockSpec((tr, d), lambda i: (jnp.minimum(i, lat_tiles - 1), 0)),
                    pl.BlockSpec((tr, d), lambda i: (jnp.maximum(i - lat_tiles, 0), 0))]
        args = list(x)
    else:
        in_specs = [row]
        args = [x]
    if has_res:
        in_specs += [row, modspec(gate_k)]
        args += [y, _mod_view(gate_mod)]
    in_specs += [pl.BlockSpec((1, d), lambda i: (0, 0)), modspec(sh_k), modspec(sc_k)]
    args += [gain.reshape(1, d).astype(F32), _mod_view(mod), _mod_view(mod)]
    out_shape, out_specs = [], []
    n_exp = 0
    if with_router:
        n_exp = router.shape[1]
        router = jnp.pad(router.astype(F32), ((0, 0), (0, LANES - n_exp)))
        r_hi = router.astype(BF16)
        router = jnp.stack([r_hi, (router - r_hi.astype(F32)).astype(BF16)])
        in_specs.append(pl.BlockSpec(router.shape, lambda i: (0, 0, 0)))
        args.append(router)
    if has_res:
        out_shape.append(jax.ShapeDtypeStruct((rows, d), F32))
        out_specs.append(row)
    if with_router:
        out_shape += [jax.ShapeDtypeStruct((rows, d // 2), jnp.uint32), jax.ShapeDtypeStruct((rows, LANES), F32)]
        out_specs += [pl.BlockSpec((tr, d // 2), lambda i: (i, 0)), pl.BlockSpec((tr, LANES), lambda i: (i, 0))]
    else:
        out_shape.append(jax.ShapeDtypeStruct((rows, d), BF16))
        out_specs.append(row)
    kern = functools.partial(_resnorm_kernel, has_res=has_res, with_router=with_router, n_exp=n_exp,
                             lat_tiles=lat_tiles)
    return pl.pallas_call(
        kern, grid=(rows // tr,), in_specs=in_specs, out_specs=out_specs, out_shape=out_shape,
        compiler_params=_cp(("arbitrary",), VMEM_LIMIT), name="resnorm",
    )(*args)


def _mm_kernel(*refs, n_x):
    w_ref, o_ref = refs[n_x], refs[n_x + 1]
    if n_x == 1:
        x = refs[0][...]
    else:
        x = jnp.concatenate([r[...] for r in refs[:n_x]], axis=1)
    o_ref[...] = jnp.dot(x, w_ref[...], preferred_element_type=F32).astype(o_ref.dtype)


def _mm(xs, w, *, rows, tm, tn, out_dtype, name):
    n = w.shape[1]
    in_specs = [pl.BlockSpec((tm, a.shape[1]), lambda i, j: (i, 0)) for a in xs]
    in_specs.append(pl.BlockSpec((w.shape[0], tn), lambda i, j: (0, j)))
    return pl.pallas_call(
        functools.partial(_mm_kernel, n_x=len(xs)),
        grid=(rows // tm, n // tn),
        in_specs=in_specs,
        out_specs=pl.BlockSpec((tm, tn), lambda i, j: (i, j)),
        out_shape=jax.ShapeDtypeStruct((rows, n), out_dtype),
        compiler_params=_cp(("arbitrary", "arbitrary"), VMEM_LIMIT), name=name,
    )(*xs, w)


def _group_meansq(x, g_ref, width):
    x2 = x * x
    hi = x2.astype(BF16)
    lo = (x2 - hi.astype(F32)).astype(BF16)
    g = g_ref[:width, :width]
    ss = jnp.dot(hi, g, preferred_element_type=F32) + jnp.dot(lo, g, preferred_element_type=F32)
    return ss * (1.0 / HEAD_DIM)


def _prep_kernel(p_ref, dv_ref, ch_ref, sh_ref, g_ref, gains_ref, o_ref, dv2_ref):
    tr = p_ref.shape[0]

    def seg(off, width):
        return p_ref[:, off:off + width].astype(F32)

    def norm(x, row, width):
        return x * lax.rsqrt(_group_meansq(x, g_ref, width) + EPS) * gains_ref[row:row + 1, :width]

    def rope(x, width):
        lane = lax.broadcasted_iota(jnp.int32, (tr, width), 1)
        first = (lane % HEAD_DIM) < (HEAD_DIM // 2)
        cos = jnp.concatenate([ch_ref[...]] * (width // LANES), axis=1)
        sin = jnp.concatenate([sh_ref[...]] * (width // LANES), axis=1)
        swapped = jnp.where(first, pltpu.roll(x, width - HEAD_DIM // 2, 1), pltpu.roll(x, HEAD_DIM // 2, 1))
        return x * cos + swapped * sin

    scale = HEAD_DIM ** -0.5
    scale2 = scale * LOG2E
    o_ref[:, P_AQ:P_AQ + 512] = (rope(norm(seg(P_AQ, 512), 0, 512), 512) * scale2).astype(BF16)
    o_ref[:, P_CQ:P_CQ + 512] = (rope(seg(P_CQ, 512), 512) * scale).astype(BF16)
    o_ref[:, P_CK:P_CK + 512] = rope(seg(P_CK, 512), 512).astype(BF16)
    o_ref[:, P_DQ:P_DQ + 512] = (norm(seg(P_DQ, 512), 2, 512) * scale2).astype(BF16)
    o_ref[:, P_DK:P_DK + 512] = norm(seg(P_DK, 512), 3, 512).astype(BF16)
    o_ref[:, P_AK:P_AK + 256] = rope(norm(seg(P_AK, 256), 1, 256), 256).astype(BF16)
    ones_blk = jnp.where(lax.broadcasted_iota(jnp.int32, (tr, LANES), 1) < HEAD_DIM, 1.0, 0.0).astype(BF16)
    for g in range(D_HEADS // 2):
        dv2_ref[:, 2 * g * LANES:(2 * g + 1) * LANES] = dv_ref[:, g * LANES:(g + 1) * LANES]
        dv2_ref[:, (2 * g + 1) * LANES:(2 * g + 2) * LANES] = ones_blk


def _prep(proj, cos_h, sin_h, gmat, gains, *, rows, seq, n_lat):
    tr = 256
    nlat = seq // tr

    def tab(i):
        return (jnp.where(i * tr < n_lat, i % nlat, nlat), 0)

    return pl.pallas_call(
        _prep_kernel,
        grid=(rows // tr,),
        in_specs=[pl.BlockSpec((tr, PREP_W), lambda i: (i, 0)),
                  pl.BlockSpec((tr, 512), lambda i: (i, P_DV // 512)),
                  pl.BlockSpec((tr, LANES), tab),
                  pl.BlockSpec((tr, LANES), tab),
                  pl.BlockSpec((512, 512), lambda i: (0, 0)),
                  pl.BlockSpec((4, 512), lambda i: (0, 0))],
        out_specs=[pl.BlockSpec((tr, PREP_W), lambda i: (i, 0)),
                   pl.BlockSpec((tr, 2 * 512), lambda i: (i, 0))],
        out_shape=[jax.ShapeDtypeStruct((rows, PREP_W), BF16), jax.ShapeDtypeStruct((rows, 2 * 512), BF16)],
        compiler_params=_cp(("arbitrary",), VMEM_LIMIT), name="prep",
    )(proj, proj, cos_h, sin_h, gmat, gains)


def _prepb_kernel(bq_ref, bkv_ref, bkr_ref, bkrs_ref, cb_ref, sb_ref, wuq_ref, wuqs_ref, wukv_ref, gains_ref,
                  q_ref, k_ref, v_ref):
    tr = bq_ref.shape[0]
    cq = bq_ref[...].astype(F32)
    cqn = (cq * lax.rsqrt(jnp.mean(cq * cq, axis=-1, keepdims=True) + EPS) * gains_ref[0:1, :]).astype(BF16)
    qup = jnp.dot(cqn, wuq_ref[...], preferred_element_type=F32)
    qup_sw = jnp.dot(cqn, wuqs_ref[...], preferred_element_type=F32)
    ckv = bkv_ref[...].astype(F32)
    ckvn = ckv * lax.rsqrt(jnp.mean(ckv * ckv, axis=-1, keepdims=True) + EPS) * gains_ref[1:2, :LANES]
    kvup = jnp.dot(ckvn.astype(BF16), wukv_ref[...], preferred_element_type=F32)
    kr = bkr_ref[...].astype(F32)
    cb, sb = cb_ref[...], sb_ref[...]
    lane = lax.broadcasted_iota(jnp.int32, (tr, LANES), 1)
    q_cos = cb * (gains_ref[2:3, :LANES] * MLA_LOGIT_SCALE)
    q_sin = sb * (gains_ref[5:6, :LANES] * MLA_LOGIT_SCALE)
    k_cos = cb * gains_ref[3:4, :LANES]
    k_rot = bkrs_ref[...].astype(F32) * (sb * gains_ref[6:7, :LANES])

    def inv_rms(x):
        return lax.rsqrt(jnp.sum(x * x, axis=-1, keepdims=True) * (1.0 / B_QK) + EPS)

    shift_lane = lane == B_QK
    neg_bound = gains_ref[4:5, :LANES]
    for h in range(B_HEADS):
        sl = slice(h * LANES, (h + 1) * LANES)
        x = qup[:, sl]
        q = (x * q_cos + qup_sw[:, sl] * q_sin) * inv_rms(x)
        q_ref[:, sl] = jnp.where(shift_lane, 1.0, q).astype(BF16)
        x = kvup[:, sl] + kr
        k = (x * k_cos + k_rot) * inv_rms(x)
        k_ref[:, sl] = jnp.where(shift_lane, neg_bound, k).astype(BF16)
    ones_blk = jnp.where(lane < B_V, 1.0, 0.0).astype(BF16)
    for g in range(B_HEADS // 2):
        v_ref[:, 2 * g * LANES:(2 * g + 1) * LANES] = kvup[:, (B_HEADS + g) * LANES:(B_HEADS + g + 1) * LANES].astype(BF16)
        v_ref[:, (2 * g + 1) * LANES:(2 * g + 2) * LANES] = ones_blk


def _prepb(proj, cos_b, sin_b, wuq, wuq_sw, wukv, gains, *, rows, seq, n_lat):
    tr = 256
    nlat = seq // tr

    def tab(i):
        return (jnp.where(i * tr < n_lat, i % nlat, nlat), 0)

    return pl.pallas_call(
        _prepb_kernel,
        grid=(rows // tr,),
        in_specs=[pl.BlockSpec((tr, B_Q_RANK), lambda i: (i, P_BQ // B_Q_RANK)),
                  pl.BlockSpec((tr, LANES), lambda i: (i, P_BKV // LANES)),
                  pl.BlockSpec((tr, LANES), lambda i: (i, P_BKR // LANES)),
                  pl.BlockSpec((tr, LANES), lambda i: (i, P_BKR_SW // LANES)),
                  pl.BlockSpec((tr, LANES), tab),
                  pl.BlockSpec((tr, LANES), tab),
                  pl.BlockSpec(wuq.shape, lambda i: (0, 0)),
                  pl.BlockSpec(wuq.shape, lambda i: (0, 0)),
                  pl.BlockSpec(wukv.shape, lambda i: (0, 0)),
                  pl.BlockSpec((8, B_Q_RANK), lambda i: (0, 0))],
        out_specs=[pl.BlockSpec((tr, B_HEADS * LANES), lambda i: (i, 0)),
                   pl.BlockSpec((tr, B_HEADS * LANES), lambda i: (i, 0)),
                   pl.BlockSpec((tr, B_HEADS * LANES), lambda i: (i, 0))],
        out_shape=[jax.ShapeDtypeStruct((rows, B_HEADS * LANES), BF16),
                   jax.ShapeDtypeStruct((rows, B_HEADS * LANES), BF16),
                   jax.ShapeDtypeStruct((rows, B_HEADS * LANES), BF16)],
        compiler_params=_cp(("arbitrary",), VMEM_LIMIT), name="prep_mla",
    )(proj, proj, proj, proj, cos_b, sin_b, wuq, wuq_sw, wukv, gains)


def _wina_kernel(par_ref, q_ref, kp_ref, ko_ref, kn_ref, kc_ref, vp_ref, vo_ref, vn_ref, vc_ref, o_ref,
                 *, nb, seq):
    n = pl.program_id(1)
    blk = A_BLOCK
    rep = A_HEADS // A_KV_HEADS
    n_ctx = kc_ref.shape[0]
    kcat = jnp.concatenate([kp_ref[...], ko_ref[...], kn_ref[...], kc_ref[...]], axis=0)
    vcat = jnp.concatenate([vp_ref[...], vo_ref[...], vn_ref[...], vc_ref[...]], axis=0)
    n_keys = 3 * blk + n_ctx
    qi = lax.broadcasted_iota(jnp.int32, (blk, n_keys), 0)
    kj = lax.broadcasted_iota(jnp.int32, (blk, n_keys), 1)
    band = kj - blk
    kpos = n * blk + band
    valid = ((jnp.abs(qi - band) <= blk) & (kpos >= 0) & (kpos < seq) & (n < nb)) | (kj >= 3 * blk)
    low = lax.broadcasted_iota(jnp.int32, (blk, LANES), 1) < HEAD_DIM
    low_k = lax.broadcasted_iota(jnp.int32, (n_keys, LANES), 1) < HEAD_DIM
    bound = par_ref[A_HEADS]
    safe = par_ref[A_HEADS + 1] > 0.0

    def attend(use_max):
        mask = jnp.where(valid, 0.0 if use_max else -bound, NEG_INF)
        mask = jnp.concatenate([mask] * rep, axis=0)
        for g in range(A_KV_HEADS):
            kg = kcat[:, g * LANES:(g + 1) * LANES]
            v1 = jnp.where(low_k, vcat[:, g * LANES:(g + 1) * LANES], jnp.ones((), BF16))
            qs, sinks = [], []
            for r in range(rep):
                h = g * rep + r
                qc = q_ref[:, (h // 2) * LANES:(h // 2 + 1) * LANES]
                qs.append(jnp.where(low if h % 2 == 0 else jnp.logical_not(low), qc, jnp.zeros_like(qc)))
                sinks.append(jnp.full((blk, 1), par_ref[h], F32))
            s = _nt(jnp.concatenate(qs, axis=0), kg) + mask
            sink = jnp.concatenate(sinks, axis=0)
            if use_max:
                m = jnp.maximum(jnp.max(s, axis=-1, keepdims=True), sink)
                s, sink = s - m, sink - m
            else:
                sink = sink - bound
            o = jnp.dot(jnp.exp2(s).astype(BF16), v1, preferred_element_type=F32)
            res = o / (o[:, HEAD_DIM:HEAD_DIM + 1] + jnp.exp2(sink))
            for c in range(rep // 2):
                even = res[(2 * c) * blk:(2 * c + 1) * blk]
                odd = pltpu.roll(res[(2 * c + 1) * blk:(2 * c + 2) * blk], HEAD_DIM, 1)
                j = (g * rep) // 2 + c
                o_ref[:, j * LANES:(j + 1) * LANES] = jnp.where(low, even, odd).astype(o_ref.dtype)

    @pl.when(safe)
    def _():
        attend(False)

    @pl.when(jnp.logical_not(safe))
    def _():
        attend(True)


def _wina(prep, proj, sink, *, n_batch, seq, n_ctx, with_ctx):
    blk = A_BLOCK
    nb = seq // blk
    ncb = n_ctx // blk
    steps = nb + (ncb if with_ctx else 0)
    lat_blocks = n_batch * nb
    rows_out = n_batch * seq + (n_batch * n_ctx if with_ctx else 0)

    def qmap(b, n, s):
        return (jnp.where(n < nb, b * nb + n, lat_blocks + b * ncb + (n - nb)), 0)

    def kmap(delta, col):
        def f(b, n, s):
            return (b * nb + jnp.clip(n + delta, 0, nb - 1), col)
        return f

    def cmap(col):
        def f(b, n, s):
            return (n_batch * seq // n_ctx + b, col)
        return f

    kcol, vcol = P_AK // 256, P_AV // 256
    in_specs = [pl.BlockSpec((blk, 512), qmap)]
    in_specs += [pl.BlockSpec((blk, 256), kmap(dl, kcol)) for dl in (-1, 0, 1)]
    in_specs += [pl.BlockSpec((n_ctx, 256), cmap(kcol))]
    in_specs += [pl.BlockSpec((blk, 256), kmap(dl, vcol)) for dl in (-1, 0, 1)]
    in_specs += [pl.BlockSpec((n_ctx, 256), cmap(vcol))]
    return pl.pallas_call(
        functools.partial(_wina_kernel, nb=nb, seq=seq),
        grid_spec=pltpu.PrefetchScalarGridSpec(
            num_scalar_prefetch=1, grid=(n_batch, steps), in_specs=in_specs,
            out_specs=pl.BlockSpec((blk, 512), qmap)),
        out_shape=jax.ShapeDtypeStruct((rows_out, 512), BF16),
        compiler_params=_cp(("arbitrary", "arbitrary"), VMEM_LIMIT), name="mixer_window",
    )(sink, prep, prep, prep, prep, prep, proj, proj, proj, proj)


def _mla_kernel(safe_ref, q_ref, kl_ref, vl_ref, kc_ref, vc_ref, o_ref, *, nq, tk_fast, tk_exact, hps):
    tq = q_ref.shape[0]
    seq = kl_ref.shape[0]
    is_lat = pl.program_id(2) < nq
    safe = safe_ref[0] > 0
    pair_w = 2 * LANES
    lane = lax.broadcasted_iota(jnp.int32, (tq, LANES), 1)

    def q(h):
        return q_ref[:, h * LANES:(h + 1) * LANES]

    def write(nums, dens):
        for g in range(hps // 2):
            o_ref[:, g * LANES:(g + 1) * LANES] = jnp.where(
                lane < B_V, nums[2 * g] / dens[2 * g], nums[2 * g + 1] / dens[2 * g + 1]).astype(o_ref.dtype)

    def fast(chunks):
        accs = [None] * hps
        for k_ref, v_ref, off, n in chunks:
            for h in range(hps):
                g = h // 2
                p = jnp.exp2(_nt(q(h), k_ref[off:off + n, h * LANES:(h + 1) * LANES])).astype(BF16)
                d = jnp.dot(p, v_ref[off:off + n, g * pair_w:(g + 1) * pair_w], preferred_element_type=F32)
                accs[h] = d if accs[h] is None else accs[h] + d
        write([a[:, :LANES] for a in accs], [a[:, LANES:LANES + 1] for a in accs])

    ctx_chunk = (kc_ref, vc_ref, 0, kc_ref.shape[0])

    @pl.when(safe & is_lat)
    def _():
        fast([(kl_ref, vl_ref, c * tk_fast, tk_fast) for c in range(seq // tk_fast)] + [ctx_chunk])

    @pl.when(safe & jnp.logical_not(is_lat))
    def _():
        fast([ctx_chunk])

    @pl.when(jnp.logical_not(safe))
    def _():
        def step(kf, vf, carry):
            out = []
            for h in range(hps):
                m, l, acc = carry[h]
                s = _nt(q(h), kf(h))
                m_new = jnp.maximum(m, jnp.max(s, axis=-1, keepdims=True))
                a = jnp.exp2(m - m_new)
                p = jnp.exp2(s - m_new)
                l = a * l + jnp.sum(p, axis=-1, keepdims=True)
                acc = a * acc + jnp.dot(p.astype(BF16), vf(h // 2), preferred_element_type=F32)
                out.append((m_new, l, acc))
            return tuple(out)

        def body(c, carry):
            off = pl.multiple_of(c * tk_exact, tk_exact)
            return step(lambda h: kl_ref[pl.ds(off, tk_exact), h * LANES:(h + 1) * LANES],
                        lambda g: vl_ref[pl.ds(off, tk_exact), g * pair_w:g * pair_w + LANES], carry)

        init = tuple((jnp.full((tq, 1), -jnp.inf, F32), jnp.zeros((tq, 1), F32), jnp.zeros((tq, LANES), F32))
                     for _ in range(hps))
        carry = lax.fori_loop(0, jnp.where(is_lat, seq // tk_exact, 0), body, init)
        carry = step(lambda h: kc_ref[:, h * LANES:(h + 1) * LANES],
                     lambda g: vc_ref[:, g * pair_w:g * pair_w + LANES], carry)
        write([c[2] for c in carry], [c[1] for c in carry])


def _mla_logit_bound(q_gain, k_gain):
    b = B_QK * MLA_LOGIT_SCALE * jnp.max(jnp.abs(q_gain)) * jnp.max(jnp.abs(k_gain))
    return (1.02 * b + 0.5).astype(F32)


def _mla(qb, kb, vb, safe, *, n_batch, seq, n_ctx, with_ctx):
    hps = 4
    tq = 256
    assert n_ctx == tq
    nq = seq // tq
    ctx_blk = n_batch * seq // n_ctx
    rows_out = n_batch * seq + (n_batch * n_ctx if with_ctx else 0)
    kw, vw = hps * LANES, hps * B_V

    def qmap(b, j, i, s):
        return (jnp.where(i < nq, b * nq + i, ctx_blk + b), j)

    return pl.pallas_call(
        functools.partial(_mla_kernel, nq=nq, tk_fast=min(2048, seq), tk_exact=512, hps=hps),
        grid_spec=pltpu.PrefetchScalarGridSpec(
            num_scalar_prefetch=1,
            grid=(n_batch, B_HEADS // hps, nq + (1 if with_ctx else 0)),
            in_specs=[pl.BlockSpec((tq, kw), qmap),
                      pl.BlockSpec((seq, kw), lambda b, j, i, s: (b, j)),
                      pl.BlockSpec((seq, kw), lambda b, j, i, s: (b, j)),
                      pl.BlockSpec((n_ctx, kw), lambda b, j, i, s: (ctx_blk + b, j)),
                      pl.BlockSpec((n_ctx, kw), lambda b, j, i, s: (ctx_blk + b, j))],
            out_specs=pl.BlockSpec((tq, vw), qmap)),
        out_shape=jax.ShapeDtypeStruct((rows_out, 512), BF16),
        compiler_params=_cp(("arbitrary", "arbitrary", "arbitrary"), VMEM_LIMIT), name="mixer_mla",
    )(safe, qb, kb, vb, kb, vb)


def _ret_kernel(qf_ref, kf_ref, vf_ref, qr_ref, kr_ref, vr_ref, ld_ref, of_ref, or_ref,
                s_ref, dec_ref, qw_ref, kw_ref, cd_ref):
    step = pl.program_id(1)
    cc = qf_ref.shape[0]
    low = lax.broadcasted_iota(jnp.int32, (cc, LANES), 1) < HEAD_DIM
    sr = lax.broadcasted_iota(jnp.int32, (LANES, LANES), 0)
    blockdiag = (sr < HEAD_DIM) == (lax.broadcasted_iota(jnp.int32, (LANES, LANES), 1) < HEAD_DIM)

    @pl.when(step == 0)
    def _():
        s_ref[...] = jnp.zeros_like(s_ref)
        qi = lax.broadcasted_iota(jnp.int32, (cc, cc), 0)
        ki = lax.broadcasted_iota(jnp.int32, (cc, cc), 1)
        pos = lax.broadcasted_iota(jnp.int32, (cc, LANES), 0).astype(F32)
        for d in range(2):
            diff = (qi - ki) if d == 0 else (ki - qi)
            dpos = jnp.maximum(diff, 0).astype(F32)
            qpow = (pos + 1.0) if d == 0 else (cc - pos)
            kpow = (cc - 1.0 - pos) if d == 0 else pos
            for j in range(C_HEADS // 2):
                lg_e = -jnp.exp(ld_ref[d, 2 * j:2 * j + 1, :])
                lg_o = -jnp.exp(ld_ref[d, 2 * j + 1:2 * j + 2, :])
                lgl = jnp.where(low[0:1, :], lg_e, lg_o)
                wide_e = jnp.concatenate([lg_e] * (cc // LANES), axis=1)
                wide_o = jnp.concatenate([lg_o] * (cc // LANES), axis=1)
                dec_ref[d, j, :cc] = jnp.where(diff >= 0, jnp.exp(wide_e * dpos), 0.0)
                dec_ref[d, j, cc:] = jnp.where(diff >= 0, jnp.exp(wide_o * dpos), 0.0)
                qw_ref[d, j] = jnp.exp(lgl * qpow)
                kw_ref[d, j] = jnp.exp(lgl * kpow)
                cd_ref[d, j] = jnp.where(sr < HEAD_DIM, jnp.exp(lg_e * cc), jnp.exp(lg_o * cc))

    pairs = C_HEADS // 2
    dirs = ((qf_ref, kf_ref, vf_ref, of_ref), (qr_ref, kr_ref, vr_ref, or_ref))
    states = [[s_ref[d, j] for j in range(pairs)] for d in range(2)]
    outs, new_states = [], []
    for d, (q_ref, k_ref, v_ref, _) in enumerate(dirs):
        for j in range(pairs):
            sl = slice(j * LANES, (j + 1) * LANES)
            q, k, v = q_ref[:, sl], k_ref[:, sl], v_ref[:, sl]
            zero = jnp.zeros_like(q)
            q2 = jnp.concatenate([jnp.where(low, q, zero), jnp.where(low, zero, q)], axis=0)
            sc = (_nt(q2, k) * dec_ref[d, j]).astype(BF16)
            o2 = jnp.dot(sc, v, preferred_element_type=F32)
            o_intra = jnp.where(low, o2[:cc], o2[cc:])
            state = states[d][j]
            qw = (q.astype(F32) * qw_ref[d, j]).astype(BF16)
            o_cross = jnp.dot(qw, state.astype(BF16), preferred_element_type=F32)
            kw = (k.astype(F32) * kw_ref[d, j]).astype(BF16)
            kv = lax.dot_general(kw, v, (((0,), (0,)), ((), ())), preferred_element_type=F32)
            new_states.append(state * cd_ref[d, j] + jnp.where(blockdiag, kv, 0.0))
            outs.append(o_intra + o_cross)
    for d in range(2):
        for j in range(pairs):
            s_ref[d, j] = new_states[d * pairs + j]
            dirs[d][3][:, j * LANES:(j + 1) * LANES] = outs[d * pairs + j].astype(BF16)


def _retention(prep, proj, ld_head, *, n_batch, seq, n_ctx):
    cc = RET_CHUNK
    nl, nc = seq // cc, n_ctx // cc
    steps = nc + nl
    lat_blocks = n_batch * nl
    rows = n_batch * (seq + n_ctx)

    def rowblk(b, d, s):
        c_ctx = jnp.where(d == 0, s, nc - 1 - s)
        c_lat = jnp.where(d == 0, s - nc, nl - 1 - (s - nc))
        return jnp.where(s < nc, lat_blocks + b * nc + c_ctx, b * nl + c_lat)

    def spec(d, col):
        return pl.BlockSpec((cc, 512), lambda b, s: (rowblk(b, d, s), col))

    pairs = C_HEADS // 2
    return pl.pallas_call(
        _ret_kernel,
        grid=(n_batch, steps),
        in_specs=[spec(d, col) for d in (0, 1) for col in (P_CQ // 512, P_CK // 512, P_CV // 512)]
        + [pl.BlockSpec((2, C_HEADS, LANES), lambda b, s: (0, 0, 0))],
        out_specs=[spec(0, 0), spec(1, 0)],
        out_shape=[jax.ShapeDtypeStruct((rows, 512), BF16)] * 2,
        scratch_shapes=[pltpu.VMEM((2, pairs, LANES, LANES), F32),
                        pltpu.VMEM((2, pairs, 2 * cc, cc), F32),
                        pltpu.VMEM((2, pairs, cc, LANES), F32),
                        pltpu.VMEM((2, pairs, cc, LANES), F32),
                        pltpu.VMEM((2, pairs, LANES, LANES), F32)],
        compiler_params=_cp(("arbitrary", "arbitrary"), VMEM_LIMIT), name="mixer_retention",
    )(prep, prep, proj, prep, prep, proj, ld_head)


def _retfin_kernel(of_ref, or_ref, g_ref, gm_ref, out_ref):
    o = of_ref[...].astype(F32) + or_ref[...].astype(F32)
    on = o * lax.rsqrt(_group_meansq(o, gm_ref, 512) + EPS)
    g = g_ref[...].astype(F32)
    out_ref[...] = (g * jax.nn.sigmoid(g) * on).astype(out_ref.dtype)


def _retention_finish(o_fwd, o_rev, proj, gmat, *, rows):
    tr = 256
    return pl.pallas_call(
        _retfin_kernel,
        grid=(rows // tr,),
        in_specs=[pl.BlockSpec((tr, 512), lambda i: (i, 0)),
                  pl.BlockSpec((tr, 512), lambda i: (i, 0)),
                  pl.BlockSpec((tr, 512), lambda i: (i, P_CG // 512)),
                  pl.BlockSpec((512, 512), lambda i: (0, 0))],
        out_specs=pl.BlockSpec((tr, 512), lambda i: (i, 0)),
        out_shape=jax.ShapeDtypeStruct((rows, 512), BF16),
        compiler_params=_cp(("arbitrary",), VMEM_LIMIT), name="retention_finish",
    )(o_fwd, o_rev, proj, gmat)


def _nbr_kernel(par_ref, q_ref, k_ref, v_ref, kc_ref, vc_ref, abm_ref, o_ref, *, rows):
    st = pl.program_id(1)
    start = jnp.clip(NA_QROWS * st - NA_ROWS // 2, 0, rows - NA_KROWS)
    off = pl.multiple_of(start * GRID_W, GRID_W)
    nk = NA_KROWS * GRID_W
    tq = q_ref.shape[0]
    low = lax.broadcasted_iota(jnp.int32, (tq, LANES), 1) < HEAD_DIM
    bound = par_ref[0]
    safe = par_ref[1] > 0.0
    pair_w = 2 * LANES

    def attend(use_max):
        for j in range(D_HEADS // 2):
            sl = slice(j * LANES, (j + 1) * LANES)
            sl2 = slice(j * pair_w, (j + 1) * pair_w)
            qc = q_ref[:, sl]
            zero = jnp.zeros_like(qc)
            q2 = jnp.concatenate([jnp.where(low, qc, zero), jnp.where(low, zero, qc)], axis=0)
            bias = jnp.concatenate([abm_ref[0, 2 * j], abm_ref[0, 2 * j + 1]], axis=0)
            s1 = _nt(q2, k_ref[pl.ds(off, nk), sl]) + bias
            s2 = _nt(q2, kc_ref[:, sl]) - bound
            if use_max:
                m = jnp.maximum(jnp.max(s1, axis=-1, keepdims=True), jnp.max(s2, axis=-1, keepdims=True))
                s1, s2 = s1 - m, s2 - m
            o = (jnp.dot(jnp.exp2(s1).astype(BF16), v_ref[pl.ds(off, nk), sl2], preferred_element_type=F32)
                 + jnp.dot(jnp.exp2(s2).astype(BF16), vc_ref[:, sl2], preferred_element_type=F32))
            res = o[:, :LANES] / o[:, LANES:LANES + 1]
            o_ref[:, sl] = jnp.where(low, res[:tq], res[tq:]).astype(o_ref.dtype)

    @pl.when(safe)
    def _():
        attend(False)

    @pl.when(jnp.logical_not(safe))
    def _():
        attend(True)


def _nbr(prep, dv2, abm, par, *, n_batch, seq, n_ctx, with_ctx):
    rows = seq // GRID_W
    tq = NA_QROWS * GRID_W
    assert n_ctx == tq and rows % NA_QROWS == 0 and rows >= NA_KROWS + NA_QROWS
    nst = rows // NA_QROWS
    steps = nst + (1 if with_ctx else 0)
    rows_out = n_batch * seq + (n_batch * n_ctx if with_ctx else 0)

    def qmap(col):
        def f(b, s, par):
            return (jnp.where(s < nst, b * nst + s, n_batch * nst + b), col)
        return f

    def case(b, s, par):
        c = jnp.where(s == 0, 0, jnp.where(s == nst - 1, 2, jnp.where(s == nst, 3, 1)))
        return (c, 0, 0, 0)

    ctx_blk = n_batch * seq // n_ctx
    return pl.pallas_call(
        functools.partial(_nbr_kernel, rows=rows),
        grid_spec=pltpu.PrefetchScalarGridSpec(
            num_scalar_prefetch=1, grid=(n_batch, steps),
            in_specs=[pl.BlockSpec((tq, 512), qmap(P_DQ // 512)),
                      pl.BlockSpec((seq, 512), lambda b, s, par: (b, P_DK // 512)),
                      pl.BlockSpec((seq, 2 * 512), lambda b, s, par: (b, 0)),
                      pl.BlockSpec((n_ctx, 512), lambda b, s, par: (ctx_blk + b, P_DK // 512)),
                      pl.BlockSpec((n_ctx, 2 * 512), lambda b, s, par: (ctx_blk + b, 0)),
                      pl.BlockSpec((1, D_HEADS, tq, NA_KROWS * GRID_W), case)],
            out_specs=pl.BlockSpec((tq, 512), qmap(0))),
        out_shape=jax.ShapeDtypeStruct((rows_out, 512), BF16),
        compiler_params=_cp(("arbitrary", "arbitrary"), VMEM_LIMIT), name="mixer_neighbourhood",
    )(par, prep, prep, dv2, prep, dv2, abm)


def _softmax_bound(q_gain, k_gain, extra):
    b = HEAD_DIM ** 0.5 * LOG2E * jnp.max(jnp.abs(q_gain)) * jnp.max(jnp.abs(k_gain)) + LOG2E * extra
    return (1.02 * b + 0.5).astype(F32)


def _nbr_bias_tables(rpb, rows, bound):
    w = GRID_W
    cidx = np.arange(w)
    col_start = np.clip(cidx - NA_COLS // 2, 0, w - NA_COLS)
    col_ok = (cidx[None, :] >= col_start[:, None]) & (cidx[None, :] < col_start[:, None] + NA_COLS)
    d_col = np.clip(cidx[None, :] - cidx[:, None] + (NA_COLS - 1), 0, 2 * NA_COLS - 2)
    n_heads, n_dr, n_dc = rpb.shape
    onehot = jnp.asarray((d_col[:, :, None] == np.arange(n_dc)[None, None, :]).astype(np.float32))
    t = jnp.einsum("hrd,qkd->hqrk", rpb.astype(F32), onehot, precision=lax.Precision.HIGHEST)
    t = t * LOG2E - bound
    t = jnp.where(jnp.asarray(col_ok)[None, :, None, :], t, NEG_INF).reshape(n_heads, w, n_dr * w)
    cases = []
    for r0 in (0, NA_QROWS, rows - NA_QROWS):
        start = int(np.clip(r0 - NA_ROWS // 2, 0, rows - NA_KROWS))
        blocks = []
        for i in range(NA_QROWS):
            r = r0 + i
            row_start = int(np.clip(r - NA_ROWS // 2, 0, rows - NA_ROWS))
            kk0 = row_start - start
            dr0 = row_start - r + (NA_ROWS - 1)
            blk = t[:, :, dr0 * w:(dr0 + NA_ROWS) * w]
            blocks.append(jnp.pad(blk, ((0, 0), (0, 0), (kk0 * w, (NA_KROWS - NA_ROWS - kk0) * w)),
                                  constant_values=NEG_INF))
        cases.append(jnp.concatenate(blocks, axis=1))
    cases.append(jnp.full((n_heads, NA_QROWS * w, NA_KROWS * w), NEG_INF, F32))
    return jnp.stack(cases)


def _swiglu_accumulate(x_ref, wg_ref, wu_ref, wd_ref, acc_ref, n_rows):
    x = x_ref[:n_rows]
    g = jnp.dot(x, wg_ref[0].astype(BF16), preferred_element_type=F32)
    u = jnp.dot(x, wu_ref[0].astype(BF16), preferred_element_type=F32)
    a = (g * jax.nn.sigmoid(g) * u).astype(BF16)
    acc_ref[:n_rows] += jnp.dot(a, wd_ref[0].astype(BF16), preferred_element_type=F32)


def _ffn_kernel(x_ref, wg_ref, wu_ref, wd_ref, o_ref, acc_ref):
    j = pl.program_id(1)

    @pl.when(j == 0)
    def _():
        acc_ref[...] = jnp.zeros_like(acc_ref)

    _swiglu_accumulate(x_ref, wg_ref, wu_ref, wd_ref, acc_ref, x_ref.shape[0])

    @pl.when(j == pl.num_programs(1) - 1)
    def _():
        o_ref[...] = acc_ref[...].astype(o_ref.dtype)


def _ffn(x, w_gate_up, w_down, layer, *, rows, tm, fc):
    _, d, ff2 = w_gate_up.shape
    nj = ff2 // 2 // fc
    return pl.pallas_call(
        _ffn_kernel,
        grid=(rows // tm, nj),
        in_specs=[pl.BlockSpec((tm, d), lambda i, j: (i, 0)),
                  pl.BlockSpec((1, d, fc), lambda i, j: (layer, 0, j)),
                  pl.BlockSpec((1, d, fc), lambda i, j: (layer, 0, nj + j)),
                  pl.BlockSpec((1, fc, d), lambda i, j: (layer, j, 0))],
        out_specs=pl.BlockSpec((tm, d), lambda i, j: (i, 0)),
        out_shape=jax.ShapeDtypeStruct((rows, d), BF16),
        scratch_shapes=[pltpu.VMEM((tm, d), F32)],
        compiler_params=_cp(("arbitrary", "arbitrary"), VMEM_LIMIT), name="swiglu",
    )(x, w_gate_up, w_gate_up, w_down)


def _moe_ffn_kernel(te_ref, tv_ref, tok_ref, hp_ref, wg_ref, wu_ref, wd_ref, o_ref, gbuf_ref, x_ref, acc_ref, sem,
                    *, rows_per_step):
    i, j = pl.program_id(0), pl.program_id(1)
    n_tiles = pl.num_programs(0)
    tm, d = x_ref.shape
    half = d // 2
    valid = tv_ref[i] > 0
    slot = i % 2

    def issue(tile, first_row, n_rows, to_slot):
        def body(r, carry):
            row = first_row + r
            pltpu.make_async_copy(hp_ref.at[pl.ds(tok_ref[tile * tm + row], 1), :],
                                  gbuf_ref.at[to_slot].at[pl.ds(row, 1), :], sem.at[to_slot]).start()
            return carry

        lax.fori_loop(0, n_rows, body, 0, unroll=8)

    @pl.when((i == 0) & (j == 0) & valid)
    def _():
        issue(0, 0, tm, 0)

    @pl.when((j == 0) & valid)
    def _():
        pltpu.make_async_copy(hp_ref.at[pl.ds(0, tm), :], gbuf_ref.at[slot], sem.at[slot]).wait()
        lo, hi = _unpack_halves(gbuf_ref[slot])
        x_ref[:, :half] = lo.astype(BF16)
        x_ref[:, half:] = hi.astype(BF16)
        acc_ref[...] = jnp.zeros_like(acc_ref)

    nxt = jnp.minimum(i + 1, n_tiles - 1)

    @pl.when((i + 1 < n_tiles) & (tv_ref[nxt] > 0) & (j * rows_per_step < tm))
    def _():
        issue(nxt, j * rows_per_step, rows_per_step, 1 - slot)

    for q in range(1, MOE_ROW_SPLITS + 1):
        n_rows = q * (tm // MOE_ROW_SPLITS)

        @pl.when(tv_ref[i] == n_rows)
        def _():
            _swiglu_accumulate(x_ref, wg_ref, wu_ref, wd_ref, acc_ref, n_rows)

    @pl.when(j == pl.num_programs(1) - 1)
    def _():
        o_ref[...] = jnp.where(valid, _pack_halves(acc_ref[...]), jnp.uint32(0))


def _moe_ffn(hp, w_gate_up, w_down, layer, tile_expert, tile_valid, row_token, *, tm, fc):
    _, _, d, ff2 = w_gate_up.shape
    nj = ff2 // 2 // fc
    n_rows = row_token.shape[0]
    issue_steps = 1 << (nj.bit_length() - 1)

    def jeff(i, j, tv):
        return jnp.where(tv[i] > 0, j, nj - 1)

    return pl.pallas_call(
        functools.partial(_moe_ffn_kernel, rows_per_step=tm // issue_steps),
        grid_spec=pltpu.PrefetchScalarGridSpec(
            num_scalar_prefetch=3, grid=(n_rows // tm, nj),
            in_specs=[pl.BlockSpec(memory_space=pl.ANY),
                      pl.BlockSpec((None, 1, d, fc), lambda i, j, te, tv, tok: (layer, te[i], 0, jeff(i, j, tv))),
                      pl.BlockSpec((None, 1, d, fc),
                                   lambda i, j, te, tv, tok: (layer, te[i], 0, nj + jeff(i, j, tv))),
                      pl.BlockSpec((None, 1, fc, d), lambda i, j, te, tv, tok: (layer, te[i], jeff(i, j, tv), 0))],
            out_specs=pl.BlockSpec((tm, d // 2), lambda i, j, te, tv, tok: (i, 0)),
            scratch_shapes=[pltpu.VMEM((2, tm, d // 2), jnp.uint32), pltpu.VMEM((tm, d), BF16),
                            pltpu.VMEM((tm, d), F32), pltpu.SemaphoreType.DMA((2,))]),
        out_shape=jax.ShapeDtypeStruct((n_rows, d // 2), jnp.uint32),
        compiler_params=_cp(("arbitrary", "arbitrary"), VMEM_LIMIT_MOE), name="moe_swiglu",
    )(tile_expert, tile_valid, row_token, hp, w_gate_up, w_gate_up, w_down)


def _combine_kernel(p0_ref, p1_ref, y_ref, x_ref, route_ref, gate_ref, o_ref, b0_ref, b1_ref, sem):
    _, n, half = b0_ref.shape
    i = pl.program_id(0)
    slot = i % 2

    def issue(tile, to_slot):
        def body(r, carry):
            k = tile * n + r
            pltpu.make_async_copy(y_ref.at[pl.ds(p0_ref[k], 1), :], b0_ref.at[to_slot].at[pl.ds(r, 1), :],
                                  sem.at[0, to_slot]).start()
            pltpu.make_async_copy(y_ref.at[pl.ds(p1_ref[k], 1), :], b1_ref.at[to_slot].at[pl.ds(r, 1), :],
                                  sem.at[1, to_slot]).start()
            return carry

        lax.fori_loop(0, n, body, 0, unroll=8)

    @pl.when(i == 0)
    def _():
        issue(0, 0)

    pltpu.make_async_copy(y_ref.at[pl.ds(0, n), :], b0_ref.at[slot], sem.at[0, slot]).wait()
    pltpu.make_async_copy(y_ref.at[pl.ds(0, n), :], b1_ref.at[slot], sem.at[1, slot]).wait()

    @pl.when(i + 1 < pl.num_programs(0))
    def _():
        issue(i + 1, 1 - slot)

    route = route_ref[...]
    w0, w1 = route[:, 2:3], route[:, 3:4]
    lo0, hi0 = _unpack_halves(b0_ref[slot])
    lo1, hi1 = _unpack_halves(b1_ref[slot])
    gate = gate_ref[0, 0]
    o_ref[:, :half] = x_ref[:, :half] + gate[:, :half] * (w0 * lo0 + w1 * lo1)
    o_ref[:, half:] = x_ref[:, half:] + gate[:, half:] * (w0 * hi0 + w1 * hi1)


def _combine(y, x, route, mod, gate_k, pos0, pos1, *, seq, n_batch):
    t, d = x.shape
    tr = 256
    return pl.pallas_call(
        _combine_kernel,
        grid_spec=pltpu.PrefetchScalarGridSpec(
            num_scalar_prefetch=2, grid=(t // tr,),
            in_specs=[pl.BlockSpec(memory_space=pl.ANY),
                      pl.BlockSpec((tr, d), lambda i, a, b: (i, 0)),
                      pl.BlockSpec((tr, LANES), lambda i, a, b: (i, 0)),
                      pl.BlockSpec((1, 1, 1, d),
                                   lambda i, a, b: (jnp.minimum((i * tr) // seq, n_batch), gate_k, 0, 0))],
            out_specs=pl.BlockSpec((tr, d), lambda i, a, b: (i, 0)),
            scratch_shapes=[pltpu.VMEM((2, tr, d // 2), jnp.uint32), pltpu.VMEM((2, tr, d // 2), jnp.uint32),
                            pltpu.SemaphoreType.DMA((2, 2))]),
        out_shape=jax.ShapeDtypeStruct((t, d), F32),
        compiler_params=_cp(("arbitrary",), VMEM_LIMIT), name="moe_combine",
    )(pos0, pos1, y, x, route, _mod_view(mod))


def _route_meta(idx, n_exp, tm):
    t = idx.shape[0]
    flat = idx.reshape(-1)
    onehot = (flat[:, None] == jnp.arange(n_exp, dtype=jnp.int32)[None, :]).astype(jnp.int32)
    counts = jnp.sum(onehot, axis=0)
    rank = jnp.take_along_axis(jnp.cumsum(onehot, axis=0) - onehot, flat[:, None], axis=1)[:, 0]
    padded = ((counts + tm - 1) // tm) * tm
    ends = jnp.cumsum(padded)
    starts = ends - padded
    pos = starts[flat] + rank
    n_rows = TOP_K * t + n_exp * tm
    row_token = jnp.zeros((n_rows,), jnp.int32).at[pos].set(jnp.arange(TOP_K * t, dtype=jnp.int32) // TOP_K)
    tile_start = jnp.arange(n_rows // tm, dtype=jnp.int32) * tm
    tile_valid = tile_start < ends[-1]
    tile_expert = jnp.sum((ends[None, :] <= tile_start[:, None]).astype(jnp.int32), axis=1)
    tile_expert = jnp.minimum(tile_expert, n_exp - 1)
    last_valid = jnp.maximum(ends[-1] // tm - 1, 0)
    tile_expert = jnp.where(tile_valid, tile_expert, tile_expert[last_valid])
    sub = tm // MOE_ROW_SPLITS
    used = jnp.clip(counts[tile_expert] - (tile_start - starts[tile_expert]), 0, tm)
    tile_rows = jnp.where(tile_valid, ((used + sub - 1) // sub) * sub, 0).astype(jnp.int32)
    pos = pos.reshape(t, TOP_K).astype(jnp.int32)
    return row_token, tile_expert, tile_rows, pos[:, 0], pos[:, 1]


def _rope_tables(seq, pad_rows):
    t = jnp.arange(seq, dtype=jnp.int32)
    row = (t // GRID_W).astype(F32)
    col = (t % GRID_W).astype(F32)

    def angles(rot_dim):
        n_freq = rot_dim // 4
        freqs = ROPE_THETA ** (-jnp.arange(n_freq, dtype=F32) / n_freq)
        ang = jnp.concatenate([row[:, None] * freqs[None, :], col[:, None] * freqs[None, :]], axis=-1)
        return jnp.cos(ang), jnp.sin(ang)

    ch, sh = angles(HEAD_DIM)
    cos_h = jnp.concatenate([ch] * 4, axis=1)
    sin_h = jnp.concatenate([-sh, sh, -sh, sh], axis=1)
    cb, sb = angles(B_ROPE)
    one = jnp.ones((seq, B_NOPE), F32)
    zero = jnp.zeros((seq, B_NOPE), F32)
    tail1 = jnp.ones((seq, LANES - B_QK), F32)
    tail0 = jnp.zeros((seq, LANES - B_QK), F32)
    cos_b = jnp.concatenate([one, cb, cb, tail1], axis=1)
    sin_b = jnp.concatenate([zero, -sb, sb, tail0], axis=1)

    def pad(tab, fill):
        return jnp.concatenate([tab, jnp.full((pad_rows, LANES), fill, F32)], axis=0)

    return pad(cos_h, 1.0), pad(sin_h, 0.0), pad(cos_b, 1.0), pad(sin_b, 0.0)


def _permute_w_in(w):
    d = w.shape[0]
    o = np.cumsum([0, 512, 128, 128, B_Q_RANK, B_KV_RANK, B_ROPE, 512, 512, 512, 512, 512, 512, 512])
    aq, ak, av, bq, bkv, bkr, cq, ck, cv, cg, dq, dk, dv = [w[:, o[i]:o[i + 1]] for i in range(13)]

    def dup(m):
        return jnp.concatenate([m[:, :64], m[:, :64], m[:, 64:], m[:, 64:]], axis=1)

    z = lambda n: jnp.zeros((d, n), w.dtype)
    half = B_ROPE // 2
    bkr_swapped = jnp.concatenate([bkr[:, half:], bkr[:, :half]], axis=1)
    parts = [aq, cq, ck, dq, dk, dup(ak), dup(av), cv, cg, dv, bq, bkv,
             z(B_NOPE), bkr, z(LANES - B_QK), z(B_NOPE), bkr_swapped, z(LANES - B_QK)]
    return jnp.concatenate(parts, axis=1).astype(BF16)


def _block_diag_ones():
    i = np.arange(512) // HEAD_DIM
    return jnp.asarray((i[:, None] == i[None, :]).astype(np.float32), dtype=BF16)


def _rope_partner(m):
    half = B_ROPE // 2
    return jnp.concatenate([m[..., :B_NOPE], m[..., B_NOPE + half:], m[..., B_NOPE:B_NOPE + half]], axis=-1)


def _pad_lanes(v, width):
    return jnp.concatenate([v.astype(F32), jnp.zeros((width - v.shape[0],), F32)])


def kernel(x, c, ctx, c_ctx, w_ada, b_ada, norm_mix, norm_ffn, w_in, w_out, a_q_norm, a_k_norm, a_sink,
           b_q_a_norm, b_kv_a_norm, b_w_uq, b_w_ukv, b_q_norm, b_k_norm, c_log_decay, d_q_norm, d_k_norm,
           d_rpb, ffn_w_gate_up, ffn_w_down, moe_router, moe_w_gate_up, moe_w_down):
    n_batch, seq, d = x.shape
    n_ctx = ctx.shape[1]
    depth = w_ada.shape[0]
    n_lat = n_batch * seq
    n_all = n_lat + n_batch * n_ctx
    grid_rows = seq // GRID_W
    tm = next(t for t in (1024, 512, 256) if n_lat % t == 0 and n_all % t == 0)

    cc = jnp.concatenate([c, c_ctx[None, :], jnp.zeros((8 - n_batch - 1, d), F32)], axis=0)
    mod = _ada(cc, w_ada, b_ada)
    xs = (x.reshape(n_lat, d), ctx.reshape(n_batch * n_ctx, d))
    cos_h, sin_h, cos_b, sin_b = _rope_tables(seq, 256)
    gmat = _block_diag_ones()

    y = None
    y_gate, y_mod = 0, None
    for l in range(depth):
        last = l == depth - 1
        mod_l = mod[l]
        if y is None:
            (h,) = _resnorm(xs, None, None, 0, mod_l, norm_mix[l], 0, 1, rows=n_all, seq=seq, n_batch=n_batch)
        else:
            xs, h = _resnorm(xs, y, y_mod, y_gate, mod_l, norm_mix[l], 0, 1, rows=n_all, seq=seq,
                             n_batch=n_batch)
        proj = _mm([h], _permute_w_in(w_in[l]), rows=n_all, tm=tm, tn=768, out_dtype=BF16, name="proj_in")
        gains = jnp.stack([jnp.tile(a_q_norm[l], 8), _pad_lanes(jnp.tile(a_k_norm[l], 4), 512),
                           jnp.tile(d_q_norm[l], 8), jnp.tile(d_k_norm[l], 8)]).astype(F32)
        prep, dv2 = _prep(proj, cos_h, sin_h, gmat, gains, rows=n_all, seq=seq, n_lat=n_lat)
        wuq = jnp.pad(b_w_uq[l].reshape(B_Q_RANK, B_HEADS, B_QK),
                      ((0, 0), (0, 0), (0, LANES - B_QK))).reshape(B_Q_RANK, B_HEADS * LANES).astype(BF16)
        wukv3 = b_w_ukv[l].reshape(B_KV_RANK, B_HEADS, B_NOPE + B_V)
        wukv = jnp.concatenate(
            [jnp.pad(wukv3[:, :, :B_NOPE], ((0, 0), (0, 0), (0, LANES - B_NOPE))).reshape(B_KV_RANK, -1),
             wukv3[:, :, B_NOPE:].reshape(B_KV_RANK, -1)], axis=1).astype(BF16)
        bound = _mla_logit_bound(b_q_norm[l], b_k_norm[l])
        mla_safe = (bound <= SOFTMAX_SAFE_BOUND).astype(jnp.int32).reshape(1)
        gains_b = jnp.stack([b_q_a_norm[l].astype(F32), _pad_lanes(b_kv_a_norm[l], B_Q_RANK),
                             _pad_lanes(b_q_norm[l], B_Q_RANK), _pad_lanes(b_k_norm[l], B_Q_RANK),
                             jnp.full((B_Q_RANK,), -1.0, F32) * bound,
                             _pad_lanes(_rope_partner(b_q_norm[l]), B_Q_RANK),
                             _pad_lanes(_rope_partner(b_k_norm[l]), B_Q_RANK), jnp.zeros((B_Q_RANK,), F32)])
        wuq3 = b_w_uq[l].reshape(B_Q_RANK, B_HEADS, B_QK)
        wuq_sw = jnp.pad(_rope_partner(wuq3), ((0, 0), (0, 0), (0, LANES - B_QK))).reshape(wuq.shape).astype(BF16)
        qb, kb, vb = _prepb(proj, cos_b, sin_b, wuq, wuq_sw, wukv, gains_b, rows=n_all, seq=seq, n_lat=n_lat)

        sink = a_sink[l].astype(F32)
        bound_a = _softmax_bound(a_q_norm[l], a_k_norm[l], jnp.maximum(jnp.max(sink), 0.0))
        par_a = jnp.concatenate([sink * LOG2E, bound_a[None], (bound_a <= SOFTMAX_SAFE_BOUND).astype(F32)[None]])
        oa = _wina(prep, proj, par_a, n_batch=n_batch, seq=seq, n_ctx=n_ctx, with_ctx=not last)
        ob = _mla(qb, kb, vb, mla_safe, n_batch=n_batch, seq=seq, n_ctx=n_ctx, with_ctx=not last)
        ld_head = jnp.broadcast_to(c_log_decay[l].astype(F32)[:, :, None], (2, C_HEADS, LANES))
        oc_fwd, oc_rev = _retention(prep, proj, ld_head, n_batch=n_batch, seq=seq, n_ctx=n_ctx)
        bound_d = _softmax_bound(d_q_norm[l], d_k_norm[l], jnp.max(jnp.abs(d_rpb[l])))
        par_d = jnp.stack([bound_d, (bound_d <= SOFTMAX_SAFE_BOUND).astype(F32)])
        abm = _nbr_bias_tables(d_rpb[l], grid_rows, bound_d)
        od = _nbr(prep, dv2, abm, par_d, n_batch=n_batch, seq=seq, n_ctx=n_ctx, with_ctx=not last)
        rows_l = n_lat if last else n_all
        oc = _retention_finish(oc_fwd, oc_rev, proj, gmat, rows=rows_l)
        ymix = _mm([oa, ob, oc, od], w_out[l].astype(BF16), rows=rows_l, tm=tm, tn=1024, out_dtype=BF16,
                   name="proj_out")
        i = l // 2
        if l % 2 == 0:
            xs, h2 = _resnorm(xs, ymix, mod_l, 2, mod_l, norm_ffn[l], 3, 4, rows=rows_l, seq=seq,
                              n_batch=n_batch)
            y = _ffn(h2, ffn_w_gate_up, ffn_w_down, i, rows=rows_l, tm=tm, fc=512)
            y_gate, y_mod = 5, mod_l
        else:
            xs, hp, route = _resnorm(xs, ymix, mod_l, 2, mod_l, norm_ffn[l], 3, 4, rows=rows_l, seq=seq,
                                     n_batch=n_batch, router=moe_router[i])
            top_idx = route[:, :TOP_K].astype(jnp.int32)
            row_token, tile_expert, tile_valid, pos0, pos1 = _route_meta(top_idx, moe_router.shape[2], tm)
            yg = _moe_ffn(hp, moe_w_gate_up, moe_w_down, i, tile_expert, tile_valid, row_token, tm=tm, fc=512)
            xs = _combine(yg, xs, route, mod_l, 5, pos0, pos1, seq=seq, n_batch=n_batch)
            y = None
    if y is not None:
        xs, _ = _resnorm(xs, y, y_mod, y_gate, y_mod, norm_ffn[depth - 1], 3, 4, rows=xs.shape[0], seq=seq,
                         n_batch=n_batch)
    return xs[:n_lat].reshape(n_batch, seq, d)
```

```python
import functools

import numpy as np
import jax
import jax.numpy as jnp
from jax import lax
from jax.experimental import pallas as pl
from jax.experimental.pallas import tpu as pltpu

F32 = jnp.float32
BF16 = jnp.bfloat16

GRID_W = 64
HEAD_DIM = 64
ROPE_THETA = 10000.0
EPS = 1e-6
NEG_INF = -1e30

A_HEADS = 8
A_KV_HEADS = 2
A_BLOCK = 128
B_HEADS = 8
B_Q_RANK = 384
B_KV_RANK = 128
B_NOPE = 64
B_ROPE = 32
B_QK = B_NOPE + B_ROPE
B_V = 64
C_HEADS = 8
D_HEADS = 8
NA_ROWS = 8
NA_COLS = 16
TOP_K = 2

LANES = 128
NA_QROWS = 4
NA_KROWS = NA_ROWS + NA_QROWS
NA_CASES = 4
MOE_ROW_SPLITS = 8
PREP_ROWS = 512
RET_CHUNK = 256

P_AQ, P_CQ, P_CK, P_DQ, P_DK, P_AK, P_AV, P_CV, P_CG, P_DV, P_BQ, P_BKV, P_BKR, P_BKR_SW, P_END = (
    0, 512, 1024, 1536, 2048, 2560, 2816, 3072, 3584, 4096, 4608, 4992, 5120, 5248, 5376)
PREP_W = P_AV

VMEM_LIMIT = 56 * 1024 * 1024
VMEM_LIMIT_MOE = 61 * 1024 * 1024

LOG2E = float(np.log2(np.e))
MLA_LOGIT_SCALE = B_QK ** -0.5 * LOG2E
SOFTMAX_SAFE_BOUND = 40.0


def _cp(sem, vmem=None):
    return pltpu.CompilerParams(dimension_semantics=sem, vmem_limit_bytes=vmem)


def _nt(a, b):
    return lax.dot_general(a, b, (((1,), (1,)), ((), ())), preferred_element_type=F32)


_HI_HALF = 0xFFFF0000


def _pack_halves(x):
    n = x.shape[1] // 2
    bits = pltpu.bitcast(x.astype(BF16).astype(F32), jnp.uint32)
    return (bits[:, :n] >> 16) | (bits[:, n:] & jnp.uint32(_HI_HALF))


def _unpack_halves(w):
    return pltpu.bitcast(w << 16, F32), pltpu.bitcast(w & jnp.uint32(_HI_HALF), F32)


def _ada_kernel(c_ref, w_ref, b_ref, o_ref):
    c = c_ref[...]
    s = (c * jax.nn.sigmoid(c)).astype(BF16)
    o_ref[0] = jnp.dot(s, w_ref[0].astype(BF16), preferred_element_type=F32) + b_ref[0]


def _ada(cc, w_ada, b_ada):
    depth, d, n = w_ada.shape
    tn = 1024
    return pl.pallas_call(
        _ada_kernel,
        grid=(depth, n // tn),
        in_specs=[pl.BlockSpec((8, d), lambda l, j: (0, 0)),
                  pl.BlockSpec((1, d, tn), lambda l, j: (l, 0, j)),
                  pl.BlockSpec((1, 1, tn), lambda l, j: (l, 0, j))],
        out_specs=pl.BlockSpec((1, 8, tn), lambda l, j: (l, 0, j)),
        out_shape=jax.ShapeDtypeStruct((depth, 8, n), F32),
        compiler_params=_cp(("arbitrary", "arbitrary"), VMEM_LIMIT),
        name="ada",
    )(cc, w_ada, b_ada.reshape(depth, 1, n))


def _resnorm_kernel(*refs, has_res, with_router, n_exp, lat_tiles):
    refs = list(refs)
    x_ref = refs.pop(0)
    if lat_tiles is not None:
        xc_ref = refs.pop(0)
        x = jnp.where(pl.program_id(0) < lat_tiles, x_ref[...], xc_ref[...])
    else:
        x = x_ref[...]
    if has_res:
        y_ref = refs.pop(0)
        gate_ref = refs.pop(0)
    gain_ref, sh_ref, sc_ref = refs[:3]
    refs = refs[3:]
    if with_router:
        router_ref = refs.pop(0)
    if has_res:
        xo_ref = refs.pop(0)
    h_ref = refs.pop(0)
    if has_res:
        x = x + gate_ref[0, 0] * y_ref[...].astype(F32)
        xo_ref[...] = x
    ms = jnp.mean(x * x, axis=-1, keepdims=True)
    h = (x * lax.rsqrt(ms + EPS) * gain_ref[...]) * (1.0 + sc_ref[0, 0]) + sh_ref[0, 0]
    if with_router:
        (route_ref,) = refs
        h_ref[...] = _pack_halves(h)
        h_hi = h.astype(BF16)
        h_lo = (h - h_hi.astype(F32)).astype(BF16)
        r_hi, r_lo = router_ref[0], router_ref[1]
        logits = (jnp.dot(h_hi, r_hi, preferred_element_type=F32) + jnp.dot(h_lo, r_hi, preferred_element_type=F32)
                  + jnp.dot(h_hi, r_lo, preferred_element_type=F32))
        lane = lax.broadcasted_iota(jnp.int32, logits.shape, 1)
        logits = jnp.where(lane < n_exp, logits, -jnp.inf)
        m1 = jnp.max(logits, axis=-1, keepdims=True)
        i1 = jnp.min(jnp.where(logits == m1, lane, LANES), axis=-1, keepdims=True)
        rest = jnp.where(lane == i1, -jnp.inf, logits)
        m2 = jnp.max(rest, axis=-1, keepdims=True)
        i2 = jnp.min(jnp.where(rest == m2, lane, LANES), axis=-1, keepdims=True)
        e2 = jnp.exp(m2 - m1)
        g1 = 1.0 / (1.0 + e2)
        g2 = e2 / (1.0 + e2)
        route = jnp.where(lane == 0, i1.astype(F32),
                          jnp.where(lane == 1, i2.astype(F32),
                                    jnp.where(lane == 2, g1, jnp.where(lane == 3, g2, 0.0))))
        route_ref[...] = route
    else:
        h_ref[...] = h.astype(h_ref.dtype)


def _mod_view(mod):
    return mod.reshape(mod.shape[0], 6, 1, mod.shape[1] // 6)


def _resnorm(x, y, gate_mod, gate_k, mod, gain, sh_k, sc_k, *, rows, seq, n_batch, router=None):
    tr = 512
    has_res = y is not None
    with_router = router is not None
    split = isinstance(x, tuple)
    d = x[0].shape[1] if split else x.shape[1]

    def grp(i):
        return jnp.minimum((i * tr) // seq, n_batch)

    def modspec(k):
        return pl.BlockSpec((1, 1, 1, d), lambda i: (grp(i), k, 0, 0))

    row = pl.BlockSpec((tr, d), lambda i: (i, 0))
    lat_tiles = None
    if split:
        lat_tiles = x[0].shape[0] // tr
        in_specs = [pl.BlockSpec((tr, d), lambda i: (jnp.minimum(i, lat_tiles - 1), 0)),
                    pl.BlockSpec((tr, d), lambda i: (jnp.maximum(i - lat_tiles, 0), 0))]
        args = list(x)
    else:
        in_specs = [row]
        args = [x]
    if has_res:
        in_specs += [row, modspec(gate_k)]
        args += [y, _mod_view(gate_mod)]
    in_specs += [pl.BlockSpec((1, d), lambda i: (0, 0)), modspec(sh_k), modspec(sc_k)]
    args += [gain.reshape(1, d).astype(F32), _mod_view(mod), _mod_view(mod)]
    out_shape, out_specs = [], []
    n_exp = 0
    if with_router:
        n_exp = router.shape[1]
        router = jnp.pad(router.astype(F32), ((0, 0), (0, LANES - n_exp)))
        r_hi = router.astype(BF16)
        router = jnp.stack([r_hi, (router - r_hi.astype(F32)).astype(BF16)])
        in_specs.append(pl.BlockSpec(router.shape, lambda i: (0, 0, 0)))
        args.append(router)
    if has_res:
        out_shape.append(jax.ShapeDtypeStruct((rows, d), F32))
        out_specs.append(row)
    if with_router:
        out_shape += [jax.ShapeDtypeStruct((rows, d // 2), jnp.uint32), jax.ShapeDtypeStruct((rows, LANES), F32)]
        out_specs += [pl.BlockSpec((tr, d // 2), lambda i: (i, 0)), pl.BlockSpec((tr, LANES), lambda i: (i, 0))]
    else:
        out_shape.append(jax.ShapeDtypeStruct((rows, d), BF16))
        out_specs.append(row)
    kern = functools.partial(_resnorm_kernel, has_res=has_res, with_router=with_router, n_exp=n_exp,
                             lat_tiles=lat_tiles)
    return pl.pallas_call(
        kern, grid=(rows // tr,), in_specs=in_specs, out_specs=out_specs, out_shape=out_shape,
        compiler_params=_cp(("arbitrary",), VMEM_LIMIT), name="resnorm",
    )(*args)


def _mm_kernel(*refs, n_x):
    w_ref, o_ref = refs[n_x], refs[n_x + 1]
    if n_x == 1:
        x = refs[0][...]
    else:
        x = jnp.concatenate([r[...] for r in refs[:n_x]], axis=1)
    o_ref[...] = jnp.dot(x, w_ref[...], preferred_element_type=F32).astype(o_ref.dtype)


def _mm(xs, w, *, rows, tm, tn, out_dtype, name):
    n = w.shape[1]
    in_specs = [pl.BlockSpec((tm, a.shape[1]), lambda i, j: (i, 0)) for a in xs]
    in_specs.append(pl.BlockSpec((w.shape[0], tn), lambda i, j: (0, j)))
    return pl.pallas_call(
        functools.partial(_mm_kernel, n_x=len(xs)),
        grid=(rows // tm, n // tn),
        in_specs=in_specs,
        out_specs=pl.BlockSpec((tm, tn), lambda i, j: (i, j)),
        out_shape=jax.ShapeDtypeStruct((rows, n), out_dtype),
        compiler_params=_cp(("arbitrary", "arbitrary"), VMEM_LIMIT), name=name,
    )(*xs, w)


def _group_meansq(x, g_ref, width):
    x2 = x * x
    hi = x2.astype(BF16)
    lo = (x2 - hi.astype(F32)).astype(BF16)
    g = g_ref[:width, :width]
    ss = jnp.dot(hi, g, preferred_element_type=F32) + jnp.dot(lo, g, preferred_element_type=F32)
    return ss * (1.0 / HEAD_DIM)


def _prep_kernel(p_ref, dv_ref, ch_ref, sh_ref, g_ref, gains_ref, o_ref, dv2_ref):
    tr = p_ref.shape[0]

    def seg(off, width):
        return p_ref[:, off:off + width].astype(F32)

    def norm(x, row, width):
        return x * lax.rsqrt(_group_meansq(x, g_ref, width) + EPS) * gains_ref[row:row + 1, :width]

    def rope(x, width):
        lane = lax.broadcasted_iota(jnp.int32, (tr, width), 1)
        first = (lane % HEAD_DIM) < (HEAD_DIM // 2)
        cos = jnp.concatenate([ch_ref[...]] * (width // LANES), axis=1)
        sin = jnp.concatenate([sh_ref[...]] * (width // LANES), axis=1)
        swapped = jnp.where(first, pltpu.roll(x, width - HEAD_DIM // 2, 1), pltpu.roll(x, HEAD_DIM // 2, 1))
        return x * cos + swapped * sin

    scale = HEAD_DIM ** -0.5
    scale2 = scale * LOG2E
    o_ref[:, P_AQ:P_AQ + 512] = (rope(norm(seg(P_AQ, 512), 0, 512), 512) * scale2).astype(BF16)
    o_ref[:, P_CQ:P_CQ + 512] = (rope(seg(P_CQ, 512), 512) * scale).astype(BF16)
    o_ref[:, P_CK:P_CK + 512] = rope(seg(P_CK, 512), 512).astype(BF16)
    o_ref[:, P_DQ:P_DQ + 512] = (norm(seg(P_DQ, 512), 2, 512) * scale2).astype(BF16)
    o_ref[:, P_DK:P_DK + 512] = norm(seg(P_DK, 512), 3, 512).astype(BF16)
    o_ref[:, P_AK:P_AK + 256] = rope(norm(seg(P_AK, 256), 1, 256), 256).astype(BF16)
    ones_blk = jnp.where(lax.broadcasted_iota(jnp.int32, (tr, LANES), 1) < HEAD_DIM, 1.0, 0.0).astype(BF16)
    for g in range(D_HEADS // 2):
        dv2_ref[:, 2 * g * LANES:(2 * g + 1) * LANES] = dv_ref[:, g * LANES:(g + 1) * LANES]
        dv2_ref[:, (2 * g + 1) * LANES:(2 * g + 2) * LANES] = ones_blk


def _prep(proj, cos_h, sin_h, gmat, gains, *, rows, seq, n_lat):
    tr = PREP_ROWS
    nlat = seq // tr

    def tab(i):
        return (jnp.where(i * tr < n_lat, i % nlat, nlat), 0)

    return pl.pallas_call(
        _prep_kernel,
        grid=(rows // tr,),
        in_specs=[pl.BlockSpec((tr, PREP_W), lambda i: (i, 0)),
                  pl.BlockSpec((tr, 512), lambda i: (i, P_DV // 512)),
                  pl.BlockSpec((tr, LANES), tab),
                  pl.BlockSpec((tr, LANES), tab),
                  pl.BlockSpec((512, 512), lambda i: (0, 0)),
                  pl.BlockSpec((4, 512), lambda i: (0, 0))],
        out_specs=[pl.BlockSpec((tr, PREP_W), lambda i: (i, 0)),
                   pl.BlockSpec((tr, 2 * 512), lambda i: (i, 0))],
        out_shape=[jax.ShapeDtypeStruct((rows, PREP_W), BF16), jax.ShapeDtypeStruct((rows, 2 * 512), BF16)],
        compiler_params=_cp(("arbitrary",), VMEM_LIMIT), name="prep",
    )(proj, proj, cos_h, sin_h, gmat, gains)


def _prepb_kernel(bq_ref, bkv_ref, bkr_ref, bkrs_ref, cb_ref, sb_ref, wuq_ref, wuqs_ref, wukv_ref, gains_ref,
                  q_ref, k_ref, v_ref):
    tr = bq_ref.shape[0]
    cq = bq_ref[...].astype(F32)
    cqn = (cq * lax.rsqrt(jnp.mean(cq * cq, axis=-1, keepdims=True) + EPS) * gains_ref[0:1, :]).astype(BF16)
    qup = jnp.dot(cqn, wuq_ref[...], preferred_element_type=F32)
    qup_sw = jnp.dot(cqn, wuqs_ref[...], preferred_element_type=F32)
    ckv = bkv_ref[...].astype(F32)
    ckvn = ckv * lax.rsqrt(jnp.mean(ckv * ckv, axis=-1, keepdims=True) + EPS) * gains_ref[1:2, :LANES]
    kvup = jnp.dot(ckvn.astype(BF16), wukv_ref[...], preferred_element_type=F32)
    kr = bkr_ref[...].astype(F32)
    cb, sb = cb_ref[...], sb_ref[...]
    lane = lax.broadcasted_iota(jnp.int32, (tr, LANES), 1)
    q_cos = cb * (gains_ref[2:3, :LANES] * MLA_LOGIT_SCALE)
    q_sin = sb * (gains_ref[5:6, :LANES] * MLA_LOGIT_SCALE)
    k_cos = cb * gains_ref[3:4, :LANES]
    k_rot = bkrs_ref[...].astype(F32) * (sb * gains_ref[6:7, :LANES])

    def inv_rms(x):
        return lax.rsqrt(jnp.sum(x * x, axis=-1, keepdims=True) * (1.0 / B_QK) + EPS)

    shift_lane = lane == B_QK
    neg_bound = gains_ref[4:5, :LANES]
    for h in range(B_HEADS):
        sl = slice(h * LANES, (h + 1) * LANES)
        x = qup[:, sl]
        q = (x * q_cos + qup_sw[:, sl] * q_sin) * inv_rms(x)
        q_ref[:, sl] = jnp.where(shift_lane, 1.0, q).astype(BF16)
        x = kvup[:, sl] + kr
        k = (x * k_cos + k_rot) * inv_rms(x)
        k_ref[:, sl] = jnp.where(shift_lane, neg_bound, k).astype(BF16)
    ones_blk = jnp.where(lane < B_V, 1.0, 0.0).astype(BF16)
    for g in range(B_HEADS // 2):
        v_ref[:, 2 * g * LANES:(2 * g + 1) * LANES] = kvup[:, (B_HEADS + g) * LANES:(B_HEADS + g + 1) * LANES].astype(BF16)
        v_ref[:, (2 * g + 1) * LANES:(2 * g + 2) * LANES] = ones_blk


def _prepb(proj, cos_b, sin_b, wuq, wuq_sw, wukv, gains, *, rows, seq, n_lat):
    tr = PREP_ROWS
    nlat = seq // tr

    def tab(i):
        return (jnp.where(i * tr < n_lat, i % nlat, nlat), 0)

    return pl.pallas_call(
        _prepb_kernel,
        grid=(rows // tr,),
        in_specs=[pl.BlockSpec((tr, B_Q_RANK), lambda i: (i, P_BQ // B_Q_RANK)),
                  pl.BlockSpec((tr, LANES), lambda i: (i, P_BKV // LANES)),
                  pl.BlockSpec((tr, LANES), lambda i: (i, P_BKR // LANES)),
                  pl.BlockSpec((tr, LANES), lambda i: (i, P_BKR_SW // LANES)),
                  pl.BlockSpec((tr, LANES), tab),
                  pl.BlockSpec((tr, LANES), tab),
                  pl.BlockSpec(wuq.shape, lambda i: (0, 0)),
                  pl.BlockSpec(wuq.shape, lambda i: (0, 0)),
                  pl.BlockSpec(wukv.shape, lambda i: (0, 0)),
                  pl.BlockSpec((8, B_Q_RANK), lambda i: (0, 0))],
        out_specs=[pl.BlockSpec((tr, B_HEADS * LANES), lambda i: (i, 0)),
                   pl.BlockSpec((tr, B_HEADS * LANES), lambda i: (i, 0)),
                   pl.BlockSpec((tr, B_HEADS * LANES), lambda i: (i, 0))],
        out_shape=[jax.ShapeDtypeStruct((rows, B_HEADS * LANES), BF16),
                   jax.ShapeDtypeStruct((rows, B_HEADS * LANES), BF16),
                   jax.ShapeDtypeStruct((rows, B_HEADS * LANES), BF16)],
        compiler_params=_cp(("arbitrary",), VMEM_LIMIT), name="prep_mla",
    )(proj, proj, proj, proj, cos_b, sin_b, wuq, wuq_sw, wukv, gains)


def _wina_kernel(par_ref, q_ref, kp_ref, ko_ref, kn_ref, kc_ref, vp_ref, vo_ref, vn_ref, vc_ref, o_ref,
                 *, nb, seq):
    n = pl.program_id(1)
    blk = A_BLOCK
    rep = A_HEADS // A_KV_HEADS
    n_ctx = kc_ref.shape[0]
    kcat = jnp.concatenate([kp_ref[...], ko_ref[...], kn_ref[...], kc_ref[...]], axis=0)
    vcat = jnp.concatenate([vp_ref[...], vo_ref[...], vn_ref[...], vc_ref[...]], axis=0)
    n_keys = 3 * blk + n_ctx
    qi = lax.broadcasted_iota(jnp.int32, (blk, n_keys), 0)
    kj = lax.broadcasted_iota(jnp.int32, (blk, n_keys), 1)
    band = kj - blk
    kpos = n * blk + band
    valid = ((jnp.abs(qi - band) <= blk) & (kpos >= 0) & (kpos < seq) & (n < nb)) | (kj >= 3 * blk)
    low = lax.broadcasted_iota(jnp.int32, (blk, LANES), 1) < HEAD_DIM
    low_k = lax.broadcasted_iota(jnp.int32, (n_keys, LANES), 1) < HEAD_DIM
    bound = par_ref[A_HEADS]
    safe = par_ref[A_HEADS + 1] > 0.0

    def attend(use_max):
        mask = jnp.where(valid, 0.0 if use_max else -bound, NEG_INF)
        mask = jnp.concatenate([mask] * rep, axis=0)
        for g in range(A_KV_HEADS):
            kg = kcat[:, g * LANES:(g + 1) * LANES]
            v1 = jnp.where(low_k, vcat[:, g * LANES:(g + 1) * LANES], jnp.ones((), BF16))
            qs, sinks = [], []
            for r in range(rep):
                h = g * rep + r
                qc = q_ref[:, (h // 2) * LANES:(h // 2 + 1) * LANES]
                qs.append(jnp.where(low if h % 2 == 0 else jnp.logical_not(low), qc, jnp.zeros_like(qc)))
                sinks.append(jnp.full((blk, 1), par_ref[h], F32))
            s = _nt(jnp.concatenate(qs, axis=0), kg) + mask
            sink = jnp.concatenate(sinks, axis=0)
            if use_max:
                m = jnp.maximum(jnp.max(s, axis=-1, keepdims=True), sink)
                s, sink = s - m, sink - m
            else:
                sink = sink - bound
            o = jnp.dot(jnp.exp2(s).astype(BF16), v1, preferred_element_type=F32)
            res = o / (o[:, HEAD_DIM:HEAD_DIM + 1] + jnp.exp2(sink))
            for c in range(rep // 2):
                even = res[(2 * c) * blk:(2 * c + 1) * blk]
                odd = pltpu.roll(res[(2 * c + 1) * blk:(2 * c + 2) * blk], HEAD_DIM, 1)
                j = (g * rep) // 2 + c
                o_ref[:, j * LANES:(j + 1) * LANES] = jnp.where(low, even, odd).astype(o_ref.dtype)

    @pl.when(safe)
    def _():
        attend(False)

    @pl.when(jnp.logical_not(safe))
    def _():
        attend(True)


def _wina(prep, proj, sink, *, n_batch, seq, n_ctx, with_ctx):
    blk = A_BLOCK
    nb = seq // blk
    ncb = n_ctx // blk
    steps = nb + (ncb if with_ctx else 0)
    lat_blocks = n_batch * nb
    rows_out = n_batch * seq + (n_batch * n_ctx if with_ctx else 0)

    def qmap(b, n, s):
        return (jnp.where(n < nb, b * nb + n, lat_blocks + b * ncb + (n - nb)), 0)

    def kmap(delta, col):
        def f(b, n, s):
            return (b * nb + jnp.clip(n + delta, 0, nb - 1), col)
        return f

    def cmap(col):
        def f(b, n, s):
            return (n_batch * seq // n_ctx + b, col)
        return f

    kcol, vcol = P_AK // 256, P_AV // 256
    in_specs = [pl.BlockSpec((blk, 512), qmap)]
    in_specs += [pl.BlockSpec((blk, 256), kmap(dl, kcol)) for dl in (-1, 0, 1)]
    in_specs += [pl.BlockSpec((n_ctx, 256), cmap(kcol))]
    in_specs += [pl.BlockSpec((blk, 256), kmap(dl, vcol)) for dl in (-1, 0, 1)]
    in_specs += [pl.BlockSpec((n_ctx, 256), cmap(vcol))]
    return pl.pallas_call(
        functools.partial(_wina_kernel, nb=nb, seq=seq),
        grid_spec=pltpu.PrefetchScalarGridSpec(
            num_scalar_prefetch=1, grid=(n_batch, steps), in_specs=in_specs,
            out_specs=pl.BlockSpec((blk, 512), qmap)),
        out_shape=jax.ShapeDtypeStruct((rows_out, 512), BF16),
        compiler_params=_cp(("arbitrary", "arbitrary"), VMEM_LIMIT), name="mixer_window",
    )(sink, prep, prep, prep, prep, prep, proj, proj, proj, proj)


def _mla_kernel(safe_ref, q_ref, kl_ref, vl_ref, kc_ref, vc_ref, o_ref, *, nq, tk_fast, tk_exact, hps):
    tq = q_ref.shape[0]
    seq = kl_ref.shape[0]
    is_lat = pl.program_id(2) < nq
    safe = safe_ref[0] > 0
    pair_w = 2 * LANES
    lane = lax.broadcasted_iota(jnp.int32, (tq, LANES), 1)

    def q(h):
        return q_ref[:, h * LANES:(h + 1) * LANES]

    def write(nums, dens):
        for g in range(hps // 2):
            o_ref[:, g * LANES:(g + 1) * LANES] = jnp.where(
                lane < B_V, nums[2 * g] / dens[2 * g], nums[2 * g + 1] / dens[2 * g + 1]).astype(o_ref.dtype)

    def fast(chunks):
        accs = [None] * hps
        for k_ref, v_ref, off, n in chunks:
            for h in range(hps):
                g = h // 2
                p = jnp.exp2(_nt(q(h), k_ref[off:off + n, h * LANES:(h + 1) * LANES])).astype(BF16)
                d = jnp.dot(p, v_ref[off:off + n, g * pair_w:(g + 1) * pair_w], preferred_element_type=F32)
                accs[h] = d if accs[h] is None else accs[h] + d
        write([a[:, :LANES] for a in accs], [a[:, LANES:LANES + 1] for a in accs])

    ctx_chunk = (kc_ref, vc_ref, 0, kc_ref.shape[0])

    @pl.when(safe & is_lat)
    def _():
        fast([(kl_ref, vl_ref, c * tk_fast, tk_fast) for c in range(seq // tk_fast)] + [ctx_chunk])

    @pl.when(safe & jnp.logical_not(is_lat))
    def _():
        fast([ctx_chunk])

    @pl.when(jnp.logical_not(safe))
    def _():
        def step(kf, vf, carry):
            out = []
            for h in range(hps):
                m, l, acc = carry[h]
                s = _nt(q(h), kf(h))
                m_new = jnp.maximum(m, jnp.max(s, axis=-1, keepdims=True))
                a = jnp.exp2(m - m_new)
                p = jnp.exp2(s - m_new)
                l = a * l + jnp.sum(p, axis=-1, keepdims=True)
                acc = a * acc + jnp.dot(p.astype(BF16), vf(h // 2), preferred_element_type=F32)
                out.append((m_new, l, acc))
            return tuple(out)

        def body(c, carry):
            off = pl.multiple_of(c * tk_exact, tk_exact)
            return step(lambda h: kl_ref[pl.ds(off, tk_exact), h * LANES:(h + 1) * LANES],
                        lambda g: vl_ref[pl.ds(off, tk_exact), g * pair_w:g * pair_w + LANES], carry)

        init = tuple((jnp.full((tq, 1), -jnp.inf, F32), jnp.zeros((tq, 1), F32), jnp.zeros((tq, LANES), F32))
                     for _ in range(hps))
        carry = lax.fori_loop(0, jnp.where(is_lat, seq // tk_exact, 0), body, init)
        carry = step(lambda h: kc_ref[:, h * LANES:(h + 1) * LANES],
                     lambda g: vc_ref[:, g * pair_w:g * pair_w + LANES], carry)
        write([c[2] for c in carry], [c[1] for c in carry])


def _mla_logit_bound(q_gain, k_gain):
    b = B_QK * MLA_LOGIT_SCALE * jnp.max(jnp.abs(q_gain)) * jnp.max(jnp.abs(k_gain))
    return (1.02 * b + 0.5).astype(F32)


def _mla(qb, kb, vb, safe, *, n_batch, seq, n_ctx, with_ctx):
    hps = 4
    tq = 256
    assert n_ctx == tq
    nq = seq // tq
    ctx_blk = n_batch * seq // n_ctx
    rows_out = n_batch * seq + (n_batch * n_ctx if with_ctx else 0)
    kw, vw = hps * LANES, hps * B_V

    def qmap(b, j, i, s):
        return (jnp.where(i < nq, b * nq + i, ctx_blk + b), j)

    return pl.pallas_call(
        functools.partial(_mla_kernel, nq=nq, tk_fast=min(2048, seq), tk_exact=512, hps=hps),
        grid_spec=pltpu.PrefetchScalarGridSpec(
            num_scalar_prefetch=1,
            grid=(n_batch, B_HEADS // hps, nq + (1 if with_ctx else 0)),
            in_specs=[pl.BlockSpec((tq, kw), qmap),
                      pl.BlockSpec((seq, kw), lambda b, j, i, s: (b, j)),
                      pl.BlockSpec((seq, kw), lambda b, j, i, s: (b, j)),
                      pl.BlockSpec((n_ctx, kw), lambda b, j, i, s: (ctx_blk + b, j)),
                      pl.BlockSpec((n_ctx, kw), lambda b, j, i, s: (ctx_blk + b, j))],
            out_specs=pl.BlockSpec((tq, vw), qmap)),
        out_shape=jax.ShapeDtypeStruct((rows_out, 512), BF16),
        compiler_params=_cp(("arbitrary", "arbitrary", "arbitrary"), VMEM_LIMIT), name="mixer_mla",
    )(safe, qb, kb, vb, kb, vb)


def _ret_kernel(qf_ref, kf_ref, vf_ref, qr_ref, kr_ref, vr_ref, ld_ref, of_ref, or_ref,
                s_ref, dec_ref, qw_ref, kw_ref, cd_ref):
    step = pl.program_id(1)
    cc = qf_ref.shape[0]
    low = lax.broadcasted_iota(jnp.int32, (cc, LANES), 1) < HEAD_DIM
    sr = lax.broadcasted_iota(jnp.int32, (LANES, LANES), 0)
    blockdiag = (sr < HEAD_DIM) == (lax.broadcasted_iota(jnp.int32, (LANES, LANES), 1) < HEAD_DIM)

    @pl.when(step == 0)
    def _():
        s_ref[...] = jnp.zeros_like(s_ref)
        qi = lax.broadcasted_iota(jnp.int32, (cc, cc), 0)
        ki = lax.broadcasted_iota(jnp.int32, (cc, cc), 1)
        pos = lax.broadcasted_iota(jnp.int32, (cc, LANES), 0).astype(F32)
        for d in range(2):
            diff = (qi - ki) if d == 0 else (ki - qi)
            dpos = jnp.maximum(diff, 0).astype(F32)
            qpow = (pos + 1.0) if d == 0 else (cc - pos)
            kpow = (cc - 1.0 - pos) if d == 0 else pos
            for j in range(C_HEADS // 2):
                lg_e = -jnp.exp(ld_ref[d, 2 * j:2 * j + 1, :])
                lg_o = -jnp.exp(ld_ref[d, 2 * j + 1:2 * j + 2, :])
                lgl = jnp.where(low[0:1, :], lg_e, lg_o)
                wide_e = jnp.concatenate([lg_e] * (cc // LANES), axis=1)
                wide_o = jnp.concatenate([lg_o] * (cc // LANES), axis=1)
                dec_ref[d, j, :cc] = jnp.where(diff >= 0, jnp.exp(wide_e * dpos), 0.0)
                dec_ref[d, j, cc:] = jnp.where(diff >= 0, jnp.exp(wide_o * dpos), 0.0)
                qw_ref[d, j] = jnp.exp(lgl * qpow)
                kw_ref[d, j] = jnp.exp(lgl * kpow)
                cd_ref[d, j] = jnp.where(sr < HEAD_DIM, jnp.exp(lg_e * cc), jnp.exp(lg_o * cc))

    pairs = C_HEADS // 2
    dirs = ((qf_ref, kf_ref, vf_ref, of_ref), (qr_ref, kr_ref, vr_ref, or_ref))
    states = [[s_ref[d, j] for j in range(pairs)] for d in range(2)]
    outs, new_states = [], []
    for d, (q_ref, k_ref, v_ref, _) in enumerate(dirs):
        for j in range(pairs):
            sl = slice(j * LANES, (j + 1) * LANES)
            q, k, v = q_ref[:, sl], k_ref[:, sl], v_ref[:, sl]
            zero = jnp.zeros_like(q)
            q2 = jnp.concatenate([jnp.where(low, q, zero), jnp.where(low, zero, q)], axis=0)
            sc = (_nt(q2, k) * dec_ref[d, j]).astype(BF16)
            o2 = jnp.dot(sc, v, preferred_element_type=F32)
            o_intra = jnp.where(low, o2[:cc], o2[cc:])
            state = states[d][j]
            qw = (q.astype(F32) * qw_ref[d, j]).astype(BF16)
            o_cross = jnp.dot(qw, state.astype(BF16), preferred_element_type=F32)
            kw = (k.astype(F32) * kw_ref[d, j]).astype(BF16)
            kv = lax.dot_general(kw, v, (((0,), (0,)), ((), ())), preferred_element_type=F32)
            new_states.append(state * cd_ref[d, j] + jnp.where(blockdiag, kv, 0.0))
            outs.append(o_intra + o_cross)
    for d in range(2):
        for j in range(pairs):
            s_ref[d, j] = new_states[d * pairs + j]
            dirs[d][3][:, j * LANES:(j + 1) * LANES] = outs[d * pairs + j].astype(BF16)


def _retention(prep, proj, ld_head, *, n_batch, seq, n_ctx):
    cc = RET_CHUNK
    nl, nc = seq // cc, n_ctx // cc
    steps = nc + nl
    lat_blocks = n_batch * nl
    rows = n_batch * (seq + n_ctx)

    def rowblk(b, d, s):
        c_ctx = jnp.where(d == 0, s, nc - 1 - s)
        c_lat = jnp.where(d == 0, s - nc, nl - 1 - (s - nc))
        return jnp.where(s < nc, lat_blocks + b * nc + c_ctx, b * nl + c_lat)

    def spec(d, col):
        return pl.BlockSpec((cc, 512), lambda b, s: (rowblk(b, d, s), col))

    pairs = C_HEADS // 2
    return pl.pallas_call(
        _ret_kernel,
        grid=(n_batch, steps),
        in_specs=[spec(d, col) for d in (0, 1) for col in (P_CQ // 512, P_CK // 512, P_CV // 512)]
        + [pl.BlockSpec((2, C_HEADS, LANES), lambda b, s: (0, 0, 0))],
        out_specs=[spec(0, 0), spec(1, 0)],
        out_shape=[jax.ShapeDtypeStruct((rows, 512), BF16)] * 2,
        scratch_shapes=[pltpu.VMEM((2, pairs, LANES, LANES), F32),
                        pltpu.VMEM((2, pairs, 2 * cc, cc), F32),
                        pltpu.VMEM((2, pairs, cc, LANES), F32),
                        pltpu.VMEM((2, pairs, cc, LANES), F32),
                        pltpu.VMEM((2, pairs, LANES, LANES), F32)],
        compiler_params=_cp(("arbitrary", "arbitrary"), VMEM_LIMIT), name="mixer_retention",
    )(prep, prep, proj, prep, prep, proj, ld_head)


def _retfin_kernel(of_ref, or_ref, g_ref, gm_ref, out_ref):
    o = of_ref[...].astype(F32) + or_ref[...].astype(F32)
    on = o * lax.rsqrt(_group_meansq(o, gm_ref, 512) + EPS)
    g = g_ref[...].astype(F32)
    out_ref[...] = (g * jax.nn.sigmoid(g) * on).astype(out_ref.dtype)


def _retention_finish(o_fwd, o_rev, proj, gmat, *, rows):
    tr = 256
    return pl.pallas_call(
        _retfin_kernel,
        grid=(rows // tr,),
        in_specs=[pl.BlockSpec((tr, 512), lambda i: (i, 0)),
                  pl.BlockSpec((tr, 512), lambda i: (i, 0)),
                  pl.BlockSpec((tr, 512), lambda i: (i, P_CG // 512)),
                  pl.BlockSpec((512, 512), lambda i: (0, 0))],
        out_specs=pl.BlockSpec((tr, 512), lambda i: (i, 0)),
        out_shape=jax.ShapeDtypeStruct((rows, 512), BF16),
        compiler_params=_cp(("arbitrary",), VMEM_LIMIT), name="retention_finish",
    )(o_fwd, o_rev, proj, gmat)


def _nbr_kernel(par_ref, q_ref, k_ref, v_ref, kc_ref, vc_ref, abm_ref, o_ref, *, rows):
    st = pl.program_id(1)
    start = jnp.clip(NA_QROWS * st - NA_ROWS // 2, 0, rows - NA_KROWS)
    off = pl.multiple_of(start * GRID_W, GRID_W)
    nk = NA_KROWS * GRID_W
    tq = q_ref.shape[0]
    low = lax.broadcasted_iota(jnp.int32, (tq, LANES), 1) < HEAD_DIM
    bound = par_ref[0]
    safe = par_ref[1] > 0.0
    pair_w = 2 * LANES

    def attend(use_max):
        for j in range(D_HEADS // 2):
            sl = slice(j * LANES, (j + 1) * LANES)
            sl2 = slice(j * pair_w, (j + 1) * pair_w)
            qc = q_ref[:, sl]
            zero = jnp.zeros_like(qc)
            q2 = jnp.concatenate([jnp.where(low, qc, zero), jnp.where(low, zero, qc)], axis=0)
            bias = jnp.concatenate([abm_ref[0, 2 * j], abm_ref[0, 2 * j + 1]], axis=0)
            s1 = _nt(q2, k_ref[pl.ds(off, nk), sl]) + bias
            s2 = _nt(q2, kc_ref[:, sl]) - bound
            if use_max:
                m = jnp.maximum(jnp.max(s1, axis=-1, keepdims=True), jnp.max(s2, axis=-1, keepdims=True))
                s1, s2 = s1 - m, s2 - m
            o = (jnp.dot(jnp.exp2(s1).astype(BF16), v_ref[pl.ds(off, nk), sl2], preferred_element_type=F32)
                 + jnp.dot(jnp.exp2(s2).astype(BF16), vc_ref[:, sl2], preferred_element_type=F32))
            res = o[:, :LANES] / o[:, LANES:LANES + 1]
            o_ref[:, sl] = jnp.where(low, res[:tq], res[tq:]).astype(o_ref.dtype)

    @pl.when(safe)
    def _():
        attend(False)

    @pl.when(jnp.logical_not(safe))
    def _():
        attend(True)


def _nbr(prep, dv2, abm, par, *, n_batch, seq, n_ctx, with_ctx):
    rows = seq // GRID_W
    tq = NA_QROWS * GRID_W
    assert n_ctx == tq and rows % NA_QROWS == 0 and rows >= NA_KROWS + NA_QROWS
    nst = rows // NA_QROWS
    steps = nst + (1 if with_ctx else 0)
    rows_out = n_batch * seq + (n_batch * n_ctx if with_ctx else 0)

    def qmap(col):
        def f(b, s, par):
            return (jnp.where(s < nst, b * nst + s, n_batch * nst + b), col)
        return f

    def case(b, s, par):
        c = jnp.where(s == 0, 0, jnp.where(s == nst - 1, 2, jnp.where(s == nst, 3, 1)))
        return (c, 0, 0, 0)

    ctx_blk = n_batch * seq // n_ctx
    return pl.pallas_call(
        functools.partial(_nbr_kernel, rows=rows),
        grid_spec=pltpu.PrefetchScalarGridSpec(
            num_scalar_prefetch=1, grid=(n_batch, steps),
            in_specs=[pl.BlockSpec((tq, 512), qmap(P_DQ // 512)),
                      pl.BlockSpec((seq, 512), lambda b, s, par: (b, P_DK // 512)),
                      pl.BlockSpec((seq, 2 * 512), lambda b, s, par: (b, 0)),
                      pl.BlockSpec((n_ctx, 512), lambda b, s, par: (ctx_blk + b, P_DK // 512)),
                      pl.BlockSpec((n_ctx, 2 * 512), lambda b, s, par: (ctx_blk + b, 0)),
                      pl.BlockSpec((1, D_HEADS, tq, NA_KROWS * GRID_W), case)],
            out_specs=pl.BlockSpec((tq, 512), qmap(0))),
        out_shape=jax.ShapeDtypeStruct((rows_out, 512), BF16),
        compiler_params=_cp(("arbitrary", "arbitrary"), VMEM_LIMIT), name="mixer_neighbourhood",
    )(par, prep, prep, dv2, prep, dv2, abm)


def _softmax_bound(q_gain, k_gain, extra):
    b = HEAD_DIM ** 0.5 * LOG2E * jnp.max(jnp.abs(q_gain)) * jnp.max(jnp.abs(k_gain)) + LOG2E * extra
    return (1.02 * b + 0.5).astype(F32)


def _nbr_bias_tables(rpb, rows, bound):
    w = GRID_W
    cidx = np.arange(w)
    col_start = np.clip(cidx - NA_COLS // 2, 0, w - NA_COLS)
    col_ok = (cidx[None, :] >= col_start[:, None]) & (cidx[None, :] < col_start[:, None] + NA_COLS)
    d_col = np.clip(cidx[None, :] - cidx[:, None] + (NA_COLS - 1), 0, 2 * NA_COLS - 2)
    n_heads, n_dr, n_dc = rpb.shape
    onehot = jnp.asarray((d_col[:, :, None] == np.arange(n_dc)[None, None, :]).astype(np.float32))
    t = jnp.einsum("hrd,qkd->hqrk", rpb.astype(F32), onehot, precision=lax.Precision.HIGHEST)
    t = t * LOG2E - bound
    t = jnp.where(jnp.asarray(col_ok)[None, :, None, :], t, NEG_INF).reshape(n_heads, w, n_dr * w)
    cases = []
    for r0 in (0, NA_QROWS, rows - NA_QROWS):
        start = int(np.clip(r0 - NA_ROWS // 2, 0, rows - NA_KROWS))
        blocks = []
        for i in range(NA_QROWS):
            r = r0 + i
            row_start = int(np.clip(r - NA_ROWS // 2, 0, rows - NA_ROWS))
            kk0 = row_start - start
            dr0 = row_start - r + (NA_ROWS - 1)
            blk = t[:, :, dr0 * w:(dr0 + NA_ROWS) * w]
            blocks.append(jnp.pad(blk, ((0, 0), (0, 0), (kk0 * w, (NA_KROWS - NA_ROWS - kk0) * w)),
                                  constant_values=NEG_INF))
        cases.append(jnp.concatenate(blocks, axis=1))
    cases.append(jnp.full((n_heads, NA_QROWS * w, NA_KROWS * w), NEG_INF, F32))
    return jnp.stack(cases)


def _swiglu_accumulate(x_ref, wg_ref, wu_ref, wd_ref, acc_ref, n_rows):
    x = x_ref[:n_rows]
    g = jnp.dot(x, wg_ref[0].astype(BF16), preferred_element_type=F32)
    u = jnp.dot(x, wu_ref[0].astype(BF16), preferred_element_type=F32)
    a = (g * jax.nn.sigmoid(g) * u).astype(BF16)
    acc_ref[:n_rows] += jnp.dot(a, wd_ref[0].astype(BF16), preferred_element_type=F32)


def _ffn_kernel(x_ref, wg_ref, wu_ref, wd_ref, o_ref, acc_ref):
    j = pl.program_id(1)

    @pl.when(j == 0)
    def _():
        acc_ref[...] = jnp.zeros_like(acc_ref)

    _swiglu_accumulate(x_ref, wg_ref, wu_ref, wd_ref, acc_ref, x_ref.shape[0])

    @pl.when(j == pl.num_programs(1) - 1)
    def _():
        o_ref[...] = acc_ref[...].astype(o_ref.dtype)


def _ffn(x, w_gate_up, w_down, layer, *, rows, tm, fc):
    _, d, ff2 = w_gate_up.shape
    nj = ff2 // 2 // fc
    return pl.pallas_call(
        _ffn_kernel,
        grid=(rows // tm, nj),
        in_specs=[pl.BlockSpec((tm, d), lambda i, j: (i, 0)),
                  pl.BlockSpec((1, d, fc), lambda i, j: (layer, 0, j)),
                  pl.BlockSpec((1, d, fc), lambda i, j: (layer, 0, nj + j)),
                  pl.BlockSpec((1, fc, d), lambda i, j: (layer, j, 0))],
        out_specs=pl.BlockSpec((tm, d), lambda i, j: (i, 0)),
        out_shape=jax.ShapeDtypeStruct((rows, d), BF16),
        scratch_shapes=[pltpu.VMEM((tm, d), F32)],
        compiler_params=_cp(("arbitrary", "arbitrary"), VMEM_LIMIT), name="swiglu",
    )(x, w_gate_up, w_gate_up, w_down)


def _moe_ffn_kernel(te_ref, tv_ref, tok_ref, hp_ref, wg_ref, wu_ref, wd_ref, o_ref, gbuf_ref, x_ref, acc_ref, sem,
                    *, rows_per_step):
    i, j = pl.program_id(0), pl.program_id(1)
    n_tiles = pl.num_programs(0)
    tm, d = x_ref.shape
    half = d // 2
    valid = tv_ref[i] > 0
    slot = i % 2

    def issue(tile, first_row, n_rows, to_slot):
        def body(r, carry):
            row = first_row + r
            pltpu.make_async_copy(hp_ref.at[pl.ds(tok_ref[tile * tm + row], 1), :],
                                  gbuf_ref.at[to_slot].at[pl.ds(row, 1), :], sem.at[to_slot]).start()
            return carry

        lax.fori_loop(0, n_rows, body, 0, unroll=8)

    @pl.when((i == 0) & (j == 0) & valid)
    def _():
        issue(0, 0, tm, 0)

    @pl.when((j == 0) & valid)
    def _():
        pltpu.make_async_copy(hp_ref.at[pl.ds(0, tm), :], gbuf_ref.at[slot], sem.at[slot]).wait()
        lo, hi = _unpack_halves(gbuf_ref[slot])
        x_ref[:, :half] = lo.astype(BF16)
        x_ref[:, half:] = hi.astype(BF16)
        acc_ref[...] = jnp.zeros_like(acc_ref)

    nxt = jnp.minimum(i + 1, n_tiles - 1)

    @pl.when((i + 1 < n_tiles) & (tv_ref[nxt] > 0) & (j * rows_per_step < tm))
    def _():
        issue(nxt, j * rows_per_step, rows_per_step, 1 - slot)

    for q in range(1, MOE_ROW_SPLITS + 1):
        n_rows = q * (tm // MOE_ROW_SPLITS)

        @pl.when(tv_ref[i] == n_rows)
        def _():
            _swiglu_accumulate(x_ref, wg_ref, wu_ref, wd_ref, acc_ref, n_rows)

    @pl.when(j == pl.num_programs(1) - 1)
    def _():
        o_ref[...] = jnp.where(valid, _pack_halves(acc_ref[...]), jnp.uint32(0))


def _moe_ffn(hp, w_gate_up, w_down, layer, tile_expert, tile_valid, row_token, *, tm, fc):
    _, _, d, ff2 = w_gate_up.shape
    nj = ff2 // 2 // fc
    n_rows = row_token.shape[0]
    issue_steps = 1 << (nj.bit_length() - 1)

    def jeff(i, j, tv):
        return jnp.where(tv[i] > 0, j, nj - 1)

    return pl.pallas_call(
        functools.partial(_moe_ffn_kernel, rows_per_step=tm // issue_steps),
        grid_spec=pltpu.PrefetchScalarGridSpec(
            num_scalar_prefetch=3, grid=(n_rows // tm, nj),
            in_specs=[pl.BlockSpec(memory_space=pl.ANY),
                      pl.BlockSpec((None, 1, d, fc), lambda i, j, te, tv, tok: (layer, te[i], 0, jeff(i, j, tv))),
                      pl.BlockSpec((None, 1, d, fc),
                                   lambda i, j, te, tv, tok: (layer, te[i], 0, nj + jeff(i, j, tv))),
                      pl.BlockSpec((None, 1, fc, d), lambda i, j, te, tv, tok: (layer, te[i], jeff(i, j, tv), 0))],
            out_specs=pl.BlockSpec((tm, d // 2), lambda i, j, te, tv, tok: (i, 0)),
            scratch_shapes=[pltpu.VMEM((2, tm, d // 2), jnp.uint32), pltpu.VMEM((tm, d), BF16),
                            pltpu.VMEM((tm, d), F32), pltpu.SemaphoreType.DMA((2,))]),
        out_shape=jax.ShapeDtypeStruct((n_rows, d // 2), jnp.uint32),
        compiler_params=_cp(("arbitrary", "arbitrary"), VMEM_LIMIT_MOE), name="moe_swiglu",
    )(tile_expert, tile_valid, row_token, hp, w_gate_up, w_gate_up, w_down)


def _combine_kernel(p0_ref, p1_ref, y_ref, x_ref, route_ref, gate_ref, o_ref, b0_ref, b1_ref, sem):
    _, n, half = b0_ref.shape
    i = pl.program_id(0)
    slot = i % 2

    def issue(tile, to_slot):
        def body(r, carry):
            k = tile * n + r
            pltpu.make_async_copy(y_ref.at[pl.ds(p0_ref[k], 1), :], b0_ref.at[to_slot].at[pl.ds(r, 1), :],
                                  sem.at[0, to_slot]).start()
            pltpu.make_async_copy(y_ref.at[pl.ds(p1_ref[k], 1), :], b1_ref.at[to_slot].at[pl.ds(r, 1), :],
                                  sem.at[1, to_slot]).start()
            return carry

        lax.fori_loop(0, n, body, 0, unroll=8)

    @pl.when(i == 0)
    def _():
        issue(0, 0)

    pltpu.make_async_copy(y_ref.at[pl.ds(0, n), :], b0_ref.at[slot], sem.at[0, slot]).wait()
    pltpu.make_async_copy(y_ref.at[pl.ds(0, n), :], b1_ref.at[slot], sem.at[1, slot]).wait()

    @pl.when(i + 1 < pl.num_programs(0))
    def _():
        issue(i + 1, 1 - slot)

    route = route_ref[...]
    w0, w1 = route[:, 2:3], route[:, 3:4]
    lo0, hi0 = _unpack_halves(b0_ref[slot])
    lo1, hi1 = _unpack_halves(b1_ref[slot])
    gate = gate_ref[0, 0]
    o_ref[:, :half] = x_ref[:, :half] + gate[:, :half] * (w0 * lo0 + w1 * lo1)
    o_ref[:, half:] = x_ref[:, half:] + gate[:, half:] * (w0 * hi0 + w1 * hi1)


def _combine(y, x, route, mod, gate_k, pos0, pos1, *, seq, n_batch):
    t, d = x.shape
    tr = 256
    return pl.pallas_call(
        _combine_kernel,
        grid_spec=pltpu.PrefetchScalarGridSpec(
            num_scalar_prefetch=2, grid=(t // tr,),
            in_specs=[pl.BlockSpec(memory_space=pl.ANY),
                      pl.BlockSpec((tr, d), lambda i, a, b: (i, 0)),
                      pl.BlockSpec((tr, LANES), lambda i, a, b: (i, 0)),
                      pl.BlockSpec((1, 1, 1, d),
                                   lambda i, a, b: (jnp.minimum((i * tr) // seq, n_batch), gate_k, 0, 0))],
            out_specs=pl.BlockSpec((tr, d), lambda i, a, b: (i, 0)),
            scratch_shapes=[pltpu.VMEM((2, tr, d // 2), jnp.uint32), pltpu.VMEM((2, tr, d // 2), jnp.uint32),
                            pltpu.SemaphoreType.DMA((2, 2))]),
        out_shape=jax.ShapeDtypeStruct((t, d), F32),
        compiler_params=_cp(("arbitrary",), VMEM_LIMIT), name="moe_combine",
    )(pos0, pos1, y, x, route, _mod_view(mod))


def _route_meta(idx, n_exp, tm):
    t = idx.shape[0]
    flat = idx.reshape(-1)
    onehot = (flat[:, None] == jnp.arange(n_exp, dtype=jnp.int32)[None, :]).astype(jnp.int32)
    counts = jnp.sum(onehot, axis=0)
    rank = jnp.take_along_axis(jnp.cumsum(onehot, axis=0) - onehot, flat[:, None], axis=1)[:, 0]
    padded = ((counts + tm - 1) // tm) * tm
    ends = jnp.cumsum(padded)
    starts = ends - padded
    pos = starts[flat] + rank
    n_rows = TOP_K * t + n_exp * tm
    row_token = jnp.zeros((n_rows,), jnp.int32).at[pos].set(jnp.arange(TOP_K * t, dtype=jnp.int32) // TOP_K)
    tile_start = jnp.arange(n_rows // tm, dtype=jnp.int32) * tm
    tile_valid = tile_start < ends[-1]
    tile_expert = jnp.sum((ends[None, :] <= tile_start[:, None]).astype(jnp.int32), axis=1)
    tile_expert = jnp.minimum(tile_expert, n_exp - 1)
    last_valid = jnp.maximum(ends[-1] // tm - 1, 0)
    tile_expert = jnp.where(tile_valid, tile_expert, tile_expert[last_valid])
    sub = tm // MOE_ROW_SPLITS
    used = jnp.clip(counts[tile_expert] - (tile_start - starts[tile_expert]), 0, tm)
    tile_rows = jnp.where(tile_valid, ((used + sub - 1) // sub) * sub, 0).astype(jnp.int32)
    pos = pos.reshape(t, TOP_K).astype(jnp.int32)
    return row_token, tile_expert, tile_rows, pos[:, 0], pos[:, 1]


def _rope_tables(seq, pad_rows):
    t = jnp.arange(seq, dtype=jnp.int32)
    row = (t // GRID_W).astype(F32)
    col = (t % GRID_W).astype(F32)

    def angles(rot_dim):
        n_freq = rot_dim // 4
        freqs = ROPE_THETA ** (-jnp.arange(n_freq, dtype=F32) / n_freq)
        ang = jnp.concatenate([row[:, None] * freqs[None, :], col[:, None] * freqs[None, :]], axis=-1)
        return jnp.cos(ang), jnp.sin(ang)

    ch, sh = angles(HEAD_DIM)
    cos_h = jnp.concatenate([ch] * 4, axis=1)
    sin_h = jnp.concatenate([-sh, sh, -sh, sh], axis=1)
    cb, sb = angles(B_ROPE)
    one = jnp.ones((seq, B_NOPE), F32)
    zero = jnp.zeros((seq, B_NOPE), F32)
    tail1 = jnp.ones((seq, LANES - B_QK), F32)
    tail0 = jnp.zeros((seq, LANES - B_QK), F32)
    cos_b = jnp.concatenate([one, cb, cb, tail1], axis=1)
    sin_b = jnp.concatenate([zero, -sb, sb, tail0], axis=1)

    def pad(tab, fill):
        return jnp.concatenate([tab, jnp.full((pad_rows, LANES), fill, F32)], axis=0)

    return pad(cos_h, 1.0), pad(sin_h, 0.0), pad(cos_b, 1.0), pad(sin_b, 0.0)


def _permute_w_in(w):
    d = w.shape[0]
    o = np.cumsum([0, 512, 128, 128, B_Q_RANK, B_KV_RANK, B_ROPE, 512, 512, 512, 512, 512, 512, 512])
    aq, ak, av, bq, bkv, bkr, cq, ck, cv, cg, dq, dk, dv = [w[:, o[i]:o[i + 1]] for i in range(13)]

    def dup(m):
        return jnp.concatenate([m[:, :64], m[:, :64], m[:, 64:], m[:, 64:]], axis=1)

    z = lambda n: jnp.zeros((d, n), w.dtype)
    half = B_ROPE // 2
    bkr_swapped = jnp.concatenate([bkr[:, half:], bkr[:, :half]], axis=1)
    parts = [aq, cq, ck, dq, dk, dup(ak), dup(av), cv, cg, dv, bq, bkv,
             z(B_NOPE), bkr, z(LANES - B_QK), z(B_NOPE), bkr_swapped, z(LANES - B_QK)]
    return jnp.concatenate(parts, axis=1).astype(BF16)


def _block_diag_ones():
    i = np.arange(512) // HEAD_DIM
    return jnp.asarray((i[:, None] == i[None, :]).astype(np.float32), dtype=BF16)


def _rope_partner(m):
    half = B_ROPE // 2
    return jnp.concatenate([m[..., :B_NOPE], m[..., B_NOPE + half:], m[..., B_NOPE:B_NOPE + half]], axis=-1)


def _pad_lanes(v, width):
    return jnp.concatenate([v.astype(F32), jnp.zeros((width - v.shape[0],), F32)])


def kernel(x, c, ctx, c_ctx, w_ada, b_ada, norm_mix, norm_ffn, w_in, w_out, a_q_norm, a_k_norm, a_sink,
           b_q_a_norm, b_kv_a_norm, b_w_uq, b_w_ukv, b_q_norm, b_k_norm, c_log_decay, d_q_norm, d_k_norm,
           d_rpb, ffn_w_gate_up, ffn_w_down, moe_router, moe_w_gate_up, moe_w_down):
    n_batch, seq, d = x.shape
    n_ctx = ctx.shape[1]
    depth = w_ada.shape[0]
    n_lat = n_batch * seq
    n_all = n_lat + n_batch * n_ctx
    grid_rows = seq // GRID_W
    tm = next(t for t in (1024, 512, 256) if n_lat % t == 0 and n_all % t == 0)

    cc = jnp.concatenate([c, c_ctx[None, :], jnp.zeros((8 - n_batch - 1, d), F32)], axis=0)
    mod = _ada(cc, w_ada, b_ada)
    xs = (x.reshape(n_lat, d), ctx.reshape(n_batch * n_ctx, d))
    cos_h, sin_h, cos_b, sin_b = _rope_tables(seq, PREP_ROWS)
    gmat = _block_diag_ones()

    y = None
    y_gate, y_mod = 0, None
    for l in range(depth):
        last = l == depth - 1
        mod_l = mod[l]
        if y is None:
            (h,) = _resnorm(xs, None, None, 0, mod_l, norm_mix[l], 0, 1, rows=n_all, seq=seq, n_batch=n_batch)
        else:
            xs, h = _resnorm(xs, y, y_mod, y_gate, mod_l, norm_mix[l], 0, 1, rows=n_all, seq=seq,
                             n_batch=n_batch)
        proj = _mm([h], _permute_w_in(w_in[l]), rows=n_all, tm=tm, tn=1792, out_dtype=BF16, name="proj_in")
        gains = jnp.stack([jnp.tile(a_q_norm[l], 8), _pad_lanes(jnp.tile(a_k_norm[l], 4), 512),
                           jnp.tile(d_q_norm[l], 8), jnp.tile(d_k_norm[l], 8)]).astype(F32)
        prep, dv2 = _prep(proj, cos_h, sin_h, gmat, gains, rows=n_all, seq=seq, n_lat=n_lat)
        wuq = jnp.pad(b_w_uq[l].reshape(B_Q_RANK, B_HEADS, B_QK),
                      ((0, 0), (0, 0), (0, LANES - B_QK))).reshape(B_Q_RANK, B_HEADS * LANES).astype(BF16)
        wukv3 = b_w_ukv[l].reshape(B_KV_RANK, B_HEADS, B_NOPE + B_V)
        wukv = jnp.concatenate(
            [jnp.pad(wukv3[:, :, :B_NOPE], ((0, 0), (0, 0), (0, LANES - B_NOPE))).reshape(B_KV_RANK, -1),
             wukv3[:, :, B_NOPE:].reshape(B_KV_RANK, -1)], axis=1).astype(BF16)
        bound = _mla_logit_bound(b_q_norm[l], b_k_norm[l])
        mla_safe = (bound <= SOFTMAX_SAFE_BOUND).astype(jnp.int32).reshape(1)
        gains_b = jnp.stack([b_q_a_norm[l].astype(F32), _pad_lanes(b_kv_a_norm[l], B_Q_RANK),
                             _pad_lanes(b_q_norm[l], B_Q_RANK), _pad_lanes(b_k_norm[l], B_Q_RANK),
                             jnp.full((B_Q_RANK,), -1.0, F32) * bound,
                             _pad_lanes(_rope_partner(b_q_norm[l]), B_Q_RANK),
                             _pad_lanes(_rope_partner(b_k_norm[l]), B_Q_RANK), jnp.zeros((B_Q_RANK,), F32)])
        wuq3 = b_w_uq[l].reshape(B_Q_RANK, B_HEADS, B_QK)
        wuq_sw = jnp.pad(_rope_partner(wuq3), ((0, 0), (0, 0), (0, LANES - B_QK))).reshape(wuq.shape).astype(BF16)
        qb, kb, vb = _prepb(proj, cos_b, sin_b, wuq, wuq_sw, wukv, gains_b, rows=n_all, seq=seq, n_lat=n_lat)

        sink = a_sink[l].astype(F32)
        bound_a = _softmax_bound(a_q_norm[l], a_k_norm[l], jnp.maximum(jnp.max(sink), 0.0))
        par_a = jnp.concatenate([sink * LOG2E, bound_a[None], (bound_a <= SOFTMAX_SAFE_BOUND).astype(F32)[None]])
        oa = _wina(prep, proj, par_a, n_batch=n_batch, seq=seq, n_ctx=n_ctx, with_ctx=not last)
        ob = _mla(qb, kb, vb, mla_safe, n_batch=n_batch, seq=seq, n_ctx=n_ctx, with_ctx=not last)
        ld_head = jnp.broadcast_to(c_log_decay[l].astype(F32)[:, :, None], (2, C_HEADS, LANES))
        oc_fwd, oc_rev = _retention(prep, proj, ld_head, n_batch=n_batch, seq=seq, n_ctx=n_ctx)
        bound_d = _softmax_bound(d_q_norm[l], d_k_norm[l], jnp.max(jnp.abs(d_rpb[l])))
        par_d = jnp.stack([bound_d, (bound_d <= SOFTMAX_SAFE_BOUND).astype(F32)])
        abm = _nbr_bias_tables(d_rpb[l], grid_rows, bound_d)
        od = _nbr(prep, dv2, abm, par_d, n_batch=n_batch, seq=seq, n_ctx=n_ctx, with_ctx=not last)
        rows_l = n_lat if last else n_all
        oc = _retention_finish(oc_fwd, oc_rev, proj, gmat, rows=rows_l)
        ymix = _mm([oa, ob, oc, od], w_out[l].astype(BF16), rows=rows_l, tm=tm, tn=1024, out_dtype=BF16,
                   name="proj_out")
        i = l // 2
        if l % 2 == 0:
            xs, h2 = _resnorm(xs, ymix, mod_l, 2, mod_l, norm_ffn[l], 3, 4, rows=rows_l, seq=seq,
                              n_batch=n_batch)
            y = _ffn(h2, ffn_w_gate_up, ffn_w_down, i, rows=rows_l, tm=tm, fc=512)
            y_gate, y_mod = 5, mod_l
        else:
            xs, hp, route = _resnorm(xs, ymix, mod_l, 2, mod_l, norm_ffn[l], 3, 4, rows=rows_l, seq=seq,
                                     n_batch=n_batch, router=moe_router[i])
            top_idx = route[:, :TOP_K].astype(jnp.int32)
            row_token, tile_expert, tile_valid, pos0, pos1 = _route_meta(top_idx, moe_router.shape[2], tm)
            yg = _moe_ffn(hp, moe_w_gate_up, moe_w_down, i, tile_expert, tile_valid, row_token, tm=tm, fc=512)
            xs = _combine(yg, xs, route, mod_l, 5, pos0, pos1, seq=seq, n_batch=n_batch)
            y = None
    if y is not None:
        xs, _ = _resnorm(xs, y, y_mod, y_gate, y_mod, norm_ffn[depth - 1], 3, 4, rows=xs.shape[0], seq=seq,
                         n_batch=n_batch)
    return xs[:n_lat].reshape(n_batch, seq, d)
```

```python
import functools

import numpy as np
import jax
import jax.numpy as jnp
from jax import lax
from jax.experimental import pallas as pl
from jax.experimental.pallas import tpu as pltpu

F32 = jnp.float32
BF16 = jnp.bfloat16

GRID_W = 64
HEAD_DIM = 64
ROPE_THETA = 10000.0
EPS = 1e-6
NEG_INF = -1e30

A_HEADS = 8
A_KV_HEADS = 2
A_BLOCK = 128
B_HEADS = 8
B_Q_RANK = 384
B_KV_RANK = 128
B_NOPE = 64
B_ROPE = 32
B_QK = B_NOPE + B_ROPE
B_V = 64
C_HEADS = 8
D_HEADS = 8
NA_ROWS = 8
NA_COLS = 16
TOP_K = 2

LANES = 128
NA_QROWS = 4
NA_KROWS = NA_ROWS + NA_QROWS
NA_CASES = 4
MOE_ROW_SPLITS = 8
PREP_ROWS = 512
RET_CHUNK = 256

P_AQ, P_CQ, P_CK, P_DQ, P_DK, P_AK, P_AV, P_CV, P_CG, P_DV, P_BQ, P_BKV, P_BKR, P_BKR_SW, P_END = (
    0, 512, 1024, 1536, 2048, 2560, 2816, 3072, 3584, 4096, 4608, 4992, 5120, 5248, 5376)
PREP_W = P_AV

VMEM_LIMIT = 56 * 1024 * 1024
VMEM_LIMIT_MOE = 61 * 1024 * 1024

LOG2E = float(np.log2(np.e))
MLA_LOGIT_SCALE = B_QK ** -0.5 * LOG2E
SOFTMAX_SAFE_BOUND = 40.0


def _cp(sem, vmem=None):
    return pltpu.CompilerParams(dimension_semantics=sem, vmem_limit_bytes=vmem)


def _nt(a, b):
    return lax.dot_general(a, b, (((1,), (1,)), ((), ())), preferred_element_type=F32)


_HI_HALF = 0xFFFF0000


def _pack_halves(x):
    n = x.shape[1] // 2
    bits = pltpu.bitcast(x.astype(BF16).astype(F32), jnp.uint32)
    return (bits[:, :n] >> 16) | (bits[:, n:] & jnp.uint32(_HI_HALF))


def _unpack_halves(w):
    return pltpu.bitcast(w << 16, F32), pltpu.bitcast(w & jnp.uint32(_HI_HALF), F32)


def _ada_kernel(c_ref, w_ref, b_ref, o_ref):
    c = c_ref[...]
    s = (c * jax.nn.sigmoid(c)).astype(BF16)
    o_ref[0] = jnp.dot(s, w_ref[0].astype(BF16), preferred_element_type=F32) + b_ref[0]


def _ada(cc, w_ada, b_ada):
    depth, d, n = w_ada.shape
    tn = 1024
    return pl.pallas_call(
        _ada_kernel,
        grid=(depth, n // tn),
        in_specs=[pl.BlockSpec((8, d), lambda l, j: (0, 0)),
                  pl.BlockSpec((1, d, tn), lambda l, j: (l, 0, j)),
                  pl.BlockSpec((1, 1, tn), lambda l, j: (l, 0, j))],
        out_specs=pl.BlockSpec((1, 8, tn), lambda l, j: (l, 0, j)),
        out_shape=jax.ShapeDtypeStruct((depth, 8, n), F32),
        compiler_params=_cp(("arbitrary", "arbitrary"), VMEM_LIMIT),
        name="ada",
    )(cc, w_ada, b_ada.reshape(depth, 1, n))


def _resnorm_kernel(*refs, has_res, with_router, n_exp, lat_tiles):
    refs = list(refs)
    x_ref = refs.pop(0)
    if lat_tiles is not None:
        xc_ref = refs.pop(0)
        x = jnp.where(pl.program_id(0) < lat_tiles, x_ref[...], xc_ref[...])
    else:
        x = x_ref[...]
    if has_res:
        y_ref = refs.pop(0)
        gate_ref = refs.pop(0)
    gain_ref, sh_ref, sc_ref = refs[:3]
    refs = refs[3:]
    if with_router:
        router_ref = refs.pop(0)
    if has_res:
        xo_ref = refs.pop(0)
    h_ref = refs.pop(0)
    if has_res:
        x = x + gate_ref[0, 0] * y_ref[...].astype(F32)
        xo_ref[...] = x
    ms = jnp.mean(x * x, axis=-1, keepdims=True)
    h = (x * lax.rsqrt(ms + EPS) * gain_ref[...]) * (1.0 + sc_ref[0, 0]) + sh_ref[0, 0]
    if with_router:
        (route_ref,) = refs
        h_ref[...] = _pack_halves(h)
        h_hi = h.astype(BF16)
        h_lo = (h - h_hi.astype(F32)).astype(BF16)
        r_hi, r_lo = router_ref[0], router_ref[1]
        logits = (jnp.dot(h_hi, r_hi, preferred_element_type=F32) + jnp.dot(h_lo, r_hi, preferred_element_type=F32)
                  + jnp.dot(h_hi, r_lo, preferred_element_type=F32))
        lane = lax.broadcasted_iota(jnp.int32, logits.shape, 1)
        logits = jnp.where(lane < n_exp, logits, -jnp.inf)
        m1 = jnp.max(logits, axis=-1, keepdims=True)
        i1 = jnp.min(jnp.where(logits == m1, lane, LANES), axis=-1, keepdims=True)
        rest = jnp.where(lane == i1, -jnp.inf, logits)
        m2 = jnp.max(rest, axis=-1, keepdims=True)
        i2 = jnp.min(jnp.where(rest == m2, lane, LANES), axis=-1, keepdims=True)
        e2 = jnp.exp(m2 - m1)
        g1 = 1.0 / (1.0 + e2)
        g2 = e2 / (1.0 + e2)
        route = jnp.where(lane == 0, i1.astype(F32),
                          jnp.where(lane == 1, i2.astype(F32),
                                    jnp.where(lane == 2, g1, jnp.where(lane == 3, g2, 0.0))))
        route_ref[...] = route
    else:
        h_ref[...] = h.astype(h_ref.dtype)


def _mod_view(mod):
    return mod.reshape(mod.shape[0], 6, 1, mod.shape[1] // 6)


def _resnorm(x, y, gate_mod, gate_k, mod, gain, sh_k, sc_k, *, rows, seq, n_batch, router=None):
    tr = 512
    has_res = y is not None
    with_router = router is not None
    split = isinstance(x, tuple)
    d = x[0].shape[1] if split else x.shape[1]

    def grp(i):
        return jnp.minimum((i * tr) // seq, n_batch)

    def modspec(k):
        return pl.BlockSpec((1, 1, 1, d), lambda i: (grp(i), k, 0, 0))

    row = pl.BlockSpec((tr, d), lambda i: (i, 0))
    lat_tiles = None
    if split:
        lat_tiles = x[0].shape[0] // tr
        in_specs = [pl.BlockSpec((tr, d), lambda i: (jnp.minimum(i, lat_tiles - 1), 0)),
                    pl.BlockSpec((tr, d), lambda i: (jnp.maximum(i - lat_tiles, 0), 0))]
        args = list(x)
    else:
        in_specs = [row]
        args = [x]
    if has_res:
        in_specs += [row, modspec(gate_k)]
        args += [y, _mod_view(gate_mod)]
    in_specs += [pl.BlockSpec((1, d), lambda i: (0, 0)), modspec(sh_k), modspec(sc_k)]
    args += [gain.reshape(1, d).astype(F32), _mod_view(mod), _mod_view(mod)]
    out_shape, out_specs = [], []
    n_exp = 0
    if with_router:
        n_exp = router.shape[1]
        router = jnp.pad(router.astype(F32), ((0, 0), (0, LANES - n_exp)))
        r_hi = router.astype(BF16)
        router = jnp.stack([r_hi, (router - r_hi.astype(F32)).astype(BF16)])
        in_specs.append(pl.BlockSpec(router.shape, lambda i: (0, 0, 0)))
        args.append(router)
    if has_res:
        out_shape.append(jax.ShapeDtypeStruct((rows, d), F32))
        out_specs.append(row)
    if with_router:
        out_shape += [jax.ShapeDtypeStruct((rows, d // 2), jnp.uint32), jax.ShapeDtypeStruct((rows, LANES), F32)]
        out_specs += [pl.BlockSpec((tr, d // 2), lambda i: (i, 0)), pl.BlockSpec((tr, LANES), lambda i: (i, 0))]
    else:
        out_shape.append(jax.ShapeDtypeStruct((rows, d), BF16))
        out_specs.append(row)
    kern = functools.partial(_resnorm_kernel, has_res=has_res, with_router=with_router, n_exp=n_exp,
                             lat_tiles=lat_tiles)
    return pl.pallas_call(
        kern, grid=(rows // tr,), in_specs=in_specs, out_specs=out_specs, out_shape=out_shape,
        compiler_params=_cp(("arbitrary",), VMEM_LIMIT), name="resnorm",
    )(*args)


def _mm_kernel(*refs, n_x):
    w_ref, o_ref = refs[n_x], refs[n_x + 1]
    if n_x == 1:
        x = refs[0][...]
    else:
        x = jnp.concatenate([r[...] for r in refs[:n_x]], axis=1)
    o_ref[...] = jnp.dot(x, w_ref[...], preferred_element_type=F32).astype(o_ref.dtype)


def _mm(xs, w, *, rows, tm, tn, out_dtype, name):
    n = w.shape[1]
    in_specs = [pl.BlockSpec((tm, a.shape[1]), lambda i, j: (i, 0)) for a in xs]
    in_specs.append(pl.BlockSpec((w.shape[0], tn), lambda i, j: (0, j)))
    return pl.pallas_call(
        functools.partial(_mm_kernel, n_x=len(xs)),
        grid=(rows // tm, n // tn),
        in_specs=in_specs,
        out_specs=pl.BlockSpec((tm, tn), lambda i, j: (i, j)),
        out_shape=jax.ShapeDtypeStruct((rows, n), out_dtype),
        compiler_params=_cp(("arbitrary", "arbitrary"), VMEM_LIMIT), name=name,
    )(*xs, w)


def _group_meansq(x, g_ref, width):
    x2 = x * x
    hi = x2.astype(BF16)
    lo = (x2 - hi.astype(F32)).astype(BF16)
    g = g_ref[:width, :width]
    ss = jnp.dot(hi, g, preferred_element_type=F32) + jnp.dot(lo, g, preferred_element_type=F32)
    return ss * (1.0 / HEAD_DIM)


def _prep_kernel(p_ref, dv_ref, ch_ref, sh_ref, g_ref, gains_ref, o_ref, dv2_ref):
    tr = p_ref.shape[0]

    def seg(off, width):
        return p_ref[:, off:off + width].astype(F32)

    def norm(x, row, width):
        return x * lax.rsqrt(_group_meansq(x, g_ref, width) + EPS) * gains_ref[row:row + 1, :width]

    def rope(x, width):
        lane = lax.broadcasted_iota(jnp.int32, (tr, width), 1)
        first = (lane % HEAD_DIM) < (HEAD_DIM // 2)
        cos = jnp.concatenate([ch_ref[...]] * (width // LANES), axis=1)
        sin = jnp.concatenate([sh_ref[...]] * (width // LANES), axis=1)
        swapped = jnp.where(first, pltpu.roll(x, width - HEAD_DIM // 2, 1), pltpu.roll(x, HEAD_DIM // 2, 1))
        return x * cos + swapped * sin

    scale = HEAD_DIM ** -0.5
    scale2 = scale * LOG2E
    o_ref[:, P_AQ:P_AQ + 512] = (rope(norm(seg(P_AQ, 512), 0, 512), 512) * scale2).astype(BF16)
    o_ref[:, P_CQ:P_CQ + 512] = (rope(seg(P_CQ, 512), 512) * scale).astype(BF16)
    o_ref[:, P_CK:P_CK + 512] = rope(seg(P_CK, 512), 512).astype(BF16)
    o_ref[:, P_DQ:P_DQ + 512] = (norm(seg(P_DQ, 512), 2, 512) * scale2).astype(BF16)
    o_ref[:, P_DK:P_DK + 512] = norm(seg(P_DK, 512), 3, 512).astype(BF16)
    o_ref[:, P_AK:P_AK + 256] = rope(norm(seg(P_AK, 256), 1, 256), 256).astype(BF16)
    ones_blk = jnp.where(lax.broadcasted_iota(jnp.int32, (tr, LANES), 1) < HEAD_DIM, 1.0, 0.0).astype(BF16)
    for g in range(D_HEADS // 2):
        dv2_ref[:, 2 * g * LANES:(2 * g + 1) * LANES] = dv_ref[:, g * LANES:(g + 1) * LANES]
        dv2_ref[:, (2 * g + 1) * LANES:(2 * g + 2) * LANES] = ones_blk


def _prep(proj, cos_h, sin_h, gmat, gains, *, rows, seq, n_lat):
    tr = PREP_ROWS
    nlat = seq // tr

    def tab(i):
        return (jnp.where(i * tr < n_lat, i % nlat, nlat), 0)

    return pl.pallas_call(
        _prep_kernel,
        grid=(rows // tr,),
        in_specs=[pl.BlockSpec((tr, PREP_W), lambda i: (i, 0)),
                  pl.BlockSpec((tr, 512), lambda i: (i, P_DV // 512)),
                  pl.BlockSpec((tr, LANES), tab),
                  pl.BlockSpec((tr, LANES), tab),
                  pl.BlockSpec((512, 512), lambda i: (0, 0)),
                  pl.BlockSpec((4, 512), lambda i: (0, 0))],
        out_specs=[pl.BlockSpec((tr, PREP_W), lambda i: (i, 0)),
                   pl.BlockSpec((tr, 2 * 512), lambda i: (i, 0))],
        out_shape=[jax.ShapeDtypeStruct((rows, PREP_W), BF16), jax.ShapeDtypeStruct((rows, 2 * 512), BF16)],
        compiler_params=_cp(("arbitrary",), VMEM_LIMIT), name="prep",
    )(proj, proj, cos_h, sin_h, gmat, gains)


def _prepb_kernel(bq_ref, bkv_ref, bkr_ref, bkrs_ref, cb_ref, sb_ref, wuq_ref, wuqs_ref, wukv_ref, gains_ref,
                  q_ref, k_ref, v_ref):
    tr = bq_ref.shape[0]
    cq = bq_ref[...].astype(F32)
    cqn = (cq * lax.rsqrt(jnp.mean(cq * cq, axis=-1, keepdims=True) + EPS) * gains_ref[0:1, :]).astype(BF16)
    qup = jnp.dot(cqn, wuq_ref[...], preferred_element_type=F32)
    qup_sw = jnp.dot(cqn, wuqs_ref[...], preferred_element_type=F32)
    ckv = bkv_ref[...].astype(F32)
    ckvn = ckv * lax.rsqrt(jnp.mean(ckv * ckv, axis=-1, keepdims=True) + EPS) * gains_ref[1:2, :LANES]
    kvup = jnp.dot(ckvn.astype(BF16), wukv_ref[...], preferred_element_type=F32)
    kr = bkr_ref[...].astype(F32)
    cb, sb = cb_ref[...], sb_ref[...]
    lane = lax.broadcasted_iota(jnp.int32, (tr, LANES), 1)
    q_cos = cb * (gains_ref[2:3, :LANES] * MLA_LOGIT_SCALE)
    q_sin = sb * (gains_ref[5:6, :LANES] * MLA_LOGIT_SCALE)
    k_cos = cb * gains_ref[3:4, :LANES]
    k_rot = bkrs_ref[...].astype(F32) * (sb * gains_ref[6:7, :LANES])

    def inv_rms(x):
        return lax.rsqrt(jnp.sum(x * x, axis=-1, keepdims=True) * (1.0 / B_QK) + EPS)

    shift_lane = lane == B_QK
    neg_bound = gains_ref[4:5, :LANES]
    for h in range(B_HEADS):
        sl = slice(h * LANES, (h + 1) * LANES)
        x = qup[:, sl]
        q = (x * q_cos + qup_sw[:, sl] * q_sin) * inv_rms(x)
        q_ref[:, sl] = jnp.where(shift_lane, 1.0, q).astype(BF16)
        x = kvup[:, sl] + kr
        k = (x * k_cos + k_rot) * inv_rms(x)
        k_ref[:, sl] = jnp.where(shift_lane, neg_bound, k).astype(BF16)
    ones_blk = jnp.where(lane < B_V, 1.0, 0.0).astype(BF16)
    for g in range(B_HEADS // 2):
        v_ref[:, 2 * g * LANES:(2 * g + 1) * LANES] = kvup[:, (B_HEADS + g) * LANES:(B_HEADS + g + 1) * LANES].astype(BF16)
        v_ref[:, (2 * g + 1) * LANES:(2 * g + 2) * LANES] = ones_blk


def _prepb(proj, cos_b, sin_b, wuq, wuq_sw, wukv, gains, *, rows, seq, n_lat):
    tr = PREP_ROWS
    nlat = seq // tr

    def tab(i):
        return (jnp.where(i * tr < n_lat, i % nlat, nlat), 0)

    return pl.pallas_call(
        _prepb_kernel,
        grid=(rows // tr,),
        in_specs=[pl.BlockSpec((tr, B_Q_RANK), lambda i: (i, P_BQ // B_Q_RANK)),
                  pl.BlockSpec((tr, LANES), lambda i: (i, P_BKV // LANES)),
                  pl.BlockSpec((tr, LANES), lambda i: (i, P_BKR // LANES)),
                  pl.BlockSpec((tr, LANES), lambda i: (i, P_BKR_SW // LANES)),
                  pl.BlockSpec((tr, LANES), tab),
                  pl.BlockSpec((tr, LANES), tab),
                  pl.BlockSpec(wuq.shape, lambda i: (0, 0)),
                  pl.BlockSpec(wuq.shape, lambda i: (0, 0)),
                  pl.BlockSpec(wukv.shape, lambda i: (0, 0)),
                  pl.BlockSpec((8, B_Q_RANK), lambda i: (0, 0))],
        out_specs=[pl.BlockSpec((tr, B_HEADS * LANES), lambda i: (i, 0)),
                   pl.BlockSpec((tr, B_HEADS * LANES), lambda i: (i, 0)),
                   pl.BlockSpec((tr, B_HEADS * LANES), lambda i: (i, 0))],
        out_shape=[jax.ShapeDtypeStruct((rows, B_HEADS * LANES), BF16),
                   jax.ShapeDtypeStruct((rows, B_HEADS * LANES), BF16),
                   jax.ShapeDtypeStruct((rows, B_HEADS * LANES), BF16)],
        compiler_params=_cp(("arbitrary",), VMEM_LIMIT), name="prep_mla",
    )(proj, proj, proj, proj, cos_b, sin_b, wuq, wuq_sw, wukv, gains)


def _wina_kernel(par_ref, q_ref, kp_ref, ko_ref, kn_ref, kc_ref, vp_ref, vo_ref, vn_ref, vc_ref, o_ref,
                 *, nb, seq):
    n = pl.program_id(1)
    blk = A_BLOCK
    rep = A_HEADS // A_KV_HEADS
    n_ctx = kc_ref.shape[0]
    kcat = jnp.concatenate([kp_ref[...], ko_ref[...], kn_ref[...], kc_ref[...]], axis=0)
    vcat = jnp.concatenate([vp_ref[...], vo_ref[...], vn_ref[...], vc_ref[...]], axis=0)
    n_keys = 3 * blk + n_ctx
    qi = lax.broadcasted_iota(jnp.int32, (blk, n_keys), 0)
    kj = lax.broadcasted_iota(jnp.int32, (blk, n_keys), 1)
    band = kj - blk
    kpos = n * blk + band
    valid = ((jnp.abs(qi - band) <= blk) & (kpos >= 0) & (kpos < seq) & (n < nb)) | (kj >= 3 * blk)
    low = lax.broadcasted_iota(jnp.int32, (blk, LANES), 1) < HEAD_DIM
    low_k = lax.broadcasted_iota(jnp.int32, (n_keys, LANES), 1) < HEAD_DIM
    bound = par_ref[A_HEADS]
    safe = par_ref[A_HEADS + 1] > 0.0

    def attend(use_max):
        mask = jnp.where(valid, 0.0 if use_max else -bound, NEG_INF)
        mask = jnp.concatenate([mask] * rep, axis=0)
        for g in range(A_KV_HEADS):
            kg = kcat[:, g * LANES:(g + 1) * LANES]
            v1 = jnp.where(low_k, vcat[:, g * LANES:(g + 1) * LANES], jnp.ones((), BF16))
            qs, sinks = [], []
            for r in range(rep):
                h = g * rep + r
                qc = q_ref[:, (h // 2) * LANES:(h // 2 + 1) * LANES]
                qs.append(jnp.where(low if h % 2 == 0 else jnp.logical_not(low), qc, jnp.zeros_like(qc)))
                sinks.append(jnp.full((blk, 1), par_ref[h], F32))
            s = _nt(jnp.concatenate(qs, axis=0), kg) + mask
            sink = jnp.concatenate(sinks, axis=0)
            if use_max:
                m = jnp.maximum(jnp.max(s, axis=-1, keepdims=True), sink)
                s, sink = s - m, sink - m
            else:
                sink = sink - bound
            o = jnp.dot(jnp.exp2(s).astype(BF16), v1, preferred_element_type=F32)
            res = o / (o[:, HEAD_DIM:HEAD_DIM + 1] + jnp.exp2(sink))
            for c in range(rep // 2):
                even = res[(2 * c) * blk:(2 * c + 1) * blk]
                odd = pltpu.roll(res[(2 * c + 1) * blk:(2 * c + 2) * blk], HEAD_DIM, 1)
                j = (g * rep) // 2 + c
                o_ref[:, j * LANES:(j + 1) * LANES] = jnp.where(low, even, odd).astype(o_ref.dtype)

    @pl.when(safe)
    def _():
        attend(False)

    @pl.when(jnp.logical_not(safe))
    def _():
        attend(True)


def _wina(prep, proj, sink, *, n_batch, seq, n_ctx, with_ctx):
    blk = A_BLOCK
    nb = seq // blk
    ncb = n_ctx // blk
    steps = nb + (ncb if with_ctx else 0)
    lat_blocks = n_batch * nb
    rows_out = n_batch * seq + (n_batch * n_ctx if with_ctx else 0)

    def qmap(b, n, s):
        return (jnp.where(n < nb, b * nb + n, lat_blocks + b * ncb + (n - nb)), 0)

    def kmap(delta, col):
        def f(b, n, s):
            return (b * nb + jnp.clip(n + delta, 0, nb - 1), col)
        return f

    def cmap(col):
        def f(b, n, s):
            return (n_batch * seq // n_ctx + b, col)
        return f

    kcol, vcol = P_AK // 256, P_AV // 256
    in_specs = [pl.BlockSpec((blk, 512), qmap)]
    in_specs += [pl.BlockSpec((blk, 256), kmap(dl, kcol)) for dl in (-1, 0, 1)]
    in_specs += [pl.BlockSpec((n_ctx, 256), cmap(kcol))]
    in_specs += [pl.BlockSpec((blk, 256), kmap(dl, vcol)) for dl in (-1, 0, 1)]
    in_specs += [pl.BlockSpec((n_ctx, 256), cmap(vcol))]
    return pl.pallas_call(
        functools.partial(_wina_kernel, nb=nb, seq=seq),
        grid_spec=pltpu.PrefetchScalarGridSpec(
            num_scalar_prefetch=1, grid=(n_batch, steps), in_specs=in_specs,
            out_specs=pl.BlockSpec((blk, 512), qmap)),
        out_shape=jax.ShapeDtypeStruct((rows_out, 512), BF16),
        compiler_params=_cp(("arbitrary", "arbitrary"), VMEM_LIMIT), name="mixer_window",
    )(sink, prep, prep, prep, prep, prep, proj, proj, proj, proj)


def _mla_kernel(safe_ref, q_ref, kl_ref, vl_ref, kc_ref, vc_ref, o_ref, *, nq, tk_fast, tk_exact, hps):
    tq = q_ref.shape[0]
    seq = kl_ref.shape[0]
    is_lat = pl.program_id(2) < nq
    safe = safe_ref[0] > 0
    pair_w = 2 * LANES
    lane = lax.broadcasted_iota(jnp.int32, (tq, LANES), 1)

    def q(h):
        return q_ref[:, h * LANES:(h + 1) * LANES]

    def write(nums, dens):
        for g in range(hps // 2):
            o_ref[:, g * LANES:(g + 1) * LANES] = jnp.where(
                lane < B_V, nums[2 * g] / dens[2 * g], nums[2 * g + 1] / dens[2 * g + 1]).astype(o_ref.dtype)

    def fast(chunks):
        accs = [None] * hps
        for k_ref, v_ref, off, n in chunks:
            for h in range(hps):
                g = h // 2
                p = jnp.exp2(_nt(q(h), k_ref[off:off + n, h * LANES:(h + 1) * LANES])).astype(BF16)
                d = jnp.dot(p, v_ref[off:off + n, g * pair_w:(g + 1) * pair_w], preferred_element_type=F32)
                accs[h] = d if accs[h] is None else accs[h] + d
        write([a[:, :LANES] for a in accs], [a[:, LANES:LANES + 1] for a in accs])

    ctx_chunk = (kc_ref, vc_ref, 0, kc_ref.shape[0])

    @pl.when(safe & is_lat)
    def _():
        fast([(kl_ref, vl_ref, c * tk_fast, tk_fast) for c in range(seq // tk_fast)] + [ctx_chunk])

    @pl.when(safe & jnp.logical_not(is_lat))
    def _():
        fast([ctx_chunk])

    @pl.when(jnp.logical_not(safe))
    def _():
        def step(kf, vf, carry):
            out = []
            for h in range(hps):
                m, l, acc = carry[h]
                s = _nt(q(h), kf(h))
                m_new = jnp.maximum(m, jnp.max(s, axis=-1, keepdims=True))
                a = jnp.exp2(m - m_new)
                p = jnp.exp2(s - m_new)
                l = a * l + jnp.sum(p, axis=-1, keepdims=True)
                acc = a * acc + jnp.dot(p.astype(BF16), vf(h // 2), preferred_element_type=F32)
                out.append((m_new, l, acc))
            return tuple(out)

        def body(c, carry):
            off = pl.multiple_of(c * tk_exact, tk_exact)
            return step(lambda h: kl_ref[pl.ds(off, tk_exact), h * LANES:(h + 1) * LANES],
                        lambda g: vl_ref[pl.ds(off, tk_exact), g * pair_w:g * pair_w + LANES], carry)

        init = tuple((jnp.full((tq, 1), -jnp.inf, F32), jnp.zeros((tq, 1), F32), jnp.zeros((tq, LANES), F32))
                     for _ in range(hps))
        carry = lax.fori_loop(0, jnp.where(is_lat, seq // tk_exact, 0), body, init)
        carry = step(lambda h: kc_ref[:, h * LANES:(h + 1) * LANES],
                     lambda g: vc_ref[:, g * pair_w:g * pair_w + LANES], carry)
        write([c[2] for c in carry], [c[1] for c in carry])


def _mla_logit_bound(q_gain, k_gain):
    b = B_QK * MLA_LOGIT_SCALE * jnp.max(jnp.abs(q_gain)) * jnp.max(jnp.abs(k_gain))
    return (1.02 * b + 0.5).astype(F32)


def _mla(qb, kb, vb, safe, *, n_batch, seq, n_ctx, with_ctx):
    hps = 4
    tq = 256
    assert n_ctx == tq
    nq = seq // tq
    ctx_blk = n_batch * seq // n_ctx
    rows_out = n_batch * seq + (n_batch * n_ctx if with_ctx else 0)
    kw, vw = hps * LANES, hps * B_V

    def qmap(b, j, i, s):
        return (jnp.where(i < nq, b * nq + i, ctx_blk + b), j)

    return pl.pallas_call(
        functools.partial(_mla_kernel, nq=nq, tk_fast=min(2048, seq), tk_exact=512, hps=hps),
        grid_spec=pltpu.PrefetchScalarGridSpec(
            num_scalar_prefetch=1,
            grid=(n_batch, B_HEADS // hps, nq + (1 if with_ctx else 0)),
            in_specs=[pl.BlockSpec((tq, kw), qmap),
                      pl.BlockSpec((seq, kw), lambda b, j, i, s: (b, j)),
                      pl.BlockSpec((seq, kw), lambda b, j, i, s: (b, j)),
                      pl.BlockSpec((n_ctx, kw), lambda b, j, i, s: (ctx_blk + b, j)),
                      pl.BlockSpec((n_ctx, kw), lambda b, j, i, s: (ctx_blk + b, j))],
            out_specs=pl.BlockSpec((tq, vw), qmap)),
        out_shape=jax.ShapeDtypeStruct((rows_out, 512), BF16),
        compiler_params=_cp(("arbitrary", "arbitrary", "arbitrary"), VMEM_LIMIT), name="mixer_mla",
    )(safe, qb, kb, vb, kb, vb)


def _ret_kernel(qf_ref, kf_ref, vf_ref, qr_ref, kr_ref, vr_ref, ld_ref, of_ref, or_ref,
                s_ref, dec_ref, qw_ref, kw_ref, cd_ref):
    step = pl.program_id(1)
    cc = qf_ref.shape[0]
    low = lax.broadcasted_iota(jnp.int32, (cc, LANES), 1) < HEAD_DIM
    sr = lax.broadcasted_iota(jnp.int32, (LANES, LANES), 0)
    blockdiag = (sr < HEAD_DIM) == (lax.broadcasted_iota(jnp.int32, (LANES, LANES), 1) < HEAD_DIM)

    @pl.when(step == 0)
    def _():
        s_ref[...] = jnp.zeros_like(s_ref)
        qi = lax.broadcasted_iota(jnp.int32, (cc, cc), 0)
        ki = lax.broadcasted_iota(jnp.int32, (cc, cc), 1)
        pos = lax.broadcasted_iota(jnp.int32, (cc, LANES), 0).astype(F32)
        for d in range(2):
            diff = (qi - ki) if d == 0 else (ki - qi)
            dpos = jnp.maximum(diff, 0).astype(F32)
            qpow = (pos + 1.0) if d == 0 else (cc - pos)
            kpow = (cc - 1.0 - pos) if d == 0 else pos
            for j in range(C_HEADS // 2):
                lg_e = -jnp.exp(ld_ref[d, 2 * j:2 * j + 1, :])
                lg_o = -jnp.exp(ld_ref[d, 2 * j + 1:2 * j + 2, :])
                lgl = jnp.where(low[0:1, :], lg_e, lg_o)
                wide_e = jnp.concatenate([lg_e] * (cc // LANES), axis=1)
                wide_o = jnp.concatenate([lg_o] * (cc // LANES), axis=1)
                dec_ref[d, j, :cc] = jnp.where(diff >= 0, jnp.exp(wide_e * dpos), 0.0)
                dec_ref[d, j, cc:] = jnp.where(diff >= 0, jnp.exp(wide_o * dpos), 0.0)
                qw_ref[d, j] = jnp.exp(lgl * qpow)
                kw_ref[d, j] = jnp.exp(lgl * kpow)
                cd_ref[d, j] = jnp.where(sr < HEAD_DIM, jnp.exp(lg_e * cc), jnp.exp(lg_o * cc))

    pairs = C_HEADS // 2
    dirs = ((qf_ref, kf_ref, vf_ref, of_ref), (qr_ref, kr_ref, vr_ref, or_ref))
    states = [[s_ref[d, j] for j in range(pairs)] for d in range(2)]
    outs, new_states = [], []
    for d, (q_ref, k_ref, v_ref, _) in enumerate(dirs):
        for j in range(pairs):
            sl = slice(j * LANES, (j + 1) * LANES)
            q, k, v = q_ref[:, sl], k_ref[:, sl], v_ref[:, sl]
            zero = jnp.zeros_like(q)
            q2 = jnp.concatenate([jnp.where(low, q, zero), jnp.where(low, zero, q)], axis=0)
            sc = (_nt(q2, k) * dec_ref[d, j]).astype(BF16)
            o2 = jnp.dot(sc, v, preferred_element_type=F32)
            o_intra = jnp.where(low, o2[:cc], o2[cc:])
            state = states[d][j]
            qw = (q.astype(F32) * qw_ref[d, j]).astype(BF16)
            o_cross = jnp.dot(qw, state.astype(BF16), preferred_element_type=F32)
            kw = (k.astype(F32) * kw_ref[d, j]).astype(BF16)
            kv = lax.dot_general(kw, v, (((0,), (0,)), ((), ())), preferred_element_type=F32)
            new_states.append(state * cd_ref[d, j] + jnp.where(blockdiag, kv, 0.0))
            outs.append(o_intra + o_cross)
    for d in range(2):
        for j in range(pairs):
            s_ref[d, j] = new_states[d * pairs + j]
            dirs[d][3][:, j * LANES:(j + 1) * LANES] = outs[d * pairs + j].astype(BF16)


def _retention(prep, proj, ld_head, *, n_batch, seq, n_ctx):
    cc = RET_CHUNK
    nl, nc = seq // cc, n_ctx // cc
    steps = nc + nl
    lat_blocks = n_batch * nl
    rows = n_batch * (seq + n_ctx)

    def rowblk(b, d, s):
        c_ctx = jnp.where(d == 0, s, nc - 1 - s)
        c_lat = jnp.where(d == 0, s - nc, nl - 1 - (s - nc))
        return jnp.where(s < nc, lat_blocks + b * nc + c_ctx, b * nl + c_lat)

    def spec(d, col):
        return pl.BlockSpec((cc, 512), lambda b, s: (rowblk(b, d, s), col))

    pairs = C_HEADS // 2
    return pl.pallas_call(
        _ret_kernel,
        grid=(n_batch, steps),
        in_specs=[spec(d, col) for d in (0, 1) for col in (P_CQ // 512, P_CK // 512, P_CV // 512)]
        + [pl.BlockSpec((2, C_HEADS, LANES), lambda b, s: (0, 0, 0))],
        out_specs=[spec(0, 0), spec(1, 0)],
        out_shape=[jax.ShapeDtypeStruct((rows, 512), BF16)] * 2,
        scratch_shapes=[pltpu.VMEM((2, pairs, LANES, LANES), F32),
                        pltpu.VMEM((2, pairs, 2 * cc, cc), F32),
                        pltpu.VMEM((2, pairs, cc, LANES), F32),
                        pltpu.VMEM((2, pairs, cc, LANES), F32),
                        pltpu.VMEM((2, pairs, LANES, LANES), F32)],
        compiler_params=_cp(("arbitrary", "arbitrary"), VMEM_LIMIT), name="mixer_retention",
    )(prep, prep, proj, prep, prep, proj, ld_head)


def _retfin_kernel(of_ref, or_ref, g_ref, gm_ref, out_ref):
    o = of_ref[...].astype(F32) + or_ref[...].astype(F32)
    on = o * lax.rsqrt(_group_meansq(o, gm_ref, 512) + EPS)
    g = g_ref[...].astype(F32)
    out_ref[...] = (g * jax.nn.sigmoid(g) * on).astype(out_ref.dtype)


def _retention_finish(o_fwd, o_rev, proj, gmat, *, rows):
    tr = 512
    return pl.pallas_call(
        _retfin_kernel,
        grid=(rows // tr,),
        in_specs=[pl.BlockSpec((tr, 512), lambda i: (i, 0)),
                  pl.BlockSpec((tr, 512), lambda i: (i, 0)),
                  pl.BlockSpec((tr, 512), lambda i: (i, P_CG // 512)),
                  pl.BlockSpec((512, 512), lambda i: (0, 0))],
        out_specs=pl.BlockSpec((tr, 512), lambda i: (i, 0)),
        out_shape=jax.ShapeDtypeStruct((rows, 512), BF16),
        compiler_params=_cp(("arbitrary",), VMEM_LIMIT), name="retention_finish",
    )(o_fwd, o_rev, proj, gmat)


def _nbr_kernel(par_ref, q_ref, k_ref, v_ref, kc_ref, vc_ref, abm_ref, o_ref, *, rows):
    st = pl.program_id(1)
    start = jnp.clip(NA_QROWS * st - NA_ROWS // 2, 0, rows - NA_KROWS)
    off = pl.multiple_of(start * GRID_W, GRID_W)
    nk = NA_KROWS * GRID_W
    tq = q_ref.shape[0]
    low = lax.broadcasted_iota(jnp.int32, (tq, LANES), 1) < HEAD_DIM
    bound = par_ref[0]
    safe = par_ref[1] > 0.0
    pair_w = 2 * LANES

    def attend(use_max):
        for j in range(D_HEADS // 2):
            sl = slice(j * LANES, (j + 1) * LANES)
            sl2 = slice(j * pair_w, (j + 1) * pair_w)
            qc = q_ref[:, sl]
            zero = jnp.zeros_like(qc)
            q2 = jnp.concatenate([jnp.where(low, qc, zero), jnp.where(low, zero, qc)], axis=0)
            bias = jnp.concatenate([abm_ref[0, 2 * j], abm_ref[0, 2 * j + 1]], axis=0)
            s1 = _nt(q2, k_ref[pl.ds(off, nk), sl]) + bias
            s2 = _nt(q2, kc_ref[:, sl]) - bound
            if use_max:
                m = jnp.maximum(jnp.max(s1, axis=-1, keepdims=True), jnp.max(s2, axis=-1, keepdims=True))
                s1, s2 = s1 - m, s2 - m
            o = (jnp.dot(jnp.exp2(s1).astype(BF16), v_ref[pl.ds(off, nk), sl2], preferred_element_type=F32)
                 + jnp.dot(jnp.exp2(s2).astype(BF16), vc_ref[:, sl2], preferred_element_type=F32))
            res = o[:, :LANES] / o[:, LANES:LANES + 1]
            o_ref[:, sl] = jnp.where(low, res[:tq], res[tq:]).astype(o_ref.dtype)

    @pl.when(safe)
    def _():
        attend(False)

    @pl.when(jnp.logical_not(safe))
    def _():
        attend(True)


def _nbr(prep, dv2, abm, par, *, n_batch, seq, n_ctx, with_ctx):
    rows = seq // GRID_W
    tq = NA_QROWS * GRID_W
    assert n_ctx == tq and rows % NA_QROWS == 0 and rows >= NA_KROWS + NA_QROWS
    nst = rows // NA_QROWS
    steps = nst + (1 if with_ctx else 0)
    rows_out = n_batch * seq + (n_batch * n_ctx if with_ctx else 0)

    def qmap(col):
        def f(b, s, par):
            return (jnp.where(s < nst, b * nst + s, n_batch * nst + b), col)
        return f

    def case(b, s, par):
        c = jnp.where(s == 0, 0, jnp.where(s == nst - 1, 2, jnp.where(s == nst, 3, 1)))
        return (c, 0, 0, 0)

    ctx_blk = n_batch * seq // n_ctx
    return pl.pallas_call(
        functools.partial(_nbr_kernel, rows=rows),
        grid_spec=pltpu.PrefetchScalarGridSpec(
            num_scalar_prefetch=1, grid=(n_batch, steps),
            in_specs=[pl.BlockSpec((tq, 512), qmap(P_DQ // 512)),
                      pl.BlockSpec((seq, 512), lambda b, s, par: (b, P_DK // 512)),
                      pl.BlockSpec((seq, 2 * 512), lambda b, s, par: (b, 0)),
                      pl.BlockSpec((n_ctx, 512), lambda b, s, par: (ctx_blk + b, P_DK // 512)),
                      pl.BlockSpec((n_ctx, 2 * 512), lambda b, s, par: (ctx_blk + b, 0)),
                      pl.BlockSpec((1, D_HEADS, tq, NA_KROWS * GRID_W), case)],
            out_specs=pl.BlockSpec((tq, 512), qmap(0))),
        out_shape=jax.ShapeDtypeStruct((rows_out, 512), BF16),
        compiler_params=_cp(("arbitrary", "arbitrary"), VMEM_LIMIT), name="mixer_neighbourhood",
    )(par, prep, prep, dv2, prep, dv2, abm)


def _softmax_bound(q_gain, k_gain, extra):
    b = HEAD_DIM ** 0.5 * LOG2E * jnp.max(jnp.abs(q_gain)) * jnp.max(jnp.abs(k_gain)) + LOG2E * extra
    return (1.02 * b + 0.5).astype(F32)


def _nbr_bias_tables(rpb, rows, bound):
    w = GRID_W
    cidx = np.arange(w)
    col_start = np.clip(cidx - NA_COLS // 2, 0, w - NA_COLS)
    col_ok = (cidx[None, :] >= col_start[:, None]) & (cidx[None, :] < col_start[:, None] + NA_COLS)
    d_col = np.clip(cidx[None, :] - cidx[:, None] + (NA_COLS - 1), 0, 2 * NA_COLS - 2)
    n_heads, n_dr, n_dc = rpb.shape
    onehot = jnp.asarray((d_col[:, :, None] == np.arange(n_dc)[None, None, :]).astype(np.float32))
    t = jnp.einsum("hrd,qkd->hqrk", rpb.astype(F32), onehot, precision=lax.Precision.HIGHEST)
    t = t * LOG2E - bound
    t = jnp.where(jnp.asarray(col_ok)[None, :, None, :], t, NEG_INF).reshape(n_heads, w, n_dr * w)
    cases = []
    for r0 in (0, NA_QROWS, rows - NA_QROWS):
        start = int(np.clip(r0 - NA_ROWS // 2, 0, rows - NA_KROWS))
        blocks = []
        for i in range(NA_QROWS):
            r = r0 + i
            row_start = int(np.clip(r - NA_ROWS // 2, 0, rows - NA_ROWS))
            kk0 = row_start - start
            dr0 = row_start - r + (NA_ROWS - 1)
            blk = t[:, :, dr0 * w:(dr0 + NA_ROWS) * w]
            blocks.append(jnp.pad(blk, ((0, 0), (0, 0), (kk0 * w, (NA_KROWS - NA_ROWS - kk0) * w)),
                                  constant_values=NEG_INF))
        cases.append(jnp.concatenate(blocks, axis=1))
    cases.append(jnp.full((n_heads, NA_QROWS * w, NA_KROWS * w), NEG_INF, F32))
    return jnp.stack(cases)


def _swiglu_accumulate(x_ref, wg_ref, wu_ref, wd_ref, acc_ref, n_rows):
    x = x_ref[:n_rows]
    g = jnp.dot(x, wg_ref[0].astype(BF16), preferred_element_type=F32)
    u = jnp.dot(x, wu_ref[0].astype(BF16), preferred_element_type=F32)
    a = (g * jax.nn.sigmoid(g) * u).astype(BF16)
    acc_ref[:n_rows] += jnp.dot(a, wd_ref[0].astype(BF16), preferred_element_type=F32)


def _ffn_kernel(x_ref, wg_ref, wu_ref, wd_ref, o_ref, acc_ref):
    j = pl.program_id(1)

    @pl.when(j == 0)
    def _():
        acc_ref[...] = jnp.zeros_like(acc_ref)

    _swiglu_accumulate(x_ref, wg_ref, wu_ref, wd_ref, acc_ref, x_ref.shape[0])

    @pl.when(j == pl.num_programs(1) - 1)
    def _():
        o_ref[...] = acc_ref[...].astype(o_ref.dtype)


def _ffn(x, w_gate_up, w_down, layer, *, rows, tm, fc):
    _, d, ff2 = w_gate_up.shape
    nj = ff2 // 2 // fc
    return pl.pallas_call(
        _ffn_kernel,
        grid=(rows // tm, nj),
        in_specs=[pl.BlockSpec((tm, d), lambda i, j: (i, 0)),
                  pl.BlockSpec((1, d, fc), lambda i, j: (layer, 0, j)),
                  pl.BlockSpec((1, d, fc), lambda i, j: (layer, 0, nj + j)),
                  pl.BlockSpec((1, fc, d), lambda i, j: (layer, j, 0))],
        out_specs=pl.BlockSpec((tm, d), lambda i, j: (i, 0)),
        out_shape=jax.ShapeDtypeStruct((rows, d), BF16),
        scratch_shapes=[pltpu.VMEM((tm, d), F32)],
        compiler_params=_cp(("arbitrary", "arbitrary"), VMEM_LIMIT), name="swiglu",
    )(x, w_gate_up, w_gate_up, w_down)


def _moe_ffn_kernel(te_ref, tv_ref, tok_ref, hp_ref, wg_ref, wu_ref, wd_ref, o_ref, gbuf_ref, x_ref, acc_ref, sem,
                    *, rows_per_step):
    i, j = pl.program_id(0), pl.program_id(1)
    n_tiles = pl.num_programs(0)
    tm, d = x_ref.shape
    half = d // 2
    valid = tv_ref[i] > 0
    slot = i % 2

    def issue(tile, first_row, n_rows, to_slot):
        def body(r, carry):
            row = first_row + r
            pltpu.make_async_copy(hp_ref.at[pl.ds(tok_ref[tile * tm + row], 1), :],
                                  gbuf_ref.at[to_slot].at[pl.ds(row, 1), :], sem.at[to_slot]).start()
            return carry

        lax.fori_loop(0, n_rows, body, 0, unroll=8)

    @pl.when((i == 0) & (j == 0) & valid)
    def _():
        issue(0, 0, tm, 0)

    @pl.when((j == 0) & valid)
    def _():
        pltpu.make_async_copy(hp_ref.at[pl.ds(0, tm), :], gbuf_ref.at[slot], sem.at[slot]).wait()
        lo, hi = _unpack_halves(gbuf_ref[slot])
        x_ref[:, :half] = lo.astype(BF16)
        x_ref[:, half:] = hi.astype(BF16)
        acc_ref[...] = jnp.zeros_like(acc_ref)

    nxt = jnp.minimum(i + 1, n_tiles - 1)

    @pl.when((i + 1 < n_tiles) & (tv_ref[nxt] > 0) & (j * rows_per_step < tm))
    def _():
        issue(nxt, j * rows_per_step, rows_per_step, 1 - slot)

    for q in range(1, MOE_ROW_SPLITS + 1):
        n_rows = q * (tm // MOE_ROW_SPLITS)

        @pl.when(tv_ref[i] == n_rows)
        def _():
            _swiglu_accumulate(x_ref, wg_ref, wu_ref, wd_ref, acc_ref, n_rows)

    @pl.when(j == pl.num_programs(1) - 1)
    def _():
        o_ref[...] = jnp.where(valid, _pack_halves(acc_ref[...]), jnp.uint32(0))


def _moe_ffn(hp, w_gate_up, w_down, layer, tile_expert, tile_valid, row_token, *, tm, fc):
    _, _, d, ff2 = w_gate_up.shape
    nj = ff2 // 2 // fc
    n_rows = row_token.shape[0]
    issue_steps = 1 << (nj.bit_length() - 1)

    def jeff(i, j, tv):
        return jnp.where(tv[i] > 0, j, nj - 1)

    return pl.pallas_call(
        functools.partial(_moe_ffn_kernel, rows_per_step=tm // issue_steps),
        grid_spec=pltpu.PrefetchScalarGridSpec(
            num_scalar_prefetch=3, grid=(n_rows // tm, nj),
            in_specs=[pl.BlockSpec(memory_space=pl.ANY),
                      pl.BlockSpec((None, 1, d, fc), lambda i, j, te, tv, tok: (layer, te[i], 0, jeff(i, j, tv))),
                      pl.BlockSpec((None, 1, d, fc),
                                   lambda i, j, te, tv, tok: (layer, te[i], 0, nj + jeff(i, j, tv))),
                      pl.BlockSpec((None, 1, fc, d), lambda i, j, te, tv, tok: (layer, te[i], jeff(i, j, tv), 0))],
            out_specs=pl.BlockSpec((tm, d // 2), lambda i, j, te, tv, tok: (i, 0)),
            scratch_shapes=[pltpu.VMEM((2, tm, d // 2), jnp.uint32), pltpu.VMEM((tm, d), BF16),
                            pltpu.VMEM((tm, d), F32), pltpu.SemaphoreType.DMA((2,))]),
        out_shape=jax.ShapeDtypeStruct((n_rows, d // 2), jnp.uint32),
        compiler_params=_cp(("arbitrary", "arbitrary"), VMEM_LIMIT_MOE), name="moe_swiglu",
    )(tile_expert, tile_valid, row_token, hp, w_gate_up, w_gate_up, w_down)


def _combine_kernel(p0_ref, p1_ref, y_ref, x_ref, route_ref, gate_ref, o_ref, b0_ref, b1_ref, sem):
    _, n, half = b0_ref.shape
    i = pl.program_id(0)
    slot = i % 2

    def issue(tile, to_slot):
        def body(r, carry):
            k = tile * n + r
            pltpu.make_async_copy(y_ref.at[pl.ds(p0_ref[k], 1), :], b0_ref.at[to_slot].at[pl.ds(r, 1), :],
                                  sem.at[0, to_slot]).start()
            pltpu.make_async_copy(y_ref.at[pl.ds(p1_ref[k], 1), :], b1_ref.at[to_slot].at[pl.ds(r, 1), :],
                                  sem.at[1, to_slot]).start()
            return carry

        lax.fori_loop(0, n, body, 0, unroll=8)

    @pl.when(i == 0)
    def _():
        issue(0, 0)

    pltpu.make_async_copy(y_ref.at[pl.ds(0, n), :], b0_ref.at[slot], sem.at[0, slot]).wait()
    pltpu.make_async_copy(y_ref.at[pl.ds(0, n), :], b1_ref.at[slot], sem.at[1, slot]).wait()

    @pl.when(i + 1 < pl.num_programs(0))
    def _():
        issue(i + 1, 1 - slot)

    route = route_ref[...]
    w0, w1 = route[:, 2:3], route[:, 3:4]
    lo0, hi0 = _unpack_halves(b0_ref[slot])
    lo1, hi1 = _unpack_halves(b1_ref[slot])
    gate = gate_ref[0, 0]
    o_ref[:, :half] = x_ref[:, :half] + gate[:, :half] * (w0 * lo0 + w1 * lo1)
    o_ref[:, half:] = x_ref[:, half:] + gate[:, half:] * (w0 * hi0 + w1 * hi1)


def _combine(y, x, route, mod, gate_k, pos0, pos1, *, seq, n_batch):
    t, d = x.shape
    tr = 256
    return pl.pallas_call(
        _combine_kernel,
        grid_spec=pltpu.PrefetchScalarGridSpec(
            num_scalar_prefetch=2, grid=(t // tr,),
            in_specs=[pl.BlockSpec(memory_space=pl.ANY),
                      pl.BlockSpec((tr, d), lambda i, a, b: (i, 0)),
                      pl.BlockSpec((tr, LANES), lambda i, a, b: (i, 0)),
                      pl.BlockSpec((1, 1, 1, d),
                                   lambda i, a, b: (jnp.minimum((i * tr) // seq, n_batch), gate_k, 0, 0))],
            out_specs=pl.BlockSpec((tr, d), lambda i, a, b: (i, 0)),
            scratch_shapes=[pltpu.VMEM((2, tr, d // 2), jnp.uint32), pltpu.VMEM((2, tr, d // 2), jnp.uint32),
                            pltpu.SemaphoreType.DMA((2, 2))]),
        out_shape=jax.ShapeDtypeStruct((t, d), F32),
        compiler_params=_cp(("arbitrary",), VMEM_LIMIT), name="moe_combine",
    )(pos0, pos1, y, x, route, _mod_view(mod))


def _route_meta(idx, n_exp, tm):
    t = idx.shape[0]
    flat = idx.reshape(-1)
    onehot = (flat[:, None] == jnp.arange(n_exp, dtype=jnp.int32)[None, :]).astype(jnp.int32)
    counts = jnp.sum(onehot, axis=0)
    rank = jnp.take_along_axis(jnp.cumsum(onehot, axis=0) - onehot, flat[:, None], axis=1)[:, 0]
    padded = ((counts + tm - 1) // tm) * tm
    ends = jnp.cumsum(padded)
    starts = ends - padded
    pos = starts[flat] + rank
    n_rows = TOP_K * t + n_exp * tm
    row_token = jnp.zeros((n_rows,), jnp.int32).at[pos].set(jnp.arange(TOP_K * t, dtype=jnp.int32) // TOP_K)
    tile_start = jnp.arange(n_rows // tm, dtype=jnp.int32) * tm
    tile_valid = tile_start < ends[-1]
    tile_expert = jnp.sum((ends[None, :] <= tile_start[:, None]).astype(jnp.int32), axis=1)
    tile_expert = jnp.minimum(tile_expert, n_exp - 1)
    last_valid = jnp.maximum(ends[-1] // tm - 1, 0)
    tile_expert = jnp.where(tile_valid, tile_expert, tile_expert[last_valid])
    sub = tm // MOE_ROW_SPLITS
    used = jnp.clip(counts[tile_expert] - (tile_start - starts[tile_expert]), 0, tm)
    tile_rows = jnp.where(tile_valid, ((used + sub - 1) // sub) * sub, 0).astype(jnp.int32)
    pos = pos.reshape(t, TOP_K).astype(jnp.int32)
    return row_token, tile_expert, tile_rows, pos[:, 0], pos[:, 1]


def _rope_tables(seq, pad_rows):
    t = jnp.arange(seq, dtype=jnp.int32)
    row = (t // GRID_W).astype(F32)
    col = (t % GRID_W).astype(F32)

    def angles(rot_dim):
        n_freq = rot_dim // 4
        freqs = ROPE_THETA ** (-jnp.arange(n_freq, dtype=F32) / n_freq)
        ang = jnp.concatenate([row[:, None] * freqs[None, :], col[:, None] * freqs[None, :]], axis=-1)
        return jnp.cos(ang), jnp.sin(ang)

    ch, sh = angles(HEAD_DIM)
    cos_h = jnp.concatenate([ch] * 4, axis=1)
    sin_h = jnp.concatenate([-sh, sh, -sh, sh], axis=1)
    cb, sb = angles(B_ROPE)
    one = jnp.ones((seq, B_NOPE), F32)
    zero = jnp.zeros((seq, B_NOPE), F32)
    tail1 = jnp.ones((seq, LANES - B_QK), F32)
    tail0 = jnp.zeros((seq, LANES - B_QK), F32)
    cos_b = jnp.concatenate([one, cb, cb, tail1], axis=1)
    sin_b = jnp.concatenate([zero, -sb, sb, tail0], axis=1)

    def pad(tab, fill):
        return jnp.concatenate([tab, jnp.full((pad_rows, LANES), fill, F32)], axis=0)

    return pad(cos_h, 1.0), pad(sin_h, 0.0), pad(cos_b, 1.0), pad(sin_b, 0.0)


def _permute_w_in(w):
    d = w.shape[0]
    o = np.cumsum([0, 512, 128, 128, B_Q_RANK, B_KV_RANK, B_ROPE, 512, 512, 512, 512, 512, 512, 512])
    aq, ak, av, bq, bkv, bkr, cq, ck, cv, cg, dq, dk, dv = [w[:, o[i]:o[i + 1]] for i in range(13)]

    def dup(m):
        return jnp.concatenate([m[:, :64], m[:, :64], m[:, 64:], m[:, 64:]], axis=1)

    z = lambda n: jnp.zeros((d, n), w.dtype)
    half = B_ROPE // 2
    bkr_swapped = jnp.concatenate([bkr[:, half:], bkr[:, :half]], axis=1)
    parts = [aq, cq, ck, dq, dk, dup(ak), dup(av), cv, cg, dv, bq, bkv,
             z(B_NOPE), bkr, z(LANES - B_QK), z(B_NOPE), bkr_swapped, z(LANES - B_QK)]
    return jnp.concatenate(parts, axis=1).astype(BF16)


def _block_diag_ones():
    i = np.arange(512) // HEAD_DIM
    return jnp.asarray((i[:, None] == i[None, :]).astype(np.float32), dtype=BF16)


def _rope_partner(m):
    half = B_ROPE // 2
    return jnp.concatenate([m[..., :B_NOPE], m[..., B_NOPE + half:], m[..., B_NOPE:B_NOPE + half]], axis=-1)


def _pad_lanes(v, width):
    return jnp.concatenate([v.astype(F32), jnp.zeros((width - v.shape[0],), F32)])


def kernel(x, c, ctx, c_ctx, w_ada, b_ada, norm_mix, norm_ffn, w_in, w_out, a_q_norm, a_k_norm, a_sink,
           b_q_a_norm, b_kv_a_norm, b_w_uq, b_w_ukv, b_q_norm, b_k_norm, c_log_decay, d_q_norm, d_k_norm,
           d_rpb, ffn_w_gate_up, ffn_w_down, moe_router, moe_w_gate_up, moe_w_down):
    n_batch, seq, d = x.shape
    n_ctx = ctx.shape[1]
    depth = w_ada.shape[0]
    n_lat = n_batch * seq
    n_all = n_lat + n_batch * n_ctx
    grid_rows = seq // GRID_W
    tm = next(t for t in (1024, 512, 256) if n_lat % t == 0 and n_all % t == 0)

    cc = jnp.concatenate([c, c_ctx[None, :], jnp.zeros((8 - n_batch - 1, d), F32)], axis=0)
    mod = _ada(cc, w_ada, b_ada)
    xs = (x.reshape(n_lat, d), ctx.reshape(n_batch * n_ctx, d))
    cos_h, sin_h, cos_b, sin_b = _rope_tables(seq, PREP_ROWS)
    gmat = _block_diag_ones()

    y = None
    y_gate, y_mod = 0, None
    for l in range(depth):
        last = l == depth - 1
        mod_l = mod[l]
        if y is None:
            (h,) = _resnorm(xs, None, None, 0, mod_l, norm_mix[l], 0, 1, rows=n_all, seq=seq, n_batch=n_batch)
        else:
            xs, h = _resnorm(xs, y, y_mod, y_gate, mod_l, norm_mix[l], 0, 1, rows=n_all, seq=seq,
                             n_batch=n_batch)
        proj = _mm([h], _permute_w_in(w_in[l]), rows=n_all, tm=tm, tn=1792, out_dtype=BF16, name="proj_in")
        gains = jnp.stack([jnp.tile(a_q_norm[l], 8), _pad_lanes(jnp.tile(a_k_norm[l], 4), 512),
                           jnp.tile(d_q_norm[l], 8), jnp.tile(d_k_norm[l], 8)]).astype(F32)
        prep, dv2 = _prep(proj, cos_h, sin_h, gmat, gains, rows=n_all, seq=seq, n_lat=n_lat)
        wuq = jnp.pad(b_w_uq[l].reshape(B_Q_RANK, B_HEADS, B_QK),
                      ((0, 0), (0, 0), (0, LANES - B_QK))).reshape(B_Q_RANK, B_HEADS * LANES).astype(BF16)
        wukv3 = b_w_ukv[l].reshape(B_KV_RANK, B_HEADS, B_NOPE + B_V)
        wukv = jnp.concatenate(
            [jnp.pad(wukv3[:, :, :B_NOPE], ((0, 0), (0, 0), (0, LANES - B_NOPE))).reshape(B_KV_RANK, -1),
             wukv3[:, :, B_NOPE:].reshape(B_KV_RANK, -1)], axis=1).astype(BF16)
        bound = _mla_logit_bound(b_q_norm[l], b_k_norm[l])
        mla_safe = (bound <= SOFTMAX_SAFE_BOUND).astype(jnp.int32).reshape(1)
        gains_b = jnp.stack([b_q_a_norm[l].astype(F32), _pad_lanes(b_kv_a_norm[l], B_Q_RANK),
                             _pad_lanes(b_q_norm[l], B_Q_RANK), _pad_lanes(b_k_norm[l], B_Q_RANK),
                             jnp.full((B_Q_RANK,), -1.0, F32) * bound,
                             _pad_lanes(_rope_partner(b_q_norm[l]), B_Q_RANK),
                             _pad_lanes(_rope_partner(b_k_norm[l]), B_Q_RANK), jnp.zeros((B_Q_RANK,), F32)])
        wuq3 = b_w_uq[l].reshape(B_Q_RANK, B_HEADS, B_QK)
        wuq_sw = jnp.pad(_rope_partner(wuq3), ((0, 0), (0, 0), (0, LANES - B_QK))).reshape(wuq.shape).astype(BF16)
        qb, kb, vb = _prepb(proj, cos_b, sin_b, wuq, wuq_sw, wukv, gains_b, rows=n_all, seq=seq, n_lat=n_lat)

        sink = a_sink[l].astype(F32)
        bound_a = _softmax_bound(a_q_norm[l], a_k_norm[l], jnp.maximum(jnp.max(sink), 0.0))
        par_a = jnp.concatenate([sink * LOG2E, bound_a[None], (bound_a <= SOFTMAX_SAFE_BOUND).astype(F32)[None]])
        oa = _wina(prep, proj, par_a, n_batch=n_batch, seq=seq, n_ctx=n_ctx, with_ctx=not last)
        ob = _mla(qb, kb, vb, mla_safe, n_batch=n_batch, seq=seq, n_ctx=n_ctx, with_ctx=not last)
        ld_head = jnp.broadcast_to(c_log_decay[l].astype(F32)[:, :, None], (2, C_HEADS, LANES))
        oc_fwd, oc_rev = _retention(prep, proj, ld_head, n_batch=n_batch, seq=seq, n_ctx=n_ctx)
        bound_d = _softmax_bound(d_q_norm[l], d_k_norm[l], jnp.max(jnp.abs(d_rpb[l])))
        par_d = jnp.stack([bound_d, (bound_d <= SOFTMAX_SAFE_BOUND).astype(F32)])
        abm = _nbr_bias_tables(d_rpb[l], grid_rows, bound_d)
        od = _nbr(prep, dv2, abm, par_d, n_batch=n_batch, seq=seq, n_ctx=n_ctx, with_ctx=not last)
        rows_l = n_lat if last else n_all
        oc = _retention_finish(oc_fwd, oc_rev, proj, gmat, rows=rows_l)
        ymix = _mm([oa, ob, oc, od], w_out[l].astype(BF16), rows=rows_l, tm=tm, tn=2048, out_dtype=BF16,
                   name="proj_out")
        i = l // 2
        if l % 2 == 0:
            xs, h2 = _resnorm(xs, ymix, mod_l, 2, mod_l, norm_ffn[l], 3, 4, rows=rows_l, seq=seq,
                              n_batch=n_batch)
            y = _ffn(h2, ffn_w_gate_up, ffn_w_down, i, rows=rows_l, tm=tm, fc=512)
            y_gate, y_mod = 5, mod_l
        else:
            xs, hp, route = _resnorm(xs, ymix, mod_l, 2, mod_l, norm_ffn[l], 3, 4, rows=rows_l, seq=seq,
                                     n_batch=n_batch, router=moe_router[i])
            top_idx = route[:, :TOP_K].astype(jnp.int32)
            row_token, tile_expert, tile_valid, pos0, pos1 = _route_meta(top_idx, moe_router.shape[2], tm)
            yg = _moe_ffn(hp, moe_w_gate_up, moe_w_down, i, tile_expert, tile_valid, row_token, tm=tm, fc=512)
            xs = _combine(yg, xs, route, mod_l, 5, pos0, pos1, seq=seq, n_batch=n_batch)
            y = None
    if y is not None:
        xs, _ = _resnorm(xs, y, y_mod, y_gate, y_mod, norm_ffn[depth - 1], 3, 4, rows=xs.shape[0], seq=seq,
                         n_batch=n_batch)
    return xs[:n_lat].reshape(n_batch, seq, d)
```

```python
import functools

import numpy as np
import jax
import jax.numpy as jnp
from jax import lax
from jax.experimental import pallas as pl
from jax.experimental.pallas import tpu as pltpu

F32 = jnp.float32
BF16 = jnp.bfloat16

GRID_W = 64
HEAD_DIM = 64
ROPE_THETA = 10000.0
EPS = 1e-6
NEG_INF = -1e30

A_HEADS = 8
A_KV_HEADS = 2
A_BLOCK = 128
B_HEADS = 8
B_Q_RANK = 384
B_KV_RANK = 128
B_NOPE = 64
B_ROPE = 32
B_QK = B_NOPE + B_ROPE
B_V = 64
C_HEADS = 8
D_HEADS = 8
NA_ROWS = 8
NA_COLS = 16
TOP_K = 2

LANES = 128
NA_QROWS = 4
NA_KROWS = NA_ROWS + NA_QROWS
NA_CASES = 4
MOE_ROW_SPLITS = 8
PREP_ROWS = 512
RET_CHUNK = 256

P_AQ, P_CQ, P_CK, P_DQ, P_DK, P_AK, P_AV, P_CV, P_CG, P_DV, P_BQ, P_BKV, P_BKR, P_BKR_SW, P_END = (
    0, 512, 1024, 1536, 2048, 2560, 2816, 3072, 3584, 4096, 4608, 4992, 5120, 5248, 5376)
PREP_W = P_AV

VMEM_LIMIT = 56 * 1024 * 1024
VMEM_LIMIT_MOE = 61 * 1024 * 1024

LOG2E = float(np.log2(np.e))
MLA_LOGIT_SCALE = B_QK ** -0.5 * LOG2E
SOFTMAX_SAFE_BOUND = 40.0


def _cp(sem, vmem=None):
    return pltpu.CompilerParams(dimension_semantics=sem, vmem_limit_bytes=vmem)


def _nt(a, b):
    return lax.dot_general(a, b, (((1,), (1,)), ((), ())), preferred_element_type=F32)


_HI_HALF = 0xFFFF0000


def _pack_halves(x):
    n = x.shape[1] // 2
    bits = pltpu.bitcast(x.astype(BF16).astype(F32), jnp.uint32)
    return (bits[:, :n] >> 16) | (bits[:, n:] & jnp.uint32(_HI_HALF))


def _unpack_halves(w):
    return pltpu.bitcast(w << 16, F32), pltpu.bitcast(w & jnp.uint32(_HI_HALF), F32)


def _ada_kernel(c_ref, w_ref, b_ref, o_ref):
    c = c_ref[...]
    s = (c * jax.nn.sigmoid(c)).astype(BF16)
    o_ref[0] = jnp.dot(s, w_ref[0].astype(BF16), preferred_element_type=F32) + b_ref[0]


def _ada(cc, w_ada, b_ada):
    depth, d, n = w_ada.shape
    tn = 1024
    return pl.pallas_call(
        _ada_kernel,
        grid=(depth, n // tn),
        in_specs=[pl.BlockSpec((8, d), lambda l, j: (0, 0)),
                  pl.BlockSpec((1, d, tn), lambda l, j: (l, 0, j)),
                  pl.BlockSpec((1, 1, tn), lambda l, j: (l, 0, j))],
        out_specs=pl.BlockSpec((1, 8, tn), lambda l, j: (l, 0, j)),
        out_shape=jax.ShapeDtypeStruct((depth, 8, n), F32),
        compiler_params=_cp(("arbitrary", "arbitrary"), VMEM_LIMIT),
        name="ada",
    )(cc, w_ada, b_ada.reshape(depth, 1, n))


def _resnorm_kernel(*refs, has_res, with_router, n_exp, lat_tiles):
    refs = list(refs)
    x_ref = refs.pop(0)
    if lat_tiles is not None:
        xc_ref = refs.pop(0)
        x = jnp.where(pl.program_id(0) < lat_tiles, x_ref[...], xc_ref[...])
    else:
        x = x_ref[...]
    if has_res:
        y_ref = refs.pop(0)
        gate_ref = refs.pop(0)
    gain_ref, sh_ref, sc_ref = refs[:3]
    refs = refs[3:]
    if with_router:
        router_ref = refs.pop(0)
    if has_res:
        xo_ref = refs.pop(0)
    h_ref = refs.pop(0)
    if has_res:
        x = x + gate_ref[0, 0] * y_ref[...].astype(F32)
        xo_ref[...] = x
    ms = jnp.mean(x * x, axis=-1, keepdims=True)
    h = (x * lax.rsqrt(ms + EPS) * gain_ref[...]) * (1.0 + sc_ref[0, 0]) + sh_ref[0, 0]
    if with_router:
        (route_ref,) = refs
        h_ref[...] = _pack_halves(h)
        h_hi = h.astype(BF16)
        h_lo = (h - h_hi.astype(F32)).astype(BF16)
        r_hi, r_lo = router_ref[0], router_ref[1]
        logits = (jnp.dot(h_hi, r_hi, preferred_element_type=F32) + jnp.dot(h_lo, r_hi, preferred_element_type=F32)
                  + jnp.dot(h_hi, r_lo, preferred_element_type=F32))
        lane = lax.broadcasted_iota(jnp.int32, logits.shape, 1)
        logits = jnp.where(lane < n_exp, logits, -jnp.inf)
        m1 = jnp.max(logits, axis=-1, keepdims=True)
        i1 = jnp.min(jnp.where(logits == m1, lane, LANES), axis=-1, keepdims=True)
        rest = jnp.where(lane == i1, -jnp.inf, logits)
        m2 = jnp.max(rest, axis=-1, keepdims=True)
        i2 = jnp.min(jnp.where(rest == m2, lane, LANES), axis=-1, keepdims=True)
        e2 = jnp.exp(m2 - m1)
        g1 = 1.0 / (1.0 + e2)
        g2 = e2 / (1.0 + e2)
        route = jnp.where(lane == 0, i1.astype(F32),
                          jnp.where(lane == 1, i2.astype(F32),
                                    jnp.where(lane == 2, g1, jnp.where(lane == 3, g2, 0.0))))
        route_ref[...] = route
    else:
        h_ref[...] = h.astype(h_ref.dtype)


def _mod_view(mod):
    return mod.reshape(mod.shape[0], 6, 1, mod.shape[1] // 6)


def _resnorm(x, y, gate_mod, gate_k, mod, gain, sh_k, sc_k, *, rows, seq, n_batch, router=None):
    tr = 512
    has_res = y is not None
    with_router = router is not None
    split = isinstance(x, tuple)
    d = x[0].shape[1] if split else x.shape[1]

    def grp(i):
        return jnp.minimum((i * tr) // seq, n_batch)

    def modspec(k):
        return pl.BlockSpec((1, 1, 1, d), lambda i: (grp(i), k, 0, 0))

    row = pl.BlockSpec((tr, d), lambda i: (i, 0))
    lat_tiles = None
    if split:
        lat_tiles = x[0].shape[0] // tr
        in_specs = [pl.BlockSpec((tr, d), lambda i: (jnp.minimum(i, lat_tiles - 1), 0)),
                    pl.BlockSpec((tr, d), lambda i: (jnp.maximum(i - lat_tiles, 0), 0))]
        args = list(x)
    else:
        in_specs = [row]
        args = [x]
    if has_res:
        in_specs += [row, modspec(gate_k)]
        args += [y, _mod_view(gate_mod)]
    in_specs += [pl.BlockSpec((1, d), lambda i: (0, 0)), modspec(sh_k), modspec(sc_k)]
    args += [gain.reshape(1, d).astype(F32), _mod_view(mod), _mod_view(mod)]
    out_shape, out_specs = [], []
    n_exp = 0
    if with_router:
        n_exp = router.shape[1]
        router = jnp.pad(router.astype(F32), ((0, 0), (0, LANES - n_exp)))
        r_hi = router.astype(BF16)
        router = jnp.stack([r_hi, (router - r_hi.astype(F32)).astype(BF16)])
        in_specs.append(pl.BlockSpec(router.shape, lambda i: (0, 0, 0)))
        args.append(router)
    if has_res:
        out_shape.append(jax.ShapeDtypeStruct((rows, d), F32))
        out_specs.append(row)
    if with_router:
        out_shape += [jax.ShapeDtypeStruct((rows, d // 2), jnp.uint32), jax.ShapeDtypeStruct((rows, LANES), F32)]
        out_specs += [pl.BlockSpec((tr, d // 2), lambda i: (i, 0)), pl.BlockSpec((tr, LANES), lambda i: (i, 0))]
    else:
        out_shape.append(jax.ShapeDtypeStruct((rows, d), BF16))
        out_specs.append(row)
    kern = functools.partial(_resnorm_kernel, has_res=has_res, with_router=with_router, n_exp=n_exp,
                             lat_tiles=lat_tiles)
    return pl.pallas_call(
        kern, grid=(rows // tr,), in_specs=in_specs, out_specs=out_specs, out_shape=out_shape,
        compiler_params=_cp(("arbitrary",), VMEM_LIMIT), name="resnorm",
    )(*args)


def _mm_kernel(*refs, n_x):
    w_ref, o_ref = refs[n_x], refs[n_x + 1]
    if n_x == 1:
        x = refs[0][...]
    else:
        x = jnp.concatenate([r[...] for r in refs[:n_x]], axis=1)
    o_ref[...] = jnp.dot(x, w_ref[...], preferred_element_type=F32).astype(o_ref.dtype)


def _mm(xs, w, *, rows, tm, tn, out_dtype, name):
    n = w.shape[1]
    in_specs = [pl.BlockSpec((tm, a.shape[1]), lambda i, j: (i, 0)) for a in xs]
    in_specs.append(pl.BlockSpec((w.shape[0], tn), lambda i, j: (0, j)))
    return pl.pallas_call(
        functools.partial(_mm_kernel, n_x=len(xs)),
        grid=(rows // tm, n // tn),
        in_specs=in_specs,
        out_specs=pl.BlockSpec((tm, tn), lambda i, j: (i, j)),
        out_shape=jax.ShapeDtypeStruct((rows, n), out_dtype),
        compiler_params=_cp(("arbitrary", "arbitrary"), VMEM_LIMIT), name=name,
    )(*xs, w)


def _group_meansq(x, g_ref, width):
    x2 = x * x
    hi = x2.astype(BF16)
    lo = (x2 - hi.astype(F32)).astype(BF16)
    g = g_ref[:width, :width]
    ss = jnp.dot(hi, g, preferred_element_type=F32) + jnp.dot(lo, g, preferred_element_type=F32)
    return ss * (1.0 / HEAD_DIM)


def _prep_kernel(p_ref, dv_ref, ch_ref, sh_ref, g_ref, gains_ref, o_ref, dv2_ref):
    tr = p_ref.shape[0]

    def seg(off, width):
        return p_ref[:, off:off + width].astype(F32)

    def norm(x, row, width):
        return x * lax.rsqrt(_group_meansq(x, g_ref, width) + EPS) * gains_ref[row:row + 1, :width]

    def rope(x, width):
        lane = lax.broadcasted_iota(jnp.int32, (tr, width), 1)
        first = (lane % HEAD_DIM) < (HEAD_DIM // 2)
        cos = jnp.concatenate([ch_ref[...]] * (width // LANES), axis=1)
        sin = jnp.concatenate([sh_ref[...]] * (width // LANES), axis=1)
        swapped = jnp.where(first, pltpu.roll(x, width - HEAD_DIM // 2, 1), pltpu.roll(x, HEAD_DIM // 2, 1))
        return x * cos + swapped * sin

    scale = HEAD_DIM ** -0.5
    scale2 = scale * LOG2E
    o_ref[:, P_AQ:P_AQ + 512] = (rope(norm(seg(P_AQ, 512), 0, 512), 512) * scale2).astype(BF16)
    o_ref[:, P_CQ:P_CQ + 512] = (rope(seg(P_CQ, 512), 512) * scale).astype(BF16)
    o_ref[:, P_CK:P_CK + 512] = rope(seg(P_CK, 512), 512).astype(BF16)
    o_ref[:, P_DQ:P_DQ + 512] = (norm(seg(P_DQ, 512), 2, 512) * scale2).astype(BF16)
    o_ref[:, P_DK:P_DK + 512] = norm(seg(P_DK, 512), 3, 512).astype(BF16)
    o_ref[:, P_AK:P_AK + 256] = rope(norm(seg(P_AK, 256), 1, 256), 256).astype(BF16)
    ones_blk = jnp.where(lax.broadcasted_iota(jnp.int32, (tr, LANES), 1) < HEAD_DIM, 1.0, 0.0).astype(BF16)
    for g in range(D_HEADS // 2):
        dv2_ref[:, 2 * g * LANES:(2 * g + 1) * LANES] = dv_ref[:, g * LANES:(g + 1) * LANES]
        dv2_ref[:, (2 * g + 1) * LANES:(2 * g + 2) * LANES] = ones_blk


def _prep(proj, cos_h, sin_h, gmat, gains, *, rows, seq, n_lat):
    tr = PREP_ROWS
    nlat = seq // tr

    def tab(i):
        return (jnp.where(i * tr < n_lat, i % nlat, nlat), 0)

    return pl.pallas_call(
        _prep_kernel,
        grid=(rows // tr,),
        in_specs=[pl.BlockSpec((tr, PREP_W), lambda i: (i, 0)),
                  pl.BlockSpec((tr, 512), lambda i: (i, P_DV // 512)),
                  pl.BlockSpec((tr, LANES), tab),
                  pl.BlockSpec((tr, LANES), tab),
                  pl.BlockSpec((512, 512), lambda i: (0, 0)),
                  pl.BlockSpec((4, 512), lambda i: (0, 0))],
        out_specs=[pl.BlockSpec((tr, PREP_W), lambda i: (i, 0)),
                   pl.BlockSpec((tr, 2 * 512), lambda i: (i, 0))],
        out_shape=[jax.ShapeDtypeStruct((rows, PREP_W), BF16), jax.ShapeDtypeStruct((rows, 2 * 512), BF16)],
        compiler_params=_cp(("arbitrary",), VMEM_LIMIT), name="prep",
    )(proj, proj, cos_h, sin_h, gmat, gains)


def _prepb_kernel(bq_ref, bkv_ref, bkr_ref, bkrs_ref, cb_ref, sb_ref, wuq_ref, wuqs_ref, wukv_ref, gains_ref,
                  q_ref, k_ref, v_ref):
    tr = bq_ref.shape[0]
    cq = bq_ref[...].astype(F32)
    cqn = (cq * lax.rsqrt(jnp.mean(cq * cq, axis=-1, keepdims=True) + EPS) * gains_ref[0:1, :]).astype(BF16)
    qup = jnp.dot(cqn, wuq_ref[...], preferred_element_type=F32)
    qup_sw = jnp.dot(cqn, wuqs_ref[...], preferred_element_type=F32)
    ckv = bkv_ref[...].astype(F32)
    ckvn = ckv * lax.rsqrt(jnp.mean(ckv * ckv, axis=-1, keepdims=True) + EPS) * gains_ref[1:2, :LANES]
    kvup = jnp.dot(ckvn.astype(BF16), wukv_ref[...], preferred_element_type=F32)
    kr = bkr_ref[...].astype(F32)
    cb, sb = cb_ref[...], sb_ref[...]
    lane = lax.broadcasted_iota(jnp.int32, (tr, LANES), 1)
    q_cos = cb * (gains_ref[2:3, :LANES] * MLA_LOGIT_SCALE)
    q_sin = sb * (gains_ref[5:6, :LANES] * MLA_LOGIT_SCALE)
    k_cos = cb * gains_ref[3:4, :LANES]
    k_rot = bkrs_ref[...].astype(F32) * (sb * gains_ref[6:7, :LANES])

    def inv_rms(x):
        return lax.rsqrt(jnp.sum(x * x, axis=-1, keepdims=True) * (1.0 / B_QK) + EPS)

    shift_lane = lane == B_QK
    neg_bound = gains_ref[4:5, :LANES]
    for h in range(B_HEADS):
        sl = slice(h * LANES, (h + 1) * LANES)
        x = qup[:, sl]
        q = (x * q_cos + qup_sw[:, sl] * q_sin) * inv_rms(x)
        q_ref[:, sl] = jnp.where(shift_lane, 1.0, q).astype(BF16)
        x = kvup[:, sl] + kr
        k = (x * k_cos + k_rot) * inv_rms(x)
        k_ref[:, sl] = jnp.where(shift_lane, neg_bound, k).astype(BF16)
    ones_blk = jnp.where(lane < B_V, 1.0, 0.0).astype(BF16)
    for g in range(B_HEADS // 2):
        v_ref[:, 2 * g * LANES:(2 * g + 1) * LANES] = kvup[:, (B_HEADS + g) * LANES:(B_HEADS + g + 1) * LANES].astype(BF16)
        v_ref[:, (2 * g + 1) * LANES:(2 * g + 2) * LANES] = ones_blk


def _prepb(proj, cos_b, sin_b, wuq, wuq_sw, wukv, gains, *, rows, seq, n_lat):
    tr = PREP_ROWS
    nlat = seq // tr

    def tab(i):
        return (jnp.where(i * tr < n_lat, i % nlat, nlat), 0)

    return pl.pallas_call(
        _prepb_kernel,
        grid=(rows // tr,),
        in_specs=[pl.BlockSpec((tr, B_Q_RANK), lambda i: (i, P_BQ // B_Q_RANK)),
                  pl.BlockSpec((tr, LANES), lambda i: (i, P_BKV // LANES)),
                  pl.BlockSpec((tr, LANES), lambda i: (i, P_BKR // LANES)),
                  pl.BlockSpec((tr, LANES), lambda i: (i, P_BKR_SW // LANES)),
                  pl.BlockSpec((tr, LANES), tab),
                  pl.BlockSpec((tr, LANES), tab),
                  pl.BlockSpec(wuq.shape, lambda i: (0, 0)),
                  pl.BlockSpec(wuq.shape, lambda i: (0, 0)),
                  pl.BlockSpec(wukv.shape, lambda i: (0, 0)),
                  pl.BlockSpec((8, B_Q_RANK), lambda i: (0, 0))],
        out_specs=[pl.BlockSpec((tr, B_HEADS * LANES), lambda i: (i, 0)),
                   pl.BlockSpec((tr, B_HEADS * LANES), lambda i: (i, 0)),
                   pl.BlockSpec((tr, B_HEADS * LANES), lambda i: (i, 0))],
        out_shape=[jax.ShapeDtypeStruct((rows, B_HEADS * LANES), BF16),
                   jax.ShapeDtypeStruct((rows, B_HEADS * LANES), BF16),
                   jax.ShapeDtypeStruct((rows, B_HEADS * LANES), BF16)],
        compiler_params=_cp(("arbitrary",), VMEM_LIMIT), name="prep_mla",
    )(proj, proj, proj, proj, cos_b, sin_b, wuq, wuq_sw, wukv, gains)


def _wina_kernel(par_ref, q_ref, kp_ref, ko_ref, kn_ref, kc_ref, vp_ref, vo_ref, vn_ref, vc_ref, o_ref,
                 *, nb, seq):
    n = pl.program_id(1)
    blk = A_BLOCK
    rep = A_HEADS // A_KV_HEADS
    n_ctx = kc_ref.shape[0]
    kcat = jnp.concatenate([kp_ref[...], ko_ref[...], kn_ref[...], kc_ref[...]], axis=0)
    vcat = jnp.concatenate([vp_ref[...], vo_ref[...], vn_ref[...], vc_ref[...]], axis=0)
    n_keys = 3 * blk + n_ctx
    qi = lax.broadcasted_iota(jnp.int32, (blk, n_keys), 0)
    kj = lax.broadcasted_iota(jnp.int32, (blk, n_keys), 1)
    band = kj - blk
    kpos = n * blk + band
    valid = ((jnp.abs(qi - band) <= blk) & (kpos >= 0) & (kpos < seq) & (n < nb)) | (kj >= 3 * blk)
    low = lax.broadcasted_iota(jnp.int32, (blk, LANES), 1) < HEAD_DIM
    low_k = lax.broadcasted_iota(jnp.int32, (n_keys, LANES), 1) < HEAD_DIM
    bound = par_ref[A_HEADS]
    safe = par_ref[A_HEADS + 1] > 0.0

    def attend(use_max):
        mask = jnp.where(valid, 0.0 if use_max else -bound, NEG_INF)
        mask = jnp.concatenate([mask] * rep, axis=0)
        for g in range(A_KV_HEADS):
            kg = kcat[:, g * LANES:(g + 1) * LANES]
            v1 = jnp.where(low_k, vcat[:, g * LANES:(g + 1) * LANES], jnp.ones((), BF16))
            qs, sinks = [], []
            for r in range(rep):
                h = g * rep + r
                qc = q_ref[:, (h // 2) * LANES:(h // 2 + 1) * LANES]
                qs.append(jnp.where(low if h % 2 == 0 else jnp.logical_not(low), qc, jnp.zeros_like(qc)))
                sinks.append(jnp.full((blk, 1), par_ref[h], F32))
            s = _nt(jnp.concatenate(qs, axis=0), kg) + mask
            sink = jnp.concatenate(sinks, axis=0)
            if use_max:
                m = jnp.maximum(jnp.max(s, axis=-1, keepdims=True), sink)
                s, sink = s - m, sink - m
            else:
                sink = sink - bound
            o = jnp.dot(jnp.exp2(s).astype(BF16), v1, preferred_element_type=F32)
            res = o / (o[:, HEAD_DIM:HEAD_DIM + 1] + jnp.exp2(sink))
            for c in range(rep // 2):
                even = res[(2 * c) * blk:(2 * c + 1) * blk]
                odd = pltpu.roll(res[(2 * c + 1) * blk:(2 * c + 2) * blk], HEAD_DIM, 1)
                j = (g * rep) // 2 + c
                o_ref[:, j * LANES:(j + 1) * LANES] = jnp.where(low, even, odd).astype(o_ref.dtype)

    @pl.when(safe)
    def _():
        attend(False)

    @pl.when(jnp.logical_not(safe))
    def _():
        attend(True)


def _wina(prep, proj, sink, *, n_batch, seq, n_ctx, with_ctx):
    blk = A_BLOCK
    nb = seq // blk
    ncb = n_ctx // blk
    steps = nb + (ncb if with_ctx else 0)
    lat_blocks = n_batch * nb
    rows_out = n_batch * seq + (n_batch * n_ctx if with_ctx else 0)

    def qmap(b, n, s):
        return (jnp.where(n < nb, b * nb + n, lat_blocks + b * ncb + (n - nb)), 0)

    def kmap(delta, col):
        def f(b, n, s):
            return (b * nb + jnp.clip(n + delta, 0, nb - 1), col)
        return f

    def cmap(col):
        def f(b, n, s):
            return (n_batch * seq // n_ctx + b, col)
        return f

    kcol, vcol = P_AK // 256, P_AV // 256
    in_specs = [pl.BlockSpec((blk, 512), qmap)]
    in_specs += [pl.BlockSpec((blk, 256), kmap(dl, kcol)) for dl in (-1, 0, 1)]
    in_specs += [pl.BlockSpec((n_ctx, 256), cmap(kcol))]
    in_specs += [pl.BlockSpec((blk, 256), kmap(dl, vcol)) for dl in (-1, 0, 1)]
    in_specs += [pl.BlockSpec((n_ctx, 256), cmap(vcol))]
    return pl.pallas_call(
        functools.partial(_wina_kernel, nb=nb, seq=seq),
        grid_spec=pltpu.PrefetchScalarGridSpec(
            num_scalar_prefetch=1, grid=(n_batch, steps), in_specs=in_specs,
            out_specs=pl.BlockSpec((blk, 512), qmap)),
        out_shape=jax.ShapeDtypeStruct((rows_out, 512), BF16),
        compiler_params=_cp(("arbitrary", "arbitrary"), VMEM_LIMIT), name="mixer_window",
    )(sink, prep, prep, prep, prep, prep, proj, proj, proj, proj)


def _mla_kernel(safe_ref, q_ref, kl_ref, vl_ref, kc_ref, vc_ref, o_ref, *, nq, tk_fast, tk_exact, hps):
    tq = q_ref.shape[0]
    seq = kl_ref.shape[0]
    is_lat = pl.program_id(2) < nq
    safe = safe_ref[0] > 0
    pair_w = 2 * LANES
    lane = lax.broadcasted_iota(jnp.int32, (tq, LANES), 1)

    def q(h):
        return q_ref[:, h * LANES:(h + 1) * LANES]

    def write(nums, dens):
        for g in range(hps // 2):
            o_ref[:, g * LANES:(g + 1) * LANES] = jnp.where(
                lane < B_V, nums[2 * g] / dens[2 * g], nums[2 * g + 1] / dens[2 * g + 1]).astype(o_ref.dtype)

    def fast(chunks):
        accs = [None] * hps
        for k_ref, v_ref, off, n in chunks:
            for h in range(hps):
                g = h // 2
                p = jnp.exp2(_nt(q(h), k_ref[off:off + n, h * LANES:(h + 1) * LANES])).astype(BF16)
                d = jnp.dot(p, v_ref[off:off + n, g * pair_w:(g + 1) * pair_w], preferred_element_type=F32)
                accs[h] = d if accs[h] is None else accs[h] + d
        write([a[:, :LANES] for a in accs], [a[:, LANES:LANES + 1] for a in accs])

    ctx_chunk = (kc_ref, vc_ref, 0, kc_ref.shape[0])

    @pl.when(safe & is_lat)
    def _():
        fast([(kl_ref, vl_ref, c * tk_fast, tk_fast) for c in range(seq // tk_fast)] + [ctx_chunk])

    @pl.when(safe & jnp.logical_not(is_lat))
    def _():
        fast([ctx_chunk])

    @pl.when(jnp.logical_not(safe))
    def _():
        def step(kf, vf, carry):
            out = []
            for h in range(hps):
                m, l, acc = carry[h]
                s = _nt(q(h), kf(h))
                m_new = jnp.maximum(m, jnp.max(s, axis=-1, keepdims=True))
                a = jnp.exp2(m - m_new)
                p = jnp.exp2(s - m_new)
                l = a * l + jnp.sum(p, axis=-1, keepdims=True)
                acc = a * acc + jnp.dot(p.astype(BF16), vf(h // 2), preferred_element_type=F32)
                out.append((m_new, l, acc))
            return tuple(out)

        def body(c, carry):
            off = pl.multiple_of(c * tk_exact, tk_exact)
            return step(lambda h: kl_ref[pl.ds(off, tk_exact), h * LANES:(h + 1) * LANES],
                        lambda g: vl_ref[pl.ds(off, tk_exact), g * pair_w:g * pair_w + LANES], carry)

        init = tuple((jnp.full((tq, 1), -jnp.inf, F32), jnp.zeros((tq, 1), F32), jnp.zeros((tq, LANES), F32))
                     for _ in range(hps))
        carry = lax.fori_loop(0, jnp.where(is_lat, seq // tk_exact, 0), body, init)
        carry = step(lambda h: kc_ref[:, h * LANES:(h + 1) * LANES],
                     lambda g: vc_ref[:, g * pair_w:g * pair_w + LANES], carry)
        write([c[2] for c in carry], [c[1] for c in carry])


def _mla_logit_bound(q_gain, k_gain):
    b = B_QK * MLA_LOGIT_SCALE * jnp.max(jnp.abs(q_gain)) * jnp.max(jnp.abs(k_gain))
    return (1.02 * b + 0.5).astype(F32)


def _mla(qb, kb, vb, safe, *, n_batch, seq, n_ctx, with_ctx):
    hps = 4
    tq = 256
    assert n_ctx == tq
    nq = seq // tq
    ctx_blk = n_batch * seq // n_ctx
    rows_out = n_batch * seq + (n_batch * n_ctx if with_ctx else 0)
    kw, vw = hps * LANES, hps * B_V

    def qmap(b, j, i, s):
        return (jnp.where(i < nq, b * nq + i, ctx_blk + b), j)

    return pl.pallas_call(
        functools.partial(_mla_kernel, nq=nq, tk_fast=min(2048, seq), tk_exact=512, hps=hps),
        grid_spec=pltpu.PrefetchScalarGridSpec(
            num_scalar_prefetch=1,
            grid=(n_batch, B_HEADS // hps, nq + (1 if with_ctx else 0)),
            in_specs=[pl.BlockSpec((tq, kw), qmap),
                      pl.BlockSpec((seq, kw), lambda b, j, i, s: (b, j)),
                      pl.BlockSpec((seq, kw), lambda b, j, i, s: (b, j)),
                      pl.BlockSpec((n_ctx, kw), lambda b, j, i, s: (ctx_blk + b, j)),
                      pl.BlockSpec((n_ctx, kw), lambda b, j, i, s: (ctx_blk + b, j))],
            out_specs=pl.BlockSpec((tq, vw), qmap)),
        out_shape=jax.ShapeDtypeStruct((rows_out, 512), BF16),
        compiler_params=_cp(("arbitrary", "arbitrary", "arbitrary"), VMEM_LIMIT), name="mixer_mla",
    )(safe, qb, kb, vb, kb, vb)


def _ret_kernel(qf_ref, kf_ref, vf_ref, qr_ref, kr_ref, vr_ref, ld_ref, of_ref, or_ref,
                s_ref, dec_ref, qw_ref, kw_ref, cd_ref):
    step = pl.program_id(1)
    cc = qf_ref.shape[0]
    low = lax.broadcasted_iota(jnp.int32, (cc, LANES), 1) < HEAD_DIM
    sr = lax.broadcasted_iota(jnp.int32, (LANES, LANES), 0)
    blockdiag = (sr < HEAD_DIM) == (lax.broadcasted_iota(jnp.int32, (LANES, LANES), 1) < HEAD_DIM)

    @pl.when(step == 0)
    def _():
        s_ref[...] = jnp.zeros_like(s_ref)
        qi = lax.broadcasted_iota(jnp.int32, (cc, cc), 0)
        ki = lax.broadcasted_iota(jnp.int32, (cc, cc), 1)
        pos = lax.broadcasted_iota(jnp.int32, (cc, LANES), 0).astype(F32)
        for d in range(2):
            diff = (qi - ki) if d == 0 else (ki - qi)
            dpos = jnp.maximum(diff, 0).astype(F32)
            qpow = (pos + 1.0) if d == 0 else (cc - pos)
            kpow = (cc - 1.0 - pos) if d == 0 else pos
            for j in range(C_HEADS // 2):
                lg_e = -jnp.exp(ld_ref[d, 2 * j:2 * j + 1, :])
                lg_o = -jnp.exp(ld_ref[d, 2 * j + 1:2 * j + 2, :])
                lgl = jnp.where(low[0:1, :], lg_e, lg_o)
                wide_e = jnp.concatenate([lg_e] * (cc // LANES), axis=1)
                wide_o = jnp.concatenate([lg_o] * (cc // LANES), axis=1)
                dec_ref[d, j, :cc] = jnp.where(diff >= 0, jnp.exp(wide_e * dpos), 0.0)
                dec_ref[d, j, cc:] = jnp.where(diff >= 0, jnp.exp(wide_o * dpos), 0.0)
                qw_ref[d, j] = jnp.exp(lgl * qpow)
                kw_ref[d, j] = jnp.exp(lgl * kpow)
                cd_ref[d, j] = jnp.where(sr < HEAD_DIM, jnp.exp(lg_e * cc), jnp.exp(lg_o * cc))

    pairs = C_HEADS // 2
    dirs = ((qf_ref, kf_ref, vf_ref, of_ref), (qr_ref, kr_ref, vr_ref, or_ref))
    states = [[s_ref[d, j] for j in range(pairs)] for d in range(2)]
    outs, new_states = [], []
    for d, (q_ref, k_ref, v_ref, _) in enumerate(dirs):
        for j in range(pairs):
            sl = slice(j * LANES, (j + 1) * LANES)
            q, k, v = q_ref[:, sl], k_ref[:, sl], v_ref[:, sl]
            zero = jnp.zeros_like(q)
            q2 = jnp.concatenate([jnp.where(low, q, zero), jnp.where(low, zero, q)], axis=0)
            sc = (_nt(q2, k) * dec_ref[d, j]).astype(BF16)
            o2 = jnp.dot(sc, v, preferred_element_type=F32)
            o_intra = jnp.where(low, o2[:cc], o2[cc:])
            state = states[d][j]
            qw = (q.astype(F32) * qw_ref[d, j]).astype(BF16)
            o_cross = jnp.dot(qw, state.astype(BF16), preferred_element_type=F32)
            kw = (k.astype(F32) * kw_ref[d, j]).astype(BF16)
            kv = lax.dot_general(kw, v, (((0,), (0,)), ((), ())), preferred_element_type=F32)
            new_states.append(state * cd_ref[d, j] + jnp.where(blockdiag, kv, 0.0))
            outs.append(o_intra + o_cross)
    for d in range(2):
        for j in range(pairs):
            s_ref[d, j] = new_states[d * pairs + j]
            dirs[d][3][:, j * LANES:(j + 1) * LANES] = outs[d * pairs + j].astype(BF16)


def _retention(prep, proj, ld_head, *, n_batch, seq, n_ctx):
    cc = RET_CHUNK
    nl, nc = seq // cc, n_ctx // cc
    steps = nc + nl
    lat_blocks = n_batch * nl
    rows = n_batch * (seq + n_ctx)

    def rowblk(b, d, s):
        c_ctx = jnp.where(d == 0, s, nc - 1 - s)
        c_lat = jnp.where(d == 0, s - nc, nl - 1 - (s - nc))
        return jnp.where(s < nc, lat_blocks + b * nc + c_ctx, b * nl + c_lat)

    def spec(d, col):
        return pl.BlockSpec((cc, 512), lambda b, s: (rowblk(b, d, s), col))

    pairs = C_HEADS // 2
    return pl.pallas_call(
        _ret_kernel,
        grid=(n_batch, steps),
        in_specs=[spec(d, col) for d in (0, 1) for col in (P_CQ // 512, P_CK // 512, P_CV // 512)]
        + [pl.BlockSpec((2, C_HEADS, LANES), lambda b, s: (0, 0, 0))],
        out_specs=[spec(0, 0), spec(1, 0)],
        out_shape=[jax.ShapeDtypeStruct((rows, 512), BF16)] * 2,
        scratch_shapes=[pltpu.VMEM((2, pairs, LANES, LANES), F32),
                        pltpu.VMEM((2, pairs, 2 * cc, cc), F32),
                        pltpu.VMEM((2, pairs, cc, LANES), F32),
                        pltpu.VMEM((2, pairs, cc, LANES), F32),
                        pltpu.VMEM((2, pairs, LANES, LANES), F32)],
        compiler_params=_cp(("arbitrary", "arbitrary"), VMEM_LIMIT), name="mixer_retention",
    )(prep, prep, proj, prep, prep, proj, ld_head)


def _retfin_kernel(of_ref, or_ref, g_ref, gm_ref, out_ref):
    o = of_ref[...].astype(F32) + or_ref[...].astype(F32)
    on = o * lax.rsqrt(_group_meansq(o, gm_ref, 512) + EPS)
    g = g_ref[...].astype(F32)
    out_ref[...] = (g * jax.nn.sigmoid(g) * on).astype(out_ref.dtype)


def _retention_finish(o_fwd, o_rev, proj, gmat, *, rows):
    tr = 512
    return pl.pallas_call(
        _retfin_kernel,
        grid=(rows // tr,),
        in_specs=[pl.BlockSpec((tr, 512), lambda i: (i, 0)),
                  pl.BlockSpec((tr, 512), lambda i: (i, 0)),
                  pl.BlockSpec((tr, 512), lambda i: (i, P_CG // 512)),
                  pl.BlockSpec((512, 512), lambda i: (0, 0))],
        out_specs=pl.BlockSpec((tr, 512), lambda i: (i, 0)),
        out_shape=jax.ShapeDtypeStruct((rows, 512), BF16),
        compiler_params=_cp(("arbitrary",), VMEM_LIMIT), name="retention_finish",
    )(o_fwd, o_rev, proj, gmat)


def _nbr_kernel(par_ref, q_ref, k_ref, v_ref, kc_ref, vc_ref, abm_ref, o_ref, *, rows):
    st = pl.program_id(1)
    start = jnp.clip(NA_QROWS * st - NA_ROWS // 2, 0, rows - NA_KROWS)
    off = pl.multiple_of(start * GRID_W, GRID_W)
    nk = NA_KROWS * GRID_W
    tq = q_ref.shape[0]
    low = lax.broadcasted_iota(jnp.int32, (tq, LANES), 1) < HEAD_DIM
    bound = par_ref[0]
    safe = par_ref[1] > 0.0
    pair_w = 2 * LANES

    def attend(use_max):
        for j in range(D_HEADS // 2):
            sl = slice(j * LANES, (j + 1) * LANES)
            sl2 = slice(j * pair_w, (j + 1) * pair_w)
            qc = q_ref[:, sl]
            zero = jnp.zeros_like(qc)
            q2 = jnp.concatenate([jnp.where(low, qc, zero), jnp.where(low, zero, qc)], axis=0)
            bias = jnp.concatenate([abm_ref[0, 2 * j], abm_ref[0, 2 * j + 1]], axis=0)
            s1 = _nt(q2, k_ref[pl.ds(off, nk), sl]) + bias
            s2 = _nt(q2, kc_ref[:, sl]) - bound
            if use_max:
                m = jnp.maximum(jnp.max(s1, axis=-1, keepdims=True), jnp.max(s2, axis=-1, keepdims=True))
                s1, s2 = s1 - m, s2 - m
            o = (jnp.dot(jnp.exp2(s1).astype(BF16), v_ref[pl.ds(off, nk), sl2], preferred_element_type=F32)
                 + jnp.dot(jnp.exp2(s2).astype(BF16), vc_ref[:, sl2], preferred_element_type=F32))
            res = o[:, :LANES] / o[:, LANES:LANES + 1]
            o_ref[:, sl] = jnp.where(low, res[:tq], res[tq:]).astype(o_ref.dtype)

    @pl.when(safe)
    def _():
        attend(False)

    @pl.when(jnp.logical_not(safe))
    def _():
        attend(True)


def _nbr(prep, dv2, abm, par, *, n_batch, seq, n_ctx, with_ctx):
    rows = seq // GRID_W
    tq = NA_QROWS * GRID_W
    assert n_ctx == tq and rows % NA_QROWS == 0 and rows >= NA_KROWS + NA_QROWS
    nst = rows // NA_QROWS
    steps = nst + (1 if with_ctx else 0)
    rows_out = n_batch * seq + (n_batch * n_ctx if with_ctx else 0)

    def qmap(col):
        def f(b, s, par):
            return (jnp.where(s < nst, b * nst + s, n_batch * nst + b), col)
        return f

    def case(b, s, par):
        c = jnp.where(s == 0, 0, jnp.where(s == nst - 1, 2, jnp.where(s == nst, 3, 1)))
        return (c, 0, 0, 0)

    ctx_blk = n_batch * seq // n_ctx
    return pl.pallas_call(
        functools.partial(_nbr_kernel, rows=rows),
        grid_spec=pltpu.PrefetchScalarGridSpec(
            num_scalar_prefetch=1, grid=(n_batch, steps),
            in_specs=[pl.BlockSpec((tq, 512), qmap(P_DQ // 512)),
                      pl.BlockSpec((seq, 512), lambda b, s, par: (b, P_DK // 512)),
                      pl.BlockSpec((seq, 2 * 512), lambda b, s, par: (b, 0)),
                      pl.BlockSpec((n_ctx, 512), lambda b, s, par: (ctx_blk + b, P_DK // 512)),
                      pl.BlockSpec((n_ctx, 2 * 512), lambda b, s, par: (ctx_blk + b, 0)),
                      pl.BlockSpec((1, D_HEADS, tq, NA_KROWS * GRID_W), case)],
            out_specs=pl.BlockSpec((tq, 512), qmap(0))),
        out_shape=jax.ShapeDtypeStruct((rows_out, 512), BF16),
        compiler_params=_cp(("arbitrary", "arbitrary"), VMEM_LIMIT), name="mixer_neighbourhood",
    )(par, prep, prep, dv2, prep, dv2, abm)


def _softmax_bound(q_gain, k_gain, extra):
    b = HEAD_DIM ** 0.5 * LOG2E * jnp.max(jnp.abs(q_gain)) * jnp.max(jnp.abs(k_gain)) + LOG2E * extra
    return (1.02 * b + 0.5).astype(F32)


def _nbr_bias_tables(rpb, rows, bound):
    w = GRID_W
    cidx = np.arange(w)
    col_start = np.clip(cidx - NA_COLS // 2, 0, w - NA_COLS)
    col_ok = (cidx[None, :] >= col_start[:, None]) & (cidx[None, :] < col_start[:, None] + NA_COLS)
    d_col = np.clip(cidx[None, :] - cidx[:, None] + (NA_COLS - 1), 0, 2 * NA_COLS - 2)
    n_heads, n_dr, n_dc = rpb.shape
    onehot = jnp.asarray((d_col[:, :, None] == np.arange(n_dc)[None, None, :]).astype(np.float32))
    t = jnp.einsum("hrd,qkd->hqrk", rpb.astype(F32), onehot, precision=lax.Precision.HIGHEST)
    t = t * LOG2E - bound
    t = jnp.where(jnp.asarray(col_ok)[None, :, None, :], t, NEG_INF).reshape(n_heads, w, n_dr * w)
    cases = []
    for r0 in (0, NA_QROWS, rows - NA_QROWS):
        start = int(np.clip(r0 - NA_ROWS // 2, 0, rows - NA_KROWS))
        blocks = []
        for i in range(NA_QROWS):
            r = r0 + i
            row_start = int(np.clip(r - NA_ROWS // 2, 0, rows - NA_ROWS))
            kk0 = row_start - start
            dr0 = row_start - r + (NA_ROWS - 1)
            blk = t[:, :, dr0 * w:(dr0 + NA_ROWS) * w]
            blocks.append(jnp.pad(blk, ((0, 0), (0, 0), (kk0 * w, (NA_KROWS - NA_ROWS - kk0) * w)),
                                  constant_values=NEG_INF))
        cases.append(jnp.concatenate(blocks, axis=1))
    cases.append(jnp.full((n_heads, NA_QROWS * w, NA_KROWS * w), NEG_INF, F32))
    return jnp.stack(cases)


def _swiglu_accumulate(x_ref, wg_ref, wu_ref, wd_ref, acc_ref, n_rows):
    x = x_ref[:n_rows]
    g = jnp.dot(x, wg_ref[0].astype(BF16), preferred_element_type=F32)
    u = jnp.dot(x, wu_ref[0].astype(BF16), preferred_element_type=F32)
    a = (g * jax.nn.sigmoid(g) * u).astype(BF16)
    acc_ref[:n_rows] += jnp.dot(a, wd_ref[0].astype(BF16), preferred_element_type=F32)


def _ffn_kernel(x_ref, wg_ref, wu_ref, wd_ref, o_ref, acc_ref):
    j = pl.program_id(1)

    @pl.when(j == 0)
    def _():
        acc_ref[...] = jnp.zeros_like(acc_ref)

    _swiglu_accumulate(x_ref, wg_ref, wu_ref, wd_ref, acc_ref, x_ref.shape[0])

    @pl.when(j == pl.num_programs(1) - 1)
    def _():
        o_ref[...] = acc_ref[...].astype(o_ref.dtype)


def _ffn(x, w_gate_up, w_down, layer, *, rows, tm, fc):
    _, d, ff2 = w_gate_up.shape
    nj = ff2 // 2 // fc
    return pl.pallas_call(
        _ffn_kernel,
        grid=(rows // tm, nj),
        in_specs=[pl.BlockSpec((tm, d), lambda i, j: (i, 0)),
                  pl.BlockSpec((1, d, fc), lambda i, j: (layer, 0, j)),
                  pl.BlockSpec((1, d, fc), lambda i, j: (layer, 0, nj + j)),
                  pl.BlockSpec((1, fc, d), lambda i, j: (layer, j, 0))],
        out_specs=pl.BlockSpec((tm, d), lambda i, j: (i, 0)),
        out_shape=jax.ShapeDtypeStruct((rows, d), BF16),
        scratch_shapes=[pltpu.VMEM((tm, d), F32)],
        compiler_params=_cp(("arbitrary", "arbitrary"), VMEM_LIMIT), name="swiglu",
    )(x, w_gate_up, w_gate_up, w_down)


def _moe_ffn_kernel(te_ref, tv_ref, tok_ref, hp_ref, wg_ref, wu_ref, wd_ref, o_ref, gbuf_ref, x_ref, acc_ref, sem,
                    *, rows_per_step):
    i, j = pl.program_id(0), pl.program_id(1)
    n_tiles = pl.num_programs(0)
    tm, d = x_ref.shape
    half = d // 2
    valid = tv_ref[i] > 0
    slot = i % 2

    def issue(tile, first_row, n_rows, to_slot):
        def body(r, carry):
            row = first_row + r
            pltpu.make_async_copy(hp_ref.at[pl.ds(tok_ref[tile * tm + row], 1), :],
                                  gbuf_ref.at[to_slot].at[pl.ds(row, 1), :], sem.at[to_slot]).start()
            return carry

        lax.fori_loop(0, n_rows, body, 0, unroll=8)

    @pl.when((i == 0) & (j == 0) & valid)
    def _():
        issue(0, 0, tm, 0)

    @pl.when((j == 0) & valid)
    def _():
        pltpu.make_async_copy(hp_ref.at[pl.ds(0, tm), :], gbuf_ref.at[slot], sem.at[slot]).wait()
        lo, hi = _unpack_halves(gbuf_ref[slot])
        x_ref[:, :half] = lo.astype(BF16)
        x_ref[:, half:] = hi.astype(BF16)
        acc_ref[...] = jnp.zeros_like(acc_ref)

    nxt = jnp.minimum(i + 1, n_tiles - 1)

    @pl.when((i + 1 < n_tiles) & (tv_ref[nxt] > 0) & (j * rows_per_step < tm))
    def _():
        issue(nxt, j * rows_per_step, rows_per_step, 1 - slot)

    for q in range(1, MOE_ROW_SPLITS + 1):
        n_rows = q * (tm // MOE_ROW_SPLITS)

        @pl.when(tv_ref[i] == n_rows)
        def _():
            _swiglu_accumulate(x_ref, wg_ref, wu_ref, wd_ref, acc_ref, n_rows)

    @pl.when(j == pl.num_programs(1) - 1)
    def _():
        o_ref[...] = jnp.where(valid, _pack_halves(acc_ref[...]), jnp.uint32(0))


def _moe_ffn(hp, w_gate_up, w_down, layer, tile_expert, tile_valid, row_token, *, tm, fc):
    _, _, d, ff2 = w_gate_up.shape
    nj = ff2 // 2 // fc
    n_rows = row_token.shape[0]
    issue_steps = 1 << (nj.bit_length() - 1)

    def jeff(i, j, tv):
        return jnp.where(tv[i] > 0, j, nj - 1)

    return pl.pallas_call(
        functools.partial(_moe_ffn_kernel, rows_per_step=tm // issue_steps),
        grid_spec=pltpu.PrefetchScalarGridSpec(
            num_scalar_prefetch=3, grid=(n_rows // tm, nj),
            in_specs=[pl.BlockSpec(memory_space=pl.ANY),
                      pl.BlockSpec((None, 1, d, fc), lambda i, j, te, tv, tok: (layer, te[i], 0, jeff(i, j, tv))),
                      pl.BlockSpec((None, 1, d, fc),
                                   lambda i, j, te, tv, tok: (layer, te[i], 0, nj + jeff(i, j, tv))),
                      pl.BlockSpec((None, 1, fc, d), lambda i, j, te, tv, tok: (layer, te[i], jeff(i, j, tv), 0))],
            out_specs=pl.BlockSpec((tm, d // 2), lambda i, j, te, tv, tok: (i, 0)),
            scratch_shapes=[pltpu.VMEM((2, tm, d // 2), jnp.uint32), pltpu.VMEM((tm, d), BF16),
                            pltpu.VMEM((tm, d), F32), pltpu.SemaphoreType.DMA((2,))]),
        out_shape=jax.ShapeDtypeStruct((n_rows, d // 2), jnp.uint32),
        compiler_params=_cp(("arbitrary", "arbitrary"), VMEM_LIMIT_MOE), name="moe_swiglu",
    )(tile_expert, tile_valid, row_token, hp, w_gate_up, w_gate_up, w_down)


def _combine_kernel(p0_ref, p1_ref, y_ref, x_ref, route_ref, gate_ref, o_ref, b0_ref, b1_ref, sem):
    _, n, half = b0_ref.shape
    i = pl.program_id(0)
    slot = i % 2

    def issue(tile, to_slot):
        def body(r, carry):
            k = tile * n + r
            pltpu.make_async_copy(y_ref.at[pl.ds(p0_ref[k], 1), :], b0_ref.at[to_slot].at[pl.ds(r, 1), :],
                                  sem.at[0, to_slot]).start()
            pltpu.make_async_copy(y_ref.at[pl.ds(p1_ref[k], 1), :], b1_ref.at[to_slot].at[pl.ds(r, 1), :],
                                  sem.at[1, to_slot]).start()
            return carry

        lax.fori_loop(0, n, body, 0, unroll=8)

    @pl.when(i == 0)
    def _():
        issue(0, 0)

    def wait(s):
        pltpu.make_async_copy(y_ref.at[pl.ds(0, n), :], b0_ref.at[s], sem.at[0, s]).wait()
        pltpu.make_async_copy(y_ref.at[pl.ds(0, n), :], b1_ref.at[s], sem.at[1, s]).wait()

    wait(slot)
    last = pl.num_programs(0) - 1
    nxt = jnp.minimum(i + 1, last)
    gate = gate_ref[0, 0]
    group = 8

    def body(t, carry):
        r0 = pl.multiple_of(t * group, group)
        for u in range(group):
            k = nxt * n + r0 + u
            pltpu.make_async_copy(y_ref.at[pl.ds(p0_ref[k], 1), :], b0_ref.at[1 - slot].at[pl.ds(r0 + u, 1), :],
                                  sem.at[0, 1 - slot]).start()
            pltpu.make_async_copy(y_ref.at[pl.ds(p1_ref[k], 1), :], b1_ref.at[1 - slot].at[pl.ds(r0 + u, 1), :],
                                  sem.at[1, 1 - slot]).start()
        rows = pl.ds(r0, group)
        route = route_ref[rows, :]
        w0, w1 = route[:, 2:3], route[:, 3:4]
        lo0, hi0 = _unpack_halves(b0_ref.at[slot][rows, :])
        lo1, hi1 = _unpack_halves(b1_ref.at[slot][rows, :])
        o_ref[rows, :half] = x_ref[rows, :half] + gate[:, :half] * (w0 * lo0 + w1 * lo1)
        o_ref[rows, half:] = x_ref[rows, half:] + gate[:, half:] * (w0 * hi0 + w1 * hi1)
        return carry

    lax.fori_loop(0, n // group, body, 0)

    @pl.when(i == last)
    def _():
        wait(1 - slot)


def _combine(y, x, route, mod, gate_k, pos0, pos1, *, seq, n_batch):
    t, d = x.shape
    tr = 256
    return pl.pallas_call(
        _combine_kernel,
        grid_spec=pltpu.PrefetchScalarGridSpec(
            num_scalar_prefetch=2, grid=(t // tr,),
            in_specs=[pl.BlockSpec(memory_space=pl.ANY),
                      pl.BlockSpec((tr, d), lambda i, a, b: (i, 0)),
                      pl.BlockSpec((tr, LANES), lambda i, a, b: (i, 0)),
                      pl.BlockSpec((1, 1, 1, d),
                                   lambda i, a, b: (jnp.minimum((i * tr) // seq, n_batch), gate_k, 0, 0))],
            out_specs=pl.BlockSpec((tr, d), lambda i, a, b: (i, 0)),
            scratch_shapes=[pltpu.VMEM((2, tr, d // 2), jnp.uint32), pltpu.VMEM((2, tr, d // 2), jnp.uint32),
                            pltpu.SemaphoreType.DMA((2, 2))]),
        out_shape=jax.ShapeDtypeStruct((t, d), F32),
        compiler_params=_cp(("arbitrary",), VMEM_LIMIT), name="moe_combine",
    )(pos0, pos1, y, x, route, _mod_view(mod))


def _route_meta(idx, n_exp, tm):
    t = idx.shape[0]
    flat = idx.reshape(-1)
    onehot = (flat[:, None] == jnp.arange(n_exp, dtype=jnp.int32)[None, :]).astype(jnp.int32)
    counts = jnp.sum(onehot, axis=0)
    rank = jnp.take_along_axis(jnp.cumsum(onehot, axis=0) - onehot, flat[:, None], axis=1)[:, 0]
    padded = ((counts + tm - 1) // tm) * tm
    ends = jnp.cumsum(padded)
    starts = ends - padded
    pos = starts[flat] + rank
    n_rows = TOP_K * t + n_exp * tm
    row_token = jnp.zeros((n_rows,), jnp.int32).at[pos].set(jnp.arange(TOP_K * t, dtype=jnp.int32) // TOP_K)
    tile_start = jnp.arange(n_rows // tm, dtype=jnp.int32) * tm
    tile_valid = tile_start < ends[-1]
    tile_expert = jnp.sum((ends[None, :] <= tile_start[:, None]).astype(jnp.int32), axis=1)
    tile_expert = jnp.minimum(tile_expert, n_exp - 1)
    last_valid = jnp.maximum(ends[-1] // tm - 1, 0)
    tile_expert = jnp.where(tile_valid, tile_expert, tile_expert[last_valid])
    sub = tm // MOE_ROW_SPLITS
    used = jnp.clip(counts[tile_expert] - (tile_start - starts[tile_expert]), 0, tm)
    tile_rows = jnp.where(tile_valid, ((used + sub - 1) // sub) * sub, 0).astype(jnp.int32)
    pos = pos.reshape(t, TOP_K).astype(jnp.int32)
    return row_token, tile_expert, tile_rows, pos[:, 0], pos[:, 1]


def _rope_tables(seq, pad_rows):
    t = jnp.arange(seq, dtype=jnp.int32)
    row = (t // GRID_W).astype(F32)
    col = (t % GRID_W).astype(F32)

    def angles(rot_dim):
        n_freq = rot_dim // 4
        freqs = ROPE_THETA ** (-jnp.arange(n_freq, dtype=F32) / n_freq)
        ang = jnp.concatenate([row[:, None] * freqs[None, :], col[:, None] * freqs[None, :]], axis=-1)
        return jnp.cos(ang), jnp.sin(ang)

    ch, sh = angles(HEAD_DIM)
    cos_h = jnp.concatenate([ch] * 4, axis=1)
    sin_h = jnp.concatenate([-sh, sh, -sh, sh], axis=1)
    cb, sb = angles(B_ROPE)
    one = jnp.ones((seq, B_NOPE), F32)
    zero = jnp.zeros((seq, B_NOPE), F32)
    tail1 = jnp.ones((seq, LANES - B_QK), F32)
    tail0 = jnp.zeros((seq, LANES - B_QK), F32)
    cos_b = jnp.concatenate([one, cb, cb, tail1], axis=1)
    sin_b = jnp.concatenate([zero, -sb, sb, tail0], axis=1)

    def pad(tab, fill):
        return jnp.concatenate([tab, jnp.full((pad_rows, LANES), fill, F32)], axis=0)

    return pad(cos_h, 1.0), pad(sin_h, 0.0), pad(cos_b, 1.0), pad(sin_b, 0.0)


def _permute_w_in(w):
    d = w.shape[0]
    o = np.cumsum([0, 512, 128, 128, B_Q_RANK, B_KV_RANK, B_ROPE, 512, 512, 512, 512, 512, 512, 512])
    aq, ak, av, bq, bkv, bkr, cq, ck, cv, cg, dq, dk, dv = [w[:, o[i]:o[i + 1]] for i in range(13)]

    def dup(m):
        return jnp.concatenate([m[:, :64], m[:, :64], m[:, 64:], m[:, 64:]], axis=1)

    z = lambda n: jnp.zeros((d, n), w.dtype)
    half = B_ROPE // 2
    bkr_swapped = jnp.concatenate([bkr[:, half:], bkr[:, :half]], axis=1)
    parts = [aq, cq, ck, dq, dk, dup(ak), dup(av), cv, cg, dv, bq, bkv,
             z(B_NOPE), bkr, z(LANES - B_QK), z(B_NOPE), bkr_swapped, z(LANES - B_QK)]
    return jnp.concatenate(parts, axis=1).astype(BF16)


def _block_diag_ones():
    i = np.arange(512) // HEAD_DIM
    return jnp.asarray((i[:, None] == i[None, :]).astype(np.float32), dtype=BF16)


def _rope_partner(m):
    half = B_ROPE // 2
    return jnp.concatenate([m[..., :B_NOPE], m[..., B_NOPE + half:], m[..., B_NOPE:B_NOPE + half]], axis=-1)


def _pad_lanes(v, width):
    return jnp.concatenate([v.astype(F32), jnp.zeros((width - v.shape[0],), F32)])


def kernel(x, c, ctx, c_ctx, w_ada, b_ada, norm_mix, norm_ffn, w_in, w_out, a_q_norm, a_k_norm, a_sink,
           b_q_a_norm, b_kv_a_norm, b_w_uq, b_w_ukv, b_q_norm, b_k_norm, c_log_decay, d_q_norm, d_k_norm,
           d_rpb, ffn_w_gate_up, ffn_w_down, moe_router, moe_w_gate_up, moe_w_down):
    n_batch, seq, d = x.shape
    n_ctx = ctx.shape[1]
    depth = w_ada.shape[0]
    n_lat = n_batch * seq
    n_all = n_lat + n_batch * n_ctx
    grid_rows = seq // GRID_W
    tm = next(t for t in (1024, 512, 256) if n_lat % t == 0 and n_all % t == 0)

    cc = jnp.concatenate([c, c_ctx[None, :], jnp.zeros((8 - n_batch - 1, d), F32)], axis=0)
    mod = _ada(cc, w_ada, b_ada)
    xs = (x.reshape(n_lat, d), ctx.reshape(n_batch * n_ctx, d))
    cos_h, sin_h, cos_b, sin_b = _rope_tables(seq, PREP_ROWS)
    gmat = _block_diag_ones()

    y = None
    y_gate, y_mod = 0, None
    for l in range(depth):
        last = l == depth - 1
        mod_l = mod[l]
        if y is None:
            (h,) = _resnorm(xs, None, None, 0, mod_l, norm_mix[l], 0, 1, rows=n_all, seq=seq, n_batch=n_batch)
        else:
            xs, h = _resnorm(xs, y, y_mod, y_gate, mod_l, norm_mix[l], 0, 1, rows=n_all, seq=seq,
                             n_batch=n_batch)
        proj = _mm([h], _permute_w_in(w_in[l]), rows=n_all, tm=tm, tn=1792, out_dtype=BF16, name="proj_in")
        gains = jnp.stack([jnp.tile(a_q_norm[l], 8), _pad_lanes(jnp.tile(a_k_norm[l], 4), 512),
                           jnp.tile(d_q_norm[l], 8), jnp.tile(d_k_norm[l], 8)]).astype(F32)
        prep, dv2 = _prep(proj, cos_h, sin_h, gmat, gains, rows=n_all, seq=seq, n_lat=n_lat)
        wuq = jnp.pad(b_w_uq[l].reshape(B_Q_RANK, B_HEADS, B_QK),
                      ((0, 0), (0, 0), (0, LANES - B_QK))).reshape(B_Q_RANK, B_HEADS * LANES).astype(BF16)
        wukv3 = b_w_ukv[l].reshape(B_KV_RANK, B_HEADS, B_NOPE + B_V)
        wukv = jnp.concatenate(
            [jnp.pad(wukv3[:, :, :B_NOPE], ((0, 0), (0, 0), (0, LANES - B_NOPE))).reshape(B_KV_RANK, -1),
             wukv3[:, :, B_NOPE:].reshape(B_KV_RANK, -1)], axis=1).astype(BF16)
        bound = _mla_logit_bound(b_q_norm[l], b_k_norm[l])
        mla_safe = (bound <= SOFTMAX_SAFE_BOUND).astype(jnp.int32).reshape(1)
        gains_b = jnp.stack([b_q_a_norm[l].astype(F32), _pad_lanes(b_kv_a_norm[l], B_Q_RANK),
                             _pad_lanes(b_q_norm[l], B_Q_RANK), _pad_lanes(b_k_norm[l], B_Q_RANK),
                             jnp.full((B_Q_RANK,), -1.0, F32) * bound,
                             _pad_lanes(_rope_partner(b_q_norm[l]), B_Q_RANK),
                             _pad_lanes(_rope_partner(b_k_norm[l]), B_Q_RANK), jnp.zeros((B_Q_RANK,), F32)])
        wuq3 = b_w_uq[l].reshape(B_Q_RANK, B_HEADS, B_QK)
        wuq_sw = jnp.pad(_rope_partner(wuq3), ((0, 0), (0, 0), (0, LANES - B_QK))).reshape(wuq.shape).astype(BF16)
        qb, kb, vb = _prepb(proj, cos_b, sin_b, wuq, wuq_sw, wukv, gains_b, rows=n_all, seq=seq, n_lat=n_lat)

        sink = a_sink[l].astype(F32)
        bound_a = _softmax_bound(a_q_norm[l], a_k_norm[l], jnp.maximum(jnp.max(sink), 0.0))
        par_a = jnp.concatenate([sink * LOG2E, bound_a[None], (bound_a <= SOFTMAX_SAFE_BOUND).astype(F32)[None]])
        oa = _wina(prep, proj, par_a, n_batch=n_batch, seq=seq, n_ctx=n_ctx, with_ctx=not last)
        ob = _mla(qb, kb, vb, mla_safe, n_batch=n_batch, seq=seq, n_ctx=n_ctx, with_ctx=not last)
        ld_head = jnp.broadcast_to(c_log_decay[l].astype(F32)[:, :, None], (2, C_HEADS, LANES))
        oc_fwd, oc_rev = _retention(prep, proj, ld_head, n_batch=n_batch, seq=seq, n_ctx=n_ctx)
        bound_d = _softmax_bound(d_q_norm[l], d_k_norm[l], jnp.max(jnp.abs(d_rpb[l])))
        par_d = jnp.stack([bound_d, (bound_d <= SOFTMAX_SAFE_BOUND).astype(F32)])
        abm = _nbr_bias_tables(d_rpb[l], grid_rows, bound_d)
        od = _nbr(prep, dv2, abm, par_d, n_batch=n_batch, seq=seq, n_ctx=n_ctx, with_ctx=not last)
        rows_l = n_lat if last else n_all
        oc = _retention_finish(oc_fwd, oc_rev, proj, gmat, rows=rows_l)
        ymix = _mm([oa, ob, oc, od], w_out[l].astype(BF16), rows=rows_l, tm=tm, tn=2048, out_dtype=BF16,
                   name="proj_out")
        i = l // 2
        if l % 2 == 0:
            xs, h2 = _resnorm(xs, ymix, mod_l, 2, mod_l, norm_ffn[l], 3, 4, rows=rows_l, seq=seq,
                              n_batch=n_batch)
            y = _ffn(h2, ffn_w_gate_up, ffn_w_down, i, rows=rows_l, tm=tm, fc=512)
            y_gate, y_mod = 5, mod_l
        else:
            xs, hp, route = _resnorm(xs, ymix, mod_l, 2, mod_l, norm_ffn[l], 3, 4, rows=rows_l, seq=seq,
                                     n_batch=n_batch, router=moe_router[i])
            top_idx = route[:, :TOP_K].astype(jnp.int32)
            row_token, tile_expert, tile_valid, pos0, pos1 = _route_meta(top_idx, moe_router.shape[2], tm)
            yg = _moe_ffn(hp, moe_w_gate_up, moe_w_down, i, tile_expert, tile_valid, row_token, tm=tm, fc=512)
            xs = _combine(yg, xs, route, mod_l, 5, pos0, pos1, seq=seq, n_batch=n_batch)
            y = None
    if y is not None:
        xs, _ = _resnorm(xs, y, y_mod, y_gate, y_mod, norm_ffn[depth - 1], 3, 4, rows=xs.shape[0], seq=seq,
                         n_batch=n_batch)
    return xs[:n_lat].reshape(n_batch, seq, d)
```
